```python
import math
import jax
import jax.numpy as jnp
from jax import lax
import numpy as np

D_MODEL = 2048
BATCH = 32
SEQ = 256
DEPTH = 4
DEC_BATCH = 8
DEC_SEQ = 4096
PAST_LEN = 256

GRID_W = 64
HEAD_DIM = 128
QBLOCK = 128
ROPE_BASE = 10000.0
NEG_INF = -1e30
N_MIXERS = 3
A_HEADS = D_MODEL // HEAD_DIM
A_KV_HEADS = A_HEADS // 4
A_GROUP = A_HEADS // A_KV_HEADS
WINDOW = 128
A_SPLIT = (A_HEADS * HEAD_DIM, A_HEADS * HEAD_DIM + A_KV_HEADS * HEAD_DIM, A_HEADS * HEAD_DIM + 2 * A_KV_HEADS * HEAD_DIM)
A_IN = 2 * A_HEADS * HEAD_DIM + 2 * A_KV_HEADS * HEAD_DIM
B_HEADS = D_MODEL // (2 * HEAD_DIM)
B_IN = 4 * B_HEADS * 2 * HEAD_DIM
C_HEADS = D_MODEL // HEAD_DIM
NA_KH = 8
NA_KW = 16
C_IN = 4 * C_HEADS * HEAD_DIM
N_A = len(range(0, DEPTH, N_MIXERS))
N_B = len(range(1, DEPTH, N_MIXERS))
N_C = len(range(2, DEPTH, N_MIXERS))
SCALE = HEAD_DIM ** -0.5

kernel_name = 'hybrid_diffusion_prefix_trunk'


def rms_norm(x, g, eps=1e-6):
    xf = x.astype(jnp.float32)
    y = xf * lax.rsqrt(jnp.mean(xf * xf, axis=-1, keepdims=True) + eps)
    return (y * g.astype(jnp.float32)).astype(x.dtype)


def ada_modulation(cvec, w, b):
    m = (jax.nn.silu(cvec) @ w + b)[:, None, :]
    shift, scale, gate = jnp.split(m, 3, axis=-1)
    return shift, scale, gate


def axial_rope_tables(n_tokens):
    d4 = HEAD_DIM // 4
    t = jnp.arange(n_tokens)
    pos = jnp.stack([t // GRID_W, t % GRID_W], axis=-1).astype(jnp.float32)
    inv_freq = ROPE_BASE ** (-jnp.arange(d4, dtype=jnp.float32) / d4)
    ang = pos[:, :, None] * inv_freq
    return jnp.cos(ang), jnp.sin(ang)


def apply_rope(x, cos, sin):
    n = x.shape[1]
    d4 = x.shape[-1] // 4
    bshape = (n,) + (1,) * (x.ndim - 3) + (2, d4)
    c = cos.reshape(bshape)
    s = sin.reshape(bshape)
    xf = x.astype(jnp.float32).reshape(x.shape[:-1] + (2, 2, d4))
    x1 = xf[..., 0, :]
    x2 = xf[..., 1, :]
    out = jnp.stack([x1 * c - x2 * s, x2 * c + x1 * s], axis=-2)
    return out.reshape(x.shape).astype(x.dtype)


def dense_attention(q, k, v, sink=None):
    b, n, h, d = q.shape
    kv = k.shape[2]
    grp = h // kv
    qb = q.reshape(b, n // QBLOCK, QBLOCK, kv, grp, d).transpose(1, 0, 2, 3, 4, 5)

    def block(qi):
        s = jnp.einsum('bqkgd,blkd->bkgql', qi, k, preferred_element_type=jnp.float32) * SCALE
        if sink is not None:
            sk = jnp.broadcast_to(sink.astype(jnp.float32).reshape(kv, grp, 1, 1), s.shape[:-1] + (1,))
            p = jax.nn.softmax(jnp.concatenate([s, sk], axis=-1), axis=-1)[..., :-1]
        else:
            p = jax.nn.softmax(s, axis=-1)
        return jnp.einsum('bkgql,blkd->bqkgd', p.astype(v.dtype), v)

    o = lax.map(block, qb)
    return o.transpose(1, 0, 2, 3, 4, 5).reshape(b, n, h * d)


def a_project(h, w_in, qg, kg):
    b, n, _ = h.shape
    q, k, v, g = jnp.split(h @ w_in, A_SPLIT, axis=-1)
    q = rms_norm(q.reshape(b, n, A_HEADS, HEAD_DIM), qg)
    k = rms_norm(k.reshape(b, n, A_KV_HEADS, HEAD_DIM), kg)
    v = v.reshape(b, n, A_KV_HEADS, HEAD_DIM)
    return q, k, v, g


def window_attn_ctx(h, w_in, qg, kg, sink):
    q, k, v, g = a_project(h, w_in, qg, kg)
    o = dense_attention(q, k, v, sink)
    return o * jax.nn.silu(g), k, v


def window_attn_lat(h, k_ctx, v_ctx, w_in, qg, kg, sink, cos, sin):
    b, n, _ = h.shape
    nb = n // QBLOCK
    n_ctx = k_ctx.shape[1]
    span = 3 * QBLOCK
    q, k, v, g = a_project(h, w_in, qg, kg)
    q = apply_rope(q, cos, sin)
    k = apply_rope(k, cos, sin)
    pad = ((0, 0), (QBLOCK, QBLOCK), (0, 0), (0, 0))
    kp = jnp.pad(k, pad).reshape(b, nb + 2, QBLOCK, A_KV_HEADS, HEAD_DIM)
    vp = jnp.pad(v, pad).reshape(b, nb + 2, QBLOCK, A_KV_HEADS, HEAD_DIM)
    kband = jnp.concatenate([kp[:, :-2], kp[:, 1:-1], kp[:, 2:]], axis=2).transpose(1, 0, 2, 3, 4)
    vband = jnp.concatenate([vp[:, :-2], vp[:, 1:-1], vp[:, 2:]], axis=2).transpose(1, 0, 2, 3, 4)
    qb = q.reshape(b, nb, QBLOCK, A_KV_HEADS, A_GROUP, HEAD_DIM).transpose(1, 0, 2, 3, 4, 5)
    rel = jnp.arange(span)[None, :] - QBLOCK - jnp.arange(QBLOCK)[:, None]
    in_window = jnp.abs(rel) <= WINDOW
    sk = jnp.broadcast_to(sink.astype(jnp.float32).reshape(A_KV_HEADS, A_GROUP, 1, 1),
                          (b, A_KV_HEADS, A_GROUP, QBLOCK, 1))

    def block(args):
        qi, ki, vi, blk = args
        kpos = (blk - 1) * QBLOCK + jnp.arange(span)
        valid = in_window & ((kpos >= 0) & (kpos < n))[None, :]
        s_loc = jnp.einsum('bqkgd,bjkd->bkgqj', qi, ki, preferred_element_type=jnp.float32) * SCALE
        s_loc = jnp.where(valid, s_loc, NEG_INF)
        s_ctx = jnp.einsum('bqkgd,blkd->bkgql', qi, k_ctx, preferred_element_type=jnp.float32) * SCALE
        p = jax.nn.softmax(jnp.concatenate([s_loc, s_ctx, sk], axis=-1), axis=-1)
        p_loc = p[..., :span].astype(vi.dtype)
        p_ctx = p[..., span:span + n_ctx].astype(v_ctx.dtype)
        return (jnp.einsum('bkgqj,bjkd->bqkgd', p_loc, vi)
                + jnp.einsum('bkgql,blkd->bqkgd', p_ctx, v_ctx))

    o = lax.map(block, (qb, kband, vband, jnp.arange(nb)))
    o = o.transpose(1, 0, 2, 3, 4, 5).reshape(b, n, A_HEADS * HEAD_DIM)
    return o * jax.nn.silu(g)


def b_project(h, w_in, qg, kg):
    b, n, _ = h.shape
    q, k, v, g = jnp.split(h @ w_in, 4, axis=-1)
    q = rms_norm(q.reshape(b, n, B_HEADS, 2, HEAD_DIM), qg)
    k = rms_norm(k.reshape(b, n, B_HEADS, 2, HEAD_DIM), kg)
    v = v.reshape(b, n, B_HEADS, 2 * HEAD_DIM)
    return q, k, v, g


def diff_lambda(lam, lambda_init):
    lam = lam.astype(jnp.float32)
    return jnp.exp(jnp.sum(lam[0] * lam[1])) - jnp.exp(jnp.sum(lam[2] * lam[3])) + lambda_init


def diff_attention(q, k, v, lam_full, subln, lambda_init, g):
    b, n = q.shape[:2]
    qb = q.reshape(b, n // QBLOCK, QBLOCK, B_HEADS, 2, HEAD_DIM).transpose(1, 0, 2, 3, 4, 5)

    def block(qi):
        s = jnp.einsum('bqhmd,bkhmd->bhmqk', qi, k, preferred_element_type=jnp.float32) * SCALE
        p = jax.nn.softmax(s, axis=-1)
        a = p[:, :, 0] - lam_full * p[:, :, 1]
        return jnp.einsum('bhqk,bkhe->bqhe', a.astype(v.dtype), v)

    o = lax.map(block, qb).transpose(1, 0, 2, 3, 4).reshape(b, n, B_HEADS, 2 * HEAD_DIM)
    o = rms_norm(o, subln) * (1.0 - lambda_init)
    return o.reshape(b, n, B_HEADS * 2 * HEAD_DIM) * jax.nn.silu(g)


def diff_attn_ctx(h, w_in, qg, kg, lam, subln, lambda_init):
    q, k, v, g = b_project(h, w_in, qg, kg)
    o = diff_attention(q, k, v, diff_lambda(lam, lambda_init), subln, lambda_init, g)
    return o, k, v


def diff_attn_lat(h, k_ctx, v_ctx, w_in, qg, kg, lam, subln, lambda_init, cos, sin):
    q, k, v, g = b_project(h, w_in, qg, kg)
    q = apply_rope(q, cos, sin)
    k = apply_rope(k, cos, sin)
    k_all = jnp.concatenate([k, k_ctx.astype(k.dtype)], axis=1)
    v_all = jnp.concatenate([v, v_ctx.astype(v.dtype)], axis=1)
    return diff_attention(q, k_all, v_all, diff_lambda(lam, lambda_init), subln, lambda_init, g)


def c_project(h, w_in, qg, kg):
    b, n, _ = h.shape
    q, k, v, g = jnp.split(h @ w_in, 4, axis=-1)
    q = rms_norm(q.reshape(b, n, C_HEADS, HEAD_DIM), qg)
    k = rms_norm(k.reshape(b, n, C_HEADS, HEAD_DIM), kg)
    v = v.reshape(b, n, C_HEADS, HEAD_DIM)
    return q, k, v, g


def na_ctx(h, w_in, qg, kg):
    q, k, v, g = c_project(h, w_in, qg, kg)
    return dense_attention(q, k, v) * jax.nn.silu(g), k, v


def na_lat(h, k_ctx, v_ctx, w_in, qg, kg, rpb):
    b, n, _ = h.shape
    rows = n // GRID_W
    kh = min(NA_KH, rows)
    q, k, v, g = c_project(h, w_in, qg, kg)
    q_rows = q.reshape(b, rows, GRID_W, C_HEADS, HEAD_DIM).transpose(1, 0, 2, 3, 4)
    k_grid = k.reshape(b, rows, GRID_W, C_HEADS, HEAD_DIM)
    v_grid = v.reshape(b, rows, GRID_W, C_HEADS, HEAD_DIM)
    cols = jnp.arange(GRID_W)
    col_start = jnp.clip(cols - NA_KW // 2, 0, GRID_W - NA_KW)
    col_idx = col_start[:, None] + jnp.arange(NA_KW)
    col_bias_idx = col_idx - cols[:, None] + NA_KW - 1
    n_loc = kh * NA_KW

    def row_block(args):
        qr, r = args
        rs = jnp.clip(r - kh // 2, 0, rows - kh)
        kb = lax.dynamic_slice_in_dim(k_grid, rs, kh, axis=1)
        vb = lax.dynamic_slice_in_dim(v_grid, rs, kh, axis=1)
        kw = kb[:, :, col_idx]
        vw = vb[:, :, col_idx]
        row_bias_idx = rs + jnp.arange(kh) - r + NA_KH - 1
        bias = rpb[:, row_bias_idx[None, :, None], col_bias_idx[:, None, :]]
        s_loc = jnp.einsum('bchd,brcwhd->bhcrw', qr, kw, preferred_element_type=jnp.float32) * SCALE
        s_loc = (s_loc + bias.astype(jnp.float32)).reshape(b, C_HEADS, GRID_W, n_loc)
        s_ctx = jnp.einsum('bchd,blhd->bhcl', qr, k_ctx, preferred_element_type=jnp.float32) * SCALE
        p = jax.nn.softmax(jnp.concatenate([s_loc, s_ctx], axis=-1), axis=-1)
        p_loc = p[..., :n_loc].reshape(b, C_HEADS, GRID_W, kh, NA_KW).astype(vw.dtype)
        p_ctx = p[..., n_loc:].astype(v_ctx.dtype)
        return (jnp.einsum('bhcrw,brcwhd->bchd', p_loc, vw)
                + jnp.einsum('bhcl,blhd->bchd', p_ctx, v_ctx))

    o = lax.map(row_block, (q_rows, jnp.arange(rows)))
    o = o.transpose(1, 0, 2, 3, 4).reshape(b, n, C_HEADS * HEAD_DIM)
    return o * jax.nn.silu(g)


def setup_inputs(seed: int = 0) -> dict:
    key = jax.random.key(seed)
    ks = jax.random.split(key, 23)
    d = D_MODEL

    def nrm(i, shape, std):
        return std * jax.random.normal(ks[i], shape, jnp.float32)

    w_std = d ** -0.5
    return {
        'x_prompt': nrm(0, (BATCH, SEQ, d), 1.0),
        'x_sample': nrm(1, (DEC_BATCH, DEC_SEQ, d), 1.0),
        'cache_a_k': nrm(2, (DEC_BATCH, N_A, PAST_LEN, A_KV_HEADS, HEAD_DIM), 1.0),
        'cache_a_v': nrm(3, (DEC_BATCH, N_A, PAST_LEN, A_KV_HEADS, HEAD_DIM), 1.0),
        'cache_b_k': nrm(4, (DEC_BATCH, N_B, PAST_LEN, B_HEADS, 2, HEAD_DIM), 1.0),
        'cache_b_v': nrm(5, (DEC_BATCH, N_B, PAST_LEN, B_HEADS, 2 * HEAD_DIM), 1.0),
        'cache_c_k': nrm(6, (DEC_BATCH, N_C, PAST_LEN, C_HEADS, HEAD_DIM), 1.0),
        'cache_c_v': nrm(7, (DEC_BATCH, N_C, PAST_LEN, C_HEADS, HEAD_DIM), 1.0),
        'c': nrm(8, (DEC_BATCH, d), 1.0),
        'c_ctx': nrm(9, (d,), 1.0),
        'ln_g': 1.0 + nrm(10, (DEPTH, d), 0.02),
        'ada_w': nrm(11, (DEPTH, d, 3 * d), 0.5 * w_std),
        'ada_b': nrm(12, (DEPTH, 3 * d), 0.02),
        'w_out': nrm(13, (DEPTH, d, d), w_std),
        'qn_g': 1.0 + nrm(14, (DEPTH, HEAD_DIM), 0.02),
        'kn_g': 1.0 + nrm(15, (DEPTH, HEAD_DIM), 0.02),
        'w_in_a': nrm(16, (N_A, d, A_IN), w_std),
        'sink_a': nrm(17, (N_A, A_HEADS), 0.5),
        'w_in_b': nrm(18, (N_B, d, B_IN), w_std),
        'lam_b': nrm(19, (N_B, 4, HEAD_DIM), 0.1),
        'subln_b': 1.0 + nrm(20, (N_B, 2 * HEAD_DIM), 0.02),
        'w_in_c': nrm(21, (N_C, d, C_IN), w_std),
        'rpb_c': nrm(22, (N_C, C_HEADS, 2 * NA_KH - 1, 2 * NA_KW - 1), 0.5),
    }


def reference(x_prompt, x_sample, cache_a_k, cache_a_v, cache_b_k, cache_b_v, cache_c_k, cache_c_v,
              c, c_ctx, ln_g, ada_w, ada_b, w_out, qn_g, kn_g, w_in_a, sink_a, w_in_b, lam_b,
              subln_b, w_in_c, rpb_c):
    cos, sin = axial_rope_tables(x_sample.shape[1])
    xp = x_prompt
    xs = x_sample
    a_k, a_v, b_k, b_v, c_k, c_v = [], [], [], [], [], []
    for l in range(DEPTH):
        kind = l % N_MIXERS
        j = l // N_MIXERS
        sh_p, sc_p, gt_p = ada_modulation(c_ctx[None, :], ada_w[l], ada_b[l])
        sh_s, sc_s, gt_s = ada_modulation(c, ada_w[l], ada_b[l])
        hp = rms_norm(xp, ln_g[l]) * (1.0 + sc_p) + sh_p
        hs = rms_norm(xs, ln_g[l]) * (1.0 + sc_s) + sh_s
        if kind == 0:
            op, kp, vp = window_attn_ctx(hp, w_in_a[j], qn_g[l], kn_g[l], sink_a[j])
            os_ = window_attn_lat(hs, cache_a_k[:, j], cache_a_v[:, j], w_in_a[j], qn_g[l], kn_g[l],
                                  sink_a[j], cos, sin)
            a_k.append(kp)
            a_v.append(vp)
        elif kind == 1:
            lambda_init = 0.8 - 0.6 * math.exp(-0.3 * l)
            op, kp, vp = diff_attn_ctx(hp, w_in_b[j], qn_g[l], kn_g[l], lam_b[j], subln_b[j], lambda_init)
            os_ = diff_attn_lat(hs, cache_b_k[:, j], cache_b_v[:, j], w_in_b[j], qn_g[l], kn_g[l],
                                lam_b[j], subln_b[j], lambda_init, cos, sin)
            b_k.append(kp)
            b_v.append(vp)
        else:
            op, kp, vp = na_ctx(hp, w_in_c[j], qn_g[l], kn_g[l])
            os_ = na_lat(hs, cache_c_k[:, j], cache_c_v[:, j], w_in_c[j], qn_g[l], kn_g[l], rpb_c[j])
            c_k.append(kp)
            c_v.append(vp)
        xp = xp + gt_p * (op @ w_out[l])
        xs = xs + gt_s * (os_ @ w_out[l])
    return (xp, xs, jnp.stack(a_k, axis=1), jnp.stack(a_v, axis=1), jnp.stack(b_k, axis=1),
            jnp.stack(b_v, axis=1), jnp.stack(c_k, axis=1), jnp.stack(c_v, axis=1))
```

```python
import functools
import math

import numpy as np
import jax
import jax.numpy as jnp
from jax import lax
from jax.experimental import pallas as pl
from jax.experimental.pallas import tpu as pltpu

D_MODEL = 2048
DEPTH = 4
GRID_W = 64
HEAD_DIM = 128
ROPE_BASE = 10000.0
NEG_INF = -1e30
N_MIXERS = 3
A_HEADS = D_MODEL // HEAD_DIM
A_KV_HEADS = A_HEADS // 4
A_GROUP = A_HEADS // A_KV_HEADS
WINDOW = 128
B_HEADS = D_MODEL // (2 * HEAD_DIM)
C_HEADS = D_MODEL // HEAD_DIM
NA_KH = 8
NA_KW = 16
SCALE = HEAD_DIM ** -0.5
EPS = 1e-6

BF16 = jnp.bfloat16
F32 = jnp.float32

VMEM_LIMIT_BYTES = 56 * 1024 * 1024
ROW_TILE = 1024
NORM_CHUNK = 128
A_QBLOCK = 128
B_QBLOCK = 256
B_KCHUNK = 512
C_ROWS = 4
C_HEAD_GROUP = 4


def _nt_dot(a, b):
    return lax.dot_general(a, b, (((1,), (1,)), ((), ())), preferred_element_type=F32)


def _params(*semantics):
    return pltpu.CompilerParams(dimension_semantics=semantics, vmem_limit_bytes=VMEM_LIMIT_BYTES)


def _ada_kernel(c_ref, w_ref, b_ref, o_ref):
    c = c_ref[...]
    a = (c * jax.nn.sigmoid(c)).astype(BF16)
    o_ref[...] = jnp.dot(a, w_ref[...].astype(BF16), preferred_element_type=F32) + b_ref[...]


def _ada_modulation(cvec, ada_w, ada_b):
    rows = cvec.shape[0]
    tn = 768
    return pl.pallas_call(
        _ada_kernel,
        out_shape=jax.ShapeDtypeStruct((DEPTH, rows, 3 * D_MODEL), F32),
        grid=(DEPTH, 3 * D_MODEL // tn),
        in_specs=[
            pl.BlockSpec((rows, D_MODEL), lambda l, j: (0, 0)),
            pl.BlockSpec((None, D_MODEL, tn), lambda l, j: (l, 0, j)),
            pl.BlockSpec((None, 1, tn), lambda l, j: (l, 0, j)),
        ],
        out_specs=pl.BlockSpec((None, rows, tn), lambda l, j: (l, 0, j)),
        compiler_params=_params("arbitrary", "arbitrary"),
        name="ada_modulation",
    )(cvec, ada_w, ada_b.reshape(DEPTH, 1, 3 * D_MODEL))


def _inproj_kernel(*refs, n_q, n_k, n_v, rope, heads_per_tile):
    x_ref, sh_ref, sc_ref, lng_ref, w_ref, qg_ref, kg_ref = refs[:7]
    if rope:
        cos_ref, sa_ref, sb_ref = refs[7:10]
    q_out, k_out, v_out, g_out, h_scr = refs[-5:]
    j = pl.program_id(1)

    @pl.when(j == 0)
    def _():
        def body(r, carry):
            rows = pl.ds(pl.multiple_of(r * NORM_CHUNK, NORM_CHUNK), NORM_CHUNK)
            x = x_ref[rows, :]
            ms = jnp.mean(x * x, axis=-1, keepdims=True)
            y = x * lax.rsqrt(ms + EPS) * lng_ref[...]
            h_scr[rows, :] = (y * (1.0 + sc_ref[...]) + sh_ref[...]).astype(BF16)
            return carry
        lax.fori_loop(0, x_ref.shape[0] // NORM_CHUNK, body, 0)

    acc = jnp.dot(h_scr[...], w_ref[...], preferred_element_type=F32)

    def normed(gain_ref, scale):
        outs = []
        for t in range(heads_per_tile):
            y = acc[:, t * HEAD_DIM:(t + 1) * HEAD_DIM]
            ms = jnp.mean(y * y, axis=-1, keepdims=True)
            y = y * lax.rsqrt(ms + EPS) * gain_ref[...]
            if rope:
                y = (y * cos_ref[...] + pltpu.roll(y, 96, 1) * sa_ref[...]
                     + pltpu.roll(y, 32, 1) * sb_ref[...])
            if scale != 1.0:
                y = y * scale
            outs.append(y)
        return jnp.concatenate(outs, axis=-1)

    @pl.when(j < n_q)
    def _():
        q_out[...] = normed(qg_ref, SCALE).astype(q_out.dtype)

    @pl.when((j >= n_q) & (j < n_q + n_k))
    def _():
        k_out[...] = normed(kg_ref, 1.0).astype(k_out.dtype)

    @pl.when((j >= n_q + n_k) & (j < n_q + n_k + n_v))
    def _():
        v_out[...] = acc.astype(v_out.dtype)

    @pl.when(j >= n_q + n_k + n_v)
    def _():
        g_out[...] = (acc * jax.nn.sigmoid(acc)).astype(g_out.dtype)


def _in_projection(x, shift, scale, ln_g, w, qg, kg, widths, rope_tabs, kv_dtype, tn):
    t_rows = x.shape[0]
    qw, kw, vw, gw = widths
    n_q, n_k, n_v, n_g = qw // tn, kw // tn, vw // tn, gw // tn
    tm = min(ROW_TILE, t_rows)
    rows_per_mod = t_rows // shift.shape[0]
    assert t_rows % tm == 0 and rows_per_mod % tm == 0
    rope = rope_tabs is not None

    def mod_map(i, j):
        return ((i * tm) // rows_per_mod, 0, 0)

    in_specs = [
        pl.BlockSpec((tm, D_MODEL), lambda i, j: (i, 0)),
        pl.BlockSpec((None, 1, D_MODEL), mod_map),
        pl.BlockSpec((None, 1, D_MODEL), mod_map),
        pl.BlockSpec((1, D_MODEL), lambda i, j: (0, 0)),
        pl.BlockSpec((D_MODEL, tn), lambda i, j: (0, j)),
        pl.BlockSpec((1, HEAD_DIM), lambda i, j: (0, 0)),
        pl.BlockSpec((1, HEAD_DIM), lambda i, j: (0, 0)),
    ]
    args = [x, shift, scale, ln_g.reshape(1, D_MODEL), w, qg.reshape(1, HEAD_DIM), kg.reshape(1, HEAD_DIM)]
    if rope:
        n_pos_tiles = rope_tabs[0].shape[0] // tm
        for tab in rope_tabs:
            in_specs.append(pl.BlockSpec((tm, HEAD_DIM), lambda i, j: (i % n_pos_tiles, 0)))
            args.append(tab)
    out_specs = [
        pl.BlockSpec((tm, tn), lambda i, j: (i, jnp.minimum(j, n_q - 1))),
        pl.BlockSpec((tm, tn), lambda i, j: (i, jnp.clip(j - n_q, 0, n_k - 1))),
        pl.BlockSpec((tm, tn), lambda i, j: (i, jnp.clip(j - n_q - n_k, 0, n_v - 1))),
        pl.BlockSpec((tm, tn), lambda i, j: (i, jnp.clip(j - n_q - n_k - n_v, 0, n_g - 1))),
    ]
    out_shape = [
        jax.ShapeDtypeStruct((t_rows, qw), BF16),
        jax.ShapeDtypeStruct((t_rows, kw), kv_dtype),
        jax.ShapeDtypeStruct((t_rows, vw), kv_dtype),
        jax.ShapeDtypeStruct((t_rows, gw), BF16),
    ]
    kern = functools.partial(_inproj_kernel, n_q=n_q, n_k=n_k, n_v=n_v, rope=rope,
                             heads_per_tile=tn // HEAD_DIM)
    return pl.pallas_call(
        kern,
        out_shape=out_shape,
        grid=(t_rows // tm, n_q + n_k + n_v + n_g),
        in_specs=in_specs,
        out_specs=out_specs,
        scratch_shapes=[pltpu.VMEM((tm, D_MODEL), BF16)],
        compiler_params=_params("arbitrary", "arbitrary"),
        name="in_projection",
    )(*args)


def _outproj_kernel(o_ref, w_ref, x_ref, gt_ref, y_ref):
    y = jnp.dot(o_ref[...], w_ref[...], preferred_element_type=F32)
    y_ref[...] = x_ref[...] + gt_ref[...] * y


def _out_projection(o, w, x, gate):
    t_rows = x.shape[0]
    tm, tn = min(ROW_TILE, t_rows), 1024
    rows_per_mod = t_rows // gate.shape[0]
    assert t_rows % tm == 0 and rows_per_mod % tm == 0
    return pl.pallas_call(
        _outproj_kernel,
        out_shape=jax.ShapeDtypeStruct((t_rows, D_MODEL), F32),
        grid=(t_rows // tm, D_MODEL // tn),
        in_specs=[
            pl.BlockSpec((tm, D_MODEL), lambda i, j: (i, 0)),
            pl.BlockSpec((D_MODEL, tn), lambda i, j: (0, j)),
            pl.BlockSpec((tm, tn), lambda i, j: (i, j)),
            pl.BlockSpec((None, 1, tn), lambda i, j: ((i * tm) // rows_per_mod, 0, j)),
        ],
        out_specs=pl.BlockSpec((tm, tn), lambda i, j: (i, j)),
        compiler_params=_params("arbitrary", "arbitrary"),
        name="out_projection",
    )(o, w, x, gate)


def _softmax_pv(scores, values, sink=None):
    m = jnp.max(scores[0], axis=-1, keepdims=True)
    for s in scores[1:]:
        m = jnp.maximum(m, jnp.max(s, axis=-1, keepdims=True))
    if sink is not None:
        m = jnp.maximum(m, sink)
    l = jnp.exp(sink - m) if sink is not None else None
    o = None
    for s, v in zip(scores, values):
        p = jnp.exp(s - m)
        ps = jnp.sum(p, axis=-1, keepdims=True)
        l = ps if l is None else l + ps
        pv = jnp.dot(p.astype(BF16), v, preferred_element_type=F32)
        o = pv if o is None else o + pv
    return o * (1.0 / l)


def _attn_a_kernel(*refs, latent, n_qblocks):
    if latent:
        (sink_ref, q_ref, kp_ref, ko_ref, kn_ref, vp_ref, vo_ref, vn_ref,
         kc_ref, vc_ref, sg_ref, o_ref) = refs
    else:
        sink_ref, q_ref, k_ref, v_ref, sg_ref, o_ref = refs
    m_rows = q_ref.shape[0]
    stacked = A_GROUP * m_rows

    if latent:
        blk = pl.program_id(1)
        span = 3 * A_QBLOCK
        qpos = lax.broadcasted_iota(jnp.int32, (stacked, span), 0) & (A_QBLOCK - 1)
        col = lax.broadcasted_iota(jnp.int32, (stacked, span), 1)
        rel = col - A_QBLOCK - qpos
        kpos = (blk - 1) * A_QBLOCK + col
        valid = (jnp.abs(rel) <= WINDOW) & (kpos >= 0) & (kpos < n_qblocks * A_QBLOCK)

    for kv in range(A_KV_HEADS):
        cols = slice(kv * HEAD_DIM, (kv + 1) * HEAD_DIM)
        heads = [kv * A_GROUP + g for g in range(A_GROUP)]
        q = jnp.concatenate([q_ref[:, h * HEAD_DIM:(h + 1) * HEAD_DIM] for h in heads], axis=0)
        sink = jnp.concatenate([jnp.full((m_rows, 1), sink_ref[h], F32) for h in heads], axis=0)
        if latent:
            k_loc = jnp.concatenate([kp_ref[:, cols], ko_ref[:, cols], kn_ref[:, cols]], axis=0)
            v_loc = jnp.concatenate([vp_ref[:, cols], vo_ref[:, cols], vn_ref[:, cols]], axis=0)
            s_loc = jnp.where(valid, _nt_dot(q, k_loc), NEG_INF)
            s_ctx = _nt_dot(q, kc_ref[:, cols].astype(BF16))
            o = _softmax_pv([s_loc, s_ctx], [v_loc, vc_ref[:, cols].astype(BF16)], sink)
        else:
            s = _nt_dot(q, k_ref[:, cols].astype(BF16))
            o = _softmax_pv([s], [v_ref[:, cols].astype(BF16)], sink)
        for g, h in enumerate(heads):
            hc = slice(h * HEAD_DIM, (h + 1) * HEAD_DIM)
            o_ref[:, hc] = (o[g * m_rows:(g + 1) * m_rows] * sg_ref[:, hc].astype(F32)).astype(o_ref.dtype)


def _attn_a_ctx(q, k, v, sg, sink, n_batch, seq):
    kvw = A_KV_HEADS * HEAD_DIM
    row = lambda b: (b, 0)
    return pl.pallas_call(
        functools.partial(_attn_a_kernel, latent=False, n_qblocks=1),
        out_shape=jax.ShapeDtypeStruct(q.shape, BF16),
        grid=(n_batch,),
        in_specs=[
            pl.BlockSpec(memory_space=pltpu.SMEM),
            pl.BlockSpec((seq, D_MODEL), row),
            pl.BlockSpec((seq, kvw), row),
            pl.BlockSpec((seq, kvw), row),
            pl.BlockSpec((seq, D_MODEL), row),
        ],
        out_specs=pl.BlockSpec((seq, D_MODEL), row),
        compiler_params=_params("arbitrary"),
        name="attn_a_ctx",
    )(sink, q, k, v, sg)


def _attn_a_lat(q, k, v, k_ctx, v_ctx, sg, sink, n_batch, seq):
    kvw = A_KV_HEADS * HEAD_DIM
    nb = seq // A_QBLOCK
    own = lambda b, i: (b * nb + i, 0)
    prev = lambda b, i: (b * nb + jnp.maximum(i - 1, 0), 0)
    nxt = lambda b, i: (b * nb + jnp.minimum(i + 1, nb - 1), 0)
    ctx = lambda b, i: (b, 0, 0)
    n_ctx = k_ctx.shape[1]
    return pl.pallas_call(
        functools.partial(_attn_a_kernel, latent=True, n_qblocks=nb),
        out_shape=jax.ShapeDtypeStruct(q.shape, BF16),
        grid=(n_batch, nb),
        in_specs=[
            pl.BlockSpec(memory_space=pltpu.SMEM),
            pl.BlockSpec((A_QBLOCK, D_MODEL), own),
            pl.BlockSpec((A_QBLOCK, kvw), prev),
            pl.BlockSpec((A_QBLOCK, kvw), own),
            pl.BlockSpec((A_QBLOCK, kvw), nxt),
            pl.BlockSpec((A_QBLOCK, kvw), prev),
            pl.BlockSpec((A_QBLOCK, kvw), own),
            pl.BlockSpec((A_QBLOCK, kvw), nxt),
            pl.BlockSpec((None, n_ctx, kvw), ctx),
            pl.BlockSpec((None, n_ctx, kvw), ctx),
            pl.BlockSpec((A_QBLOCK, D_MODEL), own),
        ],
        out_specs=pl.BlockSpec((A_QBLOCK, D_MODEL), own),
        compiler_params=_params("arbitrary", "arbitrary"),
        name="attn_a_lat",
    )(sink, q, k, k, k, v, v, v, k_ctx, v_ctx, sg)


def _attn_b_kernel(*refs, has_ctx, lambda_init, k_chunk):
    if has_ctx:
        lam_ref, sub_ref, q_ref, k_ref, v_ref, kc_ref, vc_ref, sg_ref, o_ref, m_scr, l_scr, acc_scr = refs
    else:
        lam_ref, sub_ref, q_ref, k_ref, v_ref, sg_ref, o_ref, m_scr, l_scr, acc_scr = refs
    tq = q_ref.shape[0]
    dv = 2 * HEAD_DIM
    m_scr[...] = jnp.full(m_scr.shape, NEG_INF, F32)
    l_scr[...] = jnp.zeros(l_scr.shape, F32)
    acc_scr[...] = jnp.zeros(acc_scr.shape, F32)

    def update(k, v):
        k = k.astype(BF16)
        v = v.astype(BF16)
        for half in range(2):
            hc = slice(half * HEAD_DIM, (half + 1) * HEAD_DIM)
            s = _nt_dot(q_ref[:, hc], k[:, hc])
            m_old = m_scr[half]
            m_new = jnp.maximum(m_old, jnp.max(s, axis=-1, keepdims=True))
            alpha = jnp.exp(m_old - m_new)
            p = jnp.exp(s - m_new)
            l_scr[half] = alpha * l_scr[half] + jnp.sum(p, axis=-1, keepdims=True)
            acc_scr[half] = alpha * acc_scr[half] + jnp.dot(p.astype(BF16), v, preferred_element_type=F32)
            m_scr[half] = m_new

    n_chunks = k_ref.shape[0] // k_chunk
    if n_chunks == 1:
        update(k_ref[...], v_ref[...])
    else:
        def body(c, carry):
            rows = pl.ds(pl.multiple_of(c * k_chunk, k_chunk), k_chunk)
            update(k_ref[rows, :], v_ref[rows, :])
            return carry
        lax.fori_loop(0, n_chunks, body, 0)
    if has_ctx:
        update(kc_ref[...], vc_ref[...])

    lam = lam_ref[...]
    lam_full = (jnp.exp(jnp.sum(lam[0:1] * lam[1:2], axis=-1, keepdims=True))
                - jnp.exp(jnp.sum(lam[2:3] * lam[3:4], axis=-1, keepdims=True)) + lambda_init)
    o = acc_scr[0] * (1.0 / l_scr[0]) - lam_full * (acc_scr[1] * (1.0 / l_scr[1]))
    ms = jnp.mean(o * o, axis=-1, keepdims=True)
    o = o * lax.rsqrt(ms + EPS) * sub_ref[...] * (1.0 - lambda_init)
    o_ref[...] = (o * sg_ref[...].astype(F32)).astype(o_ref.dtype)


def _attn_b(q, k, v, sg, lam, subln, lambda_init, n_batch, seq, k_ctx=None, v_ctx=None):
    dv = 2 * HEAD_DIM
    tq = min(B_QBLOCK, seq)
    nq = seq // tq
    has_ctx = k_ctx is not None
    qmap = lambda b, h, i: (b * nq + i, h)
    kvmap = lambda b, h, i: (b, h)
    in_specs = [
        pl.BlockSpec((4, HEAD_DIM), lambda b, h, i: (0, 0)),
        pl.BlockSpec((1, dv), lambda b, h, i: (0, 0)),
        pl.BlockSpec((tq, dv), qmap),
        pl.BlockSpec((seq, dv), kvmap),
        pl.BlockSpec((seq, dv), kvmap),
    ]
    args = [lam, subln.reshape(1, dv), q, k, v]
    if has_ctx:
        n_ctx = k_ctx.shape[1]
        in_specs += [pl.BlockSpec((None, n_ctx, dv), lambda b, h, i: (b, 0, h))] * 2
        args += [k_ctx, v_ctx]
    in_specs.append(pl.BlockSpec((tq, dv), qmap))
    args.append(sg)
    return pl.pallas_call(
        functools.partial(_attn_b_kernel, has_ctx=has_ctx, lambda_init=lambda_init,
                          k_chunk=min(B_KCHUNK, seq)),
        out_shape=jax.ShapeDtypeStruct(q.shape, BF16),
        grid=(n_batch, B_HEADS, nq),
        in_specs=in_specs,
        out_specs=pl.BlockSpec((tq, dv), qmap),
        scratch_shapes=[pltpu.VMEM((2, tq, 1), F32), pltpu.VMEM((2, tq, 1), F32),
                        pltpu.VMEM((2, tq, dv), F32)],
        compiler_params=_params("arbitrary", "arbitrary", "arbitrary"),
        name="attn_b_lat" if has_ctx else "attn_b_ctx",
    )(*args)


def _attn_c_ctx_kernel(q_ref, k_ref, v_ref, sg_ref, o_ref):
    for h in range(C_HEADS):
        hc = slice(h * HEAD_DIM, (h + 1) * HEAD_DIM)
        s = _nt_dot(q_ref[:, hc], k_ref[:, hc].astype(BF16))
        o = _softmax_pv([s], [v_ref[:, hc].astype(BF16)])
        o_ref[:, hc] = (o * sg_ref[:, hc].astype(F32)).astype(o_ref.dtype)


def _attn_c_ctx(q, k, v, sg, n_batch, seq):
    spec = pl.BlockSpec((seq, D_MODEL), lambda b: (b, 0))
    return pl.pallas_call(
        _attn_c_ctx_kernel,
        out_shape=jax.ShapeDtypeStruct(q.shape, BF16),
        grid=(n_batch,),
        in_specs=[spec, spec, spec, spec],
        out_specs=spec,
        compiler_params=_params("arbitrary"),
        name="attn_c_ctx",
    )(q, k, v, sg)


def _na_geometry(rows):
    kh = min(NA_KH, rows)
    key_rows = min(C_ROWS + kh - 1, rows)
    n_blocks = rows // C_ROWS
    cols = np.arange(GRID_W)
    col_start = np.clip(cols - NA_KW // 2, 0, GRID_W - NA_KW)
    starts, class_ids, classes, keys = [], [], [], {}
    for blk in range(n_blocks):
        r = blk * C_ROWS + np.arange(C_ROWS)
        rs = np.clip(r - kh // 2, 0, rows - kh)
        start = int(np.clip(rs[0], 0, rows - key_rows))
        key = (tuple(rs - r), start - blk * C_ROWS)
        if key not in keys:
            keys[key] = len(classes)
            kr = start + np.arange(key_rows)
            row_ok = (kr[None, :] >= rs[:, None]) & (kr[None, :] < rs[:, None] + kh)
            col_ok = (cols[None, :] >= col_start[:, None]) & (cols[None, :] < col_start[:, None] + NA_KW)
            row_idx = np.clip(kr[None, :] - r[:, None] + NA_KH - 1, 0, 2 * NA_KH - 2)
            col_idx = np.clip(cols[None, :] - cols[:, None] + NA_KW - 1, 0, 2 * NA_KW - 2)
            ok = row_ok[:, None, :, None] & col_ok[None, :, None, :]
            ri = np.broadcast_to(row_idx[:, None, :, None], ok.shape)
            ci = np.broadcast_to(col_idx[None, :, None, :], ok.shape)
            nq, nk = C_ROWS * GRID_W, key_rows * GRID_W
            classes.append((ok.reshape(nq, nk), ri.reshape(nq, nk), ci.reshape(nq, nk)))
        starts.append(start)
        class_ids.append(keys[key])
    return key_rows, np.asarray(starts, np.int32), np.asarray(class_ids, np.int32), classes


def _na_bias_tables(rpb, classes):
    tabs = []
    for ok, ri, ci in classes:
        tabs.append(jnp.where(jnp.asarray(ok)[None], rpb[:, ri, ci], NEG_INF))
    return jnp.stack(tabs, axis=0).astype(F32)


def _attn_c_lat_kernel(start_ref, cls_ref, q_ref, k_ref, v_ref, kc_ref, vc_ref, bias_ref, sg_ref, o_ref,
                       *, key_rows):
    blk = pl.program_id(2)
    n_keys = key_rows * GRID_W
    rows = pl.ds(pl.multiple_of(start_ref[blk] * GRID_W, GRID_W), n_keys)
    for h in range(C_HEAD_GROUP):
        hc = slice(h * HEAD_DIM, (h + 1) * HEAD_DIM)
        q = q_ref[:, hc]
        s_ctx = _nt_dot(q, kc_ref[:, hc].astype(BF16))
        s_loc = _nt_dot(q, k_ref[rows, hc]) + bias_ref[h]
        o = _softmax_pv([s_ctx, s_loc], [vc_ref[:, hc].astype(BF16), v_ref[rows, hc]])
        o_ref[:, hc] = (o * sg_ref[:, hc].astype(F32)).astype(o_ref.dtype)


def _attn_c_lat(q, k, v, k_ctx, v_ctx, sg, rpb, n_batch, seq):
    rows = seq // GRID_W
    key_rows, starts, class_ids, classes = _na_geometry(rows)
    bias = _na_bias_tables(rpb, classes)
    nblk = rows // C_ROWS
    tq = C_ROWS * GRID_W
    gw = C_HEAD_GROUP * HEAD_DIM
    n_ctx = k_ctx.shape[1]
    qmap = lambda b, g, i, st, cl: (b * nblk + i, g)
    kvmap = lambda b, g, i, st, cl: (b, g)
    ctxmap = lambda b, g, i, st, cl: (b, 0, g)
    grid_spec = pltpu.PrefetchScalarGridSpec(
        num_scalar_prefetch=2,
        grid=(n_batch, C_HEADS // C_HEAD_GROUP, nblk),
        in_specs=[
            pl.BlockSpec((tq, gw), qmap),
            pl.BlockSpec((seq, gw), kvmap),
            pl.BlockSpec((seq, gw), kvmap),
            pl.BlockSpec((None, n_ctx, gw), ctxmap),
            pl.BlockSpec((None, n_ctx, gw), ctxmap),
            pl.BlockSpec((None, C_HEAD_GROUP, tq, key_rows * GRID_W),
                         lambda b, g, i, st, cl: (cl[i], g, 0, 0)),
            pl.BlockSpec((tq, gw), qmap),
        ],
        out_specs=pl.BlockSpec((tq, gw), qmap),
    )
    return pl.pallas_call(
        functools.partial(_attn_c_lat_kernel, key_rows=key_rows),
        out_shape=jax.ShapeDtypeStruct(q.shape, BF16),
        grid_spec=grid_spec,
        compiler_params=_params("arbitrary", "arbitrary", "arbitrary"),
        name="attn_c_lat",
    )(jnp.asarray(starts), jnp.asarray(class_ids), q, k, v, k_ctx, v_ctx, bias, sg)


def _rope_tables(n_tokens):
    d4 = HEAD_DIM // 4
    t = jnp.arange(n_tokens)
    pos = jnp.stack([t // GRID_W, t % GRID_W], axis=-1).astype(F32)
    inv_freq = ROPE_BASE ** (-jnp.arange(d4, dtype=F32) / d4)
    ang = pos[:, :, None] * inv_freq
    cos, sin = jnp.cos(ang), jnp.sin(ang)
    zero = jnp.zeros_like(sin[:, 0])
    cos_full = jnp.concatenate([cos[:, 0], cos[:, 0], cos[:, 1], cos[:, 1]], axis=-1)
    sin_up = jnp.concatenate([-sin[:, 0], zero, -sin[:, 1], zero], axis=-1)
    sin_dn = jnp.concatenate([zero, sin[:, 0], zero, sin[:, 1]], axis=-1)
    return cos_full, sin_up, sin_dn


def kernel(x_prompt, x_sample, cache_a_k, cache_a_v, cache_b_k, cache_b_v, cache_c_k, cache_c_v, c, c_ctx,
           ln_g, ada_w, ada_b, w_out, qn_g, kn_g, w_in_a, sink_a, w_in_b, lam_b, subln_b, w_in_c, rpb_c):
    n_p, seq_p, d = x_prompt.shape
    n_s, seq_s, _ = x_sample.shape
    n_ctx = cache_a_k.shape[2]
    xp = x_prompt.reshape(n_p * seq_p, d)
    xs = x_sample.reshape(n_s * seq_s, d)
    rope_tabs = _rope_tables(seq_s)

    pad = (-(n_s + 1)) % 8
    cvec = jnp.concatenate([c, c_ctx[None, :], jnp.zeros((pad, d), F32)], axis=0)
    mod = _ada_modulation(cvec, ada_w, ada_b)
    mod = mod.reshape(DEPTH, mod.shape[1], 3, 1, d)

    new_kv = {0: ([], []), 1: ([], []), 2: ([], [])}
    for l in range(DEPTH):
        kind, j = l % N_MIXERS, l // N_MIXERS
        sh_s, sc_s, gt_s = (mod[l, :n_s, t] for t in range(3))
        sh_p, sc_p, gt_p = (mod[l, n_s:n_s + 1, t] for t in range(3))
        w_o = w_out[l].astype(BF16)
        if kind == 0:
            kvw = A_KV_HEADS * HEAD_DIM
            w_in, widths, tn = w_in_a[j].astype(BF16), (d, kvw, kvw, d), 512
        elif kind == 1:
            w_in, widths, tn = w_in_b[j].astype(BF16), (d, d, d, d), 512
        else:
            w_in, widths, tn = w_in_c[j].astype(BF16), (d, d, d, d), 512
        rope = rope_tabs if kind != 2 else None
        qp, kp, vp, gp = _in_projection(xp, sh_p, sc_p, ln_g[l], w_in, qn_g[l], kn_g[l], widths, None, F32, tn)
        qs, ks, vs, gs = _in_projection(xs, sh_s, sc_s, ln_g[l], w_in, qn_g[l], kn_g[l], widths, rope, BF16, tn)
        if kind == 0:
            kc = cache_a_k[:, j].reshape(n_s, n_ctx, kvw)
            vc = cache_a_v[:, j].reshape(n_s, n_ctx, kvw)
            op = _attn_a_ctx(qp, kp, vp, gp, sink_a[j], n_p, seq_p)
            os_ = _attn_a_lat(qs, ks, vs, kc, vc, gs, sink_a[j], n_s, seq_s)
            new_kv[0][0].append(kp.reshape(n_p, seq_p, A_KV_HEADS, HEAD_DIM))
            new_kv[0][1].append(vp.reshape(n_p, seq_p, A_KV_HEADS, HEAD_DIM))
        elif kind == 1:
            lambda_init = 0.8 - 0.6 * math.exp(-0.3 * l)
            kc = cache_b_k[:, j].reshape(n_s, n_ctx, d)
            vc = cache_b_v[:, j].reshape(n_s, n_ctx, d)
            op = _attn_b(qp, kp, vp, gp, lam_b[j], subln_b[j], lambda_init, n_p, seq_p)
            os_ = _attn_b(qs, ks, vs, gs, lam_b[j], subln_b[j], lambda_init, n_s, seq_s, kc, vc)
            new_kv[1][0].append(kp.reshape(n_p, seq_p, B_HEADS, 2, HEAD_DIM))
            new_kv[1][1].append(vp.reshape(n_p, seq_p, B_HEADS, 2 * HEAD_DIM))
        else:
            kc = cache_c_k[:, j].reshape(n_s, n_ctx, d)
            vc = cache_c_v[:, j].reshape(n_s, n_ctx, d)
            op = _attn_c_ctx(qp, kp, vp, gp, n_p, seq_p)
            os_ = _attn_c_lat(qs, ks, vs, kc, vc, gs, rpb_c[j], n_s, seq_s)
            new_kv[2][0].append(kp.reshape(n_p, seq_p, C_HEADS, HEAD_DIM))
            new_kv[2][1].append(vp.reshape(n_p, seq_p, C_HEADS, HEAD_DIM))
        xp = _out_projection(op, w_o, xp, gt_p)
        xs = _out_projection(os_, w_o, xs, gt_s)

    outs = [xp.reshape(x_prompt.shape), xs.reshape(x_sample.shape)]
    for kind in range(3):
        outs.append(jnp.stack(new_kv[kind][0], axis=1))
        outs.append(jnp.stack(new_kv[kind][1], axis=1))
    return tuple(outs)
```

```python
import functools
import math

import numpy as np
import jax
import jax.numpy as jnp
from jax import lax
from jax.experimental import pallas as pl
from jax.experimental.pallas import tpu as pltpu

D_MODEL = 2048
DEPTH = 4
GRID_W = 64
HEAD_DIM = 128
ROPE_BASE = 10000.0
NEG_INF = -1e30
N_MIXERS = 3
A_HEADS = D_MODEL // HEAD_DIM
A_KV_HEADS = A_HEADS // 4
A_GROUP = A_HEADS // A_KV_HEADS
WINDOW = 128
B_HEADS = D_MODEL // (2 * HEAD_DIM)
C_HEADS = D_MODEL // HEAD_DIM
NA_KH = 8
NA_KW = 16
SCALE = HEAD_DIM ** -0.5
LOG2E = math.log2(math.e)
Q_SCALE = SCALE * LOG2E
EPS = 1e-6

BF16 = jnp.bfloat16
F32 = jnp.float32

VMEM_LIMIT_BYTES = 56 * 1024 * 1024
ROW_TILE = 1024
NORM_CHUNK = 128
PROJ_SUB = 256
A_QBLOCK = 128
B_QBLOCK = 512
B_KCHUNK = 512
C_ROWS = 4
C_HEAD_GROUP = 4


def _nt_dot(a, b):
    return lax.dot_general(a, b, (((1,), (1,)), ((), ())), preferred_element_type=F32)


def _params(*semantics):
    return pltpu.CompilerParams(dimension_semantics=semantics, vmem_limit_bytes=VMEM_LIMIT_BYTES)


def _ada_kernel(c_ref, w_ref, b_ref, o_ref):
    c = c_ref[...]
    a = (c * jax.nn.sigmoid(c)).astype(BF16)
    o_ref[...] = jnp.dot(a, w_ref[...].astype(BF16), preferred_element_type=F32) + b_ref[...]


def _ada_modulation(cvec, ada_w, ada_b):
    rows = cvec.shape[0]
    tn = 768
    return pl.pallas_call(
        _ada_kernel,
        out_shape=jax.ShapeDtypeStruct((DEPTH, rows, 3 * D_MODEL), F32),
        grid=(DEPTH, 3 * D_MODEL // tn),
        in_specs=[
            pl.BlockSpec((rows, D_MODEL), lambda l, j: (0, 0)),
            pl.BlockSpec((None, D_MODEL, tn), lambda l, j: (l, 0, j)),
            pl.BlockSpec((None, 1, tn), lambda l, j: (l, 0, j)),
        ],
        out_specs=pl.BlockSpec((None, rows, tn), lambda l, j: (l, 0, j)),
        compiler_params=_params("arbitrary", "arbitrary"),
        name="ada_modulation",
    )(cvec, ada_w, ada_b.reshape(DEPTH, 1, 3 * D_MODEL))


def _inproj_kernel(*refs, n_q, n_k, n_v, rope, heads_per_tile):
    x_ref, sh_ref, sc_ref, lng_ref, w_ref, qg_ref, kg_ref = refs[:7]
    if rope:
        cos_ref, sa_ref, sb_ref = refs[7:10]
    q_out, k_out, v_out, g_out, h_scr = refs[-5:]
    j = pl.program_id(1)

    @pl.when(j == 0)
    def _():
        def body(r, carry):
            rows = pl.ds(pl.multiple_of(r * NORM_CHUNK, NORM_CHUNK), NORM_CHUNK)
            x = x_ref[rows, :]
            ms = jnp.mean(x * x, axis=-1, keepdims=True)
            y = x * lax.rsqrt(ms + EPS) * lng_ref[...]
            h_scr[rows, :] = (y * (1.0 + sc_ref[...]) + sh_ref[...]).astype(BF16)
            return carry
        lax.fori_loop(0, x_ref.shape[0] // NORM_CHUNK, body, 0)

    def normed(acc, rows, gain_ref, scale):
        outs = []
        for t in range(heads_per_tile):
            y = acc[:, t * HEAD_DIM:(t + 1) * HEAD_DIM]
            ms = jnp.mean(y * y, axis=-1, keepdims=True)
            y = y * lax.rsqrt(ms + EPS) * (gain_ref[...] * scale)
            if rope:
                y = (y * cos_ref[rows, :] + pltpu.roll(y, 96, 1) * sa_ref[rows, :]
                     + pltpu.roll(y, 32, 1) * sb_ref[rows, :])
            outs.append(y)
        return jnp.concatenate(outs, axis=-1)

    def project(out_ref, epilogue):
        for r in range(h_scr.shape[0] // PROJ_SUB):
            rows = slice(r * PROJ_SUB, (r + 1) * PROJ_SUB)
            acc = jnp.dot(h_scr[rows, :], w_ref[...], preferred_element_type=F32)
            out_ref[rows, :] = epilogue(acc, rows).astype(out_ref.dtype)

    @pl.when(j < n_q)
    def _():
        project(q_out, lambda acc, rows: normed(acc, rows, qg_ref, Q_SCALE))

    @pl.when((j >= n_q) & (j < n_q + n_k))
    def _():
        project(k_out, lambda acc, rows: normed(acc, rows, kg_ref, 1.0))

    @pl.when((j >= n_q + n_k) & (j < n_q + n_k + n_v))
    def _():
        project(v_out, lambda acc, rows: acc)

    @pl.when(j >= n_q + n_k + n_v)
    def _():
        project(g_out, lambda acc, rows: acc * jax.nn.sigmoid(acc))


def _in_projection(x, shift, scale, ln_g, w, qg, kg, widths, rope_tabs, kv_dtype, tn):
    t_rows = x.shape[0]
    qw, kw, vw, gw = widths
    n_q, n_k, n_v, n_g = qw // tn, kw // tn, vw // tn, gw // tn
    tm = min(ROW_TILE, t_rows)
    rows_per_mod = t_rows // shift.shape[0]
    assert t_rows % tm == 0 and rows_per_mod % tm == 0
    rope = rope_tabs is not None

    def mod_map(i, j):
        return ((i * tm) // rows_per_mod, 0, 0)

    in_specs = [
        pl.BlockSpec((tm, D_MODEL), lambda i, j: (i, 0)),
        pl.BlockSpec((None, 1, D_MODEL), mod_map),
        pl.BlockSpec((None, 1, D_MODEL), mod_map),
        pl.BlockSpec((1, D_MODEL), lambda i, j: (0, 0)),
        pl.BlockSpec((D_MODEL, tn), lambda i, j: (0, j)),
        pl.BlockSpec((1, HEAD_DIM), lambda i, j: (0, 0)),
        pl.BlockSpec((1, HEAD_DIM), lambda i, j: (0, 0)),
    ]
    args = [x, shift, scale, ln_g.reshape(1, D_MODEL), w, qg.reshape(1, HEAD_DIM), kg.reshape(1, HEAD_DIM)]
    if rope:
        n_pos_tiles = rope_tabs[0].shape[0] // tm
        for tab in rope_tabs:
            in_specs.append(pl.BlockSpec((tm, HEAD_DIM), lambda i, j: (i % n_pos_tiles, 0)))
            args.append(tab)
    out_specs = [
        pl.BlockSpec((tm, tn), lambda i, j: (i, jnp.minimum(j, n_q - 1))),
        pl.BlockSpec((tm, tn), lambda i, j: (i, jnp.clip(j - n_q, 0, n_k - 1))),
        pl.BlockSpec((tm, tn), lambda i, j: (i, jnp.clip(j - n_q - n_k, 0, n_v - 1))),
        pl.BlockSpec((tm, tn), lambda i, j: (i, jnp.clip(j - n_q - n_k - n_v, 0, n_g - 1))),
    ]
    out_shape = [
        jax.ShapeDtypeStruct((t_rows, qw), BF16),
        jax.ShapeDtypeStruct((t_rows, kw), kv_dtype),
        jax.ShapeDtypeStruct((t_rows, vw), kv_dtype),
        jax.ShapeDtypeStruct((t_rows, gw), BF16),
    ]
    kern = functools.partial(_inproj_kernel, n_q=n_q, n_k=n_k, n_v=n_v, rope=rope,
                             heads_per_tile=tn // HEAD_DIM)
    return pl.pallas_call(
        kern,
        out_shape=out_shape,
        grid=(t_rows // tm, n_q + n_k + n_v + n_g),
        in_specs=in_specs,
        out_specs=out_specs,
        scratch_shapes=[pltpu.VMEM((tm, D_MODEL), BF16)],
        compiler_params=_params("arbitrary", "arbitrary"),
        name="in_projection",
    )(*args)


def _outproj_kernel(o_ref, w_ref, x_ref, gt_ref, y_ref):
    y = jnp.dot(o_ref[...], w_ref[...], preferred_element_type=F32)
    y_ref[...] = x_ref[...] + gt_ref[...] * y


def _out_projection(o, w, x, gate):
    t_rows = x.shape[0]
    tm, tn = min(ROW_TILE, t_rows), 1024
    rows_per_mod = t_rows // gate.shape[0]
    assert t_rows % tm == 0 and rows_per_mod % tm == 0
    return pl.pallas_call(
        _outproj_kernel,
        out_shape=jax.ShapeDtypeStruct((t_rows, D_MODEL), F32),
        grid=(t_rows // tm, D_MODEL // tn),
        in_specs=[
            pl.BlockSpec((tm, D_MODEL), lambda i, j: (i, 0)),
            pl.BlockSpec((D_MODEL, tn), lambda i, j: (0, j)),
            pl.BlockSpec((tm, tn), lambda i, j: (i, j)),
            pl.BlockSpec((None, 1, tn), lambda i, j: ((i * tm) // rows_per_mod, 0, j)),
        ],
        out_specs=pl.BlockSpec((tm, tn), lambda i, j: (i, j)),
        compiler_params=_params("arbitrary", "arbitrary"),
        name="out_projection",
    )(o, w, x, gate)


def _softmax_pv(scores, values, sink=None):
    m = jnp.max(scores[0], axis=-1, keepdims=True)
    for s in scores[1:]:
        m = jnp.maximum(m, jnp.max(s, axis=-1, keepdims=True))
    if sink is not None:
        m = jnp.maximum(m, sink)
    l = jnp.exp2(sink - m) if sink is not None else None
    o = None
    for s, v in zip(scores, values):
        p = jnp.exp2(s - m)
        ps = jnp.sum(p, axis=-1, keepdims=True)
        l = ps if l is None else l + ps
        pv = jnp.dot(p.astype(BF16), v, preferred_element_type=F32)
        o = pv if o is None else o + pv
    return o * (1.0 / l)


def _attn_a_kernel(*refs, latent, n_qblocks):
    if latent:
        (sink_ref, q_ref, kp_ref, ko_ref, kn_ref, vp_ref, vo_ref, vn_ref,
         kc_ref, vc_ref, sg_ref, o_ref) = refs
    else:
        sink_ref, q_ref, k_ref, v_ref, sg_ref, o_ref = refs
    m_rows = q_ref.shape[0]
    stacked = A_GROUP * m_rows

    if latent:
        blk = pl.program_id(1)
        span = 3 * A_QBLOCK
        qpos = lax.broadcasted_iota(jnp.int32, (stacked, span), 0) & (A_QBLOCK - 1)
        col = lax.broadcasted_iota(jnp.int32, (stacked, span), 1)
        rel = col - A_QBLOCK - qpos
        kpos = (blk - 1) * A_QBLOCK + col
        valid = (jnp.abs(rel) <= WINDOW) & (kpos >= 0) & (kpos < n_qblocks * A_QBLOCK)

    for kv in range(A_KV_HEADS):
        cols = slice(kv * HEAD_DIM, (kv + 1) * HEAD_DIM)
        heads = [kv * A_GROUP + g for g in range(A_GROUP)]
        q = jnp.concatenate([q_ref[:, h * HEAD_DIM:(h + 1) * HEAD_DIM] for h in heads], axis=0)
        sink = jnp.concatenate([jnp.full((m_rows, 1), sink_ref[h] * LOG2E, F32) for h in heads], axis=0)
        if latent:
            k_loc = jnp.concatenate([kp_ref[:, cols], ko_ref[:, cols], kn_ref[:, cols]], axis=0)
            v_loc = jnp.concatenate([vp_ref[:, cols], vo_ref[:, cols], vn_ref[:, cols]], axis=0)
            s_loc = jnp.where(valid, _nt_dot(q, k_loc), NEG_INF)
            s_ctx = _nt_dot(q, kc_ref[:, cols].astype(BF16))
            o = _softmax_pv([s_loc, s_ctx], [v_loc, vc_ref[:, cols].astype(BF16)], sink)
        else:
            s = _nt_dot(q, k_ref[:, cols].astype(BF16))
            o = _softmax_pv([s], [v_ref[:, cols].astype(BF16)], sink)
        for g, h in enumerate(heads):
            hc = slice(h * HEAD_DIM, (h + 1) * HEAD_DIM)
            o_ref[:, hc] = (o[g * m_rows:(g + 1) * m_rows] * sg_ref[:, hc].astype(F32)).astype(o_ref.dtype)


def _attn_a_ctx(q, k, v, sg, sink, n_batch, seq):
    kvw = A_KV_HEADS * HEAD_DIM
    row = lambda b: (b, 0)
    return pl.pallas_call(
        functools.partial(_attn_a_kernel, latent=False, n_qblocks=1),
        out_shape=jax.ShapeDtypeStruct(q.shape, BF16),
        grid=(n_batch,),
        in_specs=[
            pl.BlockSpec(memory_space=pltpu.SMEM),
            pl.BlockSpec((seq, D_MODEL), row),
            pl.BlockSpec((seq, kvw), row),
            pl.BlockSpec((seq, kvw), row),
            pl.BlockSpec((seq, D_MODEL), row),
        ],
        out_specs=pl.BlockSpec((seq, D_MODEL), row),
        compiler_params=_params("arbitrary"),
        name="attn_a_ctx",
    )(sink, q, k, v, sg)


def _attn_a_lat(q, k, v, k_ctx, v_ctx, sg, sink, n_batch, seq):
    kvw = A_KV_HEADS * HEAD_DIM
    nb = seq // A_QBLOCK
    own = lambda b, i: (b * nb + i, 0)
    prev = lambda b, i: (b * nb + jnp.maximum(i - 1, 0), 0)
    nxt = lambda b, i: (b * nb + jnp.minimum(i + 1, nb - 1), 0)
    ctx = lambda b, i: (b, 0, 0)
    n_ctx = k_ctx.shape[1]
    return pl.pallas_call(
        functools.partial(_attn_a_kernel, latent=True, n_qblocks=nb),
        out_shape=jax.ShapeDtypeStruct(q.shape, BF16),
        grid=(n_batch, nb),
        in_specs=[
            pl.BlockSpec(memory_space=pltpu.SMEM),
            pl.BlockSpec((A_QBLOCK, D_MODEL), own),
            pl.BlockSpec((A_QBLOCK, kvw), prev),
            pl.BlockSpec((A_QBLOCK, kvw), own),
            pl.BlockSpec((A_QBLOCK, kvw), nxt),
            pl.BlockSpec((A_QBLOCK, kvw), prev),
            pl.BlockSpec((A_QBLOCK, kvw), own),
            pl.BlockSpec((A_QBLOCK, kvw), nxt),
            pl.BlockSpec((None, n_ctx, kvw), ctx),
            pl.BlockSpec((None, n_ctx, kvw), ctx),
            pl.BlockSpec((A_QBLOCK, D_MODEL), own),
        ],
        out_specs=pl.BlockSpec((A_QBLOCK, D_MODEL), own),
        compiler_params=_params("arbitrary", "arbitrary"),
        name="attn_a_lat",
    )(sink, q, k, k, k, v, v, v, k_ctx, v_ctx, sg)


def _lane_fold(x, op):
    r = x[:, 0:HEAD_DIM]
    for t in range(1, x.shape[1] // HEAD_DIM):
        r = op(r, x[:, t * HEAD_DIM:(t + 1) * HEAD_DIM])
    return r


def _attn_b_kernel(*refs, has_ctx, lambda_init, k_chunk):
    if has_ctx:
        (lam_ref, sub_ref, q_ref, k_ref, v_ref, kc_ref, vc_ref, sg_ref, o_ref,
         s_scr, sc_scr, m_scr, l_scr, acc_scr) = refs
    else:
        lam_ref, sub_ref, q_ref, k_ref, v_ref, sg_ref, o_ref, s_scr, m_scr, l_scr, acc_scr = refs
    n_chunks = k_ref.shape[0] // k_chunk
    halves = [slice(h * HEAD_DIM, (h + 1) * HEAD_DIM) for h in range(2)]

    m_scr[...] = jnp.full(m_scr.shape, NEG_INF, F32)

    def scores(c):
        rows = pl.ds(pl.multiple_of(c * k_chunk, k_chunk), k_chunk)
        k = k_ref[rows, :].astype(BF16)
        for h, hc in enumerate(halves):
            s = _nt_dot(q_ref[:, hc], k[:, hc])
            s_scr[h, c] = s
            m_scr[h] = jnp.maximum(m_scr[h], _lane_fold(s, jnp.maximum))

    if n_chunks == 1:
        scores(0)
    else:
        lax.fori_loop(0, n_chunks, lambda c, carry: (scores(c), carry)[1], 0)
    if has_ctx:
        kc = kc_ref[...].astype(BF16)
        for h, hc in enumerate(halves):
            s = _nt_dot(q_ref[:, hc], kc[:, hc])
            sc_scr[h] = s
            m_scr[h] = jnp.maximum(m_scr[h], _lane_fold(s, jnp.maximum))

    for h in range(2):
        m_scr[h] = jnp.broadcast_to(jnp.max(m_scr[h], axis=-1, keepdims=True), m_scr.shape[1:])
    l_scr[...] = jnp.zeros(l_scr.shape, F32)
    acc_scr[...] = jnp.zeros(acc_scr.shape, F32)

    def accumulate(h, s, v):
        m = m_scr[h]
        p = jnp.exp2(s - jnp.concatenate([m] * (s.shape[1] // HEAD_DIM), axis=1))
        l_scr[h] += _lane_fold(p, jnp.add)
        acc_scr[h] += jnp.dot(p.astype(BF16), v, preferred_element_type=F32)

    def weighted(c):
        rows = pl.ds(pl.multiple_of(c * k_chunk, k_chunk), k_chunk)
        v = v_ref[rows, :].astype(BF16)
        for h in range(2):
            accumulate(h, s_scr[h, c], v)

    if n_chunks == 1:
        weighted(0)
    else:
        lax.fori_loop(0, n_chunks, lambda c, carry: (weighted(c), carry)[1], 0)
    if has_ctx:
        vc = vc_ref[...].astype(BF16)
        for h in range(2):
            accumulate(h, sc_scr[h], vc)

    lam = lam_ref[...]
    lam_full = (jnp.exp(jnp.sum(lam[0:1] * lam[1:2], axis=-1, keepdims=True))
                - jnp.exp(jnp.sum(lam[2:3] * lam[3:4], axis=-1, keepdims=True)) + lambda_init)
    r0 = 1.0 / jnp.sum(l_scr[0], axis=-1, keepdims=True)
    r1 = lam_full / jnp.sum(l_scr[1], axis=-1, keepdims=True)
    o = acc_scr[0] * r0 - acc_scr[1] * r1
    ms = jnp.mean(o * o, axis=-1, keepdims=True)
    o = o * lax.rsqrt(ms + EPS) * (sub_ref[...] * (1.0 - lambda_init))
    o_ref[...] = (o * sg_ref[...].astype(F32)).astype(o_ref.dtype)


def _attn_b(q, k, v, sg, lam, subln, lambda_init, n_batch, seq, k_ctx=None, v_ctx=None):
    dv = 2 * HEAD_DIM
    tq = min(B_QBLOCK, seq)
    nq = seq // tq
    k_chunk = min(B_KCHUNK, seq)
    has_ctx = k_ctx is not None
    qmap = lambda b, h, i: (b * nq + i, h)
    kvmap = lambda b, h, i: (b, h)
    in_specs = [
        pl.BlockSpec((4, HEAD_DIM), lambda b, h, i: (0, 0)),
        pl.BlockSpec((1, dv), lambda b, h, i: (0, 0)),
        pl.BlockSpec((tq, dv), qmap),
        pl.BlockSpec((seq, dv), kvmap),
        pl.BlockSpec((seq, dv), kvmap),
    ]
    args = [lam, subln.reshape(1, dv), q, k, v]
    scratch = [pltpu.VMEM((2, seq // k_chunk, tq, k_chunk), F32)]
    if has_ctx:
        n_ctx = k_ctx.shape[1]
        in_specs += [pl.BlockSpec((None, n_ctx, dv), lambda b, h, i: (b, 0, h))] * 2
        args += [k_ctx, v_ctx]
        scratch.append(pltpu.VMEM((2, tq, n_ctx), F32))
    in_specs.append(pl.BlockSpec((tq, dv), qmap))
    args.append(sg)
    scratch += [pltpu.VMEM((2, tq, HEAD_DIM), F32), pltpu.VMEM((2, tq, HEAD_DIM), F32),
                pltpu.VMEM((2, tq, dv), F32)]
    return pl.pallas_call(
        functools.partial(_attn_b_kernel, has_ctx=has_ctx, lambda_init=lambda_init, k_chunk=k_chunk),
        out_shape=jax.ShapeDtypeStruct(q.shape, BF16),
        grid=(n_batch, B_HEADS, nq),
        in_specs=in_specs,
        out_specs=pl.BlockSpec((tq, dv), qmap),
        scratch_shapes=scratch,
        compiler_params=_params("arbitrary", "arbitrary", "arbitrary"),
        name="attn_b_lat" if has_ctx else "attn_b_ctx",
    )(*args)


def _attn_c_ctx_kernel(q_ref, k_ref, v_ref, sg_ref, o_ref):
    for h in range(C_HEADS):
        hc = slice(h * HEAD_DIM, (h + 1) * HEAD_DIM)
        s = _nt_dot(q_ref[:, hc], k_ref[:, hc].astype(BF16))
        o = _softmax_pv([s], [v_ref[:, hc].astype(BF16)])
        o_ref[:, hc] = (o * sg_ref[:, hc].astype(F32)).astype(o_ref.dtype)


def _attn_c_ctx(q, k, v, sg, n_batch, seq):
    spec = pl.BlockSpec((seq, D_MODEL), lambda b: (b, 0))
    return pl.pallas_call(
        _attn_c_ctx_kernel,
        out_shape=jax.ShapeDtypeStruct(q.shape, BF16),
        grid=(n_batch,),
        in_specs=[spec, spec, spec, spec],
        out_specs=spec,
        compiler_params=_params("arbitrary"),
        name="attn_c_ctx",
    )(q, k, v, sg)


def _na_geometry(rows):
    kh = min(NA_KH, rows)
    key_rows = min(C_ROWS + kh - 1, rows)
    n_blocks = rows // C_ROWS
    cols = np.arange(GRID_W)
    col_start = np.clip(cols - NA_KW // 2, 0, GRID_W - NA_KW)
    col_ok = (cols[None, :] >= col_start[:, None]) & (cols[None, :] < col_start[:, None] + NA_KW)
    col_delta = cols[None, :] - cols[:, None] + NA_KW - 1
    col_onehot = (col_delta[None] == np.arange(2 * NA_KW - 1)[:, None, None]) & col_ok[None]
    starts, class_ids, classes, keys = [], [], [], {}
    for blk in range(n_blocks):
        r = blk * C_ROWS + np.arange(C_ROWS)
        rs = np.clip(r - kh // 2, 0, rows - kh)
        start = int(np.clip(rs[0], 0, rows - key_rows))
        key = (tuple(rs - r), start - blk * C_ROWS)
        if key not in keys:
            keys[key] = len(classes)
            kr = start + np.arange(key_rows)
            row_ok = (kr[None, :] >= rs[:, None]) & (kr[None, :] < rs[:, None] + kh)
            row_idx = kr[None, :] - r[:, None] + NA_KH - 1
            classes.append((row_ok, row_idx))
        starts.append(start)
        class_ids.append(keys[key])
    geometry = (col_ok, col_onehot.astype(np.float32), classes)
    return key_rows, np.asarray(starts, np.int32), np.asarray(class_ids, np.int32), geometry


def _na_bias_tables(rpb, geometry):
    col_ok, col_onehot, classes = geometry
    n_heads = rpb.shape[0]
    toep = jnp.einsum("hdx,xck->hdck", rpb * LOG2E, jnp.asarray(col_onehot),
                      precision=lax.Precision.HIGHEST)
    toep = jnp.where(jnp.asarray(col_ok)[None, None], toep, NEG_INF)
    masked = jnp.full((n_heads, GRID_W, GRID_W), NEG_INF, F32)
    tabs = []
    for row_ok, row_idx in classes:
        n_r, n_kr = row_ok.shape
        blocks = [toep[:, int(row_idx[a, u])] if row_ok[a, u] else masked
                  for a in range(n_r) for u in range(n_kr)]
        tab = jnp.stack(blocks, axis=1).reshape(n_heads, n_r, n_kr, GRID_W, GRID_W)
        tabs.append(tab.transpose(0, 1, 3, 2, 4).reshape(n_heads, n_r * GRID_W, n_kr * GRID_W))
    return jnp.stack(tabs, axis=0)


def _attn_c_lat_kernel(start_ref, cls_ref, q_ref, k_ref, v_ref, kc_ref, vc_ref, bias_ref, sg_ref, o_ref,
                       *, key_rows):
    blk = pl.program_id(2)
    n_keys = key_rows * GRID_W
    rows = pl.ds(pl.multiple_of(start_ref[blk] * GRID_W, GRID_W), n_keys)
    for h in range(C_HEAD_GROUP):
        hc = slice(h * HEAD_DIM, (h + 1) * HEAD_DIM)
        q = q_ref[:, hc]
        s_ctx = _nt_dot(q, kc_ref[:, hc].astype(BF16))
        s_loc = _nt_dot(q, k_ref[rows, hc]) + bias_ref[h]
        o = _softmax_pv([s_ctx, s_loc], [vc_ref[:, hc].astype(BF16), v_ref[rows, hc]])
        o_ref[:, hc] = (o * sg_ref[:, hc].astype(F32)).astype(o_ref.dtype)


def _attn_c_lat(q, k, v, k_ctx, v_ctx, sg, rpb, n_batch, seq):
    rows = seq // GRID_W
    key_rows, starts, class_ids, geometry = _na_geometry(rows)
    bias = _na_bias_tables(rpb, geometry)
    nblk = rows // C_ROWS
    tq = C_ROWS * GRID_W
    gw = C_HEAD_GROUP * HEAD_DIM
    n_ctx = k_ctx.shape[1]
    qmap = lambda b, g, i, st, cl: (b * nblk + i, g)
    kvmap = lambda b, g, i, st, cl: (b, g)
    ctxmap = lambda b, g, i, st, cl: (b, 0, g)
    grid_spec = pltpu.PrefetchScalarGridSpec(
        num_scalar_prefetch=2,
        grid=(n_batch, C_HEADS // C_HEAD_GROUP, nblk),
        in_specs=[
            pl.BlockSpec((tq, gw), qmap),
            pl.BlockSpec((seq, gw), kvmap),
            pl.BlockSpec((seq, gw), kvmap),
            pl.BlockSpec((None, n_ctx, gw), ctxmap),
            pl.BlockSpec((None, n_ctx, gw), ctxmap),
            pl.BlockSpec((None, C_HEAD_GROUP, tq, key_rows * GRID_W),
                         lambda b, g, i, st, cl: (cl[i], g, 0, 0)),
            pl.BlockSpec((tq, gw), qmap),
        ],
        out_specs=pl.BlockSpec((tq, gw), qmap),
    )
    return pl.pallas_call(
        functools.partial(_attn_c_lat_kernel, key_rows=key_rows),
        out_shape=jax.ShapeDtypeStruct(q.shape, BF16),
        grid_spec=grid_spec,
        compiler_params=_params("arbitrary", "arbitrary", "arbitrary"),
        name="attn_c_lat",
    )(jnp.asarray(starts), jnp.asarray(class_ids), q, k, v, k_ctx, v_ctx, bias, sg)


def _rope_tables(n_tokens):
    d4 = HEAD_DIM // 4
    t = jnp.arange(n_tokens)
    pos = jnp.stack([t // GRID_W, t % GRID_W], axis=-1).astype(F32)
    inv_freq = ROPE_BASE ** (-jnp.arange(d4, dtype=F32) / d4)
    ang = pos[:, :, None] * inv_freq
    cos, sin = jnp.cos(ang), jnp.sin(ang)
    zero = jnp.zeros_like(sin[:, 0])
    cos_full = jnp.concatenate([cos[:, 0], cos[:, 0], cos[:, 1], cos[:, 1]], axis=-1)
    sin_up = jnp.concatenate([-sin[:, 0], zero, -sin[:, 1], zero], axis=-1)
    sin_dn = jnp.concatenate([zero, sin[:, 0], zero, sin[:, 1]], axis=-1)
    return cos_full, sin_up, sin_dn


def kernel(x_prompt, x_sample, cache_a_k, cache_a_v, cache_b_k, cache_b_v, cache_c_k, cache_c_v, c, c_ctx,
           ln_g, ada_w, ada_b, w_out, qn_g, kn_g, w_in_a, sink_a, w_in_b, lam_b, subln_b, w_in_c, rpb_c):
    n_p, seq_p, d = x_prompt.shape
    n_s, seq_s, _ = x_sample.shape
    n_ctx = cache_a_k.shape[2]
    xp = x_prompt.reshape(n_p * seq_p, d)
    xs = x_sample.reshape(n_s * seq_s, d)
    rope_tabs = _rope_tables(seq_s)

    pad = (-(n_s + 1)) % 8
    cvec = jnp.concatenate([c, c_ctx[None, :], jnp.zeros((pad, d), F32)], axis=0)
    mod = _ada_modulation(cvec, ada_w, ada_b)
    mod = mod.reshape(DEPTH, mod.shape[1], 3, 1, d)

    new_kv = {0: ([], []), 1: ([], []), 2: ([], [])}
    for l in range(DEPTH):
        kind, j = l % N_MIXERS, l // N_MIXERS
        sh_s, sc_s, gt_s = (mod[l, :n_s, t] for t in range(3))
        sh_p, sc_p, gt_p = (mod[l, n_s:n_s + 1, t] for t in range(3))
        w_o = w_out[l].astype(BF16)
        if kind == 0:
            kvw = A_KV_HEADS * HEAD_DIM
            w_in, widths, tn = w_in_a[j].astype(BF16), (d, kvw, kvw, d), 512
        elif kind == 1:
            w_in, widths, tn = w_in_b[j].astype(BF16), (d, d, d, d), 512
        else:
            w_in, widths, tn = w_in_c[j].astype(BF16), (d, d, d, d), 512
        rope = rope_tabs if kind != 2 else None
        qp, kp, vp, gp = _in_projection(xp, sh_p, sc_p, ln_g[l], w_in, qn_g[l], kn_g[l], widths, None, F32, tn)
        qs, ks, vs, gs = _in_projection(xs, sh_s, sc_s, ln_g[l], w_in, qn_g[l], kn_g[l], widths, rope, BF16, tn)
        if kind == 0:
            kc = cache_a_k[:, j].reshape(n_s, n_ctx, kvw)
            vc = cache_a_v[:, j].reshape(n_s, n_ctx, kvw)
            op = _attn_a_ctx(qp, kp, vp, gp, sink_a[j], n_p, seq_p)
            os_ = _attn_a_lat(qs, ks, vs, kc, vc, gs, sink_a[j], n_s, seq_s)
            new_kv[0][0].append(kp.reshape(n_p, seq_p, A_KV_HEADS, HEAD_DIM))
            new_kv[0][1].append(vp.reshape(n_p, seq_p, A_KV_HEADS, HEAD_DIM))
        elif kind == 1:
            lambda_init = 0.8 - 0.6 * math.exp(-0.3 * l)
            kc = cache_b_k[:, j].reshape(n_s, n_ctx, d)
            vc = cache_b_v[:, j].reshape(n_s, n_ctx, d)
            op = _attn_b(qp, kp, vp, gp, lam_b[j], subln_b[j], lambda_init, n_p, seq_p)
            os_ = _attn_b(qs, ks, vs, gs, lam_b[j], subln_b[j], lambda_init, n_s, seq_s, kc, vc)
            new_kv[1][0].append(kp.reshape(n_p, seq_p, B_HEADS, 2, HEAD_DIM))
            new_kv[1][1].append(vp.reshape(n_p, seq_p, B_HEADS, 2 * HEAD_DIM))
        else:
            kc = cache_c_k[:, j].reshape(n_s, n_ctx, d)
            vc = cache_c_v[:, j].reshape(n_s, n_ctx, d)
            op = _attn_c_ctx(qp, kp, vp, gp, n_p, seq_p)
            os_ = _attn_c_lat(qs, ks, vs, kc, vc, gs, rpb_c[j], n_s, seq_s)
            new_kv[2][0].append(kp.reshape(n_p, seq_p, C_HEADS, HEAD_DIM))
            new_kv[2][1].append(vp.reshape(n_p, seq_p, C_HEADS, HEAD_DIM))
        xp = _out_projection(op, w_o, xp, gt_p)
        xs = _out_projection(os_, w_o, xs, gt_s)

    outs = [xp.reshape(x_prompt.shape), xs.reshape(x_sample.shape)]
    for kind in range(3):
        outs.append(jnp.stack(new_kv[kind][0], axis=1))
        outs.append(jnp.stack(new_kv[kind][1], axis=1))
    return tuple(outs)
```

```python
import functools
import math

import numpy as np
import jax
import jax.numpy as jnp
from jax import lax
from jax.experimental import pallas as pl
from jax.experimental.pallas import tpu as pltpu

D_MODEL = 2048
DEPTH = 4
GRID_W = 64
HEAD_DIM = 128
ROPE_BASE = 10000.0
NEG_INF = -1e30
N_MIXERS = 3
A_HEADS = D_MODEL // HEAD_DIM
A_KV_HEADS = A_HEADS // 4
A_GROUP = A_HEADS // A_KV_HEADS
WINDOW = 128
B_HEADS = D_MODEL // (2 * HEAD_DIM)
C_HEADS = D_MODEL // HEAD_DIM
NA_KH = 8
NA_KW = 16
SCALE = HEAD_DIM ** -0.5
LOG2E = math.log2(math.e)
Q_SCALE = SCALE * LOG2E
EPS = 1e-6

BF16 = jnp.bfloat16
F32 = jnp.float32

VMEM_LIMIT_BYTES = 56 * 1024 * 1024
ROW_TILE = 1024
NORM_CHUNK = 128
PROJ_SUB = 256
A_QBLOCK = 128
B_QBLOCK = 512
B_KCHUNK = 512
B_UNROLL = 4
C_ROWS = 4
C_HEAD_GROUP = 4


def _nt_dot(a, b):
    return lax.dot_general(a, b, (((1,), (1,)), ((), ())), preferred_element_type=F32)


def _params(*semantics):
    return pltpu.CompilerParams(dimension_semantics=semantics, vmem_limit_bytes=VMEM_LIMIT_BYTES)


def _ada_kernel(c_ref, w_ref, b_ref, o_ref):
    c = c_ref[...]
    a = (c * jax.nn.sigmoid(c)).astype(BF16)
    o_ref[...] = jnp.dot(a, w_ref[...].astype(BF16), preferred_element_type=F32) + b_ref[...]


def _ada_modulation(cvec, ada_w, ada_b):
    rows = cvec.shape[0]
    tn = 768
    return pl.pallas_call(
        _ada_kernel,
        out_shape=jax.ShapeDtypeStruct((DEPTH, rows, 3 * D_MODEL), F32),
        grid=(DEPTH, 3 * D_MODEL // tn),
        in_specs=[
            pl.BlockSpec((rows, D_MODEL), lambda l, j: (0, 0)),
            pl.BlockSpec((None, D_MODEL, tn), lambda l, j: (l, 0, j)),
            pl.BlockSpec((None, 1, tn), lambda l, j: (l, 0, j)),
        ],
        out_specs=pl.BlockSpec((None, rows, tn), lambda l, j: (l, 0, j)),
        compiler_params=_params("arbitrary", "arbitrary"),
        name="ada_modulation",
    )(cvec, ada_w, ada_b.reshape(DEPTH, 1, 3 * D_MODEL))


def _inproj_kernel(*refs, n_q, n_k, n_v, rope, heads_per_tile):
    x_ref, sh_ref, sc_ref, lng_ref, w_ref, qg_ref, kg_ref = refs[:7]
    if rope:
        cos_ref, sa_ref, sb_ref = refs[7:10]
    q_out, k_out, v_out, g_out, h_scr = refs[-5:]
    j = pl.program_id(1)

    @pl.when(j == 0)
    def _():
        def body(r, carry):
            rows = pl.ds(pl.multiple_of(r * NORM_CHUNK, NORM_CHUNK), NORM_CHUNK)
            x = x_ref[rows, :]
            ms = jnp.mean(x * x, axis=-1, keepdims=True)
            y = x * lax.rsqrt(ms + EPS) * lng_ref[...]
            h_scr[rows, :] = (y * (1.0 + sc_ref[...]) + sh_ref[...]).astype(BF16)
            return carry
        lax.fori_loop(0, x_ref.shape[0] // NORM_CHUNK, body, 0)

    def normed(acc, rows, gain_ref, scale):
        outs = []
        for t in range(heads_per_tile):
            y = acc[:, t * HEAD_DIM:(t + 1) * HEAD_DIM]
            ms = jnp.mean(y * y, axis=-1, keepdims=True)
            y = y * lax.rsqrt(ms + EPS) * (gain_ref[...] * scale)
            if rope:
                y = (y * cos_ref[rows, :] + pltpu.roll(y, 96, 1) * sa_ref[rows, :]
                     + pltpu.roll(y, 32, 1) * sb_ref[rows, :])
            outs.append(y)
        return jnp.concatenate(outs, axis=-1)

    def project(out_ref, epilogue):
        for r in range(h_scr.shape[0] // PROJ_SUB):
            rows = slice(r * PROJ_SUB, (r + 1) * PROJ_SUB)
            acc = jnp.dot(h_scr[rows, :], w_ref[...], preferred_element_type=F32)
            out_ref[rows, :] = epilogue(acc, rows).astype(out_ref.dtype)

    @pl.when(j < n_q)
    def _():
        project(q_out, lambda acc, rows: normed(acc, rows, qg_ref, Q_SCALE))

    @pl.when((j >= n_q) & (j < n_q + n_k))
    def _():
        project(k_out, lambda acc, rows: normed(acc, rows, kg_ref, 1.0))

    @pl.when((j >= n_q + n_k) & (j < n_q + n_k + n_v))
    def _():
        project(v_out, lambda acc, rows: acc)

    @pl.when(j >= n_q + n_k + n_v)
    def _():
        project(g_out, lambda acc, rows: acc * jax.nn.sigmoid(acc))


def _in_projection(x, shift, scale, ln_g, w, qg, kg, widths, rope_tabs, kv_dtype, tn):
    t_rows = x.shape[0]
    qw, kw, vw, gw = widths
    n_q, n_k, n_v, n_g = qw // tn, kw // tn, vw // tn, gw // tn
    tm = min(ROW_TILE, t_rows)
    rows_per_mod = t_rows // shift.shape[0]
    assert t_rows % tm == 0 and rows_per_mod % tm == 0
    rope = rope_tabs is not None

    def mod_map(i, j):
        return ((i * tm) // rows_per_mod, 0, 0)

    in_specs = [
        pl.BlockSpec((tm, D_MODEL), lambda i, j: (i, 0)),
        pl.BlockSpec((None, 1, D_MODEL), mod_map),
        pl.BlockSpec((None, 1, D_MODEL), mod_map),
        pl.BlockSpec((1, D_MODEL), lambda i, j: (0, 0)),
        pl.BlockSpec((D_MODEL, tn), lambda i, j: (0, j)),
        pl.BlockSpec((1, HEAD_DIM), lambda i, j: (0, 0)),
        pl.BlockSpec((1, HEAD_DIM), lambda i, j: (0, 0)),
    ]
    args = [x, shift, scale, ln_g.reshape(1, D_MODEL), w, qg.reshape(1, HEAD_DIM), kg.reshape(1, HEAD_DIM)]
    if rope:
        n_pos_tiles = rope_tabs[0].shape[0] // tm
        for tab in rope_tabs:
            in_specs.append(pl.BlockSpec((tm, HEAD_DIM), lambda i, j: (i % n_pos_tiles, 0)))
            args.append(tab)
    out_specs = [
        pl.BlockSpec((tm, tn), lambda i, j: (i, jnp.minimum(j, n_q - 1))),
        pl.BlockSpec((tm, tn), lambda i, j: (i, jnp.clip(j - n_q, 0, n_k - 1))),
        pl.BlockSpec((tm, tn), lambda i, j: (i, jnp.clip(j - n_q - n_k, 0, n_v - 1))),
        pl.BlockSpec((tm, tn), lambda i, j: (i, jnp.clip(j - n_q - n_k - n_v, 0, n_g - 1))),
    ]
    out_shape = [
        jax.ShapeDtypeStruct((t_rows, qw), BF16),
        jax.ShapeDtypeStruct((t_rows, kw), kv_dtype),
        jax.ShapeDtypeStruct((t_rows, vw), kv_dtype),
        jax.ShapeDtypeStruct((t_rows, gw), BF16),
    ]
    kern = functools.partial(_inproj_kernel, n_q=n_q, n_k=n_k, n_v=n_v, rope=rope,
                             heads_per_tile=tn // HEAD_DIM)
    return pl.pallas_call(
        kern,
        out_shape=out_shape,
        grid=(t_rows // tm, n_q + n_k + n_v + n_g),
        in_specs=in_specs,
        out_specs=out_specs,
        scratch_shapes=[pltpu.VMEM((tm, D_MODEL), BF16)],
        compiler_params=_params("arbitrary", "arbitrary"),
        name="in_projection",
    )(*args)


def _outproj_kernel(o_ref, w_ref, x_ref, gt_ref, y_ref):
    y = jnp.dot(o_ref[...], w_ref[...], preferred_element_type=F32)
    y_ref[...] = x_ref[...] + gt_ref[...] * y


def _out_projection(o, w, x, gate):
    t_rows = x.shape[0]
    tm, tn = min(ROW_TILE, t_rows), 1024
    rows_per_mod = t_rows // gate.shape[0]
    assert t_rows % tm == 0 and rows_per_mod % tm == 0
    return pl.pallas_call(
        _outproj_kernel,
        out_shape=jax.ShapeDtypeStruct((t_rows, D_MODEL), F32),
        grid=(t_rows // tm, D_MODEL // tn),
        in_specs=[
            pl.BlockSpec((tm, D_MODEL), lambda i, j: (i, 0)),
            pl.BlockSpec((D_MODEL, tn), lambda i, j: (0, j)),
            pl.BlockSpec((tm, tn), lambda i, j: (i, j)),
            pl.BlockSpec((None, 1, tn), lambda i, j: ((i * tm) // rows_per_mod, 0, j)),
        ],
        out_specs=pl.BlockSpec((tm, tn), lambda i, j: (i, j)),
        compiler_params=_params("arbitrary", "arbitrary"),
        name="out_projection",
    )(o, w, x, gate)


def _lane_fold(x, op):
    r = x[:, 0:HEAD_DIM]
    for t in range(1, x.shape[1] // HEAD_DIM):
        r = op(r, x[:, t * HEAD_DIM:(t + 1) * HEAD_DIM])
    return r


def _with_ones(v):
    return jnp.concatenate([v, jnp.ones(v.shape, v.dtype)], axis=1)


def _softmax_pv(scores, values):
    m128 = _lane_fold(scores[0], jnp.maximum)
    for s in scores[1:]:
        m128 = jnp.maximum(m128, _lane_fold(s, jnp.maximum))
    m = jnp.broadcast_to(jnp.max(m128, axis=-1, keepdims=True), m128.shape)
    acc = None
    for s, v in zip(scores, values):
        p = jnp.exp2(s - jnp.concatenate([m] * (s.shape[1] // HEAD_DIM), axis=1)).astype(BF16)
        pv = jnp.dot(p, _with_ones(v), preferred_element_type=F32)
        acc = pv if acc is None else acc + pv
    return acc[:, :HEAD_DIM] * (1.0 / acc[:, HEAD_DIM:])


def _attn_a_kernel(*refs, latent, n_qblocks):
    if latent:
        (sink_ref, q_ref, kp_ref, ko_ref, kn_ref, vp_ref, vo_ref, vn_ref,
         kc_ref, vc_ref, sg_ref, o_ref, s_scr, m_scr, acc_scr) = refs
    else:
        sink_ref, q_ref, k_ref, v_ref, sg_ref, o_ref, s_scr, m_scr, acc_scr = refs
    m_rows = q_ref.shape[0]
    n_keys = s_scr.shape[-1]
    n_loc = 3 * A_QBLOCK
    kv_cols = [slice(kv * HEAD_DIM, (kv + 1) * HEAD_DIM) for kv in range(A_KV_HEADS)]
    kv_heads = [[kv * A_GROUP + g for g in range(A_GROUP)] for kv in range(A_KV_HEADS)]

    if latent:
        blk = pl.program_id(1)
        qpos = lax.broadcasted_iota(jnp.int32, (A_QBLOCK, n_loc), 0)
        col = lax.broadcasted_iota(jnp.int32, (A_QBLOCK, n_loc), 1)
        rel = col - A_QBLOCK - qpos
        kpos = (blk - 1) * A_QBLOCK + col
        valid = (jnp.abs(rel) <= WINDOW) & (kpos >= 0) & (kpos < n_qblocks * A_QBLOCK)
        window_bias = jnp.concatenate([jnp.where(valid, 0.0, NEG_INF)] * A_GROUP, axis=0)

    for kv, cols in enumerate(kv_cols):
        q = jnp.concatenate([q_ref[:, h * HEAD_DIM:(h + 1) * HEAD_DIM] for h in kv_heads[kv]], axis=0)
        if latent:
            k_loc = jnp.concatenate([kp_ref[:, cols], ko_ref[:, cols], kn_ref[:, cols]], axis=0)
            s_loc = _nt_dot(q, k_loc) + window_bias
            s_ctx = _nt_dot(q, kc_ref[:, cols].astype(BF16))
            s_scr[kv, :, 0:n_loc] = s_loc
            s_scr[kv, :, n_loc:n_keys] = s_ctx
            m_scr[kv] = jnp.maximum(_lane_fold(s_loc, jnp.maximum), _lane_fold(s_ctx, jnp.maximum))
        else:
            s = _nt_dot(q, k_ref[:, cols].astype(BF16))
            s_scr[kv] = s
            m_scr[kv] = _lane_fold(s, jnp.maximum)

    def sink_lanes(kv):
        return jnp.concatenate([jnp.full((m_rows, HEAD_DIM), sink_ref[h] * LOG2E, F32) for h in kv_heads[kv]], axis=0)

    for kv in range(A_KV_HEADS):
        m = jnp.max(m_scr[kv], axis=-1, keepdims=True)
        m_scr[kv] = jnp.maximum(jnp.broadcast_to(m, m_scr.shape[1:]), sink_lanes(kv))

    for kv, cols in enumerate(kv_cols):
        p = jnp.exp2(s_scr[kv] - jnp.concatenate([m_scr[kv]] * (n_keys // HEAD_DIM), axis=1)).astype(BF16)
        if latent:
            v_loc = jnp.concatenate([vp_ref[:, cols], vo_ref[:, cols], vn_ref[:, cols]], axis=0)
            acc_scr[kv] = (jnp.dot(p[:, 0:n_loc], _with_ones(v_loc), preferred_element_type=F32)
                           + jnp.dot(p[:, n_loc:n_keys], _with_ones(vc_ref[:, cols].astype(BF16)),
                                     preferred_element_type=F32))
        else:
            acc_scr[kv] = jnp.dot(p, _with_ones(v_ref[:, cols].astype(BF16)), preferred_element_type=F32)

    for kv in range(A_KV_HEADS):
        acc = acc_scr[kv]
        l = acc[:, HEAD_DIM:] + jnp.exp2(sink_lanes(kv) - m_scr[kv])
        o = acc[:, :HEAD_DIM] * (1.0 / l)
        for g, h in enumerate(kv_heads[kv]):
            hc = slice(h * HEAD_DIM, (h + 1) * HEAD_DIM)
            o_ref[:, hc] = (o[g * m_rows:(g + 1) * m_rows] * sg_ref[:, hc].astype(F32)).astype(o_ref.dtype)


def _attn_a_scratch(m_rows, n_keys):
    stacked = A_GROUP * m_rows
    return [pltpu.VMEM((A_KV_HEADS, stacked, n_keys), F32), pltpu.VMEM((A_KV_HEADS, stacked, HEAD_DIM), F32),
            pltpu.VMEM((A_KV_HEADS, stacked, 2 * HEAD_DIM), F32)]


def _attn_a_ctx(q, k, v, sg, sink, n_batch, seq):
    kvw = A_KV_HEADS * HEAD_DIM
    row = lambda b: (b, 0)
    return pl.pallas_call(
        functools.partial(_attn_a_kernel, latent=False, n_qblocks=1),
        out_shape=jax.ShapeDtypeStruct(q.shape, BF16),
        grid=(n_batch,),
        in_specs=[
            pl.BlockSpec(memory_space=pltpu.SMEM),
            pl.BlockSpec((seq, D_MODEL), row),
            pl.BlockSpec((seq, kvw), row),
            pl.BlockSpec((seq, kvw), row),
            pl.BlockSpec((seq, D_MODEL), row),
        ],
        out_specs=pl.BlockSpec((seq, D_MODEL), row),
        scratch_shapes=_attn_a_scratch(seq, seq),
        compiler_params=_params("arbitrary"),
        name="attn_a_ctx",
    )(sink, q, k, v, sg)


def _attn_a_lat(q, k, v, k_ctx, v_ctx, sg, sink, n_batch, seq):
    kvw = A_KV_HEADS * HEAD_DIM
    nb = seq // A_QBLOCK
    own = lambda b, i: (b * nb + i, 0)
    prev = lambda b, i: (b * nb + jnp.maximum(i - 1, 0), 0)
    nxt = lambda b, i: (b * nb + jnp.minimum(i + 1, nb - 1), 0)
    ctx = lambda b, i: (b, 0, 0)
    n_ctx = k_ctx.shape[1]
    return pl.pallas_call(
        functools.partial(_attn_a_kernel, latent=True, n_qblocks=nb),
        out_shape=jax.ShapeDtypeStruct(q.shape, BF16),
        grid=(n_batch, nb),
        in_specs=[
            pl.BlockSpec(memory_space=pltpu.SMEM),
            pl.BlockSpec((A_QBLOCK, D_MODEL), own),
            pl.BlockSpec((A_QBLOCK, kvw), prev),
            pl.BlockSpec((A_QBLOCK, kvw), own),
            pl.BlockSpec((A_QBLOCK, kvw), nxt),
            pl.BlockSpec((A_QBLOCK, kvw), prev),
            pl.BlockSpec((A_QBLOCK, kvw), own),
            pl.BlockSpec((A_QBLOCK, kvw), nxt),
            pl.BlockSpec((None, n_ctx, kvw), ctx),
            pl.BlockSpec((None, n_ctx, kvw), ctx),
            pl.BlockSpec((A_QBLOCK, D_MODEL), own),
        ],
        out_specs=pl.BlockSpec((A_QBLOCK, D_MODEL), own),
        scratch_shapes=_attn_a_scratch(A_QBLOCK, 3 * A_QBLOCK + n_ctx),
        compiler_params=_params("arbitrary", "arbitrary"),
        name="attn_a_lat",
    )(sink, q, k, k, k, v, v, v, k_ctx, v_ctx, sg)


def _attn_b_kernel(*refs, has_ctx, lambda_init, k_chunk):
    if has_ctx:
        (lam_ref, sub_ref, q_ref, k_ref, v_ref, kc_ref, vc_ref, sg_ref, o_ref,
         s_scr, sc_scr, m_scr, l_scr, acc_scr) = refs
    else:
        lam_ref, sub_ref, q_ref, k_ref, v_ref, sg_ref, o_ref, s_scr, m_scr, l_scr, acc_scr = refs
    n_chunks = k_ref.shape[0] // k_chunk
    halves = [slice(h * HEAD_DIM, (h + 1) * HEAD_DIM) for h in range(2)]

    m_scr[...] = jnp.full(m_scr.shape, NEG_INF, F32)

    def scores(c):
        rows = pl.ds(pl.multiple_of(c * k_chunk, k_chunk), k_chunk)
        k = k_ref[rows, :].astype(BF16)
        for h, hc in enumerate(halves):
            s = _nt_dot(q_ref[:, hc], k[:, hc])
            s_scr[h, c] = s
            m_scr[h] = jnp.maximum(m_scr[h], _lane_fold(s, jnp.maximum))

    if n_chunks == 1:
        scores(0)
    else:
        lax.fori_loop(0, n_chunks, lambda c, carry: (scores(c), carry)[1], 0, unroll=B_UNROLL)
    if has_ctx:
        kc = kc_ref[...].astype(BF16)
        for h, hc in enumerate(halves):
            s = _nt_dot(q_ref[:, hc], kc[:, hc])
            sc_scr[h] = s
            m_scr[h] = jnp.maximum(m_scr[h], _lane_fold(s, jnp.maximum))

    for h in range(2):
        m_scr[h] = jnp.broadcast_to(jnp.max(m_scr[h], axis=-1, keepdims=True), m_scr.shape[1:])
    l_scr[...] = jnp.zeros(l_scr.shape, F32)
    acc_scr[...] = jnp.zeros(acc_scr.shape, F32)

    def accumulate(h, s, v):
        m = m_scr[h]
        p = jnp.exp2(s - jnp.concatenate([m] * (s.shape[1] // HEAD_DIM), axis=1))
        l_scr[h] += _lane_fold(p, jnp.add)
        acc_scr[h] += jnp.dot(p.astype(BF16), v, preferred_element_type=F32)

    def weighted(c):
        rows = pl.ds(pl.multiple_of(c * k_chunk, k_chunk), k_chunk)
        v = v_ref[rows, :].astype(BF16)
        for h in range(2):
            accumulate(h, s_scr[h, c], v)

    if n_chunks == 1:
        weighted(0)
    else:
        lax.fori_loop(0, n_chunks, lambda c, carry: (weighted(c), carry)[1], 0, unroll=B_UNROLL)
    if has_ctx:
        vc = vc_ref[...].astype(BF16)
        for h in range(2):
            accumulate(h, sc_scr[h], vc)

    lam = lam_ref[...]
    lam_full = (jnp.exp(jnp.sum(lam[0:1] * lam[1:2], axis=-1, keepdims=True))
                - jnp.exp(jnp.sum(lam[2:3] * lam[3:4], axis=-1, keepdims=True)) + lambda_init)
    r0 = 1.0 / jnp.sum(l_scr[0], axis=-1, keepdims=True)
    r1 = lam_full / jnp.sum(l_scr[1], axis=-1, keepdims=True)
    o = acc_scr[0] * r0 - acc_scr[1] * r1
    ms = jnp.mean(o * o, axis=-1, keepdims=True)
    o = o * lax.rsqrt(ms + EPS) * (sub_ref[...] * (1.0 - lambda_init))
    o_ref[...] = (o * sg_ref[...].astype(F32)).astype(o_ref.dtype)


def _attn_b(q, k, v, sg, lam, subln, lambda_init, n_batch, seq, k_ctx=None, v_ctx=None):
    dv = 2 * HEAD_DIM
    tq = min(B_QBLOCK, seq)
    nq = seq // tq
    k_chunk = min(B_KCHUNK, seq)
    has_ctx = k_ctx is not None
    qmap = lambda b, h, i: (b * nq + i, h)
    kvmap = lambda b, h, i: (b, h)
    in_specs = [
        pl.BlockSpec((4, HEAD_DIM), lambda b, h, i: (0, 0)),
        pl.BlockSpec((1, dv), lambda b, h, i: (0, 0)),
        pl.BlockSpec((tq, dv), qmap),
        pl.BlockSpec((seq, dv), kvmap),
        pl.BlockSpec((seq, dv), kvmap),
    ]
    args = [lam, subln.reshape(1, dv), q, k, v]
    scratch = [pltpu.VMEM((2, seq // k_chunk, tq, k_chunk), F32)]
    if has_ctx:
        n_ctx = k_ctx.shape[1]
        in_specs += [pl.BlockSpec((None, n_ctx, dv), lambda b, h, i: (b, 0, h))] * 2
        args += [k_ctx, v_ctx]
        scratch.append(pltpu.VMEM((2, tq, n_ctx), F32))
    in_specs.append(pl.BlockSpec((tq, dv), qmap))
    args.append(sg)
    scratch += [pltpu.VMEM((2, tq, HEAD_DIM), F32), pltpu.VMEM((2, tq, HEAD_DIM), F32),
                pltpu.VMEM((2, tq, dv), F32)]
    return pl.pallas_call(
        functools.partial(_attn_b_kernel, has_ctx=has_ctx, lambda_init=lambda_init, k_chunk=k_chunk),
        out_shape=jax.ShapeDtypeStruct(q.shape, BF16),
        grid=(n_batch, B_HEADS, nq),
        in_specs=in_specs,
        out_specs=pl.BlockSpec((tq, dv), qmap),
        scratch_shapes=scratch,
        compiler_params=_params("arbitrary", "arbitrary", "arbitrary"),
        name="attn_b_lat" if has_ctx else "attn_b_ctx",
    )(*args)


def _attn_c_ctx_kernel(q_ref, k_ref, v_ref, sg_ref, o_ref):
    for h in range(C_HEADS):
        hc = slice(h * HEAD_DIM, (h + 1) * HEAD_DIM)
        s = _nt_dot(q_ref[:, hc], k_ref[:, hc].astype(BF16))
        o = _softmax_pv([s], [v_ref[:, hc].astype(BF16)])
        o_ref[:, hc] = (o * sg_ref[:, hc].astype(F32)).astype(o_ref.dtype)


def _attn_c_ctx(q, k, v, sg, n_batch, seq):
    spec = pl.BlockSpec((seq, D_MODEL), lambda b: (b, 0))
    return pl.pallas_call(
        _attn_c_ctx_kernel,
        out_shape=jax.ShapeDtypeStruct(q.shape, BF16),
        grid=(n_batch,),
        in_specs=[spec, spec, spec, spec],
        out_specs=spec,
        compiler_params=_params("arbitrary"),
        name="attn_c_ctx",
    )(q, k, v, sg)


def _na_geometry(rows):
    kh = min(NA_KH, rows)
    key_rows = C_ROWS + kh - 1
    key_rows = min(key_rows + key_rows % 2, rows)
    n_blocks = rows // C_ROWS
    cols = np.arange(GRID_W)
    col_start = np.clip(cols - NA_KW // 2, 0, GRID_W - NA_KW)
    col_ok = (cols[None, :] >= col_start[:, None]) & (cols[None, :] < col_start[:, None] + NA_KW)
    col_delta = cols[None, :] - cols[:, None] + NA_KW - 1
    col_onehot = (col_delta[None] == np.arange(2 * NA_KW - 1)[:, None, None]) & col_ok[None]
    starts, class_ids, classes, keys = [], [], [], {}
    for blk in range(n_blocks):
        r = blk * C_ROWS + np.arange(C_ROWS)
        rs = np.clip(r - kh // 2, 0, rows - kh)
        start = int(np.clip(rs[0], 0, rows - key_rows))
        key = (tuple(rs - r), start - blk * C_ROWS)
        if key not in keys:
            keys[key] = len(classes)
            kr = start + np.arange(key_rows)
            row_ok = (kr[None, :] >= rs[:, None]) & (kr[None, :] < rs[:, None] + kh)
            row_idx = kr[None, :] - r[:, None] + NA_KH - 1
            classes.append((row_ok, row_idx))
        starts.append(start)
        class_ids.append(keys[key])
    geometry = (col_ok, col_onehot.astype(np.float32), classes)
    return key_rows, np.asarray(starts, np.int32), np.asarray(class_ids, np.int32), geometry


def _na_bias_tables(rpb, geometry):
    col_ok, col_onehot, classes = geometry
    n_heads = rpb.shape[0]
    toep = jnp.einsum("hdx,xck->hdck", rpb * LOG2E, jnp.asarray(col_onehot),
                      precision=lax.Precision.HIGHEST)
    toep = jnp.where(jnp.asarray(col_ok)[None, None], toep, NEG_INF)
    masked = jnp.full((n_heads, GRID_W, GRID_W), NEG_INF, F32)
    tabs = []
    for row_ok, row_idx in classes:
        n_r, n_kr = row_ok.shape
        blocks = [toep[:, int(row_idx[a, u])] if row_ok[a, u] else masked
                  for a in range(n_r) for u in range(n_kr)]
        tab = jnp.stack(blocks, axis=1).reshape(n_heads, n_r, n_kr, GRID_W, GRID_W)
        tabs.append(tab.transpose(0, 1, 3, 2, 4).reshape(n_heads, n_r * GRID_W, n_kr * GRID_W))
    return jnp.stack(tabs, axis=0)


def _attn_c_lat_kernel(start_ref, cls_ref, q_ref, k_ref, v_ref, kc_ref, vc_ref, bias_ref, sg_ref, o_ref,
                       *, key_rows):
    blk = pl.program_id(2)
    n_keys = key_rows * GRID_W
    rows = pl.ds(pl.multiple_of(start_ref[blk] * GRID_W, GRID_W), n_keys)
    for h in range(C_HEAD_GROUP):
        hc = slice(h * HEAD_DIM, (h + 1) * HEAD_DIM)
        q = q_ref[:, hc]
        s_ctx = _nt_dot(q, kc_ref[:, hc].astype(BF16))
        s_loc = _nt_dot(q, k_ref[rows, hc]) + bias_ref[h]
        o = _softmax_pv([s_ctx, s_loc], [vc_ref[:, hc].astype(BF16), v_ref[rows, hc]])
        o_ref[:, hc] = (o * sg_ref[:, hc].astype(F32)).astype(o_ref.dtype)


def _attn_c_lat(q, k, v, k_ctx, v_ctx, sg, rpb, n_batch, seq):
    rows = seq // GRID_W
    key_rows, starts, class_ids, geometry = _na_geometry(rows)
    bias = _na_bias_tables(rpb, geometry)
    nblk = rows // C_ROWS
    tq = C_ROWS * GRID_W
    gw = C_HEAD_GROUP * HEAD_DIM
    n_ctx = k_ctx.shape[1]
    qmap = lambda b, g, i, st, cl: (b * nblk + i, g)
    kvmap = lambda b, g, i, st, cl: (b, g)
    ctxmap = lambda b, g, i, st, cl: (b, 0, g)
    grid_spec = pltpu.PrefetchScalarGridSpec(
        num_scalar_prefetch=2,
        grid=(n_batch, C_HEADS // C_HEAD_GROUP, nblk),
        in_specs=[
            pl.BlockSpec((tq, gw), qmap),
            pl.BlockSpec((seq, gw), kvmap),
            pl.BlockSpec((seq, gw), kvmap),
            pl.BlockSpec((None, n_ctx, gw), ctxmap),
            pl.BlockSpec((None, n_ctx, gw), ctxmap),
            pl.BlockSpec((None, C_HEAD_GROUP, tq, key_rows * GRID_W),
                         lambda b, g, i, st, cl: (cl[i], g, 0, 0)),
            pl.BlockSpec((tq, gw), qmap),
        ],
        out_specs=pl.BlockSpec((tq, gw), qmap),
    )
    return pl.pallas_call(
        functools.partial(_attn_c_lat_kernel, key_rows=key_rows),
        out_shape=jax.ShapeDtypeStruct(q.shape, BF16),
        grid_spec=grid_spec,
        compiler_params=_params("arbitrary", "arbitrary", "arbitrary"),
        name="attn_c_lat",
    )(jnp.asarray(starts), jnp.asarray(class_ids), q, k, v, k_ctx, v_ctx, bias, sg)


def _rope_tables(n_tokens):
    d4 = HEAD_DIM // 4
    t = jnp.arange(n_tokens)
    pos = jnp.stack([t // GRID_W, t % GRID_W], axis=-1).astype(F32)
    inv_freq = ROPE_BASE ** (-jnp.arange(d4, dtype=F32) / d4)
    ang = pos[:, :, None] * inv_freq
    cos, sin = jnp.cos(ang), jnp.sin(ang)
    zero = jnp.zeros_like(sin[:, 0])
    cos_full = jnp.concatenate([cos[:, 0], cos[:, 0], cos[:, 1], cos[:, 1]], axis=-1)
    sin_up = jnp.concatenate([-sin[:, 0], zero, -sin[:, 1], zero], axis=-1)
    sin_dn = jnp.concatenate([zero, sin[:, 0], zero, sin[:, 1]], axis=-1)
    return cos_full, sin_up, sin_dn


def kernel(x_prompt, x_sample, cache_a_k, cache_a_v, cache_b_k, cache_b_v, cache_c_k, cache_c_v, c, c_ctx,
           ln_g, ada_w, ada_b, w_out, qn_g, kn_g, w_in_a, sink_a, w_in_b, lam_b, subln_b, w_in_c, rpb_c):
    n_p, seq_p, d = x_prompt.shape
    n_s, seq_s, _ = x_sample.shape
    n_ctx = cache_a_k.shape[2]
    xp = x_prompt.reshape(n_p * seq_p, d)
    xs = x_sample.reshape(n_s * seq_s, d)
    rope_tabs = _rope_tables(seq_s)

    pad = (-(n_s + 1)) % 8
    cvec = jnp.concatenate([c, c_ctx[None, :], jnp.zeros((pad, d), F32)], axis=0)
    mod = _ada_modulation(cvec, ada_w, ada_b)
    mod = mod.reshape(DEPTH, mod.shape[1], 3, 1, d)

    new_kv = {0: ([], []), 1: ([], []), 2: ([], [])}
    for l in range(DEPTH):
        kind, j = l % N_MIXERS, l // N_MIXERS
        sh_s, sc_s, gt_s = (mod[l, :n_s, t] for t in range(3))
        sh_p, sc_p, gt_p = (mod[l, n_s:n_s + 1, t] for t in range(3))
        w_o = w_out[l].astype(BF16)
        if kind == 0:
            kvw = A_KV_HEADS * HEAD_DIM
            w_in, widths, tn = w_in_a[j].astype(BF16), (d, kvw, kvw, d), 512
        elif kind == 1:
            w_in, widths, tn = w_in_b[j].astype(BF16), (d, d, d, d), 512
        else:
            w_in, widths, tn = w_in_c[j].astype(BF16), (d, d, d, d), 512
        rope = rope_tabs if kind != 2 else None
        qp, kp, vp, gp = _in_projection(xp, sh_p, sc_p, ln_g[l], w_in, qn_g[l], kn_g[l], widths, None, F32, tn)
        qs, ks, vs, gs = _in_projection(xs, sh_s, sc_s, ln_g[l], w_in, qn_g[l], kn_g[l], widths, rope, BF16, tn)
        if kind == 0:
            kc = cache_a_k[:, j].reshape(n_s, n_ctx, kvw)
            vc = cache_a_v[:, j].reshape(n_s, n_ctx, kvw)
            op = _attn_a_ctx(qp, kp, vp, gp, sink_a[j], n_p, seq_p)
            os_ = _attn_a_lat(qs, ks, vs, kc, vc, gs, sink_a[j], n_s, seq_s)
            new_kv[0][0].append(kp.reshape(n_p, seq_p, A_KV_HEADS, HEAD_DIM))
            new_kv[0][1].append(vp.reshape(n_p, seq_p, A_KV_HEADS, HEAD_DIM))
        elif kind == 1:
            lambda_init = 0.8 - 0.6 * math.exp(-0.3 * l)
            kc = cache_b_k[:, j].reshape(n_s, n_ctx, d)
            vc = cache_b_v[:, j].reshape(n_s, n_ctx, d)
            op = _attn_b(qp, kp, vp, gp, lam_b[j], subln_b[j], lambda_init, n_p, seq_p)
            os_ = _attn_b(qs, ks, vs, gs, lam_b[j], subln_b[j], lambda_init, n_s, seq_s, kc, vc)
            new_kv[1][0].append(kp.reshape(n_p, seq_p, B_HEADS, 2, HEAD_DIM))
            new_kv[1][1].append(vp.reshape(n_p, seq_p, B_HEADS, 2 * HEAD_DIM))
        else:
            kc = cache_c_k[:, j].reshape(n_s, n_ctx, d)
            vc = cache_c_v[:, j].reshape(n_s, n_ctx, d)
            op = _attn_c_ctx(qp, kp, vp, gp, n_p, seq_p)
            os_ = _attn_c_lat(qs, ks, vs, kc, vc, gs, rpb_c[j], n_s, seq_s)
            new_kv[2][0].append(kp.reshape(n_p, seq_p, C_HEADS, HEAD_DIM))
            new_kv[2][1].append(vp.reshape(n_p, seq_p, C_HEADS, HEAD_DIM))
        xp = _out_projection(op, w_o, xp, gt_p)
        xs = _out_projection(os_, w_o, xs, gt_s)

    outs = [xp.reshape(x_prompt.shape), xs.reshape(x_sample.shape)]
    for kind in range(3):
        outs.append(jnp.stack(new_kv[kind][0], axis=1))
        outs.append(jnp.stack(new_kv[kind][1], axis=1))
    return tuple(outs)
```

```python
import functools
import math

import numpy as np
import jax
import jax.numpy as jnp
from jax import lax
from jax.experimental import pallas as pl
from jax.experimental.pallas import tpu as pltpu

D_MODEL = 2048
DEPTH = 4
GRID_W = 64
HEAD_DIM = 128
ROPE_BASE = 10000.0
NEG_INF = -1e30
N_MIXERS = 3
A_HEADS = D_MODEL // HEAD_DIM
A_KV_HEADS = A_HEADS // 4
A_GROUP = A_HEADS // A_KV_HEADS
WINDOW = 128
B_HEADS = D_MODEL // (2 * HEAD_DIM)
C_HEADS = D_MODEL // HEAD_DIM
NA_KH = 8
NA_KW = 16
SCALE = HEAD_DIM ** -0.5
LOG2E = math.log2(math.e)
Q_SCALE = SCALE * LOG2E
EPS = 1e-6

BF16 = jnp.bfloat16
F32 = jnp.float32

VMEM_LIMIT_BYTES = 56 * 1024 * 1024
ROW_TILE = 1024
NORM_CHUNK = 128
PROJ_SUB = 128
A_QBLOCK = 128
B_QBLOCK = 1024
B_KCHUNK = 512
B_UNROLL = 4
C_ROWS = 4
C_HEAD_GROUP = 4


def _nt_dot(a, b):
    return lax.dot_general(a, b, (((1,), (1,)), ((), ())), preferred_element_type=F32)


def _params(*semantics):
    return pltpu.CompilerParams(dimension_semantics=semantics, vmem_limit_bytes=VMEM_LIMIT_BYTES)


def _ada_kernel(c_ref, w_ref, b_ref, o_ref):
    c = c_ref[...]
    a = (c * jax.nn.sigmoid(c)).astype(BF16)
    o_ref[...] = jnp.dot(a, w_ref[...].astype(BF16), preferred_element_type=F32) + b_ref[...]


def _ada_modulation(cvec, ada_w, ada_b):
    rows = cvec.shape[0]
    tn = 768
    return pl.pallas_call(
        _ada_kernel,
        out_shape=jax.ShapeDtypeStruct((DEPTH, rows, 3 * D_MODEL), F32),
        grid=(DEPTH, 3 * D_MODEL // tn),
        in_specs=[
            pl.BlockSpec((rows, D_MODEL), lambda l, j: (0, 0)),
            pl.BlockSpec((None, D_MODEL, tn), lambda l, j: (l, 0, j)),
            pl.BlockSpec((None, 1, tn), lambda l, j: (l, 0, j)),
        ],
        out_specs=pl.BlockSpec((None, rows, tn), lambda l, j: (l, 0, j)),
        compiler_params=_params("arbitrary", "arbitrary"),
        name="ada_modulation",
    )(cvec, ada_w, ada_b.reshape(DEPTH, 1, 3 * D_MODEL))


def _inproj_kernel(*refs, n_q, n_k, n_v, rope, heads_per_tile):
    x_ref, sh_ref, sc_ref, lng_ref, w_ref, qg_ref, kg_ref = refs[:7]
    if rope:
        cos_ref, sa_ref, sb_ref = refs[7:10]
    q_out, k_out, v_out, g_out, h_scr = refs[-5:]
    j = pl.program_id(1)

    @pl.when(j == 0)
    def _():
        gain = lng_ref[...] * (1.0 + sc_ref[...])

        def body(r, carry):
            rows = pl.ds(pl.multiple_of(r * NORM_CHUNK, NORM_CHUNK), NORM_CHUNK)
            x = x_ref[rows, :]
            ms = jnp.mean(x * x, axis=-1, keepdims=True)
            h_scr[rows, :] = (x * lax.rsqrt(ms + EPS) * gain + sh_ref[...]).astype(BF16)
            return carry
        lax.fori_loop(0, x_ref.shape[0] // NORM_CHUNK, body, 0)

    def normed(acc, rows, gain_ref, scale):
        outs = []
        for t in range(heads_per_tile):
            y = acc[:, t * HEAD_DIM:(t + 1) * HEAD_DIM]
            ms = jnp.mean(y * y, axis=-1, keepdims=True)
            y = y * lax.rsqrt(ms + EPS) * (gain_ref[...] * scale)
            if rope:
                y = (y * cos_ref[rows, :] + pltpu.roll(y, 96, 1) * sa_ref[rows, :]
                     + pltpu.roll(y, 32, 1) * sb_ref[rows, :])
            outs.append(y)
        return jnp.concatenate(outs, axis=-1)

    def project(out_ref, epilogue):
        for r in range(h_scr.shape[0] // PROJ_SUB):
            rows = slice(r * PROJ_SUB, (r + 1) * PROJ_SUB)
            acc = jnp.dot(h_scr[rows, :], w_ref[...], preferred_element_type=F32)
            out_ref[rows, :] = epilogue(acc, rows).astype(out_ref.dtype)

    @pl.when(j < n_q)
    def _():
        project(q_out, lambda acc, rows: normed(acc, rows, qg_ref, Q_SCALE))

    @pl.when((j >= n_q) & (j < n_q + n_k))
    def _():
        project(k_out, lambda acc, rows: normed(acc, rows, kg_ref, 1.0))

    @pl.when((j >= n_q + n_k) & (j < n_q + n_k + n_v))
    def _():
        project(v_out, lambda acc, rows: acc)

    @pl.when(j >= n_q + n_k + n_v)
    def _():
        project(g_out, lambda acc, rows: acc * jax.nn.sigmoid(acc))


def _in_projection(x, shift, scale, ln_g, w, qg, kg, widths, rope_tabs, kv_dtype, tn):
    t_rows = x.shape[0]
    qw, kw, vw, gw = widths
    n_q, n_k, n_v, n_g = qw // tn, kw // tn, vw // tn, gw // tn
    tm = min(ROW_TILE, t_rows)
    rows_per_mod = t_rows // shift.shape[0]
    assert t_rows % tm == 0 and rows_per_mod % tm == 0
    rope = rope_tabs is not None

    def mod_map(i, j):
        return ((i * tm) // rows_per_mod, 0, 0)

    in_specs = [
        pl.BlockSpec((tm, D_MODEL), lambda i, j: (i, 0)),
        pl.BlockSpec((None, 1, D_MODEL), mod_map),
        pl.BlockSpec((None, 1, D_MODEL), mod_map),
        pl.BlockSpec((1, D_MODEL), lambda i, j: (0, 0)),
        pl.BlockSpec((D_MODEL, tn), lambda i, j: (0, j)),
        pl.BlockSpec((1, HEAD_DIM), lambda i, j: (0, 0)),
        pl.BlockSpec((1, HEAD_DIM), lambda i, j: (0, 0)),
    ]
    args = [x, shift, scale, ln_g.reshape(1, D_MODEL), w, qg.reshape(1, HEAD_DIM), kg.reshape(1, HEAD_DIM)]
    if rope:
        n_pos_tiles = rope_tabs[0].shape[0] // tm
        for tab in rope_tabs:
            in_specs.append(pl.BlockSpec((tm, HEAD_DIM), lambda i, j: (i % n_pos_tiles, 0)))
            args.append(tab)
    out_specs = [
        pl.BlockSpec((tm, tn), lambda i, j: (i, jnp.minimum(j, n_q - 1))),
        pl.BlockSpec((tm, tn), lambda i, j: (i, jnp.clip(j - n_q, 0, n_k - 1))),
        pl.BlockSpec((tm, tn), lambda i, j: (i, jnp.clip(j - n_q - n_k, 0, n_v - 1))),
        pl.BlockSpec((tm, tn), lambda i, j: (i, jnp.clip(j - n_q - n_k - n_v, 0, n_g - 1))),
    ]
    out_shape = [
        jax.ShapeDtypeStruct((t_rows, qw), BF16),
        jax.ShapeDtypeStruct((t_rows, kw), kv_dtype),
        jax.ShapeDtypeStruct((t_rows, vw), kv_dtype),
        jax.ShapeDtypeStruct((t_rows, gw), BF16),
    ]
    kern = functools.partial(_inproj_kernel, n_q=n_q, n_k=n_k, n_v=n_v, rope=rope,
                             heads_per_tile=tn // HEAD_DIM)
    return pl.pallas_call(
        kern,
        out_shape=out_shape,
        grid=(t_rows // tm, n_q + n_k + n_v + n_g),
        in_specs=in_specs,
        out_specs=out_specs,
        scratch_shapes=[pltpu.VMEM((tm, D_MODEL), BF16)],
        compiler_params=_params("arbitrary", "arbitrary"),
        name="in_projection",
    )(*args)


def _outproj_kernel(o_ref, w_ref, x_ref, gt_ref, y_ref):
    y = jnp.dot(o_ref[...], w_ref[...], preferred_element_type=F32)
    y_ref[...] = x_ref[...] + gt_ref[...] * y


def _out_projection(o, w, x, gate):
    t_rows = x.shape[0]
    tm, tn = min(ROW_TILE, t_rows), 1024
    rows_per_mod = t_rows // gate.shape[0]
    assert t_rows % tm == 0 and rows_per_mod % tm == 0
    return pl.pallas_call(
        _outproj_kernel,
        out_shape=jax.ShapeDtypeStruct((t_rows, D_MODEL), F32),
        grid=(t_rows // tm, D_MODEL // tn),
        in_specs=[
            pl.BlockSpec((tm, D_MODEL), lambda i, j: (i, 0)),
            pl.BlockSpec((D_MODEL, tn), lambda i, j: (0, j)),
            pl.BlockSpec((tm, tn), lambda i, j: (i, j)),
            pl.BlockSpec((None, 1, tn), lambda i, j: ((i * tm) // rows_per_mod, 0, j)),
        ],
        out_specs=pl.BlockSpec((tm, tn), lambda i, j: (i, j)),
        compiler_params=_params("arbitrary", "arbitrary"),
        name="out_projection",
    )(o, w, x, gate)


def _lane_fold(x, op):
    r = x[:, 0:HEAD_DIM]
    for t in range(1, x.shape[1] // HEAD_DIM):
        r = op(r, x[:, t * HEAD_DIM:(t + 1) * HEAD_DIM])
    return r


def _with_ones(v):
    return jnp.concatenate([v, jnp.ones(v.shape, v.dtype)], axis=1)


def _softmax_pv(scores, values):
    m128 = _lane_fold(scores[0], jnp.maximum)
    for s in scores[1:]:
        m128 = jnp.maximum(m128, _lane_fold(s, jnp.maximum))
    m = jnp.broadcast_to(jnp.max(m128, axis=-1, keepdims=True), m128.shape)
    acc = None
    for s, v in zip(scores, values):
        p = jnp.exp2(s - jnp.concatenate([m] * (s.shape[1] // HEAD_DIM), axis=1)).astype(BF16)
        pv = jnp.dot(p, _with_ones(v), preferred_element_type=F32)
        acc = pv if acc is None else acc + pv
    return acc[:, :HEAD_DIM] * (1.0 / acc[:, HEAD_DIM:])


def _attn_a_kernel(*refs, latent, n_qblocks):
    if latent:
        (sink_ref, q_ref, kp_ref, ko_ref, kn_ref, vp_ref, vo_ref, vn_ref,
         kc_ref, vc_ref, sg_ref, o_ref, s_scr, m_scr, acc_scr) = refs
    else:
        sink_ref, q_ref, k_ref, v_ref, sg_ref, o_ref, s_scr, m_scr, acc_scr = refs
    m_rows = q_ref.shape[0]
    n_keys = s_scr.shape[-1]
    n_loc = 3 * A_QBLOCK
    kv_cols = [slice(kv * HEAD_DIM, (kv + 1) * HEAD_DIM) for kv in range(A_KV_HEADS)]
    kv_heads = [[kv * A_GROUP + g for g in range(A_GROUP)] for kv in range(A_KV_HEADS)]

    if latent:
        blk = pl.program_id(1)
        qpos = lax.broadcasted_iota(jnp.int32, (A_QBLOCK, n_loc), 0)
        col = lax.broadcasted_iota(jnp.int32, (A_QBLOCK, n_loc), 1)
        rel = col - A_QBLOCK - qpos
        kpos = (blk - 1) * A_QBLOCK + col
        valid = (jnp.abs(rel) <= WINDOW) & (kpos >= 0) & (kpos < n_qblocks * A_QBLOCK)
        window_bias = jnp.concatenate([jnp.where(valid, 0.0, NEG_INF)] * A_GROUP, axis=0)

    for kv, cols in enumerate(kv_cols):
        q = jnp.concatenate([q_ref[:, h * HEAD_DIM:(h + 1) * HEAD_DIM] for h in kv_heads[kv]], axis=0)
        if latent:
            k_loc = jnp.concatenate([kp_ref[:, cols], ko_ref[:, cols], kn_ref[:, cols]], axis=0)
            s_loc = _nt_dot(q, k_loc) + window_bias
            s_ctx = _nt_dot(q, kc_ref[:, cols].astype(BF16))
            s_scr[kv, :, 0:n_loc] = s_loc
            s_scr[kv, :, n_loc:n_keys] = s_ctx
            m_scr[kv] = jnp.maximum(_lane_fold(s_loc, jnp.maximum), _lane_fold(s_ctx, jnp.maximum))
        else:
            s = _nt_dot(q, k_ref[:, cols].astype(BF16))
            s_scr[kv] = s
            m_scr[kv] = _lane_fold(s, jnp.maximum)

    def sink_lanes(kv):
        return jnp.concatenate([jnp.full((m_rows, HEAD_DIM), sink_ref[h] * LOG2E, F32) for h in kv_heads[kv]], axis=0)

    for kv in range(A_KV_HEADS):
        m = jnp.max(m_scr[kv], axis=-1, keepdims=True)
        m_scr[kv] = jnp.maximum(jnp.broadcast_to(m, m_scr.shape[1:]), sink_lanes(kv))

    for kv, cols in enumerate(kv_cols):
        p = jnp.exp2(s_scr[kv] - jnp.concatenate([m_scr[kv]] * (n_keys // HEAD_DIM), axis=1)).astype(BF16)
        if latent:
            v_loc = jnp.concatenate([vp_ref[:, cols], vo_ref[:, cols], vn_ref[:, cols]], axis=0)
            acc_scr[kv] = (jnp.dot(p[:, 0:n_loc], _with_ones(v_loc), preferred_element_type=F32)
                           + jnp.dot(p[:, n_loc:n_keys], _with_ones(vc_ref[:, cols].astype(BF16)),
                                     preferred_element_type=F32))
        else:
            acc_scr[kv] = jnp.dot(p, _with_ones(v_ref[:, cols].astype(BF16)), preferred_element_type=F32)

    for kv in range(A_KV_HEADS):
        acc = acc_scr[kv]
        l = acc[:, HEAD_DIM:] + jnp.exp2(sink_lanes(kv) - m_scr[kv])
        o = acc[:, :HEAD_DIM] * (1.0 / l)
        for g, h in enumerate(kv_heads[kv]):
            hc = slice(h * HEAD_DIM, (h + 1) * HEAD_DIM)
            o_ref[:, hc] = (o[g * m_rows:(g + 1) * m_rows] * sg_ref[:, hc].astype(F32)).astype(o_ref.dtype)


def _attn_a_scratch(m_rows, n_keys):
    stacked = A_GROUP * m_rows
    return [pltpu.VMEM((A_KV_HEADS, stacked, n_keys), F32), pltpu.VMEM((A_KV_HEADS, stacked, HEAD_DIM), F32),
            pltpu.VMEM((A_KV_HEADS, stacked, 2 * HEAD_DIM), F32)]


def _attn_a_ctx(q, k, v, sg, sink, n_batch, seq):
    kvw = A_KV_HEADS * HEAD_DIM
    row = lambda b: (b, 0)
    return pl.pallas_call(
        functools.partial(_attn_a_kernel, latent=False, n_qblocks=1),
        out_shape=jax.ShapeDtypeStruct(q.shape, BF16),
        grid=(n_batch,),
        in_specs=[
            pl.BlockSpec(memory_space=pltpu.SMEM),
            pl.BlockSpec((seq, D_MODEL), row),
            pl.BlockSpec((seq, kvw), row),
            pl.BlockSpec((seq, kvw), row),
            pl.BlockSpec((seq, D_MODEL), row),
        ],
        out_specs=pl.BlockSpec((seq, D_MODEL), row),
        scratch_shapes=_attn_a_scratch(seq, seq),
        compiler_params=_params("arbitrary"),
        name="attn_a_ctx",
    )(sink, q, k, v, sg)


def _attn_a_lat(q, k, v, k_ctx, v_ctx, sg, sink, n_batch, seq):
    kvw = A_KV_HEADS * HEAD_DIM
    nb = seq // A_QBLOCK
    own = lambda b, i: (b * nb + i, 0)
    prev = lambda b, i: (b * nb + jnp.maximum(i - 1, 0), 0)
    nxt = lambda b, i: (b * nb + jnp.minimum(i + 1, nb - 1), 0)
    ctx = lambda b, i: (b, 0, 0)
    n_ctx = k_ctx.shape[1]
    return pl.pallas_call(
        functools.partial(_attn_a_kernel, latent=True, n_qblocks=nb),
        out_shape=jax.ShapeDtypeStruct(q.shape, BF16),
        grid=(n_batch, nb),
        in_specs=[
            pl.BlockSpec(memory_space=pltpu.SMEM),
            pl.BlockSpec((A_QBLOCK, D_MODEL), own),
            pl.BlockSpec((A_QBLOCK, kvw), prev),
            pl.BlockSpec((A_QBLOCK, kvw), own),
            pl.BlockSpec((A_QBLOCK, kvw), nxt),
            pl.BlockSpec((A_QBLOCK, kvw), prev),
            pl.BlockSpec((A_QBLOCK, kvw), own),
            pl.BlockSpec((A_QBLOCK, kvw), nxt),
            pl.BlockSpec((None, n_ctx, kvw), ctx),
            pl.BlockSpec((None, n_ctx, kvw), ctx),
            pl.BlockSpec((A_QBLOCK, D_MODEL), own),
        ],
        out_specs=pl.BlockSpec((A_QBLOCK, D_MODEL), own),
        scratch_shapes=_attn_a_scratch(A_QBLOCK, 3 * A_QBLOCK + n_ctx),
        compiler_params=_params("arbitrary", "arbitrary"),
        name="attn_a_lat",
    )(sink, q, k, k, k, v, v, v, k_ctx, v_ctx, sg)


def _attn_b_kernel(*refs, has_ctx, lambda_init, k_chunk):
    if has_ctx:
        (lam_ref, sub_ref, q_ref, k_ref, v_ref, kc_ref, vc_ref, sg_ref, o_ref,
         s_scr, sc_scr, m_scr, l_scr, acc_scr) = refs
    else:
        lam_ref, sub_ref, q_ref, k_ref, v_ref, sg_ref, o_ref, s_scr, m_scr, l_scr, acc_scr = refs
    n_chunks = k_ref.shape[0] // k_chunk
    halves = [slice(h * HEAD_DIM, (h + 1) * HEAD_DIM) for h in range(2)]

    m_scr[...] = jnp.full(m_scr.shape, NEG_INF, F32)

    def scores(c):
        rows = pl.ds(pl.multiple_of(c * k_chunk, k_chunk), k_chunk)
        k = k_ref[rows, :].astype(BF16)
        for h, hc in enumerate(halves):
            s = _nt_dot(q_ref[:, hc], k[:, hc])
            s_scr[h, c] = s
            m_scr[h] = jnp.maximum(m_scr[h], _lane_fold(s, jnp.maximum))

    if n_chunks == 1:
        scores(0)
    else:
        lax.fori_loop(0, n_chunks, lambda c, carry: (scores(c), carry)[1], 0, unroll=B_UNROLL)
    if has_ctx:
        kc = kc_ref[...].astype(BF16)
        for h, hc in enumerate(halves):
            s = _nt_dot(q_ref[:, hc], kc[:, hc])
            sc_scr[h] = s
            m_scr[h] = jnp.maximum(m_scr[h], _lane_fold(s, jnp.maximum))

    for h in range(2):
        m_scr[h] = jnp.broadcast_to(jnp.max(m_scr[h], axis=-1, keepdims=True), m_scr.shape[1:])
    l_scr[...] = jnp.zeros(l_scr.shape, F32)
    acc_scr[...] = jnp.zeros(acc_scr.shape, F32)

    def accumulate(h, s, v):
        m = m_scr[h]
        p = jnp.exp2(s - jnp.concatenate([m] * (s.shape[1] // HEAD_DIM), axis=1))
        l_scr[h] += _lane_fold(p, jnp.add)
        acc_scr[h] += jnp.dot(p.astype(BF16), v, preferred_element_type=F32)

    def weighted(c):
        rows = pl.ds(pl.multiple_of(c * k_chunk, k_chunk), k_chunk)
        v = v_ref[rows, :].astype(BF16)
        for h in range(2):
            accumulate(h, s_scr[h, c], v)

    if n_chunks == 1:
        weighted(0)
    else:
        lax.fori_loop(0, n_chunks, lambda c, carry: (weighted(c), carry)[1], 0, unroll=B_UNROLL)
    if has_ctx:
        vc = vc_ref[...].astype(BF16)
        for h in range(2):
            accumulate(h, sc_scr[h], vc)

    lam = lam_ref[...]
    lam_full = (jnp.exp(jnp.sum(lam[0:1] * lam[1:2], axis=-1, keepdims=True))
                - jnp.exp(jnp.sum(lam[2:3] * lam[3:4], axis=-1, keepdims=True)) + lambda_init)
    r0 = 1.0 / jnp.sum(l_scr[0], axis=-1, keepdims=True)
    r1 = lam_full / jnp.sum(l_scr[1], axis=-1, keepdims=True)
    o = acc_scr[0] * r0 - acc_scr[1] * r1
    ms = jnp.mean(o * o, axis=-1, keepdims=True)
    o = o * lax.rsqrt(ms + EPS) * (sub_ref[...] * (1.0 - lambda_init))
    o_ref[...] = (o * sg_ref[...].astype(F32)).astype(o_ref.dtype)


def _attn_b(q, k, v, sg, lam, subln, lambda_init, n_batch, seq, k_ctx=None, v_ctx=None):
    dv = 2 * HEAD_DIM
    tq = min(B_QBLOCK, seq)
    nq = seq // tq
    k_chunk = min(B_KCHUNK, seq)
    has_ctx = k_ctx is not None
    qmap = lambda b, h, i: (b * nq + i, h)
    kvmap = lambda b, h, i: (b, h)
    in_specs = [
        pl.BlockSpec((4, HEAD_DIM), lambda b, h, i: (0, 0)),
        pl.BlockSpec((1, dv), lambda b, h, i: (0, 0)),
        pl.BlockSpec((tq, dv), qmap),
        pl.BlockSpec((seq, dv), kvmap),
        pl.BlockSpec((seq, dv), kvmap),
    ]
    args = [lam, subln.reshape(1, dv), q, k, v]
    scratch = [pltpu.VMEM((2, seq // k_chunk, tq, k_chunk), F32)]
    if has_ctx:
        n_ctx = k_ctx.shape[1]
        in_specs += [pl.BlockSpec((None, n_ctx, dv), lambda b, h, i: (b, 0, h))] * 2
        args += [k_ctx, v_ctx]
        scratch.append(pltpu.VMEM((2, tq, n_ctx), F32))
    in_specs.append(pl.BlockSpec((tq, dv), qmap))
    args.append(sg)
    scratch += [pltpu.VMEM((2, tq, HEAD_DIM), F32), pltpu.VMEM((2, tq, HEAD_DIM), F32),
                pltpu.VMEM((2, tq, dv), F32)]
    return pl.pallas_call(
        functools.partial(_attn_b_kernel, has_ctx=has_ctx, lambda_init=lambda_init, k_chunk=k_chunk),
        out_shape=jax.ShapeDtypeStruct(q.shape, BF16),
        grid=(n_batch, B_HEADS, nq),
        in_specs=in_specs,
        out_specs=pl.BlockSpec((tq, dv), qmap),
        scratch_shapes=scratch,
        compiler_params=_params("arbitrary", "arbitrary", "arbitrary"),
        name="attn_b_lat" if has_ctx else "attn_b_ctx",
    )(*args)


def _attn_c_ctx_kernel(q_ref, k_ref, v_ref, sg_ref, o_ref):
    for h in range(C_HEADS):
        hc = slice(h * HEAD_DIM, (h + 1) * HEAD_DIM)
        s = _nt_dot(q_ref[:, hc], k_ref[:, hc].astype(BF16))
        o = _softmax_pv([s], [v_ref[:, hc].astype(BF16)])
        o_ref[:, hc] = (o * sg_ref[:, hc].astype(F32)).astype(o_ref.dtype)


def _attn_c_ctx(q, k, v, sg, n_batch, seq):
    spec = pl.BlockSpec((seq, D_MODEL), lambda b: (b, 0))
    return pl.pallas_call(
        _attn_c_ctx_kernel,
        out_shape=jax.ShapeDtypeStruct(q.shape, BF16),
        grid=(n_batch,),
        in_specs=[spec, spec, spec, spec],
        out_specs=spec,
        compiler_params=_params("arbitrary"),
        name="attn_c_ctx",
    )(q, k, v, sg)


def _na_geometry(rows):
    kh = min(NA_KH, rows)
    key_rows = C_ROWS + kh - 1
    key_rows = min(key_rows + key_rows % 2, rows)
    n_blocks = rows // C_ROWS
    cols = np.arange(GRID_W)
    col_start = np.clip(cols - NA_KW // 2, 0, GRID_W - NA_KW)
    col_ok = (cols[None, :] >= col_start[:, None]) & (cols[None, :] < col_start[:, None] + NA_KW)
    col_delta = cols[None, :] - cols[:, None] + NA_KW - 1
    col_onehot = (col_delta[None] == np.arange(2 * NA_KW - 1)[:, None, None]) & col_ok[None]
    starts, class_ids, classes, keys = [], [], [], {}
    for blk in range(n_blocks):
        r = blk * C_ROWS + np.arange(C_ROWS)
        rs = np.clip(r - kh // 2, 0, rows - kh)
        start = int(np.clip(rs[0], 0, rows - key_rows))
        key = (tuple(rs - r), start - blk * C_ROWS)
        if key not in keys:
            keys[key] = len(classes)
            kr = start + np.arange(key_rows)
            row_ok = (kr[None, :] >= rs[:, None]) & (kr[None, :] < rs[:, None] + kh)
            row_idx = kr[None, :] - r[:, None] + NA_KH - 1
            classes.append((row_ok, row_idx))
        starts.append(start)
        class_ids.append(keys[key])
    geometry = (col_ok, col_onehot.astype(np.float32), classes)
    return key_rows, np.asarray(starts, np.int32), np.asarray(class_ids, np.int32), geometry


def _na_bias_tables(rpb, geometry):
    col_ok, col_onehot, classes = geometry
    n_heads = rpb.shape[0]
    toep = jnp.einsum("hdx,xck->hdck", rpb * LOG2E, jnp.asarray(col_onehot),
                      precision=lax.Precision.HIGHEST)
    toep = jnp.where(jnp.asarray(col_ok)[None, None], toep, NEG_INF)
    masked = jnp.full((n_heads, GRID_W, GRID_W), NEG_INF, F32)
    tabs = []
    for row_ok, row_idx in classes:
        n_r, n_kr = row_ok.shape
        blocks = [toep[:, int(row_idx[a, u])] if row_ok[a, u] else masked
                  for a in range(n_r) for u in range(n_kr)]
        tab = jnp.stack(blocks, axis=1).reshape(n_heads, n_r, n_kr, GRID_W, GRID_W)
        tabs.append(tab.transpose(0, 1, 3, 2, 4).reshape(n_heads, n_r * GRID_W, n_kr * GRID_W))
    return jnp.stack(tabs, axis=0)


def _attn_c_lat_kernel(start_ref, cls_ref, q_ref, k_ref, v_ref, kc_ref, vc_ref, bias_ref, sg_ref, o_ref,
                       s_scr, m_scr, acc_scr, *, key_rows):
    blk = pl.program_id(2)
    n_loc = key_rows * GRID_W
    n_keys = s_scr.shape[-1]
    rows = pl.ds(pl.multiple_of(start_ref[blk] * GRID_W, GRID_W), n_loc)
    head_cols = [slice(h * HEAD_DIM, (h + 1) * HEAD_DIM) for h in range(C_HEAD_GROUP)]

    for h, hc in enumerate(head_cols):
        q = q_ref[:, hc]
        s_loc = _nt_dot(q, k_ref[rows, hc]) + bias_ref[h]
        s_ctx = _nt_dot(q, kc_ref[:, hc].astype(BF16))
        s_scr[h, :, 0:n_loc] = s_loc
        s_scr[h, :, n_loc:n_keys] = s_ctx
        m_scr[h] = jnp.maximum(_lane_fold(s_loc, jnp.maximum), _lane_fold(s_ctx, jnp.maximum))

    for h in range(C_HEAD_GROUP):
        m_scr[h] = jnp.broadcast_to(jnp.max(m_scr[h], axis=-1, keepdims=True), m_scr.shape[1:])

    for h, hc in enumerate(head_cols):
        p = jnp.exp2(s_scr[h] - jnp.concatenate([m_scr[h]] * (n_keys // HEAD_DIM), axis=1)).astype(BF16)
        acc_scr[h] = (jnp.dot(p[:, 0:n_loc], _with_ones(v_ref[rows, hc]), preferred_element_type=F32)
                      + jnp.dot(p[:, n_loc:n_keys], _with_ones(vc_ref[:, hc].astype(BF16)),
                                preferred_element_type=F32))

    for h, hc in enumerate(head_cols):
        acc = acc_scr[h]
        o = acc[:, :HEAD_DIM] * (1.0 / acc[:, HEAD_DIM:])
        o_ref[:, hc] = (o * sg_ref[:, hc].astype(F32)).astype(o_ref.dtype)


def _attn_c_lat(q, k, v, k_ctx, v_ctx, sg, rpb, n_batch, seq):
    rows = seq // GRID_W
    key_rows, starts, class_ids, geometry = _na_geometry(rows)
    bias = _na_bias_tables(rpb, geometry)
    nblk = rows // C_ROWS
    tq = C_ROWS * GRID_W
    gw = C_HEAD_GROUP * HEAD_DIM
    n_ctx = k_ctx.shape[1]
    qmap = lambda b, g, i, st, cl: (b * nblk + i, g)
    kvmap = lambda b, g, i, st, cl: (b, g)
    ctxmap = lambda b, g, i, st, cl: (b, 0, g)
    grid_spec = pltpu.PrefetchScalarGridSpec(
        num_scalar_prefetch=2,
        grid=(n_batch, C_HEADS // C_HEAD_GROUP, nblk),
        in_specs=[
            pl.BlockSpec((tq, gw), qmap),
            pl.BlockSpec((seq, gw), kvmap),
            pl.BlockSpec((seq, gw), kvmap),
            pl.BlockSpec((None, n_ctx, gw), ctxmap),
            pl.BlockSpec((None, n_ctx, gw), ctxmap),
            pl.BlockSpec((None, C_HEAD_GROUP, tq, key_rows * GRID_W),
                         lambda b, g, i, st, cl: (cl[i], g, 0, 0)),
            pl.BlockSpec((tq, gw), qmap),
        ],
        out_specs=pl.BlockSpec((tq, gw), qmap),
        scratch_shapes=[pltpu.VMEM((C_HEAD_GROUP, tq, key_rows * GRID_W + n_ctx), F32),
                        pltpu.VMEM((C_HEAD_GROUP, tq, HEAD_DIM), F32),
                        pltpu.VMEM((C_HEAD_GROUP, tq, 2 * HEAD_DIM), F32)],
    )
    return pl.pallas_call(
        functools.partial(_attn_c_lat_kernel, key_rows=key_rows),
        out_shape=jax.ShapeDtypeStruct(q.shape, BF16),
        grid_spec=grid_spec,
        compiler_params=_params("arbitrary", "arbitrary", "arbitrary"),
        name="attn_c_lat",
    )(jnp.asarray(starts), jnp.asarray(class_ids), q, k, v, k_ctx, v_ctx, bias, sg)


def _rope_tables(n_tokens):
    d4 = HEAD_DIM // 4
    t = jnp.arange(n_tokens)
    pos = jnp.stack([t // GRID_W, t % GRID_W], axis=-1).astype(F32)
    inv_freq = ROPE_BASE ** (-jnp.arange(d4, dtype=F32) / d4)
    ang = pos[:, :, None] * inv_freq
    cos, sin = jnp.cos(ang), jnp.sin(ang)
    zero = jnp.zeros_like(sin[:, 0])
    cos_full = jnp.concatenate([cos[:, 0], cos[:, 0], cos[:, 1], cos[:, 1]], axis=-1)
    sin_up = jnp.concatenate([-sin[:, 0], zero, -sin[:, 1], zero], axis=-1)
    sin_dn = jnp.concatenate([zero, sin[:, 0], zero, sin[:, 1]], axis=-1)
    return cos_full, sin_up, sin_dn


def kernel(x_prompt, x_sample, cache_a_k, cache_a_v, cache_b_k, cache_b_v, cache_c_k, cache_c_v, c, c_ctx,
           ln_g, ada_w, ada_b, w_out, qn_g, kn_g, w_in_a, sink_a, w_in_b, lam_b, subln_b, w_in_c, rpb_c):
    n_p, seq_p, d = x_prompt.shape
    n_s, seq_s, _ = x_sample.shape
    n_ctx = cache_a_k.shape[2]
    xp = x_prompt.reshape(n_p * seq_p, d)
    xs = x_sample.reshape(n_s * seq_s, d)
    rope_tabs = _rope_tables(seq_s)

    pad = (-(n_s + 1)) % 8
    cvec = jnp.concatenate([c, c_ctx[None, :], jnp.zeros((pad, d), F32)], axis=0)
    mod = _ada_modulation(cvec, ada_w, ada_b)
    mod = mod.reshape(DEPTH, mod.shape[1], 3, 1, d)

    new_kv = {0: ([], []), 1: ([], []), 2: ([], [])}
    for l in range(DEPTH):
        kind, j = l % N_MIXERS, l // N_MIXERS
        sh_s, sc_s, gt_s = (mod[l, :n_s, t] for t in range(3))
        sh_p, sc_p, gt_p = (mod[l, n_s:n_s + 1, t] for t in range(3))
        w_o = w_out[l].astype(BF16)
        if kind == 0:
            kvw = A_KV_HEADS * HEAD_DIM
            w_in, widths, tn = w_in_a[j].astype(BF16), (d, kvw, kvw, d), 512
        elif kind == 1:
            w_in, widths, tn = w_in_b[j].astype(BF16), (d, d, d, d), 512
        else:
            w_in, widths, tn = w_in_c[j].astype(BF16), (d, d, d, d), 512
        rope = rope_tabs if kind != 2 else None
        qp, kp, vp, gp = _in_projection(xp, sh_p, sc_p, ln_g[l], w_in, qn_g[l], kn_g[l], widths, None, F32, tn)
        qs, ks, vs, gs = _in_projection(xs, sh_s, sc_s, ln_g[l], w_in, qn_g[l], kn_g[l], widths, rope, BF16, tn)
        if kind == 0:
            kc = cache_a_k[:, j].reshape(n_s, n_ctx, kvw)
            vc = cache_a_v[:, j].reshape(n_s, n_ctx, kvw)
            op = _attn_a_ctx(qp, kp, vp, gp, sink_a[j], n_p, seq_p)
            os_ = _attn_a_lat(qs, ks, vs, kc, vc, gs, sink_a[j], n_s, seq_s)
            new_kv[0][0].append(kp.reshape(n_p, seq_p, A_KV_HEADS, HEAD_DIM))
            new_kv[0][1].append(vp.reshape(n_p, seq_p, A_KV_HEADS, HEAD_DIM))
        elif kind == 1:
            lambda_init = 0.8 - 0.6 * math.exp(-0.3 * l)
            kc = cache_b_k[:, j].reshape(n_s, n_ctx, d)
            vc = cache_b_v[:, j].reshape(n_s, n_ctx, d)
            op = _attn_b(qp, kp, vp, gp, lam_b[j], subln_b[j], lambda_init, n_p, seq_p)
            os_ = _attn_b(qs, ks, vs, gs, lam_b[j], subln_b[j], lambda_init, n_s, seq_s, kc, vc)
            new_kv[1][0].append(kp.reshape(n_p, seq_p, B_HEADS, 2, HEAD_DIM))
            new_kv[1][1].append(vp.reshape(n_p, seq_p, B_HEADS, 2 * HEAD_DIM))
        else:
            kc = cache_c_k[:, j].reshape(n_s, n_ctx, d)
            vc = cache_c_v[:, j].reshape(n_s, n_ctx, d)
            op = _attn_c_ctx(qp, kp, vp, gp, n_p, seq_p)
            os_ = _attn_c_lat(qs, ks, vs, kc, vc, gs, rpb_c[j], n_s, seq_s)
            new_kv[2][0].append(kp.reshape(n_p, seq_p, C_HEADS, HEAD_DIM))
            new_kv[2][1].append(vp.reshape(n_p, seq_p, C_HEADS, HEAD_DIM))
        xp = _out_projection(op, w_o, xp, gt_p)
        xs = _out_projection(os_, w_o, xs, gt_s)

    outs = [xp.reshape(x_prompt.shape), xs.reshape(x_sample.shape)]
    for kind in range(3):
        outs.append(jnp.stack(new_kv[kind][0], axis=1))
        outs.append(jnp.stack(new_kv[kind][1], axis=1))
    return tuple(outs)
```

```python
import functools
import math

import numpy as np
import jax
import jax.numpy as jnp
from jax import lax
from jax.experimental import pallas as pl
from jax.experimental.pallas import tpu as pltpu

D_MODEL = 2048
DEPTH = 4
GRID_W = 64
HEAD_DIM = 128
ROPE_BASE = 10000.0
NEG_INF = -1e30
N_MIXERS = 3
A_HEADS = D_MODEL // HEAD_DIM
A_KV_HEADS = A_HEADS // 4
A_GROUP = A_HEADS // A_KV_HEADS
WINDOW = 128
B_HEADS = D_MODEL // (2 * HEAD_DIM)
C_HEADS = D_MODEL // HEAD_DIM
NA_KH = 8
NA_KW = 16
SCALE = HEAD_DIM ** -0.5
LOG2E = math.log2(math.e)
Q_SCALE = SCALE * LOG2E
EPS = 1e-6

BF16 = jnp.bfloat16
F32 = jnp.float32

VMEM_LIMIT_BYTES = 56 * 1024 * 1024
ROW_TILE = 1024
NORM_CHUNK = 128
PROJ_SUB = 128
A_QBLOCK = 128
B_QBLOCK = 1024
B_KCHUNK = 512
B_UNROLL = 4
C_ROWS = 4
C_HEAD_GROUP = 4


def _nt_dot(a, b):
    return lax.dot_general(a, b, (((1,), (1,)), ((), ())), preferred_element_type=F32)


def _params(*semantics):
    return pltpu.CompilerParams(dimension_semantics=semantics, vmem_limit_bytes=VMEM_LIMIT_BYTES)


def _ada_kernel(c_ref, w_ref, b_ref, o_ref):
    c = c_ref[...]
    a = (c * jax.nn.sigmoid(c)).astype(BF16)
    o_ref[...] = jnp.dot(a, w_ref[...].astype(BF16), preferred_element_type=F32) + b_ref[...]


def _ada_modulation(cvec, ada_w, ada_b):
    rows = cvec.shape[0]
    tn = 768
    return pl.pallas_call(
        _ada_kernel,
        out_shape=jax.ShapeDtypeStruct((DEPTH, rows, 3 * D_MODEL), F32),
        grid=(DEPTH, 3 * D_MODEL // tn),
        in_specs=[
            pl.BlockSpec((rows, D_MODEL), lambda l, j: (0, 0)),
            pl.BlockSpec((None, D_MODEL, tn), lambda l, j: (l, 0, j)),
            pl.BlockSpec((None, 1, tn), lambda l, j: (l, 0, j)),
        ],
        out_specs=pl.BlockSpec((None, rows, tn), lambda l, j: (l, 0, j)),
        compiler_params=_params("arbitrary", "arbitrary"),
        name="ada_modulation",
    )(cvec, ada_w, ada_b.reshape(DEPTH, 1, 3 * D_MODEL))


def _inproj_kernel(*refs, n_q, rope, heads_per_tile):
    x_ref, sh_ref, sc_ref, lng_ref, w1_ref, w2_ref, qg_ref, kg_ref = refs[:8]
    if rope:
        cos_ref, sa_ref, sb_ref = refs[8:11]
    q_out, k_out, v_out, g_out, h_scr = refs[-5:]
    j = pl.program_id(1)

    @pl.when(j == 0)
    def _():
        gain = lng_ref[...] * (1.0 + sc_ref[...])

        def body(r, carry):
            rows = pl.ds(pl.multiple_of(r * NORM_CHUNK, NORM_CHUNK), NORM_CHUNK)
            x = x_ref[rows, :]
            ms = jnp.mean(x * x, axis=-1, keepdims=True)
            h_scr[rows, :] = (x * lax.rsqrt(ms + EPS) * gain + sh_ref[...]).astype(BF16)
            return carry
        lax.fori_loop(0, x_ref.shape[0] // NORM_CHUNK, body, 0)

    def normed(acc, rows, gain_ref, scale):
        outs = []
        for t in range(heads_per_tile):
            y = acc[:, t * HEAD_DIM:(t + 1) * HEAD_DIM]
            ms = jnp.mean(y * y, axis=-1, keepdims=True)
            y = y * lax.rsqrt(ms + EPS) * (gain_ref[...] * scale)
            if rope:
                y = (y * cos_ref[rows, :] + pltpu.roll(y, 96, 1) * sa_ref[rows, :]
                     + pltpu.roll(y, 32, 1) * sb_ref[rows, :])
            outs.append(y)
        return jnp.concatenate(outs, axis=-1)

    def project(w_ref, out_ref, epilogue):
        for r in range(h_scr.shape[0] // PROJ_SUB):
            rows = slice(r * PROJ_SUB, (r + 1) * PROJ_SUB)
            acc = jnp.dot(h_scr[rows, :], w_ref[...], preferred_element_type=F32)
            out_ref[rows, :] = epilogue(acc, rows).astype(out_ref.dtype)

    @pl.when(j < n_q)
    def _():
        project(w1_ref, q_out, lambda acc, rows: normed(acc, rows, qg_ref, Q_SCALE))
        project(w2_ref, g_out, lambda acc, rows: acc * jax.nn.sigmoid(acc))

    @pl.when(j >= n_q)
    def _():
        project(w1_ref, k_out, lambda acc, rows: normed(acc, rows, kg_ref, 1.0))
        project(w2_ref, v_out, lambda acc, rows: acc)


def _in_projection(x, shift, scale, ln_g, w, qg, kg, widths, rope_tabs, kv_dtype, tn):
    t_rows = x.shape[0]
    qw, kw, vw, gw = widths
    assert qw == gw and kw == vw
    n_q, n_k = qw // tn, kw // tn
    tm = min(ROW_TILE, t_rows)
    rows_per_mod = t_rows // shift.shape[0]
    assert t_rows % tm == 0 and rows_per_mod % tm == 0
    rope = rope_tabs is not None
    g_col0, v_col0 = (qw + kw + vw) // tn, (qw + kw) // tn

    def mod_map(i, j):
        return ((i * tm) // rows_per_mod, 0, 0)

    in_specs = [
        pl.BlockSpec((tm, D_MODEL), lambda i, j: (i, 0)),
        pl.BlockSpec((None, 1, D_MODEL), mod_map),
        pl.BlockSpec((None, 1, D_MODEL), mod_map),
        pl.BlockSpec((1, D_MODEL), lambda i, j: (0, 0)),
        pl.BlockSpec((D_MODEL, tn), lambda i, j: (0, j)),
        pl.BlockSpec((D_MODEL, tn), lambda i, j: (0, jnp.where(j < n_q, g_col0 + j, v_col0 + j - n_q))),
        pl.BlockSpec((1, HEAD_DIM), lambda i, j: (0, 0)),
        pl.BlockSpec((1, HEAD_DIM), lambda i, j: (0, 0)),
    ]
    args = [x, shift, scale, ln_g.reshape(1, D_MODEL), w, w, qg.reshape(1, HEAD_DIM), kg.reshape(1, HEAD_DIM)]
    if rope:
        n_pos_tiles = rope_tabs[0].shape[0] // tm
        for tab in rope_tabs:
            in_specs.append(pl.BlockSpec((tm, HEAD_DIM), lambda i, j: (i % n_pos_tiles, 0)))
            args.append(tab)
    out_specs = [
        pl.BlockSpec((tm, tn), lambda i, j: (i, jnp.minimum(j, n_q - 1))),
        pl.BlockSpec((tm, tn), lambda i, j: (i, jnp.maximum(j - n_q, 0))),
        pl.BlockSpec((tm, tn), lambda i, j: (i, jnp.maximum(j - n_q, 0))),
        pl.BlockSpec((tm, tn), lambda i, j: (i, jnp.minimum(j, n_q - 1))),
    ]
    out_shape = [
        jax.ShapeDtypeStruct((t_rows, qw), BF16),
        jax.ShapeDtypeStruct((t_rows, kw), kv_dtype),
        jax.ShapeDtypeStruct((t_rows, vw), kv_dtype),
        jax.ShapeDtypeStruct((t_rows, gw), BF16),
    ]
    kern = functools.partial(_inproj_kernel, n_q=n_q, rope=rope, heads_per_tile=tn // HEAD_DIM)
    return pl.pallas_call(
        kern,
        out_shape=out_shape,
        grid=(t_rows // tm, n_q + n_k),
        in_specs=in_specs,
        out_specs=out_specs,
        scratch_shapes=[pltpu.VMEM((tm, D_MODEL), BF16)],
        compiler_params=_params("arbitrary", "arbitrary"),
        name="in_projection",
    )(*args)


def _outproj_kernel(o_ref, w_ref, x_ref, gt_ref, y_ref):
    y = jnp.dot(o_ref[...], w_ref[...], preferred_element_type=F32)
    y_ref[...] = x_ref[...] + gt_ref[...] * y


def _out_projection(o, w, x, gate):
    t_rows = x.shape[0]
    tm, tn = min(ROW_TILE, t_rows), 1024
    rows_per_mod = t_rows // gate.shape[0]
    assert t_rows % tm == 0 and rows_per_mod % tm == 0
    return pl.pallas_call(
        _outproj_kernel,
        out_shape=jax.ShapeDtypeStruct((t_rows, D_MODEL), F32),
        grid=(t_rows // tm, D_MODEL // tn),
        in_specs=[
            pl.BlockSpec((tm, D_MODEL), lambda i, j: (i, 0)),
            pl.BlockSpec((D_MODEL, tn), lambda i, j: (0, j)),
            pl.BlockSpec((tm, tn), lambda i, j: (i, j)),
            pl.BlockSpec((None, 1, tn), lambda i, j: ((i * tm) // rows_per_mod, 0, j)),
        ],
        out_specs=pl.BlockSpec((tm, tn), lambda i, j: (i, j)),
        compiler_params=_params("arbitrary", "arbitrary"),
        name="out_projection",
    )(o, w, x, gate)


def _lane_fold(x, op):
    r = x[:, 0:HEAD_DIM]
    for t in range(1, x.shape[1] // HEAD_DIM):
        r = op(r, x[:, t * HEAD_DIM:(t + 1) * HEAD_DIM])
    return r


def _with_ones(v):
    return jnp.concatenate([v, jnp.ones(v.shape, v.dtype)], axis=1)


def _softmax_pv(scores, values):
    m128 = _lane_fold(scores[0], jnp.maximum)
    for s in scores[1:]:
        m128 = jnp.maximum(m128, _lane_fold(s, jnp.maximum))
    m = jnp.broadcast_to(jnp.max(m128, axis=-1, keepdims=True), m128.shape)
    acc = None
    for s, v in zip(scores, values):
        p = jnp.exp2(s - jnp.concatenate([m] * (s.shape[1] // HEAD_DIM), axis=1)).astype(BF16)
        pv = jnp.dot(p, _with_ones(v), preferred_element_type=F32)
        acc = pv if acc is None else acc + pv
    return acc[:, :HEAD_DIM] * (1.0 / acc[:, HEAD_DIM:])


def _attn_a_kernel(*refs, latent, n_qblocks):
    if latent:
        (sink_ref, q_ref, kp_ref, ko_ref, kn_ref, vp_ref, vo_ref, vn_ref,
         kc_ref, vc_ref, sg_ref, o_ref, s_scr, m_scr, acc_scr) = refs
    else:
        sink_ref, q_ref, k_ref, v_ref, sg_ref, o_ref, s_scr, m_scr, acc_scr = refs
    m_rows = q_ref.shape[0]
    n_keys = s_scr.shape[-1]
    n_loc = 3 * A_QBLOCK
    kv_cols = [slice(kv * HEAD_DIM, (kv + 1) * HEAD_DIM) for kv in range(A_KV_HEADS)]
    kv_heads = [[kv * A_GROUP + g for g in range(A_GROUP)] for kv in range(A_KV_HEADS)]

    if latent:
        blk = pl.program_id(1)
        qpos = lax.broadcasted_iota(jnp.int32, (A_QBLOCK, n_loc), 0)
        col = lax.broadcasted_iota(jnp.int32, (A_QBLOCK, n_loc), 1)
        rel = col - A_QBLOCK - qpos
        kpos = (blk - 1) * A_QBLOCK + col
        valid = (jnp.abs(rel) <= WINDOW) & (kpos >= 0) & (kpos < n_qblocks * A_QBLOCK)
        window_bias = jnp.concatenate([jnp.where(valid, 0.0, NEG_INF)] * A_GROUP, axis=0)

    for kv, cols in enumerate(kv_cols):
        q = jnp.concatenate([q_ref[:, h * HEAD_DIM:(h + 1) * HEAD_DIM] for h in kv_heads[kv]], axis=0)
        if latent:
            k_loc = jnp.concatenate([kp_ref[:, cols], ko_ref[:, cols], kn_ref[:, cols]], axis=0)
            s_loc = _nt_dot(q, k_loc) + window_bias
            s_ctx = _nt_dot(q, kc_ref[:, cols].astype(BF16))
            s_scr[kv, :, 0:n_loc] = s_loc
            s_scr[kv, :, n_loc:n_keys] = s_ctx
            m_scr[kv] = jnp.maximum(_lane_fold(s_loc, jnp.maximum), _lane_fold(s_ctx, jnp.maximum))
        else:
            s = _nt_dot(q, k_ref[:, cols].astype(BF16))
            s_scr[kv] = s
            m_scr[kv] = _lane_fold(s, jnp.maximum)

    def sink_lanes(kv):
        return jnp.concatenate([jnp.full((m_rows, HEAD_DIM), sink_ref[h] * LOG2E, F32) for h in kv_heads[kv]], axis=0)

    for kv in range(A_KV_HEADS):
        m = jnp.max(m_scr[kv], axis=-1, keepdims=True)
        m_scr[kv] = jnp.maximum(jnp.broadcast_to(m, m_scr.shape[1:]), sink_lanes(kv))

    for kv, cols in enumerate(kv_cols):
        p = jnp.exp2(s_scr[kv] - jnp.concatenate([m_scr[kv]] * (n_keys // HEAD_DIM), axis=1)).astype(BF16)
        if latent:
            v_loc = jnp.concatenate([vp_ref[:, cols], vo_ref[:, cols], vn_ref[:, cols]], axis=0)
            acc_scr[kv] = (jnp.dot(p[:, 0:n_loc], _with_ones(v_loc), preferred_element_type=F32)
                           + jnp.dot(p[:, n_loc:n_keys], _with_ones(vc_ref[:, cols].astype(BF16)),
                                     preferred_element_type=F32))
        else:
            acc_scr[kv] = jnp.dot(p, _with_ones(v_ref[:, cols].astype(BF16)), preferred_element_type=F32)

    for kv in range(A_KV_HEADS):
        acc = acc_scr[kv]
        l = acc[:, HEAD_DIM:] + jnp.exp2(sink_lanes(kv) - m_scr[kv])
        o = acc[:, :HEAD_DIM] * (1.0 / l)
        for g, h in enumerate(kv_heads[kv]):
            hc = slice(h * HEAD_DIM, (h + 1) * HEAD_DIM)
            o_ref[:, hc] = (o[g * m_rows:(g + 1) * m_rows] * sg_ref[:, hc].astype(F32)).astype(o_ref.dtype)


def _attn_a_scratch(m_rows, n_keys):
    stacked = A_GROUP * m_rows
    return [pltpu.VMEM((A_KV_HEADS, stacked, n_keys), F32), pltpu.VMEM((A_KV_HEADS, stacked, HEAD_DIM), F32),
            pltpu.VMEM((A_KV_HEADS, stacked, 2 * HEAD_DIM), F32)]


def _attn_a_ctx(q, k, v, sg, sink, n_batch, seq):
    kvw = A_KV_HEADS * HEAD_DIM
    row = lambda b: (b, 0)
    return pl.pallas_call(
        functools.partial(_attn_a_kernel, latent=False, n_qblocks=1),
        out_shape=jax.ShapeDtypeStruct(q.shape, BF16),
        grid=(n_batch,),
        in_specs=[
            pl.BlockSpec(memory_space=pltpu.SMEM),
            pl.BlockSpec((seq, D_MODEL), row),
            pl.BlockSpec((seq, kvw), row),
            pl.BlockSpec((seq, kvw), row),
            pl.BlockSpec((seq, D_MODEL), row),
        ],
        out_specs=pl.BlockSpec((seq, D_MODEL), row),
        scratch_shapes=_attn_a_scratch(seq, seq),
        compiler_params=_params("arbitrary"),
        name="attn_a_ctx",
    )(sink, q, k, v, sg)


def _attn_a_lat(q, k, v, k_ctx, v_ctx, sg, sink, n_batch, seq):
    kvw = A_KV_HEADS * HEAD_DIM
    nb = seq // A_QBLOCK
    own = lambda b, i: (b * nb + i, 0)
    prev = lambda b, i: (b * nb + jnp.maximum(i - 1, 0), 0)
    nxt = lambda b, i: (b * nb + jnp.minimum(i + 1, nb - 1), 0)
    ctx = lambda b, i: (b, 0, 0)
    n_ctx = k_ctx.shape[1]
    return pl.pallas_call(
        functools.partial(_attn_a_kernel, latent=True, n_qblocks=nb),
        out_shape=jax.ShapeDtypeStruct(q.shape, BF16),
        grid=(n_batch, nb),
        in_specs=[
            pl.BlockSpec(memory_space=pltpu.SMEM),
            pl.BlockSpec((A_QBLOCK, D_MODEL), own),
            pl.BlockSpec((A_QBLOCK, kvw), prev),
            pl.BlockSpec((A_QBLOCK, kvw), own),
            pl.BlockSpec((A_QBLOCK, kvw), nxt),
            pl.BlockSpec((A_QBLOCK, kvw), prev),
            pl.BlockSpec((A_QBLOCK, kvw), own),
            pl.BlockSpec((A_QBLOCK, kvw), nxt),
            pl.BlockSpec((None, n_ctx, kvw), ctx),
            pl.BlockSpec((None, n_ctx, kvw), ctx),
            pl.BlockSpec((A_QBLOCK, D_MODEL), own),
        ],
        out_specs=pl.BlockSpec((A_QBLOCK, D_MODEL), own),
        scratch_shapes=_attn_a_scratch(A_QBLOCK, 3 * A_QBLOCK + n_ctx),
        compiler_params=_params("arbitrary", "arbitrary"),
        name="attn_a_lat",
    )(sink, q, k, k, k, v, v, v, k_ctx, v_ctx, sg)


def _attn_b_kernel(*refs, has_ctx, lambda_init, k_chunk):
    if has_ctx:
        (lam_ref, sub_ref, q_ref, k_ref, v_ref, kc_ref, vc_ref, sg_ref, o_ref,
         s_scr, sc_scr, m_scr, l_scr, acc_scr) = refs
    else:
        lam_ref, sub_ref, q_ref, k_ref, v_ref, sg_ref, o_ref, s_scr, m_scr, l_scr, acc_scr = refs
    n_chunks = k_ref.shape[0] // k_chunk
    halves = [slice(h * HEAD_DIM, (h + 1) * HEAD_DIM) for h in range(2)]

    def fold_max(h, s, first):
        m = _lane_fold(s, jnp.maximum)
        m_scr[h] = m if first else jnp.maximum(m_scr[h], m)

    def scores(c, first=False):
        rows = pl.ds(pl.multiple_of(c * k_chunk, k_chunk), k_chunk)
        k = k_ref[rows, :].astype(BF16)
        for h, hc in enumerate(halves):
            s = _nt_dot(q_ref[:, hc], k[:, hc])
            s_scr[h, c] = s
            fold_max(h, s, first)

    if has_ctx:
        kc = kc_ref[...].astype(BF16)
        for h, hc in enumerate(halves):
            s = _nt_dot(q_ref[:, hc], kc[:, hc])
            sc_scr[h] = s
            fold_max(h, s, True)
        lax.fori_loop(0, n_chunks, lambda c, carry: (scores(c), carry)[1], 0, unroll=B_UNROLL)
    else:
        assert n_chunks == 1
        scores(0, first=True)

    for h in range(2):
        m_scr[h] = jnp.broadcast_to(jnp.max(m_scr[h], axis=-1, keepdims=True), m_scr.shape[1:])

    def accumulate(h, s, v, first):
        m = m_scr[h]
        p = jnp.exp2(s - jnp.concatenate([m] * (s.shape[1] // HEAD_DIM), axis=1))
        l = _lane_fold(p, jnp.add)
        pv = jnp.dot(p.astype(BF16), v, preferred_element_type=F32)
        l_scr[h] = l if first else l_scr[h] + l
        acc_scr[h] = pv if first else acc_scr[h] + pv

    def weighted(c, first=False):
        rows = pl.ds(pl.multiple_of(c * k_chunk, k_chunk), k_chunk)
        v = v_ref[rows, :].astype(BF16)
        for h in range(2):
            accumulate(h, s_scr[h, c], v, first)

    if has_ctx:
        vc = vc_ref[...].astype(BF16)
        for h in range(2):
            accumulate(h, sc_scr[h], vc, True)
        lax.fori_loop(0, n_chunks, lambda c, carry: (weighted(c), carry)[1], 0, unroll=B_UNROLL)
    else:
        weighted(0, first=True)

    lam = lam_ref[...]
    lam_full = (jnp.exp(jnp.sum(lam[0:1] * lam[1:2], axis=-1, keepdims=True))
                - jnp.exp(jnp.sum(lam[2:3] * lam[3:4], axis=-1, keepdims=True)) + lambda_init)
    r0 = 1.0 / jnp.sum(l_scr[0], axis=-1, keepdims=True)
    r1 = lam_full / jnp.sum(l_scr[1], axis=-1, keepdims=True)
    o = acc_scr[0] * r0 - acc_scr[1] * r1
    ms = jnp.mean(o * o, axis=-1, keepdims=True)
    o = o * lax.rsqrt(ms + EPS) * (sub_ref[...] * (1.0 - lambda_init))
    o_ref[...] = (o * sg_ref[...].astype(F32)).astype(o_ref.dtype)


def _attn_b(q, k, v, sg, lam, subln, lambda_init, n_batch, seq, k_ctx=None, v_ctx=None):
    dv = 2 * HEAD_DIM
    tq = min(B_QBLOCK, seq)
    nq = seq // tq
    k_chunk = min(B_KCHUNK, seq)
    has_ctx = k_ctx is not None
    qmap = lambda b, h, i: (b * nq + i, h)
    kvmap = lambda b, h, i: (b, h)
    in_specs = [
        pl.BlockSpec((4, HEAD_DIM), lambda b, h, i: (0, 0)),
        pl.BlockSpec((1, dv), lambda b, h, i: (0, 0)),
        pl.BlockSpec((tq, dv), qmap),
        pl.BlockSpec((seq, dv), kvmap),
        pl.BlockSpec((seq, dv), kvmap),
    ]
    args = [lam, subln.reshape(1, dv), q, k, v]
    scratch = [pltpu.VMEM((2, seq // k_chunk, tq, k_chunk), F32)]
    if has_ctx:
        n_ctx = k_ctx.shape[1]
        in_specs += [pl.BlockSpec((None, n_ctx, dv), lambda b, h, i: (b, 0, h))] * 2
        args += [k_ctx, v_ctx]
        scratch.append(pltpu.VMEM((2, tq, n_ctx), F32))
    in_specs.append(pl.BlockSpec((tq, dv), qmap))
    args.append(sg)
    scratch += [pltpu.VMEM((2, tq, HEAD_DIM), F32), pltpu.VMEM((2, tq, HEAD_DIM), F32),
                pltpu.VMEM((2, tq, dv), F32)]
    return pl.pallas_call(
        functools.partial(_attn_b_kernel, has_ctx=has_ctx, lambda_init=lambda_init, k_chunk=k_chunk),
        out_shape=jax.ShapeDtypeStruct(q.shape, BF16),
        grid=(n_batch, B_HEADS, nq),
        in_specs=in_specs,
        out_specs=pl.BlockSpec((tq, dv), qmap),
        scratch_shapes=scratch,
        compiler_params=_params("arbitrary", "arbitrary", "arbitrary"),
        name="attn_b_lat" if has_ctx else "attn_b_ctx",
    )(*args)


def _attn_c_ctx_kernel(q_ref, k_ref, v_ref, sg_ref, o_ref):
    for h in range(C_HEADS):
        hc = slice(h * HEAD_DIM, (h + 1) * HEAD_DIM)
        s = _nt_dot(q_ref[:, hc], k_ref[:, hc].astype(BF16))
        o = _softmax_pv([s], [v_ref[:, hc].astype(BF16)])
        o_ref[:, hc] = (o * sg_ref[:, hc].astype(F32)).astype(o_ref.dtype)


def _attn_c_ctx(q, k, v, sg, n_batch, seq):
    spec = pl.BlockSpec((seq, D_MODEL), lambda b: (b, 0))
    return pl.pallas_call(
        _attn_c_ctx_kernel,
        out_shape=jax.ShapeDtypeStruct(q.shape, BF16),
        grid=(n_batch,),
        in_specs=[spec, spec, spec, spec],
        out_specs=spec,
        compiler_params=_params("arbitrary"),
        name="attn_c_ctx",
    )(q, k, v, sg)


def _na_geometry(rows):
    kh = min(NA_KH, rows)
    key_rows = C_ROWS + kh - 1
    key_rows = min(key_rows + key_rows % 2, rows)
    n_blocks = rows // C_ROWS
    cols = np.arange(GRID_W)
    col_start = np.clip(cols - NA_KW // 2, 0, GRID_W - NA_KW)
    col_ok = (cols[None, :] >= col_start[:, None]) & (cols[None, :] < col_start[:, None] + NA_KW)
    col_delta = cols[None, :] - cols[:, None] + NA_KW - 1
    col_onehot = (col_delta[None] == np.arange(2 * NA_KW - 1)[:, None, None]) & col_ok[None]
    starts, class_ids, classes, keys = [], [], [], {}
    for blk in range(n_blocks):
        r = blk * C_ROWS + np.arange(C_ROWS)
        rs = np.clip(r - kh // 2, 0, rows - kh)
        start = int(np.clip(rs[0], 0, rows - key_rows))
        key = (tuple(rs - r), start - blk * C_ROWS)
        if key not in keys:
            keys[key] = len(classes)
            kr = start + np.arange(key_rows)
            row_ok = (kr[None, :] >= rs[:, None]) & (kr[None, :] < rs[:, None] + kh)
            row_idx = kr[None, :] - r[:, None] + NA_KH - 1
            classes.append((row_ok, row_idx))
        starts.append(start)
        class_ids.append(keys[key])
    geometry = (col_ok, col_onehot.astype(np.float32), classes)
    return key_rows, np.asarray(starts, np.int32), np.asarray(class_ids, np.int32), geometry


def _na_bias_tables(rpb, geometry):
    col_ok, col_onehot, classes = geometry
    n_heads = rpb.shape[0]
    toep = jnp.einsum("hdx,xck->hdck", rpb * LOG2E, jnp.asarray(col_onehot),
                      precision=lax.Precision.HIGHEST)
    toep = jnp.where(jnp.asarray(col_ok)[None, None], toep, NEG_INF)
    masked = jnp.full((n_heads, GRID_W, GRID_W), NEG_INF, F32)
    tabs = []
    for row_ok, row_idx in classes:
        n_r, n_kr = row_ok.shape
        blocks = [toep[:, int(row_idx[a, u])] if row_ok[a, u] else masked
                  for a in range(n_r) for u in range(n_kr)]
        tab = jnp.stack(blocks, axis=1).reshape(n_heads, n_r, n_kr, GRID_W, GRID_W)
        tabs.append(tab.transpose(0, 1, 3, 2, 4).reshape(n_heads, n_r * GRID_W, n_kr * GRID_W))
    return jnp.stack(tabs, axis=0)


def _attn_c_lat_kernel(start_ref, cls_ref, q_ref, k_ref, v_ref, kc_ref, vc_ref, bias_ref, sg_ref, o_ref,
                       s_scr, m_scr, acc_scr, *, key_rows):
    blk = pl.program_id(2)
    n_loc = key_rows * GRID_W
    n_keys = s_scr.shape[-1]
    rows = pl.ds(pl.multiple_of(start_ref[blk] * GRID_W, GRID_W), n_loc)
    head_cols = [slice(h * HEAD_DIM, (h + 1) * HEAD_DIM) for h in range(C_HEAD_GROUP)]

    for h, hc in enumerate(head_cols):
        q = q_ref[:, hc]
        s_loc = _nt_dot(q, k_ref[rows, hc]) + bias_ref[h]
        s_ctx = _nt_dot(q, kc_ref[:, hc].astype(BF16))
        s_scr[h, :, 0:n_loc] = s_loc
        s_scr[h, :, n_loc:n_keys] = s_ctx
        m_scr[h] = jnp.maximum(_lane_fold(s_loc, jnp.maximum), _lane_fold(s_ctx, jnp.maximum))

    for h in range(C_HEAD_GROUP):
        m_scr[h] = jnp.broadcast_to(jnp.max(m_scr[h], axis=-1, keepdims=True), m_scr.shape[1:])

    for h, hc in enumerate(head_cols):
        p = jnp.exp2(s_scr[h] - jnp.concatenate([m_scr[h]] * (n_keys // HEAD_DIM), axis=1)).astype(BF16)
        acc_scr[h] = (jnp.dot(p[:, 0:n_loc], _with_ones(v_ref[rows, hc]), preferred_element_type=F32)
                      + jnp.dot(p[:, n_loc:n_keys], _with_ones(vc_ref[:, hc].astype(BF16)),
                                preferred_element_type=F32))

    for h, hc in enumerate(head_cols):
        acc = acc_scr[h]
        o = acc[:, :HEAD_DIM] * (1.0 / acc[:, HEAD_DIM:])
        o_ref[:, hc] = (o * sg_ref[:, hc].astype(F32)).astype(o_ref.dtype)


def _attn_c_lat(q, k, v, k_ctx, v_ctx, sg, rpb, n_batch, seq):
    rows = seq // GRID_W
    key_rows, starts, class_ids, geometry = _na_geometry(rows)
    bias = _na_bias_tables(rpb, geometry)
    nblk = rows // C_ROWS
    tq = C_ROWS * GRID_W
    gw = C_HEAD_GROUP * HEAD_DIM
    n_ctx = k_ctx.shape[1]
    qmap = lambda b, g, i, st, cl: (b * nblk + i, g)
    kvmap = lambda b, g, i, st, cl: (b, g)
    ctxmap = lambda b, g, i, st, cl: (b, 0, g)
    grid_spec = pltpu.PrefetchScalarGridSpec(
        num_scalar_prefetch=2,
        grid=(n_batch, C_HEADS // C_HEAD_GROUP, nblk),
        in_specs=[
            pl.BlockSpec((tq, gw), qmap),
            pl.BlockSpec((seq, gw), kvmap),
            pl.BlockSpec((seq, gw), kvmap),
            pl.BlockSpec((None, n_ctx, gw), ctxmap),
            pl.BlockSpec((None, n_ctx, gw), ctxmap),
            pl.BlockSpec((None, C_HEAD_GROUP, tq, key_rows * GRID_W),
                         lambda b, g, i, st, cl: (cl[i], g, 0, 0)),
            pl.BlockSpec((tq, gw), qmap),
        ],
        out_specs=pl.BlockSpec((tq, gw), qmap),
        scratch_shapes=[pltpu.VMEM((C_HEAD_GROUP, tq, key_rows * GRID_W + n_ctx), F32),
                        pltpu.VMEM((C_HEAD_GROUP, tq, HEAD_DIM), F32),
                        pltpu.VMEM((C_HEAD_GROUP, tq, 2 * HEAD_DIM), F32)],
    )
    return pl.pallas_call(
        functools.partial(_attn_c_lat_kernel, key_rows=key_rows),
        out_shape=jax.ShapeDtypeStruct(q.shape, BF16),
        grid_spec=grid_spec,
        compiler_params=_params("arbitrary", "arbitrary", "arbitrary"),
        name="attn_c_lat",
    )(jnp.asarray(starts), jnp.asarray(class_ids), q, k, v, k_ctx, v_ctx, bias, sg)


def _rope_tables(n_tokens):
    d4 = HEAD_DIM // 4
    t = jnp.arange(n_tokens)
    pos = jnp.stack([t // GRID_W, t % GRID_W], axis=-1).astype(F32)
    inv_freq = ROPE_BASE ** (-jnp.arange(d4, dtype=F32) / d4)
    ang = pos[:, :, None] * inv_freq
    cos, sin = jnp.cos(ang), jnp.sin(ang)
    zero = jnp.zeros_like(sin[:, 0])
    cos_full = jnp.concatenate([cos[:, 0], cos[:, 0], cos[:, 1], cos[:, 1]], axis=-1)
    sin_up = jnp.concatenate([-sin[:, 0], zero, -sin[:, 1], zero], axis=-1)
    sin_dn = jnp.concatenate([zero, sin[:, 0], zero, sin[:, 1]], axis=-1)
    return cos_full, sin_up, sin_dn


def kernel(x_prompt, x_sample, cache_a_k, cache_a_v, cache_b_k, cache_b_v, cache_c_k, cache_c_v, c, c_ctx,
           ln_g, ada_w, ada_b, w_out, qn_g, kn_g, w_in_a, sink_a, w_in_b, lam_b, subln_b, w_in_c, rpb_c):
    n_p, seq_p, d = x_prompt.shape
    n_s, seq_s, _ = x_sample.shape
    n_ctx = cache_a_k.shape[2]
    xp = x_prompt.reshape(n_p * seq_p, d)
    xs = x_sample.reshape(n_s * seq_s, d)
    rope_tabs = _rope_tables(seq_s)

    pad = (-(n_s + 1)) % 8
    cvec = jnp.concatenate([c, c_ctx[None, :], jnp.zeros((pad, d), F32)], axis=0)
    mod = _ada_modulation(cvec, ada_w, ada_b)
    mod = mod.reshape(DEPTH, mod.shape[1], 3, 1, d)

    new_kv = {0: ([], []), 1: ([], []), 2: ([], [])}
    for l in range(DEPTH):
        kind, j = l % N_MIXERS, l // N_MIXERS
        sh_s, sc_s, gt_s = (mod[l, :n_s, t] for t in range(3))
        sh_p, sc_p, gt_p = (mod[l, n_s:n_s + 1, t] for t in range(3))
        w_o = w_out[l].astype(BF16)
        if kind == 0:
            kvw = A_KV_HEADS * HEAD_DIM
            w_in, widths, tn = w_in_a[j].astype(BF16), (d, kvw, kvw, d), 512
        elif kind == 1:
            w_in, widths, tn = w_in_b[j].astype(BF16), (d, d, d, d), 512
        else:
            w_in, widths, tn = w_in_c[j].astype(BF16), (d, d, d, d), 512
        rope = rope_tabs if kind != 2 else None
        qp, kp, vp, gp = _in_projection(xp, sh_p, sc_p, ln_g[l], w_in, qn_g[l], kn_g[l], widths, None, F32, tn)
        qs, ks, vs, gs = _in_projection(xs, sh_s, sc_s, ln_g[l], w_in, qn_g[l], kn_g[l], widths, rope, BF16, tn)
        if kind == 0:
            kc = cache_a_k[:, j].reshape(n_s, n_ctx, kvw)
            vc = cache_a_v[:, j].reshape(n_s, n_ctx, kvw)
            op = _attn_a_ctx(qp, kp, vp, gp, sink_a[j], n_p, seq_p)
            os_ = _attn_a_lat(qs, ks, vs, kc, vc, gs, sink_a[j], n_s, seq_s)
            new_kv[0][0].append(kp.reshape(n_p, seq_p, A_KV_HEADS, HEAD_DIM))
            new_kv[0][1].append(vp.reshape(n_p, seq_p, A_KV_HEADS, HEAD_DIM))
        elif kind == 1:
            lambda_init = 0.8 - 0.6 * math.exp(-0.3 * l)
            kc = cache_b_k[:, j].reshape(n_s, n_ctx, d)
            vc = cache_b_v[:, j].reshape(n_s, n_ctx, d)
            op = _attn_b(qp, kp, vp, gp, lam_b[j], subln_b[j], lambda_init, n_p, seq_p)
            os_ = _attn_b(qs, ks, vs, gs, lam_b[j], subln_b[j], lambda_init, n_s, seq_s, kc, vc)
            new_kv[1][0].append(kp.reshape(n_p, seq_p, B_HEADS, 2, HEAD_DIM))
            new_kv[1][1].append(vp.reshape(n_p, seq_p, B_HEADS, 2 * HEAD_DIM))
        else:
            kc = cache_c_k[:, j].reshape(n_s, n_ctx, d)
            vc = cache_c_v[:, j].reshape(n_s, n_ctx, d)
            op = _attn_c_ctx(qp, kp, vp, gp, n_p, seq_p)
            os_ = _attn_c_lat(qs, ks, vs, kc, vc, gs, rpb_c[j], n_s, seq_s)
            new_kv[2][0].append(kp.reshape(n_p, seq_p, C_HEADS, HEAD_DIM))
            new_kv[2][1].append(vp.reshape(n_p, seq_p, C_HEADS, HEAD_DIM))
        xp = _out_projection(op, w_o, xp, gt_p)
        xs = _out_projection(os_, w_o, xs, gt_s)

    outs = [xp.reshape(x_prompt.shape), xs.reshape(x_sample.shape)]
    for kind in range(3):
        outs.append(jnp.stack(new_kv[kind][0], axis=1))
        outs.append(jnp.stack(new_kv[kind][1], axis=1))
    return tuple(outs)
```

```python
import functools
import math

import numpy as np
import jax
import jax.numpy as jnp
from jax import lax
from jax.experimental import pallas as pl
from jax.experimental.pallas import tpu as pltpu

D_MODEL = 2048
DEPTH = 4
GRID_W = 64
HEAD_DIM = 128
ROPE_BASE = 10000.0
NEG_INF = -1e30
N_MIXERS = 3
A_HEADS = D_MODEL // HEAD_DIM
A_KV_HEADS = A_HEADS // 4
A_GROUP = A_HEADS // A_KV_HEADS
WINDOW = 128
B_HEADS = D_MODEL // (2 * HEAD_DIM)
C_HEADS = D_MODEL // HEAD_DIM
NA_KH = 8
NA_KW = 16
SCALE = HEAD_DIM ** -0.5
LOG2E = math.log2(math.e)
Q_SCALE = SCALE * LOG2E
EPS = 1e-6

BF16 = jnp.bfloat16
F32 = jnp.float32

VMEM_LIMIT_BYTES = 56 * 1024 * 1024
ROW_TILE = 1024
OUT_ROW_TILE = 512
OUT_SUB = 512
NORM_CHUNK = 128
PROJ_SUB = 128
A_QBLOCK = 128
B_QBLOCK = 1024
B_KCHUNK = 512
B_UNROLL = 4
C_ROWS = 4
C_HEAD_GROUP = 4


def _nt_dot(a, b):
    return lax.dot_general(a, b, (((1,), (1,)), ((), ())), preferred_element_type=F32)


def _params(*semantics):
    return pltpu.CompilerParams(dimension_semantics=semantics, vmem_limit_bytes=VMEM_LIMIT_BYTES)


def _ada_kernel(c_ref, w_ref, b_ref, o_ref):
    c = c_ref[...]
    a = (c * jax.nn.sigmoid(c)).astype(BF16)
    o_ref[...] = jnp.dot(a, w_ref[...].astype(BF16), preferred_element_type=F32) + b_ref[...]


def _ada_modulation(cvec, ada_w, ada_b):
    rows = cvec.shape[0]
    tn = 768
    return pl.pallas_call(
        _ada_kernel,
        out_shape=jax.ShapeDtypeStruct((DEPTH, rows, 3 * D_MODEL), F32),
        grid=(DEPTH, 3 * D_MODEL // tn),
        in_specs=[
            pl.BlockSpec((rows, D_MODEL), lambda l, j: (0, 0)),
            pl.BlockSpec((None, D_MODEL, tn), lambda l, j: (l, 0, j)),
            pl.BlockSpec((None, 1, tn), lambda l, j: (l, 0, j)),
        ],
        out_specs=pl.BlockSpec((None, rows, tn), lambda l, j: (l, 0, j)),
        compiler_params=_params("arbitrary", "arbitrary"),
        name="ada_modulation",
    )(cvec, ada_w, ada_b.reshape(DEPTH, 1, 3 * D_MODEL))


def _inproj_kernel(*refs, n_q, rope, heads_per_tile):
    x_ref, sh_ref, sc_ref, lng_ref, w1_ref, w2_ref, qg_ref, kg_ref = refs[:8]
    if rope:
        cos_ref, sa_ref, sb_ref = refs[8:11]
    q_out, k_out, v_out, g_out, h_scr = refs[-5:]
    j = pl.program_id(1)

    @pl.when(j == 0)
    def _():
        gain = lng_ref[...] * (1.0 + sc_ref[...])

        def body(r, carry):
            rows = pl.ds(pl.multiple_of(r * NORM_CHUNK, NORM_CHUNK), NORM_CHUNK)
            x = x_ref[rows, :]
            ms = jnp.mean(x * x, axis=-1, keepdims=True)
            h_scr[rows, :] = (x * lax.rsqrt(ms + EPS) * gain + sh_ref[...]).astype(BF16)
            return carry
        lax.fori_loop(0, x_ref.shape[0] // NORM_CHUNK, body, 0)

    def normed(acc, rows, gain_ref, scale):
        outs = []
        for t in range(heads_per_tile):
            y = acc[:, t * HEAD_DIM:(t + 1) * HEAD_DIM]
            ms = jnp.mean(y * y, axis=-1, keepdims=True)
            y = y * lax.rsqrt(ms + EPS) * (gain_ref[...] * scale)
            if rope:
                y = (y * cos_ref[rows, :] + pltpu.roll(y, 96, 1) * sa_ref[rows, :]
                     + pltpu.roll(y, 32, 1) * sb_ref[rows, :])
            outs.append(y)
        return jnp.concatenate(outs, axis=-1)

    def project(w_ref, out_ref, epilogue):
        for r in range(h_scr.shape[0] // PROJ_SUB):
            rows = slice(r * PROJ_SUB, (r + 1) * PROJ_SUB)
            acc = jnp.dot(h_scr[rows, :], w_ref[...], preferred_element_type=F32)
            out_ref[rows, :] = epilogue(acc, rows).astype(out_ref.dtype)

    @pl.when(j < n_q)
    def _():
        project(w1_ref, q_out, lambda acc, rows: normed(acc, rows, qg_ref, Q_SCALE))
        project(w2_ref, g_out, lambda acc, rows: acc * jax.nn.sigmoid(acc))

    @pl.when(j >= n_q)
    def _():
        project(w1_ref, k_out, lambda acc, rows: normed(acc, rows, kg_ref, 1.0))
        project(w2_ref, v_out, lambda acc, rows: acc)


def _in_projection(x, shift, scale, ln_g, w, qg, kg, widths, rope_tabs, kv_dtype, tn):
    t_rows = x.shape[0]
    qw, kw, vw, gw = widths
    assert qw == gw and kw == vw
    n_q, n_k = qw // tn, kw // tn
    tm = min(ROW_TILE, t_rows)
    rows_per_mod = t_rows // shift.shape[0]
    assert t_rows % tm == 0 and rows_per_mod % tm == 0
    rope = rope_tabs is not None
    g_col0, v_col0 = (qw + kw + vw) // tn, (qw + kw) // tn

    def mod_map(i, j):
        return ((i * tm) // rows_per_mod, 0, 0)

    in_specs = [
        pl.BlockSpec((tm, D_MODEL), lambda i, j: (i, 0)),
        pl.BlockSpec((None, 1, D_MODEL), mod_map),
        pl.BlockSpec((None, 1, D_MODEL), mod_map),
        pl.BlockSpec((1, D_MODEL), lambda i, j: (0, 0)),
        pl.BlockSpec((D_MODEL, tn), lambda i, j: (0, j)),
        pl.BlockSpec((D_MODEL, tn), lambda i, j: (0, jnp.where(j < n_q, g_col0 + j, v_col0 + j - n_q))),
        pl.BlockSpec((1, HEAD_DIM), lambda i, j: (0, 0)),
        pl.BlockSpec((1, HEAD_DIM), lambda i, j: (0, 0)),
    ]
    args = [x, shift, scale, ln_g.reshape(1, D_MODEL), w, w, qg.reshape(1, HEAD_DIM), kg.reshape(1, HEAD_DIM)]
    if rope:
        n_pos_tiles = rope_tabs[0].shape[0] // tm
        for tab in rope_tabs:
            in_specs.append(pl.BlockSpec((tm, HEAD_DIM), lambda i, j: (i % n_pos_tiles, 0)))
            args.append(tab)
    out_specs = [
        pl.BlockSpec((tm, tn), lambda i, j: (i, jnp.minimum(j, n_q - 1))),
        pl.BlockSpec((tm, tn), lambda i, j: (i, jnp.maximum(j - n_q, 0))),
        pl.BlockSpec((tm, tn), lambda i, j: (i, jnp.maximum(j - n_q, 0))),
        pl.BlockSpec((tm, tn), lambda i, j: (i, jnp.minimum(j, n_q - 1))),
    ]
    out_shape = [
        jax.ShapeDtypeStruct((t_rows, qw), BF16),
        jax.ShapeDtypeStruct((t_rows, kw), kv_dtype),
        jax.ShapeDtypeStruct((t_rows, vw), kv_dtype),
        jax.ShapeDtypeStruct((t_rows, gw), BF16),
    ]
    kern = functools.partial(_inproj_kernel, n_q=n_q, rope=rope, heads_per_tile=tn // HEAD_DIM)
    return pl.pallas_call(
        kern,
        out_shape=out_shape,
        grid=(t_rows // tm, n_q + n_k),
        in_specs=in_specs,
        out_specs=out_specs,
        scratch_shapes=[pltpu.VMEM((tm, D_MODEL), BF16)],
        compiler_params=_params("arbitrary", "arbitrary"),
        name="in_projection",
    )(*args)


def _outproj_kernel(o_ref, w_ref, x_ref, gt_ref, y_ref):
    for c in range(D_MODEL // OUT_SUB):
        cols = slice(c * OUT_SUB, (c + 1) * OUT_SUB)
        y = jnp.dot(o_ref[...], w_ref[:, cols], preferred_element_type=F32)
        y_ref[:, cols] = x_ref[:, cols] + gt_ref[:, cols] * y


def _out_projection(o, w, x, gate):
    t_rows = x.shape[0]
    tm = min(OUT_ROW_TILE, t_rows)
    rows_per_mod = t_rows // gate.shape[0]
    assert t_rows % tm == 0 and rows_per_mod % tm == 0
    row = lambda i: (i, 0)
    return pl.pallas_call(
        _outproj_kernel,
        out_shape=jax.ShapeDtypeStruct((t_rows, D_MODEL), F32),
        grid=(t_rows // tm,),
        in_specs=[
            pl.BlockSpec((tm, D_MODEL), row),
            pl.BlockSpec((D_MODEL, D_MODEL), lambda i: (0, 0)),
            pl.BlockSpec((tm, D_MODEL), row),
            pl.BlockSpec((None, 1, D_MODEL), lambda i: ((i * tm) // rows_per_mod, 0, 0)),
        ],
        out_specs=pl.BlockSpec((tm, D_MODEL), row),
        compiler_params=_params("arbitrary"),
        name="out_projection",
    )(o, w, x, gate)


def _lane_fold(x, op):
    r = x[:, 0:HEAD_DIM]
    for t in range(1, x.shape[1] // HEAD_DIM):
        r = op(r, x[:, t * HEAD_DIM:(t + 1) * HEAD_DIM])
    return r


def _with_ones(v):
    return jnp.concatenate([v, jnp.ones(v.shape, v.dtype)], axis=1)


def _softmax_pv(scores, values):
    m128 = _lane_fold(scores[0], jnp.maximum)
    for s in scores[1:]:
        m128 = jnp.maximum(m128, _lane_fold(s, jnp.maximum))
    m = jnp.broadcast_to(jnp.max(m128, axis=-1, keepdims=True), m128.shape)
    acc = None
    for s, v in zip(scores, values):
        p = jnp.exp2(s - jnp.concatenate([m] * (s.shape[1] // HEAD_DIM), axis=1)).astype(BF16)
        pv = jnp.dot(p, _with_ones(v), preferred_element_type=F32)
        acc = pv if acc is None else acc + pv
    return acc[:, :HEAD_DIM] * (1.0 / acc[:, HEAD_DIM:])


def _attn_a_kernel(*refs, latent, n_qblocks):
    if latent:
        (sink_ref, q_ref, kp_ref, ko_ref, kn_ref, vp_ref, vo_ref, vn_ref,
         kc_ref, vc_ref, sg_ref, o_ref, s_scr, m_scr, acc_scr) = refs
    else:
        sink_ref, q_ref, k_ref, v_ref, sg_ref, o_ref, s_scr, m_scr, acc_scr = refs
    m_rows = q_ref.shape[0]
    n_keys = s_scr.shape[-1]
    n_loc = 3 * A_QBLOCK
    kv_cols = [slice(kv * HEAD_DIM, (kv + 1) * HEAD_DIM) for kv in range(A_KV_HEADS)]
    kv_heads = [[kv * A_GROUP + g for g in range(A_GROUP)] for kv in range(A_KV_HEADS)]

    if latent:
        blk = pl.program_id(1)
        qpos = lax.broadcasted_iota(jnp.int32, (A_QBLOCK, n_loc), 0)
        col = lax.broadcasted_iota(jnp.int32, (A_QBLOCK, n_loc), 1)
        rel = col - A_QBLOCK - qpos
        kpos = (blk - 1) * A_QBLOCK + col
        valid = (jnp.abs(rel) <= WINDOW) & (kpos >= 0) & (kpos < n_qblocks * A_QBLOCK)
        window_bias = jnp.concatenate([jnp.where(valid, 0.0, NEG_INF)] * A_GROUP, axis=0)

    for kv, cols in enumerate(kv_cols):
        q = jnp.concatenate([q_ref[:, h * HEAD_DIM:(h + 1) * HEAD_DIM] for h in kv_heads[kv]], axis=0)
        if latent:
            k_loc = jnp.concatenate([kp_ref[:, cols], ko_ref[:, cols], kn_ref[:, cols]], axis=0)
            s_loc = _nt_dot(q, k_loc) + window_bias
            s_ctx = _nt_dot(q, kc_ref[:, cols].astype(BF16))
            s_scr[kv, :, 0:n_loc] = s_loc
            s_scr[kv, :, n_loc:n_keys] = s_ctx
            m_scr[kv] = jnp.maximum(_lane_fold(s_loc, jnp.maximum), _lane_fold(s_ctx, jnp.maximum))
        else:
            s = _nt_dot(q, k_ref[:, cols].astype(BF16))
            s_scr[kv] = s
            m_scr[kv] = _lane_fold(s, jnp.maximum)

    def sink_lanes(kv):
        return jnp.concatenate([jnp.full((m_rows, HEAD_DIM), sink_ref[h] * LOG2E, F32) for h in kv_heads[kv]], axis=0)

    for kv in range(A_KV_HEADS):
        m = jnp.max(m_scr[kv], axis=-1, keepdims=True)
        m_scr[kv] = jnp.maximum(jnp.broadcast_to(m, m_scr.shape[1:]), sink_lanes(kv))

    for kv, cols in enumerate(kv_cols):
        p = jnp.exp2(s_scr[kv] - jnp.concatenate([m_scr[kv]] * (n_keys // HEAD_DIM), axis=1)).astype(BF16)
        if latent:
            v_loc = jnp.concatenate([vp_ref[:, cols], vo_ref[:, cols], vn_ref[:, cols]], axis=0)
            acc_scr[kv] = (jnp.dot(p[:, 0:n_loc], _with_ones(v_loc), preferred_element_type=F32)
                           + jnp.dot(p[:, n_loc:n_keys], _with_ones(vc_ref[:, cols].astype(BF16)),
                                     preferred_element_type=F32))
        else:
            acc_scr[kv] = jnp.dot(p, _with_ones(v_ref[:, cols].astype(BF16)), preferred_element_type=F32)

    for kv in range(A_KV_HEADS):
        acc = acc_scr[kv]
        l = acc[:, HEAD_DIM:] + jnp.exp2(sink_lanes(kv) - m_scr[kv])
        o = acc[:, :HEAD_DIM] * (1.0 / l)
        for g, h in enumerate(kv_heads[kv]):
            hc = slice(h * HEAD_DIM, (h + 1) * HEAD_DIM)
            o_ref[:, hc] = (o[g * m_rows:(g + 1) * m_rows] * sg_ref[:, hc].astype(F32)).astype(o_ref.dtype)


def _attn_a_scratch(m_rows, n_keys):
    stacked = A_GROUP * m_rows
    return [pltpu.VMEM((A_KV_HEADS, stacked, n_keys), F32), pltpu.VMEM((A_KV_HEADS, stacked, HEAD_DIM), F32),
            pltpu.VMEM((A_KV_HEADS, stacked, 2 * HEAD_DIM), F32)]


def _attn_a_ctx(q, k, v, sg, sink, n_batch, seq):
    kvw = A_KV_HEADS * HEAD_DIM
    row = lambda b: (b, 0)
    return pl.pallas_call(
        functools.partial(_attn_a_kernel, latent=False, n_qblocks=1),
        out_shape=jax.ShapeDtypeStruct(q.shape, BF16),
        grid=(n_batch,),
        in_specs=[
            pl.BlockSpec(memory_space=pltpu.SMEM),
            pl.BlockSpec((seq, D_MODEL), row),
            pl.BlockSpec((seq, kvw), row),
            pl.BlockSpec((seq, kvw), row),
            pl.BlockSpec((seq, D_MODEL), row),
        ],
        out_specs=pl.BlockSpec((seq, D_MODEL), row),
        scratch_shapes=_attn_a_scratch(seq, seq),
        compiler_params=_params("arbitrary"),
        name="attn_a_ctx",
    )(sink, q, k, v, sg)


def _attn_a_lat(q, k, v, k_ctx, v_ctx, sg, sink, n_batch, seq):
    kvw = A_KV_HEADS * HEAD_DIM
    nb = seq // A_QBLOCK
    own = lambda b, i: (b * nb + i, 0)
    prev = lambda b, i: (b * nb + jnp.maximum(i - 1, 0), 0)
    nxt = lambda b, i: (b * nb + jnp.minimum(i + 1, nb - 1), 0)
    ctx = lambda b, i: (b, 0, 0)
    n_ctx = k_ctx.shape[1]
    return pl.pallas_call(
        functools.partial(_attn_a_kernel, latent=True, n_qblocks=nb),
        out_shape=jax.ShapeDtypeStruct(q.shape, BF16),
        grid=(n_batch, nb),
        in_specs=[
            pl.BlockSpec(memory_space=pltpu.SMEM),
            pl.BlockSpec((A_QBLOCK, D_MODEL), own),
            pl.BlockSpec((A_QBLOCK, kvw), prev),
            pl.BlockSpec((A_QBLOCK, kvw), own),
            pl.BlockSpec((A_QBLOCK, kvw), nxt),
            pl.BlockSpec((A_QBLOCK, kvw), prev),
            pl.BlockSpec((A_QBLOCK, kvw), own),
            pl.BlockSpec((A_QBLOCK, kvw), nxt),
            pl.BlockSpec((None, n_ctx, kvw), ctx),
            pl.BlockSpec((None, n_ctx, kvw), ctx),
            pl.BlockSpec((A_QBLOCK, D_MODEL), own),
        ],
        out_specs=pl.BlockSpec((A_QBLOCK, D_MODEL), own),
        scratch_shapes=_attn_a_scratch(A_QBLOCK, 3 * A_QBLOCK + n_ctx),
        compiler_params=_params("arbitrary", "arbitrary"),
        name="attn_a_lat",
    )(sink, q, k, k, k, v, v, v, k_ctx, v_ctx, sg)


def _attn_b_kernel(*refs, has_ctx, lambda_init, k_chunk):
    if has_ctx:
        (lam_ref, sub_ref, q_ref, k_ref, v_ref, kc_ref, vc_ref, sg_ref, o_ref,
         s_scr, sc_scr, m_scr, l_scr, acc_scr) = refs
    else:
        lam_ref, sub_ref, q_ref, k_ref, v_ref, sg_ref, o_ref, s_scr, m_scr, l_scr, acc_scr = refs
        kc_ref = vc_ref = sc_scr = None
    dv = 2 * HEAD_DIM
    for head in range(q_ref.shape[1] // dv):
        cols = slice(head * dv, (head + 1) * dv)
        view = lambda ref: None if ref is None else ref.at[:, cols]
        _attn_b_head(lam_ref, sub_ref, view(q_ref), view(k_ref), view(v_ref), view(kc_ref), view(vc_ref),
                     view(sg_ref), view(o_ref), s_scr, sc_scr, m_scr, l_scr, acc_scr,
                     lambda_init=lambda_init, k_chunk=k_chunk)


def _attn_b_head(lam_ref, sub_ref, q_ref, k_ref, v_ref, kc_ref, vc_ref, sg_ref, o_ref,
                 s_scr, sc_scr, m_scr, l_scr, acc_scr, *, lambda_init, k_chunk):
    has_ctx = kc_ref is not None
    n_chunks = k_ref.shape[0] // k_chunk
    halves = [slice(h * HEAD_DIM, (h + 1) * HEAD_DIM) for h in range(2)]

    def fold_max(h, s, first):
        m = _lane_fold(s, jnp.maximum)
        m_scr[h] = m if first else jnp.maximum(m_scr[h], m)

    def scores(c, first=False):
        rows = pl.ds(pl.multiple_of(c * k_chunk, k_chunk), k_chunk)
        k = k_ref[rows, :].astype(BF16)
        for h, hc in enumerate(halves):
            s = _nt_dot(q_ref[:, hc], k[:, hc])
            s_scr[h, c] = s
            fold_max(h, s, first)

    if has_ctx:
        kc = kc_ref[...].astype(BF16)
        for h, hc in enumerate(halves):
            s = _nt_dot(q_ref[:, hc], kc[:, hc])
            sc_scr[h] = s
            fold_max(h, s, True)
        lax.fori_loop(0, n_chunks, lambda c, carry: (scores(c), carry)[1], 0, unroll=B_UNROLL)
    else:
        assert n_chunks == 1
        scores(0, first=True)

    for h in range(2):
        m_scr[h] = jnp.broadcast_to(jnp.max(m_scr[h], axis=-1, keepdims=True), m_scr.shape[1:])

    def accumulate(h, s, v, first):
        m = m_scr[h]
        p = jnp.exp2(s - jnp.concatenate([m] * (s.shape[1] // HEAD_DIM), axis=1))
        l = _lane_fold(p, jnp.add)
        pv = jnp.dot(p.astype(BF16), v, preferred_element_type=F32)
        l_scr[h] = l if first else l_scr[h] + l
        acc_scr[h] = pv if first else acc_scr[h] + pv

    def weighted(c, first=False):
        rows = pl.ds(pl.multiple_of(c * k_chunk, k_chunk), k_chunk)
        v = v_ref[rows, :].astype(BF16)
        for h in range(2):
            accumulate(h, s_scr[h, c], v, first)

    if has_ctx:
        vc = vc_ref[...].astype(BF16)
        for h in range(2):
            accumulate(h, sc_scr[h], vc, True)
        lax.fori_loop(0, n_chunks, lambda c, carry: (weighted(c), carry)[1], 0, unroll=B_UNROLL)
    else:
        weighted(0, first=True)

    lam = lam_ref[...]
    lam_full = (jnp.exp(jnp.sum(lam[0:1] * lam[1:2], axis=-1, keepdims=True))
                - jnp.exp(jnp.sum(lam[2:3] * lam[3:4], axis=-1, keepdims=True)) + lambda_init)
    r0 = 1.0 / jnp.sum(l_scr[0], axis=-1, keepdims=True)
    r1 = lam_full / jnp.sum(l_scr[1], axis=-1, keepdims=True)
    o = acc_scr[0] * r0 - acc_scr[1] * r1
    ms = jnp.mean(o * o, axis=-1, keepdims=True)
    o = o * lax.rsqrt(ms + EPS) * (sub_ref[...] * (1.0 - lambda_init))
    o_ref[...] = (o * sg_ref[...].astype(F32)).astype(o_ref.dtype)


def _attn_b(q, k, v, sg, lam, subln, lambda_init, n_batch, seq, k_ctx=None, v_ctx=None):
    dv = 2 * HEAD_DIM
    tq = min(B_QBLOCK, seq)
    nq = seq // tq
    k_chunk = min(B_KCHUNK, seq)
    has_ctx = k_ctx is not None
    heads_per_step = 1 if has_ctx else B_HEADS
    gw = heads_per_step * dv
    qmap = lambda b, h, i: (b * nq + i, h)
    kvmap = lambda b, h, i: (b, h)
    in_specs = [
        pl.BlockSpec((4, HEAD_DIM), lambda b, h, i: (0, 0)),
        pl.BlockSpec((1, dv), lambda b, h, i: (0, 0)),
        pl.BlockSpec((tq, gw), qmap),
        pl.BlockSpec((seq, gw), kvmap),
        pl.BlockSpec((seq, gw), kvmap),
    ]
    args = [lam, subln.reshape(1, dv), q, k, v]
    scratch = [pltpu.VMEM((2, seq // k_chunk, tq, k_chunk), F32)]
    if has_ctx:
        n_ctx = k_ctx.shape[1]
        in_specs += [pl.BlockSpec((None, n_ctx, gw), lambda b, h, i: (b, 0, h))] * 2
        args += [k_ctx, v_ctx]
        scratch.append(pltpu.VMEM((2, tq, n_ctx), F32))
    in_specs.append(pl.BlockSpec((tq, gw), qmap))
    args.append(sg)
    scratch += [pltpu.VMEM((2, tq, HEAD_DIM), F32), pltpu.VMEM((2, tq, HEAD_DIM), F32),
                pltpu.VMEM((2, tq, dv), F32)]
    return pl.pallas_call(
        functools.partial(_attn_b_kernel, has_ctx=has_ctx, lambda_init=lambda_init, k_chunk=k_chunk),
        out_shape=jax.ShapeDtypeStruct(q.shape, BF16),
        grid=(n_batch, B_HEADS // heads_per_step, nq),
        in_specs=in_specs,
        out_specs=pl.BlockSpec((tq, gw), qmap),
        scratch_shapes=scratch,
        compiler_params=_params("arbitrary", "arbitrary", "arbitrary"),
        name="attn_b_lat" if has_ctx else "attn_b_ctx",
    )(*args)


def _attn_c_ctx_kernel(q_ref, k_ref, v_ref, sg_ref, o_ref):
    for h in range(C_HEADS):
        hc = slice(h * HEAD_DIM, (h + 1) * HEAD_DIM)
        s = _nt_dot(q_ref[:, hc], k_ref[:, hc].astype(BF16))
        o = _softmax_pv([s], [v_ref[:, hc].astype(BF16)])
        o_ref[:, hc] = (o * sg_ref[:, hc].astype(F32)).astype(o_ref.dtype)


def _attn_c_ctx(q, k, v, sg, n_batch, seq):
    spec = pl.BlockSpec((seq, D_MODEL), lambda b: (b, 0))
    return pl.pallas_call(
        _attn_c_ctx_kernel,
        out_shape=jax.ShapeDtypeStruct(q.shape, BF16),
        grid=(n_batch,),
        in_specs=[spec, spec, spec, spec],
        out_specs=spec,
        compiler_params=_params("arbitrary"),
        name="attn_c_ctx",
    )(q, k, v, sg)


def _na_geometry(rows):
    kh = min(NA_KH, rows)
    key_rows = C_ROWS + kh - 1
    key_rows = min(key_rows + key_rows % 2, rows)
    n_blocks = rows // C_ROWS
    cols = np.arange(GRID_W)
    col_start = np.clip(cols - NA_KW // 2, 0, GRID_W - NA_KW)
    col_ok = (cols[None, :] >= col_start[:, None]) & (cols[None, :] < col_start[:, None] + NA_KW)
    col_delta = cols[None, :] - cols[:, None] + NA_KW - 1
    col_onehot = (col_delta[None] == np.arange(2 * NA_KW - 1)[:, None, None]) & col_ok[None]
    starts, class_ids, classes, keys = [], [], [], {}
    for blk in range(n_blocks):
        r = blk * C_ROWS + np.arange(C_ROWS)
        rs = np.clip(r - kh // 2, 0, rows - kh)
        start = int(np.clip(rs[0], 0, rows - key_rows))
        key = (tuple(rs - r), start - blk * C_ROWS)
        if key not in keys:
            keys[key] = len(classes)
            kr = start + np.arange(key_rows)
            row_ok = (kr[None, :] >= rs[:, None]) & (kr[None, :] < rs[:, None] + kh)
            row_idx = kr[None, :] - r[:, None] + NA_KH - 1
            classes.append((row_ok, row_idx))
        starts.append(start)
        class_ids.append(keys[key])
    geometry = (col_ok, col_onehot.astype(np.float32), classes)
    return key_rows, np.asarray(starts, np.int32), np.asarray(class_ids, np.int32), geometry


def _na_bias_tables(rpb, geometry):
    col_ok, col_onehot, classes = geometry
    n_heads = rpb.shape[0]
    toep = jnp.einsum("hdx,xck->hdck", rpb * LOG2E, jnp.asarray(col_onehot),
                      precision=lax.Precision.HIGHEST)
    toep = jnp.where(jnp.asarray(col_ok)[None, None], toep, NEG_INF)
    masked = jnp.full((n_heads, GRID_W, GRID_W), NEG_INF, F32)
    tabs = []
    for row_ok, row_idx in classes:
        n_r, n_kr = row_ok.shape
        blocks = [toep[:, int(row_idx[a, u])] if row_ok[a, u] else masked
                  for a in range(n_r) for u in range(n_kr)]
        tab = jnp.stack(blocks, axis=1).reshape(n_heads, n_r, n_kr, GRID_W, GRID_W)
        tabs.append(tab.transpose(0, 1, 3, 2, 4).reshape(n_heads, n_r * GRID_W, n_kr * GRID_W))
    return jnp.stack(tabs, axis=0)


def _attn_c_lat_kernel(start_ref, cls_ref, q_ref, k_ref, v_ref, kc_ref, vc_ref, bias_ref, sg_ref, o_ref,
                       s_scr, m_scr, acc_scr, *, key_rows):
    blk = pl.program_id(2)
    n_loc = key_rows * GRID_W
    n_keys = s_scr.shape[-1]
    rows = pl.ds(pl.multiple_of(start_ref[blk] * GRID_W, GRID_W), n_loc)
    head_cols = [slice(h * HEAD_DIM, (h + 1) * HEAD_DIM) for h in range(C_HEAD_GROUP)]

    for h, hc in enumerate(head_cols):
        q = q_ref[:, hc]
        s_loc = _nt_dot(q, k_ref[rows, hc]) + bias_ref[h]
        s_ctx = _nt_dot(q, kc_ref[:, hc].astype(BF16))
        s_scr[h, :, 0:n_loc] = s_loc
        s_scr[h, :, n_loc:n_keys] = s_ctx
        m_scr[h] = jnp.maximum(_lane_fold(s_loc, jnp.maximum), _lane_fold(s_ctx, jnp.maximum))

    for h in range(C_HEAD_GROUP):
        m_scr[h] = jnp.broadcast_to(jnp.max(m_scr[h], axis=-1, keepdims=True), m_scr.shape[1:])

    for h, hc in enumerate(head_cols):
        p = jnp.exp2(s_scr[h] - jnp.concatenate([m_scr[h]] * (n_keys // HEAD_DIM), axis=1)).astype(BF16)
        acc_scr[h] = (jnp.dot(p[:, 0:n_loc], _with_ones(v_ref[rows, hc]), preferred_element_type=F32)
                      + jnp.dot(p[:, n_loc:n_keys], _with_ones(vc_ref[:, hc].astype(BF16)),
                                preferred_element_type=F32))

    for h, hc in enumerate(head_cols):
        acc = acc_scr[h]
        o = acc[:, :HEAD_DIM] * (1.0 / acc[:, HEAD_DIM:])
        o_ref[:, hc] = (o * sg_ref[:, hc].astype(F32)).astype(o_ref.dtype)


def _attn_c_lat(q, k, v, k_ctx, v_ctx, sg, rpb, n_batch, seq):
    rows = seq // GRID_W
    key_rows, starts, class_ids, geometry = _na_geometry(rows)
    bias = _na_bias_tables(rpb, geometry)
    nblk = rows // C_ROWS
    tq = C_ROWS * GRID_W
    gw = C_HEAD_GROUP * HEAD_DIM
    n_ctx = k_ctx.shape[1]
    qmap = lambda b, g, i, st, cl: (b * nblk + i, g)
    kvmap = lambda b, g, i, st, cl: (b, g)
    ctxmap = lambda b, g, i, st, cl: (b, 0, g)
    grid_spec = pltpu.PrefetchScalarGridSpec(
        num_scalar_prefetch=2,
        grid=(n_batch, C_HEADS // C_HEAD_GROUP, nblk),
        in_specs=[
            pl.BlockSpec((tq, gw), qmap),
            pl.BlockSpec((seq, gw), kvmap),
            pl.BlockSpec((seq, gw), kvmap),
            pl.BlockSpec((None, n_ctx, gw), ctxmap),
            pl.BlockSpec((None, n_ctx, gw), ctxmap),
            pl.BlockSpec((None, C_HEAD_GROUP, tq, key_rows * GRID_W),
                         lambda b, g, i, st, cl: (cl[i], g, 0, 0)),
            pl.BlockSpec((tq, gw), qmap),
        ],
        out_specs=pl.BlockSpec((tq, gw), qmap),
        scratch_shapes=[pltpu.VMEM((C_HEAD_GROUP, tq, key_rows * GRID_W + n_ctx), F32),
                        pltpu.VMEM((C_HEAD_GROUP, tq, HEAD_DIM), F32),
                        pltpu.VMEM((C_HEAD_GROUP, tq, 2 * HEAD_DIM), F32)],
    )
    return pl.pallas_call(
        functools.partial(_attn_c_lat_kernel, key_rows=key_rows),
        out_shape=jax.ShapeDtypeStruct(q.shape, BF16),
        grid_spec=grid_spec,
        compiler_params=_params("arbitrary", "arbitrary", "arbitrary"),
        name="attn_c_lat",
    )(jnp.asarray(starts), jnp.asarray(class_ids), q, k, v, k_ctx, v_ctx, bias, sg)


def _rope_tables(n_tokens):
    d4 = HEAD_DIM // 4
    t = jnp.arange(n_tokens)
    pos = jnp.stack([t // GRID_W, t % GRID_W], axis=-1).astype(F32)
    inv_freq = ROPE_BASE ** (-jnp.arange(d4, dtype=F32) / d4)
    ang = pos[:, :, None] * inv_freq
    cos, sin = jnp.cos(ang), jnp.sin(ang)
    zero = jnp.zeros_like(sin[:, 0])
    cos_full = jnp.concatenate([cos[:, 0], cos[:, 0], cos[:, 1], cos[:, 1]], axis=-1)
    sin_up = jnp.concatenate([-sin[:, 0], zero, -sin[:, 1], zero], axis=-1)
    sin_dn = jnp.concatenate([zero, sin[:, 0], zero, sin[:, 1]], axis=-1)
    return cos_full, sin_up, sin_dn


def kernel(x_prompt, x_sample, cache_a_k, cache_a_v, cache_b_k, cache_b_v, cache_c_k, cache_c_v, c, c_ctx,
           ln_g, ada_w, ada_b, w_out, qn_g, kn_g, w_in_a, sink_a, w_in_b, lam_b, subln_b, w_in_c, rpb_c):
    n_p, seq_p, d = x_prompt.shape
    n_s, seq_s, _ = x_sample.shape
    n_ctx = cache_a_k.shape[2]
    xp = x_prompt.reshape(n_p * seq_p, d)
    xs = x_sample.reshape(n_s * seq_s, d)
    rope_tabs = _rope_tables(seq_s)

    pad = (-(n_s + 1)) % 8
    cvec = jnp.concatenate([c, c_ctx[None, :], jnp.zeros((pad, d), F32)], axis=0)
    mod = _ada_modulation(cvec, ada_w, ada_b)
    mod = mod.reshape(DEPTH, mod.shape[1], 3, 1, d)

    new_kv = {0: ([], []), 1: ([], []), 2: ([], [])}
    for l in range(DEPTH):
        kind, j = l % N_MIXERS, l // N_MIXERS
        sh_s, sc_s, gt_s = (mod[l, :n_s, t] for t in range(3))
        sh_p, sc_p, gt_p = (mod[l, n_s:n_s + 1, t] for t in range(3))
        w_o = w_out[l].astype(BF16)
        if kind == 0:
            kvw = A_KV_HEADS * HEAD_DIM
            w_in, widths, tn = w_in_a[j].astype(BF16), (d, kvw, kvw, d), 512
        elif kind == 1:
            w_in, widths, tn = w_in_b[j].astype(BF16), (d, d, d, d), 512
        else:
            w_in, widths, tn = w_in_c[j].astype(BF16), (d, d, d, d), 512
        rope = rope_tabs if kind != 2 else None
        qp, kp, vp, gp = _in_projection(xp, sh_p, sc_p, ln_g[l], w_in, qn_g[l], kn_g[l], widths, None, F32, tn)
        qs, ks, vs, gs = _in_projection(xs, sh_s, sc_s, ln_g[l], w_in, qn_g[l], kn_g[l], widths, rope, BF16, tn)
        if kind == 0:
            kc = cache_a_k[:, j].reshape(n_s, n_ctx, kvw)
            vc = cache_a_v[:, j].reshape(n_s, n_ctx, kvw)
            op = _attn_a_ctx(qp, kp, vp, gp, sink_a[j], n_p, seq_p)
            os_ = _attn_a_lat(qs, ks, vs, kc, vc, gs, sink_a[j], n_s, seq_s)
            new_kv[0][0].append(kp.reshape(n_p, seq_p, A_KV_HEADS, HEAD_DIM))
            new_kv[0][1].append(vp.reshape(n_p, seq_p, A_KV_HEADS, HEAD_DIM))
        elif kind == 1:
            lambda_init = 0.8 - 0.6 * math.exp(-0.3 * l)
            kc = cache_b_k[:, j].reshape(n_s, n_ctx, d)
            vc = cache_b_v[:, j].reshape(n_s, n_ctx, d)
            op = _attn_b(qp, kp, vp, gp, lam_b[j], subln_b[j], lambda_init, n_p, seq_p)
            os_ = _attn_b(qs, ks, vs, gs, lam_b[j], subln_b[j], lambda_init, n_s, seq_s, kc, vc)
            new_kv[1][0].append(kp.reshape(n_p, seq_p, B_HEADS, 2, HEAD_DIM))
            new_kv[1][1].append(vp.reshape(n_p, seq_p, B_HEADS, 2 * HEAD_DIM))
        else:
            kc = cache_c_k[:, j].reshape(n_s, n_ctx, d)
            vc = cache_c_v[:, j].reshape(n_s, n_ctx, d)
            op = _attn_c_ctx(qp, kp, vp, gp, n_p, seq_p)
            os_ = _attn_c_lat(qs, ks, vs, kc, vc, gs, rpb_c[j], n_s, seq_s)
            new_kv[2][0].append(kp.reshape(n_p, seq_p, C_HEADS, HEAD_DIM))
            new_kv[2][1].append(vp.reshape(n_p, seq_p, C_HEADS, HEAD_DIM))
        xp = _out_projection(op, w_o, xp, gt_p)
        xs = _out_projection(os_, w_o, xs, gt_s)

    outs = [xp.reshape(x_prompt.shape), xs.reshape(x_sample.shape)]
    for kind in range(3):
        outs.append(jnp.stack(new_kv[kind][0], axis=1))
        outs.append(jnp.stack(new_kv[kind][1], axis=1))
    return tuple(outs)
```

```python
import functools
import math

import numpy as np
import jax
import jax.numpy as jnp
from jax import lax
from jax.experimental import pallas as pl
from jax.experimental.pallas import tpu as pltpu

D_MODEL = 2048
DEPTH = 4
GRID_W = 64
HEAD_DIM = 128
ROPE_BASE = 10000.0
NEG_INF = -1e30
N_MIXERS = 3
A_HEADS = D_MODEL // HEAD_DIM
A_KV_HEADS = A_HEADS // 4
A_GROUP = A_HEADS // A_KV_HEADS
WINDOW = 128
B_HEADS = D_MODEL // (2 * HEAD_DIM)
C_HEADS = D_MODEL // HEAD_DIM
NA_KH = 8
NA_KW = 16
SCALE = HEAD_DIM ** -0.5
LOG2E = math.log2(math.e)
Q_SCALE = SCALE * LOG2E
EPS = 1e-6

BF16 = jnp.bfloat16
F32 = jnp.float32

VMEM_LIMIT_BYTES = 56 * 1024 * 1024
ROW_TILE = 1024
OUT_ROW_TILE = 512
OUT_SUB = 512
PROJ_SUB = 128
A_QBLOCK = 128
B_QBLOCK = 1024
B_KCHUNK = 512
B_UNROLL = 4
C_ROWS = 4
C_HEAD_GROUP = 4


def _nt_dot(a, b):
    return lax.dot_general(a, b, (((1,), (1,)), ((), ())), preferred_element_type=F32)


def _params(*semantics):
    return pltpu.CompilerParams(dimension_semantics=semantics, vmem_limit_bytes=VMEM_LIMIT_BYTES)


def _ada_kernel(c_ref, w_ref, b_ref, o_ref):
    c = c_ref[...]
    a = (c * jax.nn.sigmoid(c)).astype(BF16)
    o_ref[...] = jnp.dot(a, w_ref[...].astype(BF16), preferred_element_type=F32) + b_ref[...]


def _ada_modulation(cvec, ada_w, ada_b):
    rows = cvec.shape[0]
    tn = 768
    return pl.pallas_call(
        _ada_kernel,
        out_shape=jax.ShapeDtypeStruct((DEPTH, rows, 3 * D_MODEL), F32),
        grid=(DEPTH, 3 * D_MODEL // tn),
        in_specs=[
            pl.BlockSpec((rows, D_MODEL), lambda l, j: (0, 0)),
            pl.BlockSpec((None, D_MODEL, tn), lambda l, j: (l, 0, j)),
            pl.BlockSpec((None, 1, tn), lambda l, j: (l, 0, j)),
        ],
        out_specs=pl.BlockSpec((None, rows, tn), lambda l, j: (l, 0, j)),
        compiler_params=_params("arbitrary", "arbitrary"),
        name="ada_modulation",
    )(cvec, ada_w, ada_b.reshape(DEPTH, 1, 3 * D_MODEL))


def _inproj_kernel(*refs, n_q, rope, heads_per_tile):
    x_ref, sh_ref, sc_ref, lng_ref, w1_ref, w2_ref, qg_ref, kg_ref = refs[:8]
    if rope:
        cos_ref, sa_ref, sb_ref = refs[8:11]
    q_out, k_out, v_out, g_out, h_scr = refs[-5:]
    j = pl.program_id(1)

    def modulated_norm(rows):
        x = x_ref[rows, :]
        ms = jnp.mean(x * x, axis=-1, keepdims=True)
        gain = lng_ref[...] * (1.0 + sc_ref[...])
        return (x * lax.rsqrt(ms + EPS) * gain + sh_ref[...]).astype(BF16)

    def normed(acc, rows, gain_ref, scale):
        outs = []
        for t in range(heads_per_tile):
            y = acc[:, t * HEAD_DIM:(t + 1) * HEAD_DIM]
            ms = jnp.mean(y * y, axis=-1, keepdims=True)
            y = y * lax.rsqrt(ms + EPS) * (gain_ref[...] * scale)
            if rope:
                y = (y * cos_ref[rows, :] + pltpu.roll(y, 96, 1) * sa_ref[rows, :]
                     + pltpu.roll(y, 32, 1) * sb_ref[rows, :])
            outs.append(y)
        return jnp.concatenate(outs, axis=-1)

    def project(w_ref, out_ref, epilogue, fill_h=False):
        for r in range(h_scr.shape[0] // PROJ_SUB):
            rows = slice(r * PROJ_SUB, (r + 1) * PROJ_SUB)
            if fill_h:
                h_scr[rows, :] = modulated_norm(rows)
            acc = jnp.dot(h_scr[rows, :], w_ref[...], preferred_element_type=F32)
            out_ref[rows, :] = epilogue(acc, rows).astype(out_ref.dtype)

    def q_and_gate(fill_h):
        project(w1_ref, q_out, lambda acc, rows: normed(acc, rows, qg_ref, Q_SCALE), fill_h)
        project(w2_ref, g_out, lambda acc, rows: acc * jax.nn.sigmoid(acc))

    @pl.when(j == 0)
    def _():
        q_and_gate(True)

    @pl.when((j > 0) & (j < n_q))
    def _():
        q_and_gate(False)

    @pl.when(j >= n_q)
    def _():
        project(w1_ref, k_out, lambda acc, rows: normed(acc, rows, kg_ref, 1.0))
        project(w2_ref, v_out, lambda acc, rows: acc)


def _in_projection(x, shift, scale, ln_g, w, qg, kg, widths, rope_tabs, kv_dtype, tn):
    t_rows = x.shape[0]
    qw, kw, vw, gw = widths
    assert qw == gw and kw == vw
    n_q, n_k = qw // tn, kw // tn
    tm = min(ROW_TILE, t_rows)
    rows_per_mod = t_rows // shift.shape[0]
    assert t_rows % tm == 0 and rows_per_mod % tm == 0
    rope = rope_tabs is not None
    g_col0, v_col0 = (qw + kw + vw) // tn, (qw + kw) // tn

    def mod_map(i, j):
        return ((i * tm) // rows_per_mod, 0, 0)

    in_specs = [
        pl.BlockSpec((tm, D_MODEL), lambda i, j: (i, 0)),
        pl.BlockSpec((None, 1, D_MODEL), mod_map),
        pl.BlockSpec((None, 1, D_MODEL), mod_map),
        pl.BlockSpec((1, D_MODEL), lambda i, j: (0, 0)),
        pl.BlockSpec((D_MODEL, tn), lambda i, j: (0, j)),
        pl.BlockSpec((D_MODEL, tn), lambda i, j: (0, jnp.where(j < n_q, g_col0 + j, v_col0 + j - n_q))),
        pl.BlockSpec((1, HEAD_DIM), lambda i, j: (0, 0)),
        pl.BlockSpec((1, HEAD_DIM), lambda i, j: (0, 0)),
    ]
    args = [x, shift, scale, ln_g.reshape(1, D_MODEL), w, w, qg.reshape(1, HEAD_DIM), kg.reshape(1, HEAD_DIM)]
    if rope:
        n_pos_tiles = rope_tabs[0].shape[0] // tm
        for tab in rope_tabs:
            in_specs.append(pl.BlockSpec((tm, HEAD_DIM), lambda i, j: (i % n_pos_tiles, 0)))
            args.append(tab)
    out_specs = [
        pl.BlockSpec((tm, tn), lambda i, j: (i, jnp.minimum(j, n_q - 1))),
        pl.BlockSpec((tm, tn), lambda i, j: (i, jnp.maximum(j - n_q, 0))),
        pl.BlockSpec((tm, tn), lambda i, j: (i, jnp.maximum(j - n_q, 0))),
        pl.BlockSpec((tm, tn), lambda i, j: (i, jnp.minimum(j, n_q - 1))),
    ]
    out_shape = [
        jax.ShapeDtypeStruct((t_rows, qw), BF16),
        jax.ShapeDtypeStruct((t_rows, kw), kv_dtype),
        jax.ShapeDtypeStruct((t_rows, vw), kv_dtype),
        jax.ShapeDtypeStruct((t_rows, gw), BF16),
    ]
    kern = functools.partial(_inproj_kernel, n_q=n_q, rope=rope, heads_per_tile=tn // HEAD_DIM)
    return pl.pallas_call(
        kern,
        out_shape=out_shape,
        grid=(t_rows // tm, n_q + n_k),
        in_specs=in_specs,
        out_specs=out_specs,
        scratch_shapes=[pltpu.VMEM((tm, D_MODEL), BF16)],
        compiler_params=_params("arbitrary", "arbitrary"),
        name="in_projection",
    )(*args)


def _outproj_kernel(o_ref, w_ref, x_ref, gt_ref, y_ref):
    for c in range(D_MODEL // OUT_SUB):
        cols = slice(c * OUT_SUB, (c + 1) * OUT_SUB)
        y = jnp.dot(o_ref[...], w_ref[:, cols], preferred_element_type=F32)
        y_ref[:, cols] = x_ref[:, cols] + gt_ref[:, cols] * y


def _out_projection(o, w, x, gate):
    t_rows = x.shape[0]
    tm = min(OUT_ROW_TILE, t_rows)
    rows_per_mod = t_rows // gate.shape[0]
    assert t_rows % tm == 0 and rows_per_mod % tm == 0
    row = lambda i: (i, 0)
    return pl.pallas_call(
        _outproj_kernel,
        out_shape=jax.ShapeDtypeStruct((t_rows, D_MODEL), F32),
        grid=(t_rows // tm,),
        in_specs=[
            pl.BlockSpec((tm, D_MODEL), row),
            pl.BlockSpec((D_MODEL, D_MODEL), lambda i: (0, 0)),
            pl.BlockSpec((tm, D_MODEL), row),
            pl.BlockSpec((None, 1, D_MODEL), lambda i: ((i * tm) // rows_per_mod, 0, 0)),
        ],
        out_specs=pl.BlockSpec((tm, D_MODEL), row),
        compiler_params=_params("arbitrary"),
        name="out_projection",
    )(o, w, x, gate)


def _lane_fold(x, op):
    r = x[:, 0:HEAD_DIM]
    for t in range(1, x.shape[1] // HEAD_DIM):
        r = op(r, x[:, t * HEAD_DIM:(t + 1) * HEAD_DIM])
    return r


def _with_ones(v):
    return jnp.concatenate([v, jnp.ones(v.shape, v.dtype)], axis=1)


def _softmax_pv(scores, values):
    m128 = _lane_fold(scores[0], jnp.maximum)
    for s in scores[1:]:
        m128 = jnp.maximum(m128, _lane_fold(s, jnp.maximum))
    m = jnp.broadcast_to(jnp.max(m128, axis=-1, keepdims=True), m128.shape)
    acc = None
    for s, v in zip(scores, values):
        p = jnp.exp2(s - jnp.concatenate([m] * (s.shape[1] // HEAD_DIM), axis=1)).astype(BF16)
        pv = jnp.dot(p, _with_ones(v), preferred_element_type=F32)
        acc = pv if acc is None else acc + pv
    return acc[:, :HEAD_DIM] * (1.0 / acc[:, HEAD_DIM:])


def _attn_a_kernel(*refs, latent, n_qblocks):
    if latent:
        (sink_ref, q_ref, kp_ref, ko_ref, kn_ref, vp_ref, vo_ref, vn_ref,
         kc_ref, vc_ref, sg_ref, o_ref, s_scr, m_scr, acc_scr) = refs
    else:
        sink_ref, q_ref, k_ref, v_ref, sg_ref, o_ref, s_scr, m_scr, acc_scr = refs
    m_rows = q_ref.shape[0]
    n_keys = s_scr.shape[-1]
    n_loc = 3 * A_QBLOCK
    kv_cols = [slice(kv * HEAD_DIM, (kv + 1) * HEAD_DIM) for kv in range(A_KV_HEADS)]
    kv_heads = [[kv * A_GROUP + g for g in range(A_GROUP)] for kv in range(A_KV_HEADS)]

    if latent:
        blk = pl.program_id(1)
        qpos = lax.broadcasted_iota(jnp.int32, (A_QBLOCK, n_loc), 0)
        col = lax.broadcasted_iota(jnp.int32, (A_QBLOCK, n_loc), 1)
        rel = col - A_QBLOCK - qpos
        kpos = (blk - 1) * A_QBLOCK + col
        valid = (jnp.abs(rel) <= WINDOW) & (kpos >= 0) & (kpos < n_qblocks * A_QBLOCK)
        window_bias = jnp.concatenate([jnp.where(valid, 0.0, NEG_INF)] * A_GROUP, axis=0)

    for kv, cols in enumerate(kv_cols):
        q = jnp.concatenate([q_ref[:, h * HEAD_DIM:(h + 1) * HEAD_DIM] for h in kv_heads[kv]], axis=0)
        if latent:
            k_loc = jnp.concatenate([kp_ref[:, cols], ko_ref[:, cols], kn_ref[:, cols]], axis=0)
            s_loc = _nt_dot(q, k_loc) + window_bias
            s_ctx = _nt_dot(q, kc_ref[:, cols].astype(BF16))
            s_scr[kv, :, 0:n_loc] = s_loc
            s_scr[kv, :, n_loc:n_keys] = s_ctx
            m_scr[kv] = jnp.maximum(_lane_fold(s_loc, jnp.maximum), _lane_fold(s_ctx, jnp.maximum))
        else:
            s = _nt_dot(q, k_ref[:, cols].astype(BF16))
            s_scr[kv] = s
            m_scr[kv] = _lane_fold(s, jnp.maximum)

    def sink_lanes(kv):
        return jnp.concatenate([jnp.full((m_rows, HEAD_DIM), sink_ref[h] * LOG2E, F32) for h in kv_heads[kv]], axis=0)

    for kv in range(A_KV_HEADS):
        m = jnp.max(m_scr[kv], axis=-1, keepdims=True)
        m_scr[kv] = jnp.maximum(jnp.broadcast_to(m, m_scr.shape[1:]), sink_lanes(kv))

    for kv, cols in enumerate(kv_cols):
        p = jnp.exp2(s_scr[kv] - jnp.concatenate([m_scr[kv]] * (n_keys // HEAD_DIM), axis=1)).astype(BF16)
        if latent:
            v_loc = jnp.concatenate([vp_ref[:, cols], vo_ref[:, cols], vn_ref[:, cols]], axis=0)
            acc_scr[kv] = (jnp.dot(p[:, 0:n_loc], _with_ones(v_loc), preferred_element_type=F32)
                           + jnp.dot(p[:, n_loc:n_keys], _with_ones(vc_ref[:, cols].astype(BF16)),
                                     preferred_element_type=F32))
        else:
            acc_scr[kv] = jnp.dot(p, _with_ones(v_ref[:, cols].astype(BF16)), preferred_element_type=F32)

    for kv in range(A_KV_HEADS):
        acc = acc_scr[kv]
        l = acc[:, HEAD_DIM:] + jnp.exp2(sink_lanes(kv) - m_scr[kv])
        o = acc[:, :HEAD_DIM] * (1.0 / l)
        for g, h in enumerate(kv_heads[kv]):
            hc = slice(h * HEAD_DIM, (h + 1) * HEAD_DIM)
            o_ref[:, hc] = (o[g * m_rows:(g + 1) * m_rows] * sg_ref[:, hc].astype(F32)).astype(o_ref.dtype)


def _attn_a_scratch(m_rows, n_keys):
    stacked = A_GROUP * m_rows
    return [pltpu.VMEM((A_KV_HEADS, stacked, n_keys), F32), pltpu.VMEM((A_KV_HEADS, stacked, HEAD_DIM), F32),
            pltpu.VMEM((A_KV_HEADS, stacked, 2 * HEAD_DIM), F32)]


def _attn_a_ctx(q, k, v, sg, sink, n_batch, seq):
    kvw = A_KV_HEADS * HEAD_DIM
    row = lambda b: (b, 0)
    return pl.pallas_call(
        functools.partial(_attn_a_kernel, latent=False, n_qblocks=1),
        out_shape=jax.ShapeDtypeStruct(q.shape, BF16),
        grid=(n_batch,),
        in_specs=[
            pl.BlockSpec(memory_space=pltpu.SMEM),
            pl.BlockSpec((seq, D_MODEL), row),
            pl.BlockSpec((seq, kvw), row),
            pl.BlockSpec((seq, kvw), row),
            pl.BlockSpec((seq, D_MODEL), row),
        ],
        out_specs=pl.BlockSpec((seq, D_MODEL), row),
        scratch_shapes=_attn_a_scratch(seq, seq),
        compiler_params=_params("arbitrary"),
        name="attn_a_ctx",
    )(sink, q, k, v, sg)


def _attn_a_lat(q, k, v, k_ctx, v_ctx, sg, sink, n_batch, seq):
    kvw = A_KV_HEADS * HEAD_DIM
    nb = seq // A_QBLOCK
    own = lambda b, i: (b * nb + i, 0)
    prev = lambda b, i: (b * nb + jnp.maximum(i - 1, 0), 0)
    nxt = lambda b, i: (b * nb + jnp.minimum(i + 1, nb - 1), 0)
    ctx = lambda b, i: (b, 0, 0)
    n_ctx = k_ctx.shape[1]
    return pl.pallas_call(
        functools.partial(_attn_a_kernel, latent=True, n_qblocks=nb),
        out_shape=jax.ShapeDtypeStruct(q.shape, BF16),
        grid=(n_batch, nb),
        in_specs=[
            pl.BlockSpec(memory_space=pltpu.SMEM),
            pl.BlockSpec((A_QBLOCK, D_MODEL), own),
            pl.BlockSpec((A_QBLOCK, kvw), prev),
            pl.BlockSpec((A_QBLOCK, kvw), own),
            pl.BlockSpec((A_QBLOCK, kvw), nxt),
            pl.BlockSpec((A_QBLOCK, kvw), prev),
            pl.BlockSpec((A_QBLOCK, kvw), own),
            pl.BlockSpec((A_QBLOCK, kvw), nxt),
            pl.BlockSpec((None, n_ctx, kvw), ctx),
            pl.BlockSpec((None, n_ctx, kvw), ctx),
            pl.BlockSpec((A_QBLOCK, D_MODEL), own),
        ],
        out_specs=pl.BlockSpec((A_QBLOCK, D_MODEL), own),
        scratch_shapes=_attn_a_scratch(A_QBLOCK, 3 * A_QBLOCK + n_ctx),
        compiler_params=_params("arbitrary", "arbitrary"),
        name="attn_a_lat",
    )(sink, q, k, k, k, v, v, v, k_ctx, v_ctx, sg)


def _attn_b_kernel(*refs, has_ctx, lambda_init, k_chunk):
    if has_ctx:
        (lam_ref, sub_ref, q_ref, k_ref, v_ref, kc_ref, vc_ref, sg_ref, o_ref,
         s_scr, sc_scr, m_scr, l_scr, acc_scr) = refs
    else:
        lam_ref, sub_ref, q_ref, k_ref, v_ref, sg_ref, o_ref, s_scr, m_scr, l_scr, acc_scr = refs
        kc_ref = vc_ref = sc_scr = None
    dv = 2 * HEAD_DIM
    for head in range(q_ref.shape[1] // dv):
        cols = slice(head * dv, (head + 1) * dv)
        view = lambda ref: None if ref is None else ref.at[:, cols]
        _attn_b_head(lam_ref, sub_ref, view(q_ref), view(k_ref), view(v_ref), view(kc_ref), view(vc_ref),
                     view(sg_ref), view(o_ref), s_scr, sc_scr, m_scr, l_scr, acc_scr,
                     lambda_init=lambda_init, k_chunk=k_chunk)


def _attn_b_head(lam_ref, sub_ref, q_ref, k_ref, v_ref, kc_ref, vc_ref, sg_ref, o_ref,
                 s_scr, sc_scr, m_scr, l_scr, acc_scr, *, lambda_init, k_chunk):
    has_ctx = kc_ref is not None
    n_chunks = k_ref.shape[0] // k_chunk
    halves = [slice(h * HEAD_DIM, (h + 1) * HEAD_DIM) for h in range(2)]

    def fold_max(h, s, first):
        m = _lane_fold(s, jnp.maximum)
        m_scr[h] = m if first else jnp.maximum(m_scr[h], m)

    def scores(c, first=False):
        rows = pl.ds(pl.multiple_of(c * k_chunk, k_chunk), k_chunk)
        k = k_ref[rows, :].astype(BF16)
        for h, hc in enumerate(halves):
            s = _nt_dot(q_ref[:, hc], k[:, hc])
            s_scr[h, c] = s
            fold_max(h, s, first)

    if has_ctx:
        kc = kc_ref[...].astype(BF16)
        for h, hc in enumerate(halves):
            s = _nt_dot(q_ref[:, hc], kc[:, hc])
            sc_scr[h] = s
            fold_max(h, s, True)
        lax.fori_loop(0, n_chunks, lambda c, carry: (scores(c), carry)[1], 0, unroll=B_UNROLL)
    else:
        assert n_chunks == 1
        scores(0, first=True)

    for h in range(2):
        m_scr[h] = jnp.broadcast_to(jnp.max(m_scr[h], axis=-1, keepdims=True), m_scr.shape[1:])

    def accumulate(h, s, v, first):
        m = m_scr[h]
        p = jnp.exp2(s - jnp.concatenate([m] * (s.shape[1] // HEAD_DIM), axis=1))
        l = _lane_fold(p, jnp.add)
        pv = jnp.dot(p.astype(BF16), v, preferred_element_type=F32)
        l_scr[h] = l if first else l_scr[h] + l
        acc_scr[h] = pv if first else acc_scr[h] + pv

    def weighted(c, first=False):
        rows = pl.ds(pl.multiple_of(c * k_chunk, k_chunk), k_chunk)
        v = v_ref[rows, :].astype(BF16)
        for h in range(2):
            accumulate(h, s_scr[h, c], v, first)

    if has_ctx:
        vc = vc_ref[...].astype(BF16)
        for h in range(2):
            accumulate(h, sc_scr[h], vc, True)
        lax.fori_loop(0, n_chunks, lambda c, carry: (weighted(c), carry)[1], 0, unroll=B_UNROLL)
    else:
        weighted(0, first=True)

    lam = lam_ref[...]
    lam_full = (jnp.exp(jnp.sum(lam[0:1] * lam[1:2], axis=-1, keepdims=True))
                - jnp.exp(jnp.sum(lam[2:3] * lam[3:4], axis=-1, keepdims=True)) + lambda_init)
    r0 = 1.0 / jnp.sum(l_scr[0], axis=-1, keepdims=True)
    r1 = lam_full / jnp.sum(l_scr[1], axis=-1, keepdims=True)
    o = acc_scr[0] * r0 - acc_scr[1] * r1
    ms = jnp.mean(o * o, axis=-1, keepdims=True)
    o = o * lax.rsqrt(ms + EPS) * (sub_ref[...] * (1.0 - lambda_init))
    o_ref[...] = (o * sg_ref[...].astype(F32)).astype(o_ref.dtype)


def _attn_b(q, k, v, sg, lam, subln, lambda_init, n_batch, seq, k_ctx=None, v_ctx=None):
    dv = 2 * HEAD_DIM
    tq = min(B_QBLOCK, seq)
    nq = seq // tq
    k_chunk = min(B_KCHUNK, seq)
    has_ctx = k_ctx is not None
    heads_per_step = 1 if has_ctx else B_HEADS
    gw = heads_per_step * dv
    qmap = lambda b, h, i: (b * nq + i, h)
    kvmap = lambda b, h, i: (b, h)
    in_specs = [
        pl.BlockSpec((4, HEAD_DIM), lambda b, h, i: (0, 0)),
        pl.BlockSpec((1, dv), lambda b, h, i: (0, 0)),
        pl.BlockSpec((tq, gw), qmap),
        pl.BlockSpec((seq, gw), kvmap),
        pl.BlockSpec((seq, gw), kvmap),
    ]
    args = [lam, subln.reshape(1, dv), q, k, v]
    scratch = [pltpu.VMEM((2, seq // k_chunk, tq, k_chunk), F32)]
    if has_ctx:
        n_ctx = k_ctx.shape[1]
        in_specs += [pl.BlockSpec((None, n_ctx, gw), lambda b, h, i: (b, 0, h))] * 2
        args += [k_ctx, v_ctx]
        scratch.append(pltpu.VMEM((2, tq, n_ctx), F32))
    in_specs.append(pl.BlockSpec((tq, gw), qmap))
    args.append(sg)
    scratch += [pltpu.VMEM((2, tq, HEAD_DIM), F32), pltpu.VMEM((2, tq, HEAD_DIM), F32),
                pltpu.VMEM((2, tq, dv), F32)]
    return pl.pallas_call(
        functools.partial(_attn_b_kernel, has_ctx=has_ctx, lambda_init=lambda_init, k_chunk=k_chunk),
        out_shape=jax.ShapeDtypeStruct(q.shape, BF16),
        grid=(n_batch, B_HEADS // heads_per_step, nq),
        in_specs=in_specs,
        out_specs=pl.BlockSpec((tq, gw), qmap),
        scratch_shapes=scratch,
        compiler_params=_params("arbitrary", "arbitrary", "arbitrary"),
        name="attn_b_lat" if has_ctx else "attn_b_ctx",
    )(*args)


def _attn_c_ctx_kernel(q_ref, k_ref, v_ref, sg_ref, o_ref):
    for h in range(C_HEADS):
        hc = slice(h * HEAD_DIM, (h + 1) * HEAD_DIM)
        s = _nt_dot(q_ref[:, hc], k_ref[:, hc].astype(BF16))
        o = _softmax_pv([s], [v_ref[:, hc].astype(BF16)])
        o_ref[:, hc] = (o * sg_ref[:, hc].astype(F32)).astype(o_ref.dtype)


def _attn_c_ctx(q, k, v, sg, n_batch, seq):
    spec = pl.BlockSpec((seq, D_MODEL), lambda b: (b, 0))
    return pl.pallas_call(
        _attn_c_ctx_kernel,
        out_shape=jax.ShapeDtypeStruct(q.shape, BF16),
        grid=(n_batch,),
        in_specs=[spec, spec, spec, spec],
        out_specs=spec,
        compiler_params=_params("arbitrary"),
        name="attn_c_ctx",
    )(q, k, v, sg)


def _na_geometry(rows):
    kh = min(NA_KH, rows)
    key_rows = C_ROWS + kh - 1
    key_rows = min(key_rows + key_rows % 2, rows)
    n_blocks = rows // C_ROWS
    cols = np.arange(GRID_W)
    col_start = np.clip(cols - NA_KW // 2, 0, GRID_W - NA_KW)
    col_ok = (cols[None, :] >= col_start[:, None]) & (cols[None, :] < col_start[:, None] + NA_KW)
    col_delta = cols[None, :] - cols[:, None] + NA_KW - 1
    col_onehot = (col_delta[None] == np.arange(2 * NA_KW - 1)[:, None, None]) & col_ok[None]
    starts, class_ids, classes, keys = [], [], [], {}
    for blk in range(n_blocks):
        r = blk * C_ROWS + np.arange(C_ROWS)
        rs = np.clip(r - kh // 2, 0, rows - kh)
        start = int(np.clip(rs[0], 0, rows - key_rows))
        key = (tuple(rs - r), start - blk * C_ROWS)
        if key not in keys:
            keys[key] = len(classes)
            kr = start + np.arange(key_rows)
            row_ok = (kr[None, :] >= rs[:, None]) & (kr[None, :] < rs[:, None] + kh)
            row_idx = kr[None, :] - r[:, None] + NA_KH - 1
            classes.append((row_ok, row_idx))
        starts.append(start)
        class_ids.append(keys[key])
    geometry = (col_ok, col_onehot.astype(np.float32), classes)
    return key_rows, np.asarray(starts, np.int32), np.asarray(class_ids, np.int32), geometry


def _na_bias_tables(rpb, geometry):
    col_ok, col_onehot, classes = geometry
    n_heads = rpb.shape[0]
    toep = jnp.einsum("hdx,xck->hdck", rpb * LOG2E, jnp.asarray(col_onehot),
                      precision=lax.Precision.HIGHEST)
    toep = jnp.where(jnp.asarray(col_ok)[None, None], toep, NEG_INF)
    masked = jnp.full((n_heads, GRID_W, GRID_W), NEG_INF, F32)
    tabs = []
    for row_ok, row_idx in classes:
        n_r, n_kr = row_ok.shape
        blocks = [toep[:, int(row_idx[a, u])] if row_ok[a, u] else masked
                  for a in range(n_r) for u in range(n_kr)]
        tab = jnp.stack(blocks, axis=1).reshape(n_heads, n_r, n_kr, GRID_W, GRID_W)
        tabs.append(tab.transpose(0, 1, 3, 2, 4).reshape(n_heads, n_r * GRID_W, n_kr * GRID_W))
    return jnp.stack(tabs, axis=0)


def _attn_c_lat_kernel(start_ref, cls_ref, q_ref, k_ref, v_ref, kc_ref, vc_ref, bias_ref, sg_ref, o_ref,
                       s_scr, m_scr, acc_scr, *, key_rows):
    blk = pl.program_id(2)
    n_loc = key_rows * GRID_W
    n_keys = s_scr.shape[-1]
    rows = pl.ds(pl.multiple_of(start_ref[blk] * GRID_W, GRID_W), n_loc)
    head_cols = [slice(h * HEAD_DIM, (h + 1) * HEAD_DIM) for h in range(C_HEAD_GROUP)]

    for h, hc in enumerate(head_cols):
        q = q_ref[:, hc]
        s_loc = _nt_dot(q, k_ref[rows, hc]) + bias_ref[h]
        s_ctx = _nt_dot(q, kc_ref[:, hc].astype(BF16))
        s_scr[h, :, 0:n_loc] = s_loc
        s_scr[h, :, n_loc:n_keys] = s_ctx
        m_scr[h] = jnp.maximum(_lane_fold(s_loc, jnp.maximum), _lane_fold(s_ctx, jnp.maximum))

    for h in range(C_HEAD_GROUP):
        m_scr[h] = jnp.broadcast_to(jnp.max(m_scr[h], axis=-1, keepdims=True), m_scr.shape[1:])

    for h, hc in enumerate(head_cols):
        p = jnp.exp2(s_scr[h] - jnp.concatenate([m_scr[h]] * (n_keys // HEAD_DIM), axis=1)).astype(BF16)
        acc_scr[h] = (jnp.dot(p[:, 0:n_loc], _with_ones(v_ref[rows, hc]), preferred_element_type=F32)
                      + jnp.dot(p[:, n_loc:n_keys], _with_ones(vc_ref[:, hc].astype(BF16)),
                                preferred_element_type=F32))

    for h, hc in enumerate(head_cols):
        acc = acc_scr[h]
        o = acc[:, :HEAD_DIM] * (1.0 / acc[:, HEAD_DIM:])
        o_ref[:, hc] = (o * sg_ref[:, hc].astype(F32)).astype(o_ref.dtype)


def _attn_c_lat(q, k, v, k_ctx, v_ctx, sg, rpb, n_batch, seq):
    rows = seq // GRID_W
    key_rows, starts, class_ids, geometry = _na_geometry(rows)
    bias = _na_bias_tables(rpb, geometry)
    nblk = rows // C_ROWS
    tq = C_ROWS * GRID_W
    gw = C_HEAD_GROUP * HEAD_DIM
    n_ctx = k_ctx.shape[1]
    qmap = lambda b, g, i, st, cl: (b * nblk + i, g)
    kvmap = lambda b, g, i, st, cl: (b, g)
    ctxmap = lambda b, g, i, st, cl: (b, 0, g)
    grid_spec = pltpu.PrefetchScalarGridSpec(
        num_scalar_prefetch=2,
        grid=(n_batch, C_HEADS // C_HEAD_GROUP, nblk),
        in_specs=[
            pl.BlockSpec((tq, gw), qmap),
            pl.BlockSpec((seq, gw), kvmap),
            pl.BlockSpec((seq, gw), kvmap),
            pl.BlockSpec((None, n_ctx, gw), ctxmap),
            pl.BlockSpec((None, n_ctx, gw), ctxmap),
            pl.BlockSpec((None, C_HEAD_GROUP, tq, key_rows * GRID_W),
                         lambda b, g, i, st, cl: (cl[i], g, 0, 0)),
            pl.BlockSpec((tq, gw), qmap),
        ],
        out_specs=pl.BlockSpec((tq, gw), qmap),
        scratch_shapes=[pltpu.VMEM((C_HEAD_GROUP, tq, key_rows * GRID_W + n_ctx), F32),
                        pltpu.VMEM((C_HEAD_GROUP, tq, HEAD_DIM), F32),
                        pltpu.VMEM((C_HEAD_GROUP, tq, 2 * HEAD_DIM), F32)],
    )
    return pl.pallas_call(
        functools.partial(_attn_c_lat_kernel, key_rows=key_rows),
        out_shape=jax.ShapeDtypeStruct(q.shape, BF16),
        grid_spec=grid_spec,
        compiler_params=_params("arbitrary", "arbitrary", "arbitrary"),
        name="attn_c_lat",
    )(jnp.asarray(starts), jnp.asarray(class_ids), q, k, v, k_ctx, v_ctx, bias, sg)


def _rope_tables(n_tokens):
    d4 = HEAD_DIM // 4
    t = jnp.arange(n_tokens)
    pos = jnp.stack([t // GRID_W, t % GRID_W], axis=-1).astype(F32)
    inv_freq = ROPE_BASE ** (-jnp.arange(d4, dtype=F32) / d4)
    ang = pos[:, :, None] * inv_freq
    cos, sin = jnp.cos(ang), jnp.sin(ang)
    zero = jnp.zeros_like(sin[:, 0])
    cos_full = jnp.concatenate([cos[:, 0], cos[:, 0], cos[:, 1], cos[:, 1]], axis=-1)
    sin_up = jnp.concatenate([-sin[:, 0], zero, -sin[:, 1], zero], axis=-1)
    sin_dn = jnp.concatenate([zero, sin[:, 0], zero, sin[:, 1]], axis=-1)
    return cos_full, sin_up, sin_dn


def kernel(x_prompt, x_sample, cache_a_k, cache_a_v, cache_b_k, cache_b_v, cache_c_k, cache_c_v, c, c_ctx,
           ln_g, ada_w, ada_b, w_out, qn_g, kn_g, w_in_a, sink_a, w_in_b, lam_b, subln_b, w_in_c, rpb_c):
    n_p, seq_p, d = x_prompt.shape
    n_s, seq_s, _ = x_sample.shape
    n_ctx = cache_a_k.shape[2]
    xp = x_prompt.reshape(n_p * seq_p, d)
    xs = x_sample.reshape(n_s * seq_s, d)
    rope_tabs = _rope_tables(seq_s)

    pad = (-(n_s + 1)) % 8
    cvec = jnp.concatenate([c, c_ctx[None, :], jnp.zeros((pad, d), F32)], axis=0)
    mod = _ada_modulation(cvec, ada_w, ada_b)
    mod = mod.reshape(DEPTH, mod.shape[1], 3, 1, d)

    new_kv = {0: ([], []), 1: ([], []), 2: ([], [])}
    for l in range(DEPTH):
        kind, j = l % N_MIXERS, l // N_MIXERS
        sh_s, sc_s, gt_s = (mod[l, :n_s, t] for t in range(3))
        sh_p, sc_p, gt_p = (mod[l, n_s:n_s + 1, t] for t in range(3))
        w_o = w_out[l].astype(BF16)
        if kind == 0:
            kvw = A_KV_HEADS * HEAD_DIM
            w_in, widths, tn = w_in_a[j].astype(BF16), (d, kvw, kvw, d), 512
        elif kind == 1:
            w_in, widths, tn = w_in_b[j].astype(BF16), (d, d, d, d), 512
        else:
            w_in, widths, tn = w_in_c[j].astype(BF16), (d, d, d, d), 512
        rope = rope_tabs if kind != 2 else None
        qp, kp, vp, gp = _in_projection(xp, sh_p, sc_p, ln_g[l], w_in, qn_g[l], kn_g[l], widths, None, F32, tn)
        qs, ks, vs, gs = _in_projection(xs, sh_s, sc_s, ln_g[l], w_in, qn_g[l], kn_g[l], widths, rope, BF16, tn)
        if kind == 0:
            kc = cache_a_k[:, j].reshape(n_s, n_ctx, kvw)
            vc = cache_a_v[:, j].reshape(n_s, n_ctx, kvw)
            op = _attn_a_ctx(qp, kp, vp, gp, sink_a[j], n_p, seq_p)
            os_ = _attn_a_lat(qs, ks, vs, kc, vc, gs, sink_a[j], n_s, seq_s)
            new_kv[0][0].append(kp.reshape(n_p, seq_p, A_KV_HEADS, HEAD_DIM))
            new_kv[0][1].append(vp.reshape(n_p, seq_p, A_KV_HEADS, HEAD_DIM))
        elif kind == 1:
            lambda_init = 0.8 - 0.6 * math.exp(-0.3 * l)
            kc = cache_b_k[:, j].reshape(n_s, n_ctx, d)
            vc = cache_b_v[:, j].reshape(n_s, n_ctx, d)
            op = _attn_b(qp, kp, vp, gp, lam_b[j], subln_b[j], lambda_init, n_p, seq_p)
            os_ = _attn_b(qs, ks, vs, gs, lam_b[j], subln_b[j], lambda_init, n_s, seq_s, kc, vc)
            new_kv[1][0].append(kp.reshape(n_p, seq_p, B_HEADS, 2, HEAD_DIM))
            new_kv[1][1].append(vp.reshape(n_p, seq_p, B_HEADS, 2 * HEAD_DIM))
        else:
            kc = cache_c_k[:, j].reshape(n_s, n_ctx, d)
            vc = cache_c_v[:, j].reshape(n_s, n_ctx, d)
            op = _attn_c_ctx(qp, kp, vp, gp, n_p, seq_p)
            os_ = _attn_c_lat(qs, ks, vs, kc, vc, gs, rpb_c[j], n_s, seq_s)
            new_kv[2][0].append(kp.reshape(n_p, seq_p, C_HEADS, HEAD_DIM))
            new_kv[2][1].append(vp.reshape(n_p, seq_p, C_HEADS, HEAD_DIM))
        xp = _out_projection(op, w_o, xp, gt_p)
        xs = _out_projection(os_, w_o, xs, gt_s)

    outs = [xp.reshape(x_prompt.shape), xs.reshape(x_sample.shape)]
    for kind in range(3):
        outs.append(jnp.stack(new_kv[kind][0], axis=1))
        outs.append(jnp.stack(new_kv[kind][1], axis=1))
    return tuple(outs)
```

```python
import functools
import math

import numpy as np
import jax
import jax.numpy as jnp
from jax import lax
from jax.experimental import pallas as pl
from jax.experimental.pallas import tpu as pltpu

D_MODEL = 2048
DEPTH = 4
GRID_W = 64
HEAD_DIM = 128
ROPE_BASE = 10000.0
NEG_INF = -1e30
N_MIXERS = 3
A_HEADS = D_MODEL // HEAD_DIM
A_KV_HEADS = A_HEADS // 4
A_GROUP = A_HEADS // A_KV_HEADS
WINDOW = 128
B_HEADS = D_MODEL // (2 * HEAD_DIM)
C_HEADS = D_MODEL // HEAD_DIM
NA_KH = 8
NA_KW = 16
SCALE = HEAD_DIM ** -0.5
LOG2E = math.log2(math.e)
Q_SCALE = SCALE * LOG2E
EPS = 1e-6

BF16 = jnp.bfloat16
F32 = jnp.float32

VMEM_LIMIT_BYTES = 56 * 1024 * 1024
ROW_TILE = 1024
CACHE_ROW_TILE = 512
OUT_ROW_TILE = 512
OUT_SUB = 512
PROJ_SUB = 128
A_QBLOCK = 128
B_QBLOCK = 1024
B_KCHUNK = 512
B_UNROLL = 4
C_ROWS = 4
C_HEAD_GROUP = 4


def _nt_dot(a, b):
    return lax.dot_general(a, b, (((1,), (1,)), ((), ())), preferred_element_type=F32)


def _params(*semantics):
    return pltpu.CompilerParams(dimension_semantics=semantics, vmem_limit_bytes=VMEM_LIMIT_BYTES)


def _ada_kernel(c_ref, w_ref, b_ref, o_ref):
    c = c_ref[...]
    a = (c * jax.nn.sigmoid(c)).astype(BF16)
    o_ref[...] = jnp.dot(a, w_ref[...].astype(BF16), preferred_element_type=F32) + b_ref[...]


def _ada_modulation(cvec, ada_w, ada_b):
    rows = cvec.shape[0]
    tn = 768
    return pl.pallas_call(
        _ada_kernel,
        out_shape=jax.ShapeDtypeStruct((DEPTH, rows, 3 * D_MODEL), F32),
        grid=(DEPTH, 3 * D_MODEL // tn),
        in_specs=[
            pl.BlockSpec((rows, D_MODEL), lambda l, j: (0, 0)),
            pl.BlockSpec((None, D_MODEL, tn), lambda l, j: (l, 0, j)),
            pl.BlockSpec((None, 1, tn), lambda l, j: (l, 0, j)),
        ],
        out_specs=pl.BlockSpec((None, rows, tn), lambda l, j: (l, 0, j)),
        compiler_params=_params("arbitrary", "arbitrary"),
        name="ada_modulation",
    )(cvec, ada_w, ada_b.reshape(DEPTH, 1, 3 * D_MODEL))


def _store_cache_copy(ref, rows, y):
    if len(ref.shape) == 2:
        ref[rows, :] = y
        return
    for t in range(y.shape[1] // HEAD_DIM):
        slab = y[:, t * HEAD_DIM:(t + 1) * HEAD_DIM]
        if len(ref.shape) == 3:
            ref[rows, t, :] = slab
        else:
            ref[rows, t // ref.shape[2], t % ref.shape[2], :] = slab


def _inproj_kernel(*refs, n_q, rope, heads_per_tile, has_cache):
    x_ref, sh_ref, sc_ref, lng_ref, w1_ref, w2_ref, qg_ref, kg_ref = refs[:8]
    if rope:
        cos_ref, sa_ref, sb_ref = refs[8:11]
    if has_cache:
        q_out, k_out, v_out, g_out, kc_out, vc_out, h_scr = refs[-7:]
    else:
        q_out, k_out, v_out, g_out, h_scr = refs[-5:]
    j = pl.program_id(1)

    def modulated_norm(rows):
        x = x_ref[rows, :]
        ms = jnp.mean(x * x, axis=-1, keepdims=True)
        gain = lng_ref[...] * (1.0 + sc_ref[...])
        return (x * lax.rsqrt(ms + EPS) * gain + sh_ref[...]).astype(BF16)

    def normed(acc, rows, gain_ref, scale):
        outs = []
        for t in range(heads_per_tile):
            y = acc[:, t * HEAD_DIM:(t + 1) * HEAD_DIM]
            ms = jnp.mean(y * y, axis=-1, keepdims=True)
            y = y * lax.rsqrt(ms + EPS) * (gain_ref[...] * scale)
            if rope:
                y = (y * cos_ref[rows, :] + pltpu.roll(y, 96, 1) * sa_ref[rows, :]
                     + pltpu.roll(y, 32, 1) * sb_ref[rows, :])
            outs.append(y)
        return jnp.concatenate(outs, axis=-1)

    def project(w_ref, out_ref, epilogue, fill_h=False, cache_ref=None):
        for r in range(h_scr.shape[0] // PROJ_SUB):
            rows = slice(r * PROJ_SUB, (r + 1) * PROJ_SUB)
            if fill_h:
                h_scr[rows, :] = modulated_norm(rows)
            acc = jnp.dot(h_scr[rows, :], w_ref[...], preferred_element_type=F32)
            y = epilogue(acc, rows)
            out_ref[rows, :] = y.astype(out_ref.dtype)
            if cache_ref is not None:
                _store_cache_copy(cache_ref, rows, y)

    def q_and_gate(fill_h):
        project(w1_ref, q_out, lambda acc, rows: normed(acc, rows, qg_ref, Q_SCALE), fill_h)
        project(w2_ref, g_out, lambda acc, rows: acc * jax.nn.sigmoid(acc))

    @pl.when(j == 0)
    def _():
        q_and_gate(True)

    @pl.when((j > 0) & (j < n_q))
    def _():
        q_and_gate(False)

    @pl.when(j >= n_q)
    def _():
        project(w1_ref, k_out, lambda acc, rows: normed(acc, rows, kg_ref, 1.0),
                cache_ref=kc_out if has_cache else None)
        project(w2_ref, v_out, lambda acc, rows: acc, cache_ref=vc_out if has_cache else None)


def _cache_copy_spec(t_rows, tm, width, tn, minor, n_q):
    tile = lambda j: jnp.maximum(j - n_q, 0)
    if len(minor) == 2 and width == tn:
        return ((t_rows,) + minor, pl.BlockSpec((tm,) + minor, lambda i, j: (i, 0, 0)))
    if len(minor) == 3 and minor[2] == HEAD_DIM:
        heads = tn // (minor[1] * HEAD_DIM)
        return ((t_rows,) + minor, pl.BlockSpec((tm, heads) + minor[1:], lambda i, j: (i, tile(j), 0, 0)))
    return ((t_rows, width), pl.BlockSpec((tm, tn), lambda i, j: (i, tile(j))))


def _in_projection(x, shift, scale, ln_g, w, qg, kg, widths, rope_tabs, tn, cache_minor=None):
    t_rows = x.shape[0]
    qw, kw, vw, gw = widths
    assert qw == gw and kw == vw
    n_q, n_k = qw // tn, kw // tn
    tm = min(ROW_TILE if cache_minor is None else CACHE_ROW_TILE, t_rows)
    rows_per_mod = t_rows // shift.shape[0]
    assert t_rows % tm == 0 and rows_per_mod % tm == 0
    rope = rope_tabs is not None
    g_col0, v_col0 = (qw + kw + vw) // tn, (qw + kw) // tn

    def mod_map(i, j):
        return ((i * tm) // rows_per_mod, 0, 0)

    in_specs = [
        pl.BlockSpec((tm, D_MODEL), lambda i, j: (i, 0)),
        pl.BlockSpec((None, 1, D_MODEL), mod_map),
        pl.BlockSpec((None, 1, D_MODEL), mod_map),
        pl.BlockSpec((1, D_MODEL), lambda i, j: (0, 0)),
        pl.BlockSpec((D_MODEL, tn), lambda i, j: (0, j)),
        pl.BlockSpec((D_MODEL, tn), lambda i, j: (0, jnp.where(j < n_q, g_col0 + j, v_col0 + j - n_q))),
        pl.BlockSpec((1, HEAD_DIM), lambda i, j: (0, 0)),
        pl.BlockSpec((1, HEAD_DIM), lambda i, j: (0, 0)),
    ]
    args = [x, shift, scale, ln_g.reshape(1, D_MODEL), w, w, qg.reshape(1, HEAD_DIM), kg.reshape(1, HEAD_DIM)]
    if rope:
        n_pos_tiles = rope_tabs[0].shape[0] // tm
        for tab in rope_tabs:
            in_specs.append(pl.BlockSpec((tm, HEAD_DIM), lambda i, j: (i % n_pos_tiles, 0)))
            args.append(tab)
    out_specs = [
        pl.BlockSpec((tm, tn), lambda i, j: (i, jnp.minimum(j, n_q - 1))),
        pl.BlockSpec((tm, tn), lambda i, j: (i, jnp.maximum(j - n_q, 0))),
        pl.BlockSpec((tm, tn), lambda i, j: (i, jnp.maximum(j - n_q, 0))),
        pl.BlockSpec((tm, tn), lambda i, j: (i, jnp.minimum(j, n_q - 1))),
    ]
    out_shape = [jax.ShapeDtypeStruct((t_rows, width), BF16) for width in widths]
    if cache_minor is not None:
        for width, minor in zip((kw, vw), cache_minor):
            shape, spec = _cache_copy_spec(t_rows, tm, width, tn, minor, n_q)
            out_shape.append(jax.ShapeDtypeStruct(shape, F32))
            out_specs.append(spec)
    kern = functools.partial(_inproj_kernel, n_q=n_q, rope=rope, heads_per_tile=tn // HEAD_DIM,
                             has_cache=cache_minor is not None)
    return pl.pallas_call(
        kern,
        out_shape=out_shape,
        grid=(t_rows // tm, n_q + n_k),
        in_specs=in_specs,
        out_specs=out_specs,
        scratch_shapes=[pltpu.VMEM((tm, D_MODEL), BF16)],
        compiler_params=_params("arbitrary", "arbitrary"),
        name="in_projection",
    )(*args)


def _outproj_kernel(o_ref, w_ref, x_ref, gt_ref, y_ref):
    for c in range(D_MODEL // OUT_SUB):
        cols = slice(c * OUT_SUB, (c + 1) * OUT_SUB)
        y = jnp.dot(o_ref[...], w_ref[:, cols], preferred_element_type=F32)
        y_ref[:, cols] = x_ref[:, cols] + gt_ref[:, cols] * y


def _out_projection(o, w, x, gate):
    t_rows = x.shape[0]
    tm = min(OUT_ROW_TILE, t_rows)
    rows_per_mod = t_rows // gate.shape[0]
    assert t_rows % tm == 0 and rows_per_mod % tm == 0
    row = lambda i: (i, 0)
    return pl.pallas_call(
        _outproj_kernel,
        out_shape=jax.ShapeDtypeStruct((t_rows, D_MODEL), F32),
        grid=(t_rows // tm,),
        in_specs=[
            pl.BlockSpec((tm, D_MODEL), row),
            pl.BlockSpec((D_MODEL, D_MODEL), lambda i: (0, 0)),
            pl.BlockSpec((tm, D_MODEL), row),
            pl.BlockSpec((None, 1, D_MODEL), lambda i: ((i * tm) // rows_per_mod, 0, 0)),
        ],
        out_specs=pl.BlockSpec((tm, D_MODEL), row),
        compiler_params=_params("arbitrary"),
        name="out_projection",
    )(o, w, x, gate)


def _lane_fold(x, op):
    r = x[:, 0:HEAD_DIM]
    for t in range(1, x.shape[1] // HEAD_DIM):
        r = op(r, x[:, t * HEAD_DIM:(t + 1) * HEAD_DIM])
    return r


def _with_ones(v):
    return jnp.concatenate([v, jnp.ones(v.shape, v.dtype)], axis=1)


def _softmax_pv(scores, values):
    m128 = _lane_fold(scores[0], jnp.maximum)
    for s in scores[1:]:
        m128 = jnp.maximum(m128, _lane_fold(s, jnp.maximum))
    m = jnp.broadcast_to(jnp.max(m128, axis=-1, keepdims=True), m128.shape)
    acc = None
    for s, v in zip(scores, values):
        p = jnp.exp2(s - jnp.concatenate([m] * (s.shape[1] // HEAD_DIM), axis=1)).astype(BF16)
        pv = jnp.dot(p, _with_ones(v), preferred_element_type=F32)
        acc = pv if acc is None else acc + pv
    return acc[:, :HEAD_DIM] * (1.0 / acc[:, HEAD_DIM:])


def _attn_a_kernel(*refs, latent, n_qblocks):
    if latent:
        (sink_ref, q_ref, kp_ref, ko_ref, kn_ref, vp_ref, vo_ref, vn_ref,
         kc_ref, vc_ref, sg_ref, o_ref, s_scr, m_scr, acc_scr) = refs
    else:
        sink_ref, q_ref, k_ref, v_ref, sg_ref, o_ref, s_scr, m_scr, acc_scr = refs
    m_rows = q_ref.shape[0]
    n_keys = s_scr.shape[-1]
    n_loc = 3 * A_QBLOCK
    kv_cols = [slice(kv * HEAD_DIM, (kv + 1) * HEAD_DIM) for kv in range(A_KV_HEADS)]
    kv_heads = [[kv * A_GROUP + g for g in range(A_GROUP)] for kv in range(A_KV_HEADS)]

    if latent:
        blk = pl.program_id(1)
        qpos = lax.broadcasted_iota(jnp.int32, (A_QBLOCK, n_loc), 0)
        col = lax.broadcasted_iota(jnp.int32, (A_QBLOCK, n_loc), 1)
        rel = col - A_QBLOCK - qpos
        kpos = (blk - 1) * A_QBLOCK + col
        valid = (jnp.abs(rel) <= WINDOW) & (kpos >= 0) & (kpos < n_qblocks * A_QBLOCK)
        window_bias = jnp.concatenate([jnp.where(valid, 0.0, NEG_INF)] * A_GROUP, axis=0)

    for kv, cols in enumerate(kv_cols):
        q = jnp.concatenate([q_ref[:, h * HEAD_DIM:(h + 1) * HEAD_DIM] for h in kv_heads[kv]], axis=0)
        if latent:
            k_loc = jnp.concatenate([kp_ref[:, cols], ko_ref[:, cols], kn_ref[:, cols]], axis=0)
            s_loc = _nt_dot(q, k_loc) + window_bias
            s_ctx = _nt_dot(q, kc_ref[:, cols].astype(BF16))
            s_scr[kv, :, 0:n_loc] = s_loc
            s_scr[kv, :, n_loc:n_keys] = s_ctx
            m_scr[kv] = jnp.maximum(_lane_fold(s_loc, jnp.maximum), _lane_fold(s_ctx, jnp.maximum))
        else:
            s = _nt_dot(q, k_ref[:, cols].astype(BF16))
            s_scr[kv] = s
            m_scr[kv] = _lane_fold(s, jnp.maximum)

    def sink_lanes(kv):
        return jnp.concatenate([jnp.full((m_rows, HEAD_DIM), sink_ref[h] * LOG2E, F32) for h in kv_heads[kv]], axis=0)

    for kv in range(A_KV_HEADS):
        m = jnp.max(m_scr[kv], axis=-1, keepdims=True)
        m_scr[kv] = jnp.maximum(jnp.broadcast_to(m, m_scr.shape[1:]), sink_lanes(kv))

    for kv, cols in enumerate(kv_cols):
        p = jnp.exp2(s_scr[kv] - jnp.concatenate([m_scr[kv]] * (n_keys // HEAD_DIM), axis=1)).astype(BF16)
        if latent:
            v_loc = jnp.concatenate([vp_ref[:, cols], vo_ref[:, cols], vn_ref[:, cols]], axis=0)
            acc_scr[kv] = (jnp.dot(p[:, 0:n_loc], _with_ones(v_loc), preferred_element_type=F32)
                           + jnp.dot(p[:, n_loc:n_keys], _with_ones(vc_ref[:, cols].astype(BF16)),
                                     preferred_element_type=F32))
        else:
            acc_scr[kv] = jnp.dot(p, _with_ones(v_ref[:, cols].astype(BF16)), preferred_element_type=F32)

    for kv in range(A_KV_HEADS):
        acc = acc_scr[kv]
        l = acc[:, HEAD_DIM:] + jnp.exp2(sink_lanes(kv) - m_scr[kv])
        o = acc[:, :HEAD_DIM] * (1.0 / l)
        for g, h in enumerate(kv_heads[kv]):
            hc = slice(h * HEAD_DIM, (h + 1) * HEAD_DIM)
            o_ref[:, hc] = (o[g * m_rows:(g + 1) * m_rows] * sg_ref[:, hc].astype(F32)).astype(o_ref.dtype)


def _attn_a_scratch(m_rows, n_keys):
    stacked = A_GROUP * m_rows
    return [pltpu.VMEM((A_KV_HEADS, stacked, n_keys), F32), pltpu.VMEM((A_KV_HEADS, stacked, HEAD_DIM), F32),
            pltpu.VMEM((A_KV_HEADS, stacked, 2 * HEAD_DIM), F32)]


def _attn_a_ctx(q, k, v, sg, sink, n_batch, seq):
    kvw = A_KV_HEADS * HEAD_DIM
    row = lambda b: (b, 0)
    return pl.pallas_call(
        functools.partial(_attn_a_kernel, latent=False, n_qblocks=1),
        out_shape=jax.ShapeDtypeStruct(q.shape, BF16),
        grid=(n_batch,),
        in_specs=[
            pl.BlockSpec(memory_space=pltpu.SMEM),
            pl.BlockSpec((seq, D_MODEL), row),
            pl.BlockSpec((seq, kvw), row),
            pl.BlockSpec((seq, kvw), row),
            pl.BlockSpec((seq, D_MODEL), row),
        ],
        out_specs=pl.BlockSpec((seq, D_MODEL), row),
        scratch_shapes=_attn_a_scratch(seq, seq),
        compiler_params=_params("arbitrary"),
        name="attn_a_ctx",
    )(sink, q, k, v, sg)


def _attn_a_lat(q, k, v, k_ctx, v_ctx, sg, sink, n_batch, seq):
    kvw = A_KV_HEADS * HEAD_DIM
    nb = seq // A_QBLOCK
    own = lambda b, i: (b * nb + i, 0)
    prev = lambda b, i: (b * nb + jnp.maximum(i - 1, 0), 0)
    nxt = lambda b, i: (b * nb + jnp.minimum(i + 1, nb - 1), 0)
    ctx = lambda b, i: (b, 0, 0)
    n_ctx = k_ctx.shape[1]
    return pl.pallas_call(
        functools.partial(_attn_a_kernel, latent=True, n_qblocks=nb),
        out_shape=jax.ShapeDtypeStruct(q.shape, BF16),
        grid=(n_batch, nb),
        in_specs=[
            pl.BlockSpec(memory_space=pltpu.SMEM),
            pl.BlockSpec((A_QBLOCK, D_MODEL), own),
            pl.BlockSpec((A_QBLOCK, kvw), prev),
            pl.BlockSpec((A_QBLOCK, kvw), own),
            pl.BlockSpec((A_QBLOCK, kvw), nxt),
            pl.BlockSpec((A_QBLOCK, kvw), prev),
            pl.BlockSpec((A_QBLOCK, kvw), own),
            pl.BlockSpec((A_QBLOCK, kvw), nxt),
            pl.BlockSpec((None, n_ctx, kvw), ctx),
            pl.BlockSpec((None, n_ctx, kvw), ctx),
            pl.BlockSpec((A_QBLOCK, D_MODEL), own),
        ],
        out_specs=pl.BlockSpec((A_QBLOCK, D_MODEL), own),
        scratch_shapes=_attn_a_scratch(A_QBLOCK, 3 * A_QBLOCK + n_ctx),
        compiler_params=_params("arbitrary", "arbitrary"),
        name="attn_a_lat",
    )(sink, q, k, k, k, v, v, v, k_ctx, v_ctx, sg)


def _attn_b_kernel(*refs, has_ctx, lambda_init, k_chunk):
    if has_ctx:
        (lam_ref, sub_ref, q_ref, k_ref, v_ref, kc_ref, vc_ref, sg_ref, o_ref,
         s_scr, sc_scr, m_scr, l_scr, acc_scr) = refs
    else:
        lam_ref, sub_ref, q_ref, k_ref, v_ref, sg_ref, o_ref, s_scr, m_scr, l_scr, acc_scr = refs
        kc_ref = vc_ref = sc_scr = None
    dv = 2 * HEAD_DIM
    for head in range(q_ref.shape[1] // dv):
        cols = slice(head * dv, (head + 1) * dv)
        view = lambda ref: None if ref is None else ref.at[:, cols]
        _attn_b_head(lam_ref, sub_ref, view(q_ref), view(k_ref), view(v_ref), view(kc_ref), view(vc_ref),
                     view(sg_ref), view(o_ref), s_scr, sc_scr, m_scr, l_scr, acc_scr,
                     lambda_init=lambda_init, k_chunk=k_chunk)


def _attn_b_head(lam_ref, sub_ref, q_ref, k_ref, v_ref, kc_ref, vc_ref, sg_ref, o_ref,
                 s_scr, sc_scr, m_scr, l_scr, acc_scr, *, lambda_init, k_chunk):
    has_ctx = kc_ref is not None
    n_chunks = k_ref.shape[0] // k_chunk
    halves = [slice(h * HEAD_DIM, (h + 1) * HEAD_DIM) for h in range(2)]

    def fold_max(h, s, first):
        m = _lane_fold(s, jnp.maximum)
        m_scr[h] = m if first else jnp.maximum(m_scr[h], m)

    def scores(c, first=False):
        rows = pl.ds(pl.multiple_of(c * k_chunk, k_chunk), k_chunk)
        k = k_ref[rows, :].astype(BF16)
        for h, hc in enumerate(halves):
            s = _nt_dot(q_ref[:, hc], k[:, hc])
            s_scr[h, c] = s
            fold_max(h, s, first)

    if has_ctx:
        kc = kc_ref[...].astype(BF16)
        for h, hc in enumerate(halves):
            s = _nt_dot(q_ref[:, hc], kc[:, hc])
            sc_scr[h] = s
            fold_max(h, s, True)
        lax.fori_loop(0, n_chunks, lambda c, carry: (scores(c), carry)[1], 0, unroll=B_UNROLL)
    else:
        assert n_chunks == 1
        scores(0, first=True)

    for h in range(2):
        m_scr[h] = jnp.broadcast_to(jnp.max(m_scr[h], axis=-1, keepdims=True), m_scr.shape[1:])

    def accumulate(h, s, v, first):
        m = m_scr[h]
        p = jnp.exp2(s - jnp.concatenate([m] * (s.shape[1] // HEAD_DIM), axis=1))
        l = _lane_fold(p, jnp.add)
        pv = jnp.dot(p.astype(BF16), v, preferred_element_type=F32)
        l_scr[h] = l if first else l_scr[h] + l
        acc_scr[h] = pv if first else acc_scr[h] + pv

    def weighted(c, first=False):
        rows = pl.ds(pl.multiple_of(c * k_chunk, k_chunk), k_chunk)
        v = v_ref[rows, :].astype(BF16)
        for h in range(2):
            accumulate(h, s_scr[h, c], v, first)

    if has_ctx:
        vc = vc_ref[...].astype(BF16)
        for h in range(2):
            accumulate(h, sc_scr[h], vc, True)
        lax.fori_loop(0, n_chunks, lambda c, carry: (weighted(c), carry)[1], 0, unroll=B_UNROLL)
    else:
        weighted(0, first=True)

    lam = lam_ref[...]
    lam_full = (jnp.exp(jnp.sum(lam[0:1] * lam[1:2], axis=-1, keepdims=True))
                - jnp.exp(jnp.sum(lam[2:3] * lam[3:4], axis=-1, keepdims=True)) + lambda_init)
    r0 = 1.0 / jnp.sum(l_scr[0], axis=-1, keepdims=True)
    r1 = lam_full / jnp.sum(l_scr[1], axis=-1, keepdims=True)
    o = acc_scr[0] * r0 - acc_scr[1] * r1
    ms = jnp.mean(o * o, axis=-1, keepdims=True)
    o = o * lax.rsqrt(ms + EPS) * (sub_ref[...] * (1.0 - lambda_init))
    o_ref[...] = (o * sg_ref[...].astype(F32)).astype(o_ref.dtype)


def _attn_b(q, k, v, sg, lam, subln, lambda_init, n_batch, seq, k_ctx=None, v_ctx=None):
    dv = 2 * HEAD_DIM
    tq = min(B_QBLOCK, seq)
    nq = seq // tq
    k_chunk = min(B_KCHUNK, seq)
    has_ctx = k_ctx is not None
    heads_per_step = 1 if has_ctx else B_HEADS
    gw = heads_per_step * dv
    qmap = lambda b, h, i: (b * nq + i, h)
    kvmap = lambda b, h, i: (b, h)
    in_specs = [
        pl.BlockSpec((4, HEAD_DIM), lambda b, h, i: (0, 0)),
        pl.BlockSpec((1, dv), lambda b, h, i: (0, 0)),
        pl.BlockSpec((tq, gw), qmap),
        pl.BlockSpec((seq, gw), kvmap),
        pl.BlockSpec((seq, gw), kvmap),
    ]
    args = [lam, subln.reshape(1, dv), q, k, v]
    scratch = [pltpu.VMEM((2, seq // k_chunk, tq, k_chunk), F32)]
    if has_ctx:
        n_ctx = k_ctx.shape[1]
        in_specs += [pl.BlockSpec((None, n_ctx, gw), lambda b, h, i: (b, 0, h))] * 2
        args += [k_ctx, v_ctx]
        scratch.append(pltpu.VMEM((2, tq, n_ctx), F32))
    in_specs.append(pl.BlockSpec((tq, gw), qmap))
    args.append(sg)
    scratch += [pltpu.VMEM((2, tq, HEAD_DIM), F32), pltpu.VMEM((2, tq, HEAD_DIM), F32),
                pltpu.VMEM((2, tq, dv), F32)]
    return pl.pallas_call(
        functools.partial(_attn_b_kernel, has_ctx=has_ctx, lambda_init=lambda_init, k_chunk=k_chunk),
        out_shape=jax.ShapeDtypeStruct(q.shape, BF16),
        grid=(n_batch, B_HEADS // heads_per_step, nq),
        in_specs=in_specs,
        out_specs=pl.BlockSpec((tq, gw), qmap),
        scratch_shapes=scratch,
        compiler_params=_params("arbitrary", "arbitrary", "arbitrary"),
        name="attn_b_lat" if has_ctx else "attn_b_ctx",
    )(*args)


def _attn_c_ctx_kernel(q_ref, k_ref, v_ref, sg_ref, o_ref):
    for h in range(C_HEADS):
        hc = slice(h * HEAD_DIM, (h + 1) * HEAD_DIM)
        s = _nt_dot(q_ref[:, hc], k_ref[:, hc].astype(BF16))
        o = _softmax_pv([s], [v_ref[:, hc].astype(BF16)])
        o_ref[:, hc] = (o * sg_ref[:, hc].astype(F32)).astype(o_ref.dtype)


def _attn_c_ctx(q, k, v, sg, n_batch, seq):
    spec = pl.BlockSpec((seq, D_MODEL), lambda b: (b, 0))
    return pl.pallas_call(
        _attn_c_ctx_kernel,
        out_shape=jax.ShapeDtypeStruct(q.shape, BF16),
        grid=(n_batch,),
        in_specs=[spec, spec, spec, spec],
        out_specs=spec,
        compiler_params=_params("arbitrary"),
        name="attn_c_ctx",
    )(q, k, v, sg)


def _na_geometry(rows):
    kh = min(NA_KH, rows)
    key_rows = C_ROWS + kh - 1
    key_rows = min(key_rows + key_rows % 2, rows)
    n_blocks = rows // C_ROWS
    cols = np.arange(GRID_W)
    col_start = np.clip(cols - NA_KW // 2, 0, GRID_W - NA_KW)
    col_ok = (cols[None, :] >= col_start[:, None]) & (cols[None, :] < col_start[:, None] + NA_KW)
    col_delta = cols[None, :] - cols[:, None] + NA_KW - 1
    col_onehot = (col_delta[None] == np.arange(2 * NA_KW - 1)[:, None, None]) & col_ok[None]
    starts, class_ids, classes, keys = [], [], [], {}
    for blk in range(n_blocks):
        r = blk * C_ROWS + np.arange(C_ROWS)
        rs = np.clip(r - kh // 2, 0, rows - kh)
        start = int(np.clip(rs[0], 0, rows - key_rows))
        key = (tuple(rs - r), start - blk * C_ROWS)
        if key not in keys:
            keys[key] = len(classes)
            kr = start + np.arange(key_rows)
            row_ok = (kr[None, :] >= rs[:, None]) & (kr[None, :] < rs[:, None] + kh)
            row_idx = kr[None, :] - r[:, None] + NA_KH - 1
            classes.append((row_ok, row_idx))
        starts.append(start)
        class_ids.append(keys[key])
    geometry = (col_ok, col_onehot.astype(np.float32), classes)
    return key_rows, np.asarray(starts, np.int32), np.asarray(class_ids, np.int32), geometry


def _na_bias_tables(rpb, geometry):
    col_ok, col_onehot, classes = geometry
    n_heads = rpb.shape[0]
    toep = jnp.einsum("hdx,xck->hdck", rpb * LOG2E, jnp.asarray(col_onehot),
                      precision=lax.Precision.HIGHEST)
    toep = jnp.where(jnp.asarray(col_ok)[None, None], toep, NEG_INF)
    masked = jnp.full((n_heads, GRID_W, GRID_W), NEG_INF, F32)
    tabs = []
    for row_ok, row_idx in classes:
        n_r, n_kr = row_ok.shape
        blocks = [toep[:, int(row_idx[a, u])] if row_ok[a, u] else masked
                  for a in range(n_r) for u in range(n_kr)]
        tab = jnp.stack(blocks, axis=1).reshape(n_heads, n_r, n_kr, GRID_W, GRID_W)
        tabs.append(tab.transpose(0, 1, 3, 2, 4).reshape(n_heads, n_r * GRID_W, n_kr * GRID_W))
    return jnp.stack(tabs, axis=0)


def _attn_c_lat_kernel(start_ref, cls_ref, q_ref, k_ref, v_ref, kc_ref, vc_ref, bias_ref, sg_ref, o_ref,
                       s_scr, m_scr, acc_scr, *, key_rows):
    blk = pl.program_id(2)
    n_loc = key_rows * GRID_W
    n_keys = s_scr.shape[-1]
    rows = pl.ds(pl.multiple_of(start_ref[blk] * GRID_W, GRID_W), n_loc)
    head_cols = [slice(h * HEAD_DIM, (h + 1) * HEAD_DIM) for h in range(C_HEAD_GROUP)]

    for h, hc in enumerate(head_cols):
        q = q_ref[:, hc]
        s_loc = _nt_dot(q, k_ref[rows, hc]) + bias_ref[h]
        s_ctx = _nt_dot(q, kc_ref[:, hc].astype(BF16))
        s_scr[h, :, 0:n_loc] = s_loc
        s_scr[h, :, n_loc:n_keys] = s_ctx
        m_scr[h] = jnp.maximum(_lane_fold(s_loc, jnp.maximum), _lane_fold(s_ctx, jnp.maximum))

    for h in range(C_HEAD_GROUP):
        m_scr[h] = jnp.broadcast_to(jnp.max(m_scr[h], axis=-1, keepdims=True), m_scr.shape[1:])

    for h, hc in enumerate(head_cols):
        p = jnp.exp2(s_scr[h] - jnp.concatenate([m_scr[h]] * (n_keys // HEAD_DIM), axis=1)).astype(BF16)
        acc_scr[h] = (jnp.dot(p[:, 0:n_loc], _with_ones(v_ref[rows, hc]), preferred_element_type=F32)
                      + jnp.dot(p[:, n_loc:n_keys], _with_ones(vc_ref[:, hc].astype(BF16)),
                                preferred_element_type=F32))

    for h, hc in enumerate(head_cols):
        acc = acc_scr[h]
        o = acc[:, :HEAD_DIM] * (1.0 / acc[:, HEAD_DIM:])
        o_ref[:, hc] = (o * sg_ref[:, hc].astype(F32)).astype(o_ref.dtype)


def _attn_c_lat(q, k, v, k_ctx, v_ctx, sg, rpb, n_batch, seq):
    rows = seq // GRID_W
    key_rows, starts, class_ids, geometry = _na_geometry(rows)
    bias = _na_bias_tables(rpb, geometry)
    nblk = rows // C_ROWS
    tq = C_ROWS * GRID_W
    gw = C_HEAD_GROUP * HEAD_DIM
    n_ctx = k_ctx.shape[1]
    qmap = lambda b, g, i, st, cl: (b * nblk + i, g)
    kvmap = lambda b, g, i, st, cl: (b, g)
    ctxmap = lambda b, g, i, st, cl: (b, 0, g)
    grid_spec = pltpu.PrefetchScalarGridSpec(
        num_scalar_prefetch=2,
        grid=(n_batch, C_HEADS // C_HEAD_GROUP, nblk),
        in_specs=[
            pl.BlockSpec((tq, gw), qmap),
            pl.BlockSpec((seq, gw), kvmap),
            pl.BlockSpec((seq, gw), kvmap),
            pl.BlockSpec((None, n_ctx, gw), ctxmap),
            pl.BlockSpec((None, n_ctx, gw), ctxmap),
            pl.BlockSpec((None, C_HEAD_GROUP, tq, key_rows * GRID_W),
                         lambda b, g, i, st, cl: (cl[i], g, 0, 0)),
            pl.BlockSpec((tq, gw), qmap),
        ],
        out_specs=pl.BlockSpec((tq, gw), qmap),
        scratch_shapes=[pltpu.VMEM((C_HEAD_GROUP, tq, key_rows * GRID_W + n_ctx), F32),
                        pltpu.VMEM((C_HEAD_GROUP, tq, HEAD_DIM), F32),
                        pltpu.VMEM((C_HEAD_GROUP, tq, 2 * HEAD_DIM), F32)],
    )
    return pl.pallas_call(
        functools.partial(_attn_c_lat_kernel, key_rows=key_rows),
        out_shape=jax.ShapeDtypeStruct(q.shape, BF16),
        grid_spec=grid_spec,
        compiler_params=_params("arbitrary", "arbitrary", "arbitrary"),
        name="attn_c_lat",
    )(jnp.asarray(starts), jnp.asarray(class_ids), q, k, v, k_ctx, v_ctx, bias, sg)


def _rope_tables(n_tokens):
    d4 = HEAD_DIM // 4
    t = jnp.arange(n_tokens)
    pos = jnp.stack([t // GRID_W, t % GRID_W], axis=-1).astype(F32)
    inv_freq = ROPE_BASE ** (-jnp.arange(d4, dtype=F32) / d4)
    ang = pos[:, :, None] * inv_freq
    cos, sin = jnp.cos(ang), jnp.sin(ang)
    zero = jnp.zeros_like(sin[:, 0])
    cos_full = jnp.concatenate([cos[:, 0], cos[:, 0], cos[:, 1], cos[:, 1]], axis=-1)
    sin_up = jnp.concatenate([-sin[:, 0], zero, -sin[:, 1], zero], axis=-1)
    sin_dn = jnp.concatenate([zero, sin[:, 0], zero, sin[:, 1]], axis=-1)
    return cos_full, sin_up, sin_dn


def kernel(x_prompt, x_sample, cache_a_k, cache_a_v, cache_b_k, cache_b_v, cache_c_k, cache_c_v, c, c_ctx,
           ln_g, ada_w, ada_b, w_out, qn_g, kn_g, w_in_a, sink_a, w_in_b, lam_b, subln_b, w_in_c, rpb_c):
    n_p, seq_p, d = x_prompt.shape
    n_s, seq_s, _ = x_sample.shape
    n_ctx = cache_a_k.shape[2]
    xp = x_prompt.reshape(n_p * seq_p, d)
    xs = x_sample.reshape(n_s * seq_s, d)
    rope_tabs = _rope_tables(seq_s)

    pad = (-(n_s + 1)) % 8
    cvec = jnp.concatenate([c, c_ctx[None, :], jnp.zeros((pad, d), F32)], axis=0)
    mod = _ada_modulation(cvec, ada_w, ada_b)
    mod = mod.reshape(DEPTH, mod.shape[1], 3, 1, d)

    new_kv = {0: ([], []), 1: ([], []), 2: ([], [])}
    for l in range(DEPTH):
        kind, j = l % N_MIXERS, l // N_MIXERS
        sh_s, sc_s, gt_s = (mod[l, :n_s, t] for t in range(3))
        sh_p, sc_p, gt_p = (mod[l, n_s:n_s + 1, t] for t in range(3))
        w_o = w_out[l].astype(BF16)
        if kind == 0:
            kvw = A_KV_HEADS * HEAD_DIM
            w_in, widths, tn = w_in_a[j].astype(BF16), (d, kvw, kvw, d), 512
            cache_minor = ((A_KV_HEADS, HEAD_DIM), (A_KV_HEADS, HEAD_DIM))
        elif kind == 1:
            w_in, widths, tn = w_in_b[j].astype(BF16), (d, d, d, d), 512
            cache_minor = ((B_HEADS, 2, HEAD_DIM), (B_HEADS, 2 * HEAD_DIM))
        else:
            w_in, widths, tn = w_in_c[j].astype(BF16), (d, d, d, d), 512
            cache_minor = ((C_HEADS, HEAD_DIM), (C_HEADS, HEAD_DIM))
        rope = rope_tabs if kind != 2 else None
        qp, kp, vp, gp, kp_f32, vp_f32 = _in_projection(xp, sh_p, sc_p, ln_g[l], w_in, qn_g[l], kn_g[l], widths,
                                                        None, tn, cache_minor)
        qs, ks, vs, gs = _in_projection(xs, sh_s, sc_s, ln_g[l], w_in, qn_g[l], kn_g[l], widths, rope, tn)
        new_kv[kind][0].append(kp_f32.reshape((n_p, seq_p) + cache_minor[0]))
        new_kv[kind][1].append(vp_f32.reshape((n_p, seq_p) + cache_minor[1]))
        if kind == 0:
            kc = cache_a_k[:, j].reshape(n_s, n_ctx, kvw)
            vc = cache_a_v[:, j].reshape(n_s, n_ctx, kvw)
            op = _attn_a_ctx(qp, kp, vp, gp, sink_a[j], n_p, seq_p)
            os_ = _attn_a_lat(qs, ks, vs, kc, vc, gs, sink_a[j], n_s, seq_s)
        elif kind == 1:
            lambda_init = 0.8 - 0.6 * math.exp(-0.3 * l)
            kc = cache_b_k[:, j].reshape(n_s, n_ctx, d)
            vc = cache_b_v[:, j].reshape(n_s, n_ctx, d)
            op = _attn_b(qp, kp, vp, gp, lam_b[j], subln_b[j], lambda_init, n_p, seq_p)
            os_ = _attn_b(qs, ks, vs, gs, lam_b[j], subln_b[j], lambda_init, n_s, seq_s, kc, vc)
        else:
            kc = cache_c_k[:, j].reshape(n_s, n_ctx, d)
            vc = cache_c_v[:, j].reshape(n_s, n_ctx, d)
            op = _attn_c_ctx(qp, kp, vp, gp, n_p, seq_p)
            os_ = _attn_c_lat(qs, ks, vs, kc, vc, gs, rpb_c[j], n_s, seq_s)
        xp = _out_projection(op, w_o, xp, gt_p)
        xs = _out_projection(os_, w_o, xs, gt_s)

    outs = [xp.reshape(x_prompt.shape), xs.reshape(x_sample.shape)]
    for kind in range(3):
        outs.append(jnp.stack(new_kv[kind][0], axis=1))
        outs.append(jnp.stack(new_kv[kind][1], axis=1))
    return tuple(outs)
```

```python
import functools
import math

import numpy as np
import jax
import jax.numpy as jnp
from jax import lax
from jax.experimental import pallas as pl
from jax.experimental.pallas import tpu as pltpu

D_MODEL = 2048
DEPTH = 4
GRID_W = 64
HEAD_DIM = 128
ROPE_BASE = 10000.0
NEG_INF = -1e30
N_MIXERS = 3
A_HEADS = D_MODEL // HEAD_DIM
A_KV_HEADS = A_HEADS // 4
A_GROUP = A_HEADS // A_KV_HEADS
WINDOW = 128
B_HEADS = D_MODEL // (2 * HEAD_DIM)
C_HEADS = D_MODEL // HEAD_DIM
NA_KH = 8
NA_KW = 16
SCALE = HEAD_DIM ** -0.5
LOG2E = math.log2(math.e)
Q_SCALE = SCALE * LOG2E
EPS = 1e-6

BF16 = jnp.bfloat16
F32 = jnp.float32

VMEM_LIMIT_BYTES = 56 * 1024 * 1024
ROW_TILE = 1024
OUT_ROW_TILE = 512
OUT_SUB = 512
PROJ_SUB = 128
A_QBLOCK = 128
B_QBLOCK = 1024
B_KCHUNK = 512
B_UNROLL = 4
C_ROWS = 4
C_HEAD_GROUP = 4


def _nt_dot(a, b):
    return lax.dot_general(a, b, (((1,), (1,)), ((), ())), preferred_element_type=F32)


def _params(*semantics):
    return pltpu.CompilerParams(dimension_semantics=semantics, vmem_limit_bytes=VMEM_LIMIT_BYTES)


def _ada_kernel(c_ref, w_ref, b_ref, o_ref):
    c = c_ref[...]
    a = (c * jax.nn.sigmoid(c)).astype(BF16)
    o_ref[...] = jnp.dot(a, w_ref[...].astype(BF16), preferred_element_type=F32) + b_ref[...]


def _ada_modulation(cvec, ada_w, ada_b):
    rows = cvec.shape[0]
    tn = 768
    return pl.pallas_call(
        _ada_kernel,
        out_shape=jax.ShapeDtypeStruct((DEPTH, rows, 3 * D_MODEL), F32),
        grid=(DEPTH, 3 * D_MODEL // tn),
        in_specs=[
            pl.BlockSpec((rows, D_MODEL), lambda l, j: (0, 0)),
            pl.BlockSpec((None, D_MODEL, tn), lambda l, j: (l, 0, j)),
            pl.BlockSpec((None, 1, tn), lambda l, j: (l, 0, j)),
        ],
        out_specs=pl.BlockSpec((None, rows, tn), lambda l, j: (l, 0, j)),
        compiler_params=_params("arbitrary", "arbitrary"),
        name="ada_modulation",
    )(cvec, ada_w, ada_b.reshape(DEPTH, 1, 3 * D_MODEL))


def _inproj_kernel(*refs, n_q, rope, heads_per_tile):
    x_ref, sh_ref, sc_ref, lng_ref, w1_ref, w2_ref, qg_ref, kg_ref = refs[:8]
    if rope:
        cos_ref, sa_ref, sb_ref = refs[8:11]
    q_out, k_out, v_out, g_out, h_scr = refs[-5:]
    j = pl.program_id(1)

    def modulated_norm(rows):
        x = x_ref[rows, :]
        ms = jnp.mean(x * x, axis=-1, keepdims=True)
        gain = lng_ref[...] * (1.0 + sc_ref[...])
        return (x * lax.rsqrt(ms + EPS) * gain + sh_ref[...]).astype(BF16)

    def normed(acc, rows, gain_ref, scale):
        outs = []
        for t in range(heads_per_tile):
            y = acc[:, t * HEAD_DIM:(t + 1) * HEAD_DIM]
            ms = jnp.mean(y * y, axis=-1, keepdims=True)
            y = y * lax.rsqrt(ms + EPS) * (gain_ref[...] * scale)
            if rope:
                y = (y * cos_ref[rows, :] + pltpu.roll(y, 96, 1) * sa_ref[rows, :]
                     + pltpu.roll(y, 32, 1) * sb_ref[rows, :])
            outs.append(y)
        return jnp.concatenate(outs, axis=-1)

    def project(w_ref, out_ref, epilogue, fill_h=False):
        for r in range(h_scr.shape[0] // PROJ_SUB):
            rows = slice(r * PROJ_SUB, (r + 1) * PROJ_SUB)
            if fill_h:
                h_scr[rows, :] = modulated_norm(rows)
            acc = jnp.dot(h_scr[rows, :], w_ref[...], preferred_element_type=F32)
            out_ref[rows, :] = epilogue(acc, rows).astype(out_ref.dtype)

    def q_and_gate(fill_h):
        project(w1_ref, q_out, lambda acc, rows: normed(acc, rows, qg_ref, Q_SCALE), fill_h)
        project(w2_ref, g_out, lambda acc, rows: acc * jax.nn.sigmoid(acc))

    @pl.when(j == 0)
    def _():
        q_and_gate(True)

    @pl.when((j > 0) & (j < n_q))
    def _():
        q_and_gate(False)

    @pl.when(j >= n_q)
    def _():
        project(w1_ref, k_out, lambda acc, rows: normed(acc, rows, kg_ref, 1.0))
        project(w2_ref, v_out, lambda acc, rows: acc)


def _in_projection(x, shift, scale, ln_g, w, qg, kg, widths, rope_tabs, kv_dtype, tn):
    t_rows = x.shape[0]
    qw, kw, vw, gw = widths
    assert qw == gw and kw == vw
    n_q, n_k = qw // tn, kw // tn
    tm = min(ROW_TILE, t_rows)
    rows_per_mod = t_rows // shift.shape[0]
    assert t_rows % tm == 0 and rows_per_mod % tm == 0
    rope = rope_tabs is not None
    g_col0, v_col0 = (qw + kw + vw) // tn, (qw + kw) // tn

    def mod_map(i, j):
        return ((i * tm) // rows_per_mod, 0, 0)

    in_specs = [
        pl.BlockSpec((tm, D_MODEL), lambda i, j: (i, 0)),
        pl.BlockSpec((None, 1, D_MODEL), mod_map),
        pl.BlockSpec((None, 1, D_MODEL), mod_map),
        pl.BlockSpec((1, D_MODEL), lambda i, j: (0, 0)),
        pl.BlockSpec((D_MODEL, tn), lambda i, j: (0, j)),
        pl.BlockSpec((D_MODEL, tn), lambda i, j: (0, jnp.where(j < n_q, g_col0 + j, v_col0 + j - n_q))),
        pl.BlockSpec((1, HEAD_DIM), lambda i, j: (0, 0)),
        pl.BlockSpec((1, HEAD_DIM), lambda i, j: (0, 0)),
    ]
    args = [x, shift, scale, ln_g.reshape(1, D_MODEL), w, w, qg.reshape(1, HEAD_DIM), kg.reshape(1, HEAD_DIM)]
    if rope:
        n_pos_tiles = rope_tabs[0].shape[0] // tm
        for tab in rope_tabs:
            in_specs.append(pl.BlockSpec((tm, HEAD_DIM), lambda i, j: (i % n_pos_tiles, 0)))
            args.append(tab)
    out_specs = [
        pl.BlockSpec((tm, tn), lambda i, j: (i, jnp.minimum(j, n_q - 1))),
        pl.BlockSpec((tm, tn), lambda i, j: (i, jnp.maximum(j - n_q, 0))),
        pl.BlockSpec((tm, tn), lambda i, j: (i, jnp.maximum(j - n_q, 0))),
        pl.BlockSpec((tm, tn), lambda i, j: (i, jnp.minimum(j, n_q - 1))),
    ]
    out_shape = [
        jax.ShapeDtypeStruct((t_rows, qw), BF16),
        jax.ShapeDtypeStruct((t_rows, kw), kv_dtype),
        jax.ShapeDtypeStruct((t_rows, vw), kv_dtype),
        jax.ShapeDtypeStruct((t_rows, gw), BF16),
    ]
    kern = functools.partial(_inproj_kernel, n_q=n_q, rope=rope, heads_per_tile=tn // HEAD_DIM)
    return pl.pallas_call(
        kern,
        out_shape=out_shape,
        grid=(t_rows // tm, n_q + n_k),
        in_specs=in_specs,
        out_specs=out_specs,
        scratch_shapes=[pltpu.VMEM((tm, D_MODEL), BF16)],
        compiler_params=_params("arbitrary", "arbitrary"),
        name="in_projection",
    )(*args)


def _outproj_kernel(o_ref, w_ref, x_ref, gt_ref, y_ref):
    for c in range(D_MODEL // OUT_SUB):
        cols = slice(c * OUT_SUB, (c + 1) * OUT_SUB)
        y = jnp.dot(o_ref[...], w_ref[:, cols], preferred_element_type=F32)
        y_ref[:, cols] = x_ref[:, cols] + gt_ref[:, cols] * y


def _out_projection(o, w, x, gate):
    t_rows = x.shape[0]
    tm = min(OUT_ROW_TILE, t_rows)
    rows_per_mod = t_rows // gate.shape[0]
    assert t_rows % tm == 0 and rows_per_mod % tm == 0
    row = lambda i: (i, 0)
    return pl.pallas_call(
        _outproj_kernel,
        out_shape=jax.ShapeDtypeStruct((t_rows, D_MODEL), F32),
        grid=(t_rows // tm,),
        in_specs=[
            pl.BlockSpec((tm, D_MODEL), row),
            pl.BlockSpec((D_MODEL, D_MODEL), lambda i: (0, 0)),
            pl.BlockSpec((tm, D_MODEL), row),
            pl.BlockSpec((None, 1, D_MODEL), lambda i: ((i * tm) // rows_per_mod, 0, 0)),
        ],
        out_specs=pl.BlockSpec((tm, D_MODEL), row),
        compiler_params=_params("arbitrary"),
        name="out_projection",
    )(o, w, x, gate)


def _lane_fold(x, op):
    r = x[:, 0:HEAD_DIM]
    for t in range(1, x.shape[1] // HEAD_DIM):
        r = op(r, x[:, t * HEAD_DIM:(t + 1) * HEAD_DIM])
    return r


def _with_ones(v):
    return jnp.concatenate([v, jnp.ones(v.shape, v.dtype)], axis=1)


def _softmax_pv(scores, values):
    m128 = _lane_fold(scores[0], jnp.maximum)
    for s in scores[1:]:
        m128 = jnp.maximum(m128, _lane_fold(s, jnp.maximum))
    m = jnp.broadcast_to(jnp.max(m128, axis=-1, keepdims=True), m128.shape)
    acc = None
    for s, v in zip(scores, values):
        p = jnp.exp2(s - jnp.concatenate([m] * (s.shape[1] // HEAD_DIM), axis=1)).astype(BF16)
        pv = jnp.dot(p, _with_ones(v), preferred_element_type=F32)
        acc = pv if acc is None else acc + pv
    return acc[:, :HEAD_DIM] * (1.0 / acc[:, HEAD_DIM:])


def _attn_a_kernel(*refs, latent, n_qblocks):
    if latent:
        (sink_ref, q_ref, kp_ref, ko_ref, kn_ref, vp_ref, vo_ref, vn_ref,
         kc_ref, vc_ref, sg_ref, o_ref, s_scr, m_scr, acc_scr, kc_scr, vc_scr) = refs
    else:
        sink_ref, q_ref, k_ref, v_ref, sg_ref, o_ref, s_scr, m_scr, acc_scr = refs
    m_rows = q_ref.shape[0]
    n_keys = s_scr.shape[-1]
    n_loc = 3 * A_QBLOCK
    kv_cols = [slice(kv * HEAD_DIM, (kv + 1) * HEAD_DIM) for kv in range(A_KV_HEADS)]
    kv_heads = [[kv * A_GROUP + g for g in range(A_GROUP)] for kv in range(A_KV_HEADS)]

    if latent:
        blk = pl.program_id(1)
        qpos = lax.broadcasted_iota(jnp.int32, (A_QBLOCK, n_loc), 0)
        col = lax.broadcasted_iota(jnp.int32, (A_QBLOCK, n_loc), 1)
        rel = col - A_QBLOCK - qpos
        kpos = (blk - 1) * A_QBLOCK + col
        valid = (jnp.abs(rel) <= WINDOW) & (kpos >= 0) & (kpos < n_qblocks * A_QBLOCK)
        window_bias = jnp.concatenate([jnp.where(valid, 0.0, NEG_INF)] * A_GROUP, axis=0)

        @pl.when(blk == 0)
        def _():
            for kv in range(A_KV_HEADS):
                kc_scr[kv] = kc_ref[:, kv, :].astype(BF16)
                vc_scr[kv] = vc_ref[:, kv, :].astype(BF16)

    for kv, cols in enumerate(kv_cols):
        q = jnp.concatenate([q_ref[:, h * HEAD_DIM:(h + 1) * HEAD_DIM] for h in kv_heads[kv]], axis=0)
        if latent:
            k_loc = jnp.concatenate([kp_ref[:, cols], ko_ref[:, cols], kn_ref[:, cols]], axis=0)
            s_loc = _nt_dot(q, k_loc) + window_bias
            s_ctx = _nt_dot(q, kc_scr[kv])
            s_scr[kv, :, 0:n_loc] = s_loc
            s_scr[kv, :, n_loc:n_keys] = s_ctx
            m_scr[kv] = jnp.maximum(_lane_fold(s_loc, jnp.maximum), _lane_fold(s_ctx, jnp.maximum))
        else:
            s = _nt_dot(q, k_ref[:, cols].astype(BF16))
            s_scr[kv] = s
            m_scr[kv] = _lane_fold(s, jnp.maximum)

    def sink_lanes(kv):
        return jnp.concatenate([jnp.full((m_rows, HEAD_DIM), sink_ref[h] * LOG2E, F32) for h in kv_heads[kv]], axis=0)

    for kv in range(A_KV_HEADS):
        m = jnp.max(m_scr[kv], axis=-1, keepdims=True)
        m_scr[kv] = jnp.maximum(jnp.broadcast_to(m, m_scr.shape[1:]), sink_lanes(kv))

    for kv, cols in enumerate(kv_cols):
        p = jnp.exp2(s_scr[kv] - jnp.concatenate([m_scr[kv]] * (n_keys // HEAD_DIM), axis=1)).astype(BF16)
        if latent:
            v_loc = jnp.concatenate([vp_ref[:, cols], vo_ref[:, cols], vn_ref[:, cols]], axis=0)
            acc_scr[kv] = (jnp.dot(p[:, 0:n_loc], _with_ones(v_loc), preferred_element_type=F32)
                           + jnp.dot(p[:, n_loc:n_keys], _with_ones(vc_scr[kv]), preferred_element_type=F32))
        else:
            acc_scr[kv] = jnp.dot(p, _with_ones(v_ref[:, cols].astype(BF16)), preferred_element_type=F32)

    for kv in range(A_KV_HEADS):
        acc = acc_scr[kv]
        l = acc[:, HEAD_DIM:] + jnp.exp2(sink_lanes(kv) - m_scr[kv])
        o = acc[:, :HEAD_DIM] * (1.0 / l)
        for g, h in enumerate(kv_heads[kv]):
            hc = slice(h * HEAD_DIM, (h + 1) * HEAD_DIM)
            o_ref[:, hc] = (o[g * m_rows:(g + 1) * m_rows] * sg_ref[:, hc].astype(F32)).astype(o_ref.dtype)


def _attn_a_scratch(m_rows, n_keys):
    stacked = A_GROUP * m_rows
    return [pltpu.VMEM((A_KV_HEADS, stacked, n_keys), F32), pltpu.VMEM((A_KV_HEADS, stacked, HEAD_DIM), F32),
            pltpu.VMEM((A_KV_HEADS, stacked, 2 * HEAD_DIM), F32)]


def _attn_a_ctx(q, k, v, sg, sink, n_batch, seq):
    kvw = A_KV_HEADS * HEAD_DIM
    row = lambda b: (b, 0)
    return pl.pallas_call(
        functools.partial(_attn_a_kernel, latent=False, n_qblocks=1),
        out_shape=jax.ShapeDtypeStruct(q.shape, BF16),
        grid=(n_batch,),
        in_specs=[
            pl.BlockSpec(memory_space=pltpu.SMEM),
            pl.BlockSpec((seq, D_MODEL), row),
            pl.BlockSpec((seq, kvw), row),
            pl.BlockSpec((seq, kvw), row),
            pl.BlockSpec((seq, D_MODEL), row),
        ],
        out_specs=pl.BlockSpec((seq, D_MODEL), row),
        scratch_shapes=_attn_a_scratch(seq, seq),
        compiler_params=_params("arbitrary"),
        name="attn_a_ctx",
    )(sink, q, k, v, sg)


def _attn_a_lat(q, k, v, cache_k, cache_v, cache_slot, sg, sink, n_batch, seq):
    kvw = A_KV_HEADS * HEAD_DIM
    nb = seq // A_QBLOCK
    own = lambda b, i: (b * nb + i, 0)
    prev = lambda b, i: (b * nb + jnp.maximum(i - 1, 0), 0)
    nxt = lambda b, i: (b * nb + jnp.minimum(i + 1, nb - 1), 0)
    ctx = lambda b, i: (b, cache_slot, 0, 0, 0)
    n_ctx = cache_k.shape[2]
    ctx_spec = pl.BlockSpec((None, None, n_ctx, A_KV_HEADS, HEAD_DIM), ctx)
    ctx_scratch = [pltpu.VMEM((A_KV_HEADS, n_ctx, HEAD_DIM), BF16)] * 2
    return pl.pallas_call(
        functools.partial(_attn_a_kernel, latent=True, n_qblocks=nb),
        out_shape=jax.ShapeDtypeStruct(q.shape, BF16),
        grid=(n_batch, nb),
        in_specs=[
            pl.BlockSpec(memory_space=pltpu.SMEM),
            pl.BlockSpec((A_QBLOCK, D_MODEL), own),
            pl.BlockSpec((A_QBLOCK, kvw), prev),
            pl.BlockSpec((A_QBLOCK, kvw), own),
            pl.BlockSpec((A_QBLOCK, kvw), nxt),
            pl.BlockSpec((A_QBLOCK, kvw), prev),
            pl.BlockSpec((A_QBLOCK, kvw), own),
            pl.BlockSpec((A_QBLOCK, kvw), nxt),
            ctx_spec,
            ctx_spec,
            pl.BlockSpec((A_QBLOCK, D_MODEL), own),
        ],
        out_specs=pl.BlockSpec((A_QBLOCK, D_MODEL), own),
        scratch_shapes=_attn_a_scratch(A_QBLOCK, 3 * A_QBLOCK + n_ctx) + ctx_scratch,
        compiler_params=_params("arbitrary", "arbitrary"),
        name="attn_a_lat",
    )(sink, q, k, k, k, v, v, v, cache_k, cache_v, sg)


def _attn_b_kernel(*refs, has_ctx, lambda_init, k_chunk):
    if has_ctx:
        (lam_ref, sub_ref, q_ref, k_ref, v_ref, kc_ref, vc_ref, sg_ref, o_ref,
         s_scr, sc_scr, m_scr, l_scr, acc_scr) = refs
    else:
        lam_ref, sub_ref, q_ref, k_ref, v_ref, sg_ref, o_ref, s_scr, m_scr, l_scr, acc_scr = refs
        kc_ref = vc_ref = sc_scr = None
    dv = 2 * HEAD_DIM
    for head in range(q_ref.shape[1] // dv):
        cols = slice(head * dv, (head + 1) * dv)
        view = lambda ref: None if ref is None else ref.at[:, cols]
        _attn_b_head(lam_ref, sub_ref, view(q_ref), view(k_ref), view(v_ref), view(kc_ref), view(vc_ref),
                     view(sg_ref), view(o_ref), s_scr, sc_scr, m_scr, l_scr, acc_scr,
                     lambda_init=lambda_init, k_chunk=k_chunk)


def _attn_b_head(lam_ref, sub_ref, q_ref, k_ref, v_ref, kc_ref, vc_ref, sg_ref, o_ref,
                 s_scr, sc_scr, m_scr, l_scr, acc_scr, *, lambda_init, k_chunk):
    has_ctx = kc_ref is not None
    n_chunks = k_ref.shape[0] // k_chunk
    halves = [slice(h * HEAD_DIM, (h + 1) * HEAD_DIM) for h in range(2)]

    def fold_max(h, s, first):
        m = _lane_fold(s, jnp.maximum)
        m_scr[h] = m if first else jnp.maximum(m_scr[h], m)

    def scores(c, first=False):
        rows = pl.ds(pl.multiple_of(c * k_chunk, k_chunk), k_chunk)
        k = k_ref[rows, :].astype(BF16)
        for h, hc in enumerate(halves):
            s = _nt_dot(q_ref[:, hc], k[:, hc])
            s_scr[h, c] = s
            fold_max(h, s, first)

    if has_ctx:
        kc = kc_ref[...].astype(BF16)
        for h, hc in enumerate(halves):
            s = _nt_dot(q_ref[:, hc], kc[:, hc])
            sc_scr[h] = s
            fold_max(h, s, True)
        lax.fori_loop(0, n_chunks, lambda c, carry: (scores(c), carry)[1], 0, unroll=B_UNROLL)
    else:
        assert n_chunks == 1
        scores(0, first=True)

    for h in range(2):
        m_scr[h] = jnp.broadcast_to(jnp.max(m_scr[h], axis=-1, keepdims=True), m_scr.shape[1:])

    def accumulate(h, s, v, first):
        m = m_scr[h]
        p = jnp.exp2(s - jnp.concatenate([m] * (s.shape[1] // HEAD_DIM), axis=1))
        l = _lane_fold(p, jnp.add)
        pv = jnp.dot(p.astype(BF16), v, preferred_element_type=F32)
        l_scr[h] = l if first else l_scr[h] + l
        acc_scr[h] = pv if first else acc_scr[h] + pv

    def weighted(c, first=False):
        rows = pl.ds(pl.multiple_of(c * k_chunk, k_chunk), k_chunk)
        v = v_ref[rows, :].astype(BF16)
        for h in range(2):
            accumulate(h, s_scr[h, c], v, first)

    if has_ctx:
        vc = vc_ref[...].astype(BF16)
        for h in range(2):
            accumulate(h, sc_scr[h], vc, True)
        lax.fori_loop(0, n_chunks, lambda c, carry: (weighted(c), carry)[1], 0, unroll=B_UNROLL)
    else:
        weighted(0, first=True)

    lam = lam_ref[...]
    lam_full = (jnp.exp(jnp.sum(lam[0:1] * lam[1:2], axis=-1, keepdims=True))
                - jnp.exp(jnp.sum(lam[2:3] * lam[3:4], axis=-1, keepdims=True)) + lambda_init)
    r0 = 1.0 / jnp.sum(l_scr[0], axis=-1, keepdims=True)
    r1 = lam_full / jnp.sum(l_scr[1], axis=-1, keepdims=True)
    o = acc_scr[0] * r0 - acc_scr[1] * r1
    ms = jnp.mean(o * o, axis=-1, keepdims=True)
    o = o * lax.rsqrt(ms + EPS) * (sub_ref[...] * (1.0 - lambda_init))
    o_ref[...] = (o * sg_ref[...].astype(F32)).astype(o_ref.dtype)


def _attn_b(q, k, v, sg, lam, subln, lambda_init, n_batch, seq, k_ctx=None, v_ctx=None):
    dv = 2 * HEAD_DIM
    tq = min(B_QBLOCK, seq)
    nq = seq // tq
    k_chunk = min(B_KCHUNK, seq)
    has_ctx = k_ctx is not None
    heads_per_step = 1 if has_ctx else B_HEADS
    gw = heads_per_step * dv
    qmap = lambda b, h, i: (b * nq + i, h)
    kvmap = lambda b, h, i: (b, h)
    in_specs = [
        pl.BlockSpec((4, HEAD_DIM), lambda b, h, i: (0, 0)),
        pl.BlockSpec((1, dv), lambda b, h, i: (0, 0)),
        pl.BlockSpec((tq, gw), qmap),
        pl.BlockSpec((seq, gw), kvmap),
        pl.BlockSpec((seq, gw), kvmap),
    ]
    args = [lam, subln.reshape(1, dv), q, k, v]
    scratch = [pltpu.VMEM((2, seq // k_chunk, tq, k_chunk), F32)]
    if has_ctx:
        n_ctx = k_ctx.shape[1]
        in_specs += [pl.BlockSpec((None, n_ctx, gw), lambda b, h, i: (b, 0, h))] * 2
        args += [k_ctx, v_ctx]
        scratch.append(pltpu.VMEM((2, tq, n_ctx), F32))
    in_specs.append(pl.BlockSpec((tq, gw), qmap))
    args.append(sg)
    scratch += [pltpu.VMEM((2, tq, HEAD_DIM), F32), pltpu.VMEM((2, tq, HEAD_DIM), F32),
                pltpu.VMEM((2, tq, dv), F32)]
    return pl.pallas_call(
        functools.partial(_attn_b_kernel, has_ctx=has_ctx, lambda_init=lambda_init, k_chunk=k_chunk),
        out_shape=jax.ShapeDtypeStruct(q.shape, BF16),
        grid=(n_batch, B_HEADS // heads_per_step, nq),
        in_specs=in_specs,
        out_specs=pl.BlockSpec((tq, gw), qmap),
        scratch_shapes=scratch,
        compiler_params=_params("arbitrary", "arbitrary", "arbitrary"),
        name="attn_b_lat" if has_ctx else "attn_b_ctx",
    )(*args)


def _attn_c_ctx_kernel(q_ref, k_ref, v_ref, sg_ref, o_ref):
    for h in range(C_HEADS):
        hc = slice(h * HEAD_DIM, (h + 1) * HEAD_DIM)
        s = _nt_dot(q_ref[:, hc], k_ref[:, hc].astype(BF16))
        o = _softmax_pv([s], [v_ref[:, hc].astype(BF16)])
        o_ref[:, hc] = (o * sg_ref[:, hc].astype(F32)).astype(o_ref.dtype)


def _attn_c_ctx(q, k, v, sg, n_batch, seq):
    spec = pl.BlockSpec((seq, D_MODEL), lambda b: (b, 0))
    return pl.pallas_call(
        _attn_c_ctx_kernel,
        out_shape=jax.ShapeDtypeStruct(q.shape, BF16),
        grid=(n_batch,),
        in_specs=[spec, spec, spec, spec],
        out_specs=spec,
        compiler_params=_params("arbitrary"),
        name="attn_c_ctx",
    )(q, k, v, sg)


def _na_geometry(rows):
    kh = min(NA_KH, rows)
    key_rows = C_ROWS + kh - 1
    key_rows = min(key_rows + key_rows % 2, rows)
    n_blocks = rows // C_ROWS
    cols = np.arange(GRID_W)
    col_start = np.clip(cols - NA_KW // 2, 0, GRID_W - NA_KW)
    col_ok = (cols[None, :] >= col_start[:, None]) & (cols[None, :] < col_start[:, None] + NA_KW)
    col_delta = cols[None, :] - cols[:, None] + NA_KW - 1
    col_onehot = (col_delta[None] == np.arange(2 * NA_KW - 1)[:, None, None]) & col_ok[None]
    starts, class_ids, classes, keys = [], [], [], {}
    for blk in range(n_blocks):
        r = blk * C_ROWS + np.arange(C_ROWS)
        rs = np.clip(r - kh // 2, 0, rows - kh)
        start = int(np.clip(rs[0], 0, rows - key_rows))
        key = (tuple(rs - r), start - blk * C_ROWS)
        if key not in keys:
            keys[key] = len(classes)
            kr = start + np.arange(key_rows)
            row_ok = (kr[None, :] >= rs[:, None]) & (kr[None, :] < rs[:, None] + kh)
            row_idx = kr[None, :] - r[:, None] + NA_KH - 1
            classes.append((row_ok, row_idx))
        starts.append(start)
        class_ids.append(keys[key])
    geometry = (col_ok, col_onehot.astype(np.float32), classes)
    return key_rows, np.asarray(starts, np.int32), np.asarray(class_ids, np.int32), geometry


def _na_bias_tables(rpb, geometry):
    col_ok, col_onehot, classes = geometry
    n_heads = rpb.shape[0]
    toep = jnp.einsum("hdx,xck->hdck", rpb * LOG2E, jnp.asarray(col_onehot),
                      precision=lax.Precision.HIGHEST)
    toep = jnp.where(jnp.asarray(col_ok)[None, None], toep, NEG_INF)
    n_off = toep.shape[1]
    toep = jnp.concatenate([toep, jnp.full((n_heads, 1, GRID_W, GRID_W), NEG_INF, F32)], axis=1)
    select = np.stack([np.where(row_ok, row_idx, n_off) for row_ok, row_idx in classes])
    n_cls, n_r, n_kr = select.shape
    tabs = jnp.take(toep, jnp.asarray(select.reshape(-1), jnp.int32), axis=1)
    tabs = tabs.reshape(n_heads, n_cls, n_r, n_kr, GRID_W, GRID_W).transpose(1, 0, 2, 4, 3, 5)
    return tabs.reshape(n_cls, n_heads, n_r * GRID_W, n_kr * GRID_W)


def _attn_c_lat_kernel(start_ref, cls_ref, q_ref, k_ref, v_ref, kc_ref, vc_ref, bias_ref, sg_ref, o_ref,
                       s_scr, m_scr, acc_scr, *, key_rows):
    blk = pl.program_id(2)
    n_loc = key_rows * GRID_W
    n_keys = s_scr.shape[-1]
    rows = pl.ds(pl.multiple_of(start_ref[blk] * GRID_W, GRID_W), n_loc)
    head_cols = [slice(h * HEAD_DIM, (h + 1) * HEAD_DIM) for h in range(C_HEAD_GROUP)]

    for h, hc in enumerate(head_cols):
        q = q_ref[:, hc]
        s_loc = _nt_dot(q, k_ref[rows, hc]) + bias_ref[h]
        s_ctx = _nt_dot(q, kc_ref[:, hc].astype(BF16))
        s_scr[h, :, 0:n_loc] = s_loc
        s_scr[h, :, n_loc:n_keys] = s_ctx
        m_scr[h] = jnp.maximum(_lane_fold(s_loc, jnp.maximum), _lane_fold(s_ctx, jnp.maximum))

    for h in range(C_HEAD_GROUP):
        m_scr[h] = jnp.broadcast_to(jnp.max(m_scr[h], axis=-1, keepdims=True), m_scr.shape[1:])

    for h, hc in enumerate(head_cols):
        p = jnp.exp2(s_scr[h] - jnp.concatenate([m_scr[h]] * (n_keys // HEAD_DIM), axis=1)).astype(BF16)
        acc_scr[h] = (jnp.dot(p[:, 0:n_loc], _with_ones(v_ref[rows, hc]), preferred_element_type=F32)
                      + jnp.dot(p[:, n_loc:n_keys], _with_ones(vc_ref[:, hc].astype(BF16)),
                                preferred_element_type=F32))

    for h, hc in enumerate(head_cols):
        acc = acc_scr[h]
        o = acc[:, :HEAD_DIM] * (1.0 / acc[:, HEAD_DIM:])
        o_ref[:, hc] = (o * sg_ref[:, hc].astype(F32)).astype(o_ref.dtype)


def _attn_c_lat(q, k, v, k_ctx, v_ctx, sg, rpb, n_batch, seq):
    rows = seq // GRID_W
    key_rows, starts, class_ids, geometry = _na_geometry(rows)
    bias = _na_bias_tables(rpb, geometry)
    nblk = rows // C_ROWS
    tq = C_ROWS * GRID_W
    gw = C_HEAD_GROUP * HEAD_DIM
    n_ctx = k_ctx.shape[1]
    qmap = lambda b, g, i, st, cl: (b * nblk + i, g)
    kvmap = lambda b, g, i, st, cl: (b, g)
    ctxmap = lambda b, g, i, st, cl: (b, 0, g)
    grid_spec = pltpu.PrefetchScalarGridSpec(
        num_scalar_prefetch=2,
        grid=(n_batch, C_HEADS // C_HEAD_GROUP, nblk),
        in_specs=[
            pl.BlockSpec((tq, gw), qmap),
            pl.BlockSpec((seq, gw), kvmap),
            pl.BlockSpec((seq, gw), kvmap),
            pl.BlockSpec((None, n_ctx, gw), ctxmap),
            pl.BlockSpec((None, n_ctx, gw), ctxmap),
            pl.BlockSpec((None, C_HEAD_GROUP, tq, key_rows * GRID_W),
                         lambda b, g, i, st, cl: (cl[i], g, 0, 0)),
            pl.BlockSpec((tq, gw), qmap),
        ],
        out_specs=pl.BlockSpec((tq, gw), qmap),
        scratch_shapes=[pltpu.VMEM((C_HEAD_GROUP, tq, key_rows * GRID_W + n_ctx), F32),
                        pltpu.VMEM((C_HEAD_GROUP, tq, HEAD_DIM), F32),
                        pltpu.VMEM((C_HEAD_GROUP, tq, 2 * HEAD_DIM), F32)],
    )
    return pl.pallas_call(
        functools.partial(_attn_c_lat_kernel, key_rows=key_rows),
        out_shape=jax.ShapeDtypeStruct(q.shape, BF16),
        grid_spec=grid_spec,
        compiler_params=_params("arbitrary", "arbitrary", "arbitrary"),
        name="attn_c_lat",
    )(jnp.asarray(starts), jnp.asarray(class_ids), q, k, v, k_ctx, v_ctx, bias, sg)


def _rope_tables(n_tokens):
    d4 = HEAD_DIM // 4
    t = jnp.arange(n_tokens)
    pos = jnp.stack([t // GRID_W, t % GRID_W], axis=-1).astype(F32)
    inv_freq = ROPE_BASE ** (-jnp.arange(d4, dtype=F32) / d4)
    ang = pos[:, :, None] * inv_freq
    cos, sin = jnp.cos(ang), jnp.sin(ang)
    zero = jnp.zeros_like(sin[:, 0])
    cos_full = jnp.concatenate([cos[:, 0], cos[:, 0], cos[:, 1], cos[:, 1]], axis=-1)
    sin_up = jnp.concatenate([-sin[:, 0], zero, -sin[:, 1], zero], axis=-1)
    sin_dn = jnp.concatenate([zero, sin[:, 0], zero, sin[:, 1]], axis=-1)
    return cos_full, sin_up, sin_dn


def kernel(x_prompt, x_sample, cache_a_k, cache_a_v, cache_b_k, cache_b_v, cache_c_k, cache_c_v, c, c_ctx,
           ln_g, ada_w, ada_b, w_out, qn_g, kn_g, w_in_a, sink_a, w_in_b, lam_b, subln_b, w_in_c, rpb_c):
    n_p, seq_p, d = x_prompt.shape
    n_s, seq_s, _ = x_sample.shape
    n_ctx = cache_a_k.shape[2]
    xp = x_prompt.reshape(n_p * seq_p, d)
    xs = x_sample.reshape(n_s * seq_s, d)
    rope_tabs = _rope_tables(seq_s)

    pad = (-(n_s + 1)) % 8
    cvec = jnp.concatenate([c, c_ctx[None, :], jnp.zeros((pad, d), F32)], axis=0)
    mod = _ada_modulation(cvec, ada_w, ada_b)
    mod = mod.reshape(DEPTH, mod.shape[1], 3, 1, d)

    new_kv = {0: ([], []), 1: ([], []), 2: ([], [])}
    for l in range(DEPTH):
        kind, j = l % N_MIXERS, l // N_MIXERS
        sh_s, sc_s, gt_s = (mod[l, :n_s, t] for t in range(3))
        sh_p, sc_p, gt_p = (mod[l, n_s:n_s + 1, t] for t in range(3))
        w_o = w_out[l].astype(BF16)
        if kind == 0:
            kvw = A_KV_HEADS * HEAD_DIM
            w_in, widths, tn = w_in_a[j].astype(BF16), (d, kvw, kvw, d), 512
        elif kind == 1:
            w_in, widths, tn = w_in_b[j].astype(BF16), (d, d, d, d), 512
        else:
            w_in, widths, tn = w_in_c[j].astype(BF16), (d, d, d, d), 512
        rope = rope_tabs if kind != 2 else None
        qp, kp, vp, gp = _in_projection(xp, sh_p, sc_p, ln_g[l], w_in, qn_g[l], kn_g[l], widths, None, F32, tn)
        qs, ks, vs, gs = _in_projection(xs, sh_s, sc_s, ln_g[l], w_in, qn_g[l], kn_g[l], widths, rope, BF16, tn)
        if kind == 0:
            op = _attn_a_ctx(qp, kp, vp, gp, sink_a[j], n_p, seq_p)
            os_ = _attn_a_lat(qs, ks, vs, cache_a_k, cache_a_v, j, gs, sink_a[j], n_s, seq_s)
            new_kv[0][0].append(kp.reshape(n_p, seq_p, A_KV_HEADS, HEAD_DIM))
            new_kv[0][1].append(vp.reshape(n_p, seq_p, A_KV_HEADS, HEAD_DIM))
        elif kind == 1:
            lambda_init = 0.8 - 0.6 * math.exp(-0.3 * l)
            kc = cache_b_k[:, j].reshape(n_s, n_ctx, d)
            vc = cache_b_v[:, j].reshape(n_s, n_ctx, d)
            op = _attn_b(qp, kp, vp, gp, lam_b[j], subln_b[j], lambda_init, n_p, seq_p)
            os_ = _attn_b(qs, ks, vs, gs, lam_b[j], subln_b[j], lambda_init, n_s, seq_s, kc, vc)
            new_kv[1][0].append(kp.reshape(n_p, seq_p, B_HEADS, 2, HEAD_DIM))
            new_kv[1][1].append(vp.reshape(n_p, seq_p, B_HEADS, 2 * HEAD_DIM))
        else:
            kc = cache_c_k[:, j].reshape(n_s, n_ctx, d)
            vc = cache_c_v[:, j].reshape(n_s, n_ctx, d)
            op = _attn_c_ctx(qp, kp, vp, gp, n_p, seq_p)
            os_ = _attn_c_lat(qs, ks, vs, kc, vc, gs, rpb_c[j], n_s, seq_s)
            new_kv[2][0].append(kp.reshape(n_p, seq_p, C_HEADS, HEAD_DIM))
            new_kv[2][1].append(vp.reshape(n_p, seq_p, C_HEADS, HEAD_DIM))
        xp = _out_projection(op, w_o, xp, gt_p)
        xs = _out_projection(os_, w_o, xs, gt_s)

    outs = [xp.reshape(x_prompt.shape), xs.reshape(x_sample.shape)]
    for kind in range(3):
        outs.append(jnp.stack(new_kv[kind][0], axis=1))
        outs.append(jnp.stack(new_kv[kind][1], axis=1))
    return tuple(outs)
```

```python
import functools
import math

import numpy as np
import jax
import jax.numpy as jnp
from jax import lax
from jax.experimental import pallas as pl
from jax.experimental.pallas import tpu as pltpu

D_MODEL = 2048
DEPTH = 4
GRID_W = 64
HEAD_DIM = 128
ROPE_BASE = 10000.0
NEG_INF = -1e30
N_MIXERS = 3
A_HEADS = D_MODEL // HEAD_DIM
A_KV_HEADS = A_HEADS // 4
A_GROUP = A_HEADS // A_KV_HEADS
WINDOW = 128
B_HEADS = D_MODEL // (2 * HEAD_DIM)
C_HEADS = D_MODEL // HEAD_DIM
NA_KH = 8
NA_KW = 16
SCALE = HEAD_DIM ** -0.5
LOG2E = math.log2(math.e)
Q_SCALE = SCALE * LOG2E
EPS = 1e-6

BF16 = jnp.bfloat16
F32 = jnp.float32

VMEM_LIMIT_BYTES = 56 * 1024 * 1024
ROW_TILE = 1024
OUT_ROW_TILE = 512
OUT_SUB = 512
PROJ_SUB = 128
A_QBLOCK = 128
B_QBLOCK = 1024
B_KCHUNK = 512
B_UNROLL = 4
C_ROWS = 4
C_HEAD_GROUP = 4


def _nt_dot(a, b):
    return lax.dot_general(a, b, (((1,), (1,)), ((), ())), preferred_element_type=F32)


def _params(*semantics):
    return pltpu.CompilerParams(dimension_semantics=semantics, vmem_limit_bytes=VMEM_LIMIT_BYTES)


def _ada_kernel(c_ref, w_ref, b_ref, o_ref):
    c = c_ref[...]
    a = (c * jax.nn.sigmoid(c)).astype(BF16)
    o_ref[...] = jnp.dot(a, w_ref[...].astype(BF16), preferred_element_type=F32) + b_ref[...]


def _ada_modulation(cvec, ada_w, ada_b):
    rows = cvec.shape[0]
    tn = 768
    return pl.pallas_call(
        _ada_kernel,
        out_shape=jax.ShapeDtypeStruct((DEPTH, rows, 3 * D_MODEL), F32),
        grid=(DEPTH, 3 * D_MODEL // tn),
        in_specs=[
            pl.BlockSpec((rows, D_MODEL), lambda l, j: (0, 0)),
            pl.BlockSpec((None, D_MODEL, tn), lambda l, j: (l, 0, j)),
            pl.BlockSpec((None, 1, tn), lambda l, j: (l, 0, j)),
        ],
        out_specs=pl.BlockSpec((None, rows, tn), lambda l, j: (l, 0, j)),
        compiler_params=_params("arbitrary", "arbitrary"),
        name="ada_modulation",
    )(cvec, ada_w, ada_b.reshape(DEPTH, 1, 3 * D_MODEL))


def _inproj_kernel(*refs, n_q, rope, heads_per_tile):
    x_ref, sh_ref, sc_ref, lng_ref, w1_ref, w2_ref, qg_ref, kg_ref = refs[:8]
    if rope:
        cos_ref, sa_ref, sb_ref = refs[8:11]
    q_out, k_out, v_out, g_out, h_scr = refs[-5:]
    j = pl.program_id(1)

    def modulated_norm(rows):
        x = x_ref[rows, :]
        ms = jnp.mean(x * x, axis=-1, keepdims=True)
        gain = lng_ref[...] * (1.0 + sc_ref[...])
        return (x * lax.rsqrt(ms + EPS) * gain + sh_ref[...]).astype(BF16)

    def normed(acc, rows, gain_ref, scale):
        outs = []
        for t in range(heads_per_tile):
            y = acc[:, t * HEAD_DIM:(t + 1) * HEAD_DIM]
            ms = jnp.mean(y * y, axis=-1, keepdims=True)
            y = y * lax.rsqrt(ms + EPS) * (gain_ref[...] * scale)
            if rope:
                y = (y * cos_ref[rows, :] + pltpu.roll(y, 96, 1) * sa_ref[rows, :]
                     + pltpu.roll(y, 32, 1) * sb_ref[rows, :])
            outs.append(y)
        return jnp.concatenate(outs, axis=-1)

    def project(w_ref, out_ref, epilogue, fill_h=False):
        for r in range(h_scr.shape[0] // PROJ_SUB):
            rows = slice(r * PROJ_SUB, (r + 1) * PROJ_SUB)
            if fill_h:
                h_scr[rows, :] = modulated_norm(rows)
            acc = jnp.dot(h_scr[rows, :], w_ref[...], preferred_element_type=F32)
            out_ref[rows, :] = epilogue(acc, rows).astype(out_ref.dtype)

    def q_and_gate(fill_h):
        project(w1_ref, q_out, lambda acc, rows: normed(acc, rows, qg_ref, Q_SCALE), fill_h)
        project(w2_ref, g_out, lambda acc, rows: acc * jax.nn.sigmoid(acc))

    @pl.when(j == 0)
    def _():
        q_and_gate(True)

    @pl.when((j > 0) & (j < n_q))
    def _():
        q_and_gate(False)

    @pl.when(j >= n_q)
    def _():
        project(w1_ref, k_out, lambda acc, rows: normed(acc, rows, kg_ref, 1.0))
        project(w2_ref, v_out, lambda acc, rows: acc)


def _in_projection(x, shift, scale, ln_g, w, qg, kg, widths, rope_tabs, kv_dtype, tn):
    t_rows = x.shape[0]
    qw, kw, vw, gw = widths
    assert qw == gw and kw == vw
    n_q, n_k = qw // tn, kw // tn
    tm = min(ROW_TILE, t_rows)
    rows_per_mod = t_rows // shift.shape[0]
    assert t_rows % tm == 0 and rows_per_mod % tm == 0
    rope = rope_tabs is not None
    g_col0, v_col0 = (qw + kw + vw) // tn, (qw + kw) // tn

    def mod_map(i, j):
        return ((i * tm) // rows_per_mod, 0, 0)

    in_specs = [
        pl.BlockSpec((tm, D_MODEL), lambda i, j: (i, 0)),
        pl.BlockSpec((None, 1, D_MODEL), mod_map),
        pl.BlockSpec((None, 1, D_MODEL), mod_map),
        pl.BlockSpec((1, D_MODEL), lambda i, j: (0, 0)),
        pl.BlockSpec((D_MODEL, tn), lambda i, j: (0, j)),
        pl.BlockSpec((D_MODEL, tn), lambda i, j: (0, jnp.where(j < n_q, g_col0 + j, v_col0 + j - n_q))),
        pl.BlockSpec((1, HEAD_DIM), lambda i, j: (0, 0)),
        pl.BlockSpec((1, HEAD_DIM), lambda i, j: (0, 0)),
    ]
    args = [x, shift, scale, ln_g.reshape(1, D_MODEL), w, w, qg.reshape(1, HEAD_DIM), kg.reshape(1, HEAD_DIM)]
    if rope:
        n_pos_tiles = rope_tabs[0].shape[0] // tm
        for tab in rope_tabs:
            in_specs.append(pl.BlockSpec((tm, HEAD_DIM), lambda i, j: (i % n_pos_tiles, 0)))
            args.append(tab)
    out_specs = [
        pl.BlockSpec((tm, tn), lambda i, j: (i, jnp.minimum(j, n_q - 1))),
        pl.BlockSpec((tm, tn), lambda i, j: (i, jnp.maximum(j - n_q, 0))),
        pl.BlockSpec((tm, tn), lambda i, j: (i, jnp.maximum(j - n_q, 0))),
        pl.BlockSpec((tm, tn), lambda i, j: (i, jnp.minimum(j, n_q - 1))),
    ]
    out_shape = [
        jax.ShapeDtypeStruct((t_rows, qw), BF16),
        jax.ShapeDtypeStruct((t_rows, kw), kv_dtype),
        jax.ShapeDtypeStruct((t_rows, vw), kv_dtype),
        jax.ShapeDtypeStruct((t_rows, gw), BF16),
    ]
    kern = functools.partial(_inproj_kernel, n_q=n_q, rope=rope, heads_per_tile=tn // HEAD_DIM)
    return pl.pallas_call(
        kern,
        out_shape=out_shape,
        grid=(t_rows // tm, n_q + n_k),
        in_specs=in_specs,
        out_specs=out_specs,
        scratch_shapes=[pltpu.VMEM((tm, D_MODEL), BF16)],
        compiler_params=_params("arbitrary", "arbitrary"),
        name="in_projection",
    )(*args)


def _outproj_kernel(o_ref, w_ref, x_ref, gt_ref, y_ref):
    for c in range(D_MODEL // OUT_SUB):
        cols = slice(c * OUT_SUB, (c + 1) * OUT_SUB)
        y = jnp.dot(o_ref[...], w_ref[:, cols], preferred_element_type=F32)
        y_ref[:, cols] = x_ref[:, cols] + gt_ref[:, cols] * y


def _out_projection(o, w, x, gate):
    t_rows = x.shape[0]
    tm = min(OUT_ROW_TILE, t_rows)
    rows_per_mod = t_rows // gate.shape[0]
    assert t_rows % tm == 0 and rows_per_mod % tm == 0
    row = lambda i: (i, 0)
    return pl.pallas_call(
        _outproj_kernel,
        out_shape=jax.ShapeDtypeStruct((t_rows, D_MODEL), F32),
        grid=(t_rows // tm,),
        in_specs=[
            pl.BlockSpec((tm, D_MODEL), row),
            pl.BlockSpec((D_MODEL, D_MODEL), lambda i: (0, 0)),
            pl.BlockSpec((tm, D_MODEL), row),
            pl.BlockSpec((None, 1, D_MODEL), lambda i: ((i * tm) // rows_per_mod, 0, 0)),
        ],
        out_specs=pl.BlockSpec((tm, D_MODEL), row),
        compiler_params=_params("arbitrary"),
        name="out_projection",
    )(o, w, x, gate)


def _lane_fold(x, op):
    r = x[:, 0:HEAD_DIM]
    for t in range(1, x.shape[1] // HEAD_DIM):
        r = op(r, x[:, t * HEAD_DIM:(t + 1) * HEAD_DIM])
    return r


def _with_ones(v):
    return jnp.concatenate([v, jnp.ones(v.shape, v.dtype)], axis=1)


def _softmax_pv(scores, values):
    m128 = _lane_fold(scores[0], jnp.maximum)
    for s in scores[1:]:
        m128 = jnp.maximum(m128, _lane_fold(s, jnp.maximum))
    m = jnp.broadcast_to(jnp.max(m128, axis=-1, keepdims=True), m128.shape)
    acc = None
    for s, v in zip(scores, values):
        p = jnp.exp2(s - jnp.concatenate([m] * (s.shape[1] // HEAD_DIM), axis=1)).astype(BF16)
        pv = jnp.dot(p, _with_ones(v), preferred_element_type=F32)
        acc = pv if acc is None else acc + pv
    return acc[:, :HEAD_DIM] * (1.0 / acc[:, HEAD_DIM:])


def _attn_a_kernel(*refs, latent, n_qblocks):
    if latent:
        (sink_ref, q_ref, kp_ref, ko_ref, kn_ref, vp_ref, vo_ref, vn_ref,
         kc_ref, vc_ref, sg_ref, o_ref, s_scr, m_scr, acc_scr, kc_scr, vc_scr) = refs
    else:
        sink_ref, q_ref, k_ref, v_ref, sg_ref, o_ref, kcache_ref, vcache_ref, s_scr, m_scr, acc_scr = refs
        for kv in range(A_KV_HEADS):
            kcache_ref[:, kv, :] = k_ref[:, kv * HEAD_DIM:(kv + 1) * HEAD_DIM]
            vcache_ref[:, kv, :] = v_ref[:, kv * HEAD_DIM:(kv + 1) * HEAD_DIM]
    m_rows = q_ref.shape[0]
    n_keys = s_scr.shape[-1]
    n_loc = 3 * A_QBLOCK
    kv_cols = [slice(kv * HEAD_DIM, (kv + 1) * HEAD_DIM) for kv in range(A_KV_HEADS)]
    kv_heads = [[kv * A_GROUP + g for g in range(A_GROUP)] for kv in range(A_KV_HEADS)]

    if latent:
        blk = pl.program_id(1)
        qpos = lax.broadcasted_iota(jnp.int32, (A_QBLOCK, n_loc), 0)
        col = lax.broadcasted_iota(jnp.int32, (A_QBLOCK, n_loc), 1)
        rel = col - A_QBLOCK - qpos
        kpos = (blk - 1) * A_QBLOCK + col
        valid = (jnp.abs(rel) <= WINDOW) & (kpos >= 0) & (kpos < n_qblocks * A_QBLOCK)
        window_bias = jnp.concatenate([jnp.where(valid, 0.0, NEG_INF)] * A_GROUP, axis=0)

        @pl.when(blk == 0)
        def _():
            for kv in range(A_KV_HEADS):
                kc_scr[kv] = kc_ref[:, kv, :].astype(BF16)
                vc_scr[kv] = vc_ref[:, kv, :].astype(BF16)

    for kv, cols in enumerate(kv_cols):
        q = jnp.concatenate([q_ref[:, h * HEAD_DIM:(h + 1) * HEAD_DIM] for h in kv_heads[kv]], axis=0)
        if latent:
            k_loc = jnp.concatenate([kp_ref[:, cols], ko_ref[:, cols], kn_ref[:, cols]], axis=0)
            s_loc = _nt_dot(q, k_loc) + window_bias
            s_ctx = _nt_dot(q, kc_scr[kv])
            s_scr[kv, :, 0:n_loc] = s_loc
            s_scr[kv, :, n_loc:n_keys] = s_ctx
            m_scr[kv] = jnp.maximum(_lane_fold(s_loc, jnp.maximum), _lane_fold(s_ctx, jnp.maximum))
        else:
            s = _nt_dot(q, k_ref[:, cols].astype(BF16))
            s_scr[kv] = s
            m_scr[kv] = _lane_fold(s, jnp.maximum)

    def sink_lanes(kv):
        return jnp.concatenate([jnp.full((m_rows, HEAD_DIM), sink_ref[h] * LOG2E, F32) for h in kv_heads[kv]], axis=0)

    for kv in range(A_KV_HEADS):
        m = jnp.max(m_scr[kv], axis=-1, keepdims=True)
        m_scr[kv] = jnp.maximum(jnp.broadcast_to(m, m_scr.shape[1:]), sink_lanes(kv))

    for kv, cols in enumerate(kv_cols):
        p = jnp.exp2(s_scr[kv] - jnp.concatenate([m_scr[kv]] * (n_keys // HEAD_DIM), axis=1)).astype(BF16)
        if latent:
            v_loc = jnp.concatenate([vp_ref[:, cols], vo_ref[:, cols], vn_ref[:, cols]], axis=0)
            acc_scr[kv] = (jnp.dot(p[:, 0:n_loc], _with_ones(v_loc), preferred_element_type=F32)
                           + jnp.dot(p[:, n_loc:n_keys], _with_ones(vc_scr[kv]), preferred_element_type=F32))
        else:
            acc_scr[kv] = jnp.dot(p, _with_ones(v_ref[:, cols].astype(BF16)), preferred_element_type=F32)

    for kv in range(A_KV_HEADS):
        acc = acc_scr[kv]
        l = acc[:, HEAD_DIM:] + jnp.exp2(sink_lanes(kv) - m_scr[kv])
        o = acc[:, :HEAD_DIM] * (1.0 / l)
        for g, h in enumerate(kv_heads[kv]):
            hc = slice(h * HEAD_DIM, (h + 1) * HEAD_DIM)
            o_ref[:, hc] = (o[g * m_rows:(g + 1) * m_rows] * sg_ref[:, hc].astype(F32)).astype(o_ref.dtype)


def _attn_a_scratch(m_rows, n_keys):
    stacked = A_GROUP * m_rows
    return [pltpu.VMEM((A_KV_HEADS, stacked, n_keys), F32), pltpu.VMEM((A_KV_HEADS, stacked, HEAD_DIM), F32),
            pltpu.VMEM((A_KV_HEADS, stacked, 2 * HEAD_DIM), F32)]


def _attn_a_ctx(q, k, v, sg, sink, n_batch, seq):
    kvw = A_KV_HEADS * HEAD_DIM
    row = lambda b: (b, 0)
    cache_shape = jax.ShapeDtypeStruct((n_batch, seq, A_KV_HEADS, HEAD_DIM), F32)
    cache_spec = pl.BlockSpec((None, seq, A_KV_HEADS, HEAD_DIM), lambda b: (b, 0, 0, 0))
    return pl.pallas_call(
        functools.partial(_attn_a_kernel, latent=False, n_qblocks=1),
        out_shape=[jax.ShapeDtypeStruct(q.shape, BF16), cache_shape, cache_shape],
        grid=(n_batch,),
        in_specs=[
            pl.BlockSpec(memory_space=pltpu.SMEM),
            pl.BlockSpec((seq, D_MODEL), row),
            pl.BlockSpec((seq, kvw), row),
            pl.BlockSpec((seq, kvw), row),
            pl.BlockSpec((seq, D_MODEL), row),
        ],
        out_specs=[pl.BlockSpec((seq, D_MODEL), row), cache_spec, cache_spec],
        scratch_shapes=_attn_a_scratch(seq, seq),
        compiler_params=_params("arbitrary"),
        name="attn_a_ctx",
    )(sink, q, k, v, sg)


def _attn_a_lat(q, k, v, cache_k, cache_v, cache_slot, sg, sink, n_batch, seq):
    kvw = A_KV_HEADS * HEAD_DIM
    nb = seq // A_QBLOCK
    own = lambda b, i: (b * nb + i, 0)
    prev = lambda b, i: (b * nb + jnp.maximum(i - 1, 0), 0)
    nxt = lambda b, i: (b * nb + jnp.minimum(i + 1, nb - 1), 0)
    ctx = lambda b, i: (b, cache_slot, 0, 0, 0)
    n_ctx = cache_k.shape[2]
    ctx_spec = pl.BlockSpec((None, None, n_ctx, A_KV_HEADS, HEAD_DIM), ctx)
    ctx_scratch = [pltpu.VMEM((A_KV_HEADS, n_ctx, HEAD_DIM), BF16)] * 2
    return pl.pallas_call(
        functools.partial(_attn_a_kernel, latent=True, n_qblocks=nb),
        out_shape=jax.ShapeDtypeStruct(q.shape, BF16),
        grid=(n_batch, nb),
        in_specs=[
            pl.BlockSpec(memory_space=pltpu.SMEM),
            pl.BlockSpec((A_QBLOCK, D_MODEL), own),
            pl.BlockSpec((A_QBLOCK, kvw), prev),
            pl.BlockSpec((A_QBLOCK, kvw), own),
            pl.BlockSpec((A_QBLOCK, kvw), nxt),
            pl.BlockSpec((A_QBLOCK, kvw), prev),
            pl.BlockSpec((A_QBLOCK, kvw), own),
            pl.BlockSpec((A_QBLOCK, kvw), nxt),
            ctx_spec,
            ctx_spec,
            pl.BlockSpec((A_QBLOCK, D_MODEL), own),
        ],
        out_specs=pl.BlockSpec((A_QBLOCK, D_MODEL), own),
        scratch_shapes=_attn_a_scratch(A_QBLOCK, 3 * A_QBLOCK + n_ctx) + ctx_scratch,
        compiler_params=_params("arbitrary", "arbitrary"),
        name="attn_a_lat",
    )(sink, q, k, k, k, v, v, v, cache_k, cache_v, sg)


def _attn_b_kernel(*refs, has_ctx, lambda_init, k_chunk):
    if has_ctx:
        (lam_ref, sub_ref, q_ref, k_ref, v_ref, kc_ref, vc_ref, sg_ref, o_ref,
         s_scr, sc_scr, m_scr, l_scr, acc_scr) = refs
    else:
        lam_ref, sub_ref, q_ref, k_ref, v_ref, sg_ref, o_ref, kcache_ref, s_scr, m_scr, l_scr, acc_scr = refs
        kc_ref = vc_ref = sc_scr = None
    dv = 2 * HEAD_DIM
    for head in range(q_ref.shape[1] // dv):
        cols = slice(head * dv, (head + 1) * dv)
        if not has_ctx:
            for half in range(2):
                kcache_ref[:, head, half, :] = k_ref[:, head * dv + half * HEAD_DIM:head * dv + (half + 1) * HEAD_DIM]
        view = lambda ref: None if ref is None else ref.at[:, cols]
        _attn_b_head(lam_ref, sub_ref, view(q_ref), view(k_ref), view(v_ref), view(kc_ref), view(vc_ref),
                     view(sg_ref), view(o_ref), s_scr, sc_scr, m_scr, l_scr, acc_scr,
                     lambda_init=lambda_init, k_chunk=k_chunk)


def _attn_b_head(lam_ref, sub_ref, q_ref, k_ref, v_ref, kc_ref, vc_ref, sg_ref, o_ref,
                 s_scr, sc_scr, m_scr, l_scr, acc_scr, *, lambda_init, k_chunk):
    has_ctx = kc_ref is not None
    n_chunks = k_ref.shape[0] // k_chunk
    halves = [slice(h * HEAD_DIM, (h + 1) * HEAD_DIM) for h in range(2)]

    def fold_max(h, s, first):
        m = _lane_fold(s, jnp.maximum)
        m_scr[h] = m if first else jnp.maximum(m_scr[h], m)

    def scores(c, first=False):
        rows = pl.ds(pl.multiple_of(c * k_chunk, k_chunk), k_chunk)
        k = k_ref[rows, :].astype(BF16)
        for h, hc in enumerate(halves):
            s = _nt_dot(q_ref[:, hc], k[:, hc])
            s_scr[h, c] = s
            fold_max(h, s, first)

    if has_ctx:
        kc = kc_ref[...].astype(BF16)
        for h, hc in enumerate(halves):
            s = _nt_dot(q_ref[:, hc], kc[:, hc])
            sc_scr[h] = s
            fold_max(h, s, True)
        lax.fori_loop(0, n_chunks, lambda c, carry: (scores(c), carry)[1], 0, unroll=B_UNROLL)
    else:
        assert n_chunks == 1
        scores(0, first=True)

    for h in range(2):
        m_scr[h] = jnp.broadcast_to(jnp.max(m_scr[h], axis=-1, keepdims=True), m_scr.shape[1:])

    def accumulate(h, s, v, first):
        m = m_scr[h]
        p = jnp.exp2(s - jnp.concatenate([m] * (s.shape[1] // HEAD_DIM), axis=1))
        l = _lane_fold(p, jnp.add)
        pv = jnp.dot(p.astype(BF16), v, preferred_element_type=F32)
        l_scr[h] = l if first else l_scr[h] + l
        acc_scr[h] = pv if first else acc_scr[h] + pv

    def weighted(c, first=False):
        rows = pl.ds(pl.multiple_of(c * k_chunk, k_chunk), k_chunk)
        v = v_ref[rows, :].astype(BF16)
        for h in range(2):
            accumulate(h, s_scr[h, c], v, first)

    if has_ctx:
        vc = vc_ref[...].astype(BF16)
        for h in range(2):
            accumulate(h, sc_scr[h], vc, True)
        lax.fori_loop(0, n_chunks, lambda c, carry: (weighted(c), carry)[1], 0, unroll=B_UNROLL)
    else:
        weighted(0, first=True)

    lam = lam_ref[...]
    lam_full = (jnp.exp(jnp.sum(lam[0:1] * lam[1:2], axis=-1, keepdims=True))
                - jnp.exp(jnp.sum(lam[2:3] * lam[3:4], axis=-1, keepdims=True)) + lambda_init)
    r0 = 1.0 / jnp.sum(l_scr[0], axis=-1, keepdims=True)
    r1 = lam_full / jnp.sum(l_scr[1], axis=-1, keepdims=True)
    o = acc_scr[0] * r0 - acc_scr[1] * r1
    ms = jnp.mean(o * o, axis=-1, keepdims=True)
    o = o * lax.rsqrt(ms + EPS) * (sub_ref[...] * (1.0 - lambda_init))
    o_ref[...] = (o * sg_ref[...].astype(F32)).astype(o_ref.dtype)


def _attn_b(q, k, v, sg, lam, subln, lambda_init, n_batch, seq, k_ctx=None, v_ctx=None):
    dv = 2 * HEAD_DIM
    tq = min(B_QBLOCK, seq)
    nq = seq // tq
    k_chunk = min(B_KCHUNK, seq)
    has_ctx = k_ctx is not None
    heads_per_step = 1 if has_ctx else B_HEADS
    gw = heads_per_step * dv
    qmap = lambda b, h, i: (b * nq + i, h)
    kvmap = lambda b, h, i: (b, h)
    in_specs = [
        pl.BlockSpec((4, HEAD_DIM), lambda b, h, i: (0, 0)),
        pl.BlockSpec((1, dv), lambda b, h, i: (0, 0)),
        pl.BlockSpec((tq, gw), qmap),
        pl.BlockSpec((seq, gw), kvmap),
        pl.BlockSpec((seq, gw), kvmap),
    ]
    args = [lam, subln.reshape(1, dv), q, k, v]
    scratch = [pltpu.VMEM((2, seq // k_chunk, tq, k_chunk), F32)]
    if has_ctx:
        n_ctx = k_ctx.shape[1]
        in_specs += [pl.BlockSpec((None, n_ctx, gw), lambda b, h, i: (b, 0, h))] * 2
        args += [k_ctx, v_ctx]
        scratch.append(pltpu.VMEM((2, tq, n_ctx), F32))
    in_specs.append(pl.BlockSpec((tq, gw), qmap))
    args.append(sg)
    scratch += [pltpu.VMEM((2, tq, HEAD_DIM), F32), pltpu.VMEM((2, tq, HEAD_DIM), F32),
                pltpu.VMEM((2, tq, dv), F32)]
    out_shape = jax.ShapeDtypeStruct(q.shape, BF16)
    out_specs = pl.BlockSpec((tq, gw), qmap)
    if not has_ctx:
        assert nq == 1 and k.dtype == F32
        out_shape = [out_shape, jax.ShapeDtypeStruct((n_batch, seq, B_HEADS, 2, HEAD_DIM), F32)]
        out_specs = [out_specs, pl.BlockSpec((None, seq, B_HEADS, 2, HEAD_DIM), lambda b, h, i: (b, 0, 0, 0, 0))]
    return pl.pallas_call(
        functools.partial(_attn_b_kernel, has_ctx=has_ctx, lambda_init=lambda_init, k_chunk=k_chunk),
        out_shape=out_shape,
        grid=(n_batch, B_HEADS // heads_per_step, nq),
        in_specs=in_specs,
        out_specs=out_specs,
        scratch_shapes=scratch,
        compiler_params=_params("arbitrary", "arbitrary", "arbitrary"),
        name="attn_b_lat" if has_ctx else "attn_b_ctx",
    )(*args)


def _attn_c_ctx_kernel(q_ref, k_ref, v_ref, sg_ref, o_ref):
    for h in range(C_HEADS):
        hc = slice(h * HEAD_DIM, (h + 1) * HEAD_DIM)
        s = _nt_dot(q_ref[:, hc], k_ref[:, hc].astype(BF16))
        o = _softmax_pv([s], [v_ref[:, hc].astype(BF16)])
        o_ref[:, hc] = (o * sg_ref[:, hc].astype(F32)).astype(o_ref.dtype)


def _attn_c_ctx(q, k, v, sg, n_batch, seq):
    spec = pl.BlockSpec((seq, D_MODEL), lambda b: (b, 0))
    return pl.pallas_call(
        _attn_c_ctx_kernel,
        out_shape=jax.ShapeDtypeStruct(q.shape, BF16),
        grid=(n_batch,),
        in_specs=[spec, spec, spec, spec],
        out_specs=spec,
        compiler_params=_params("arbitrary"),
        name="attn_c_ctx",
    )(q, k, v, sg)


def _na_geometry(rows):
    kh = min(NA_KH, rows)
    key_rows = C_ROWS + kh - 1
    key_rows = min(key_rows + key_rows % 2, rows)
    n_blocks = rows // C_ROWS
    cols = np.arange(GRID_W)
    col_start = np.clip(cols - NA_KW // 2, 0, GRID_W - NA_KW)
    col_ok = (cols[None, :] >= col_start[:, None]) & (cols[None, :] < col_start[:, None] + NA_KW)
    col_delta = cols[None, :] - cols[:, None] + NA_KW - 1
    col_onehot = (col_delta[None] == np.arange(2 * NA_KW - 1)[:, None, None]) & col_ok[None]
    starts, class_ids, classes, keys = [], [], [], {}
    for blk in range(n_blocks):
        r = blk * C_ROWS + np.arange(C_ROWS)
        rs = np.clip(r - kh // 2, 0, rows - kh)
        start = int(np.clip(rs[0], 0, rows - key_rows))
        key = (tuple(rs - r), start - blk * C_ROWS)
        if key not in keys:
            keys[key] = len(classes)
            kr = start + np.arange(key_rows)
            row_ok = (kr[None, :] >= rs[:, None]) & (kr[None, :] < rs[:, None] + kh)
            row_idx = kr[None, :] - r[:, None] + NA_KH - 1
            classes.append((row_ok, row_idx))
        starts.append(start)
        class_ids.append(keys[key])
    geometry = (col_ok, col_onehot.astype(np.float32), classes)
    return key_rows, np.asarray(starts, np.int32), np.asarray(class_ids, np.int32), geometry


def _na_bias_tables(rpb, geometry):
    col_ok, col_onehot, classes = geometry
    n_heads = rpb.shape[0]
    toep = jnp.einsum("hdx,xck->hdck", rpb * LOG2E, jnp.asarray(col_onehot),
                      precision=lax.Precision.HIGHEST)
    toep = jnp.where(jnp.asarray(col_ok)[None, None], toep, NEG_INF)
    n_off = toep.shape[1]
    toep = jnp.concatenate([toep, jnp.full((n_heads, 1, GRID_W, GRID_W), NEG_INF, F32)], axis=1)
    select = np.stack([np.where(row_ok, row_idx, n_off) for row_ok, row_idx in classes])
    n_cls, n_r, n_kr = select.shape
    tabs = jnp.take(toep, jnp.asarray(select.reshape(-1), jnp.int32), axis=1)
    tabs = tabs.reshape(n_heads, n_cls, n_r, n_kr, GRID_W, GRID_W).transpose(1, 0, 2, 4, 3, 5)
    return tabs.reshape(n_cls, n_heads, n_r * GRID_W, n_kr * GRID_W)


def _attn_c_lat_kernel(start_ref, cls_ref, q_ref, k_ref, v_ref, kc_ref, vc_ref, bias_ref, sg_ref, o_ref,
                       s_scr, m_scr, acc_scr, *, key_rows):
    blk = pl.program_id(2)
    n_loc = key_rows * GRID_W
    n_keys = s_scr.shape[-1]
    rows = pl.ds(pl.multiple_of(start_ref[blk] * GRID_W, GRID_W), n_loc)
    head_cols = [slice(h * HEAD_DIM, (h + 1) * HEAD_DIM) for h in range(C_HEAD_GROUP)]

    for h, hc in enumerate(head_cols):
        q = q_ref[:, hc]
        s_loc = _nt_dot(q, k_ref[rows, hc]) + bias_ref[h]
        s_ctx = _nt_dot(q, kc_ref[:, hc].astype(BF16))
        s_scr[h, :, 0:n_loc] = s_loc
        s_scr[h, :, n_loc:n_keys] = s_ctx
        m_scr[h] = jnp.maximum(_lane_fold(s_loc, jnp.maximum), _lane_fold(s_ctx, jnp.maximum))

    for h in range(C_HEAD_GROUP):
        m_scr[h] = jnp.broadcast_to(jnp.max(m_scr[h], axis=-1, keepdims=True), m_scr.shape[1:])

    for h, hc in enumerate(head_cols):
        p = jnp.exp2(s_scr[h] - jnp.concatenate([m_scr[h]] * (n_keys // HEAD_DIM), axis=1)).astype(BF16)
        acc_scr[h] = (jnp.dot(p[:, 0:n_loc], _with_ones(v_ref[rows, hc]), preferred_element_type=F32)
                      + jnp.dot(p[:, n_loc:n_keys], _with_ones(vc_ref[:, hc].astype(BF16)),
                                preferred_element_type=F32))

    for h, hc in enumerate(head_cols):
        acc = acc_scr[h]
        o = acc[:, :HEAD_DIM] * (1.0 / acc[:, HEAD_DIM:])
        o_ref[:, hc] = (o * sg_ref[:, hc].astype(F32)).astype(o_ref.dtype)


def _attn_c_lat(q, k, v, k_ctx, v_ctx, sg, rpb, n_batch, seq):
    rows = seq // GRID_W
    key_rows, starts, class_ids, geometry = _na_geometry(rows)
    bias = _na_bias_tables(rpb, geometry)
    nblk = rows // C_ROWS
    tq = C_ROWS * GRID_W
    gw = C_HEAD_GROUP * HEAD_DIM
    n_ctx = k_ctx.shape[1]
    qmap = lambda b, g, i, st, cl: (b * nblk + i, g)
    kvmap = lambda b, g, i, st, cl: (b, g)
    ctxmap = lambda b, g, i, st, cl: (b, 0, g)
    grid_spec = pltpu.PrefetchScalarGridSpec(
        num_scalar_prefetch=2,
        grid=(n_batch, C_HEADS // C_HEAD_GROUP, nblk),
        in_specs=[
            pl.BlockSpec((tq, gw), qmap),
            pl.BlockSpec((seq, gw), kvmap),
            pl.BlockSpec((seq, gw), kvmap),
            pl.BlockSpec((None, n_ctx, gw), ctxmap),
            pl.BlockSpec((None, n_ctx, gw), ctxmap),
            pl.BlockSpec((None, C_HEAD_GROUP, tq, key_rows * GRID_W),
                         lambda b, g, i, st, cl: (cl[i], g, 0, 0)),
            pl.BlockSpec((tq, gw), qmap),
        ],
        out_specs=pl.BlockSpec((tq, gw), qmap),
        scratch_shapes=[pltpu.VMEM((C_HEAD_GROUP, tq, key_rows * GRID_W + n_ctx), F32),
                        pltpu.VMEM((C_HEAD_GROUP, tq, HEAD_DIM), F32),
                        pltpu.VMEM((C_HEAD_GROUP, tq, 2 * HEAD_DIM), F32)],
    )
    return pl.pallas_call(
        functools.partial(_attn_c_lat_kernel, key_rows=key_rows),
        out_shape=jax.ShapeDtypeStruct(q.shape, BF16),
        grid_spec=grid_spec,
        compiler_params=_params("arbitrary", "arbitrary", "arbitrary"),
        name="attn_c_lat",
    )(jnp.asarray(starts), jnp.asarray(class_ids), q, k, v, k_ctx, v_ctx, bias, sg)


def _rope_tables(n_tokens):
    d4 = HEAD_DIM // 4
    t = jnp.arange(n_tokens)
    pos = jnp.stack([t // GRID_W, t % GRID_W], axis=-1).astype(F32)
    inv_freq = ROPE_BASE ** (-jnp.arange(d4, dtype=F32) / d4)
    ang = pos[:, :, None] * inv_freq
    cos, sin = jnp.cos(ang), jnp.sin(ang)
    zero = jnp.zeros_like(sin[:, 0])
    cos_full = jnp.concatenate([cos[:, 0], cos[:, 0], cos[:, 1], cos[:, 1]], axis=-1)
    sin_up = jnp.concatenate([-sin[:, 0], zero, -sin[:, 1], zero], axis=-1)
    sin_dn = jnp.concatenate([zero, sin[:, 0], zero, sin[:, 1]], axis=-1)
    return cos_full, sin_up, sin_dn


def kernel(x_prompt, x_sample, cache_a_k, cache_a_v, cache_b_k, cache_b_v, cache_c_k, cache_c_v, c, c_ctx,
           ln_g, ada_w, ada_b, w_out, qn_g, kn_g, w_in_a, sink_a, w_in_b, lam_b, subln_b, w_in_c, rpb_c):
    n_p, seq_p, d = x_prompt.shape
    n_s, seq_s, _ = x_sample.shape
    n_ctx = cache_a_k.shape[2]
    xp = x_prompt.reshape(n_p * seq_p, d)
    xs = x_sample.reshape(n_s * seq_s, d)
    rope_tabs = _rope_tables(seq_s)

    pad = (-(n_s + 1)) % 8
    cvec = jnp.concatenate([c, c_ctx[None, :], jnp.zeros((pad, d), F32)], axis=0)
    mod = _ada_modulation(cvec, ada_w, ada_b)
    mod = mod.reshape(DEPTH, mod.shape[1], 3, 1, d)

    new_kv = {0: ([], []), 1: ([], []), 2: ([], [])}
    for l in range(DEPTH):
        kind, j = l % N_MIXERS, l // N_MIXERS
        sh_s, sc_s, gt_s = (mod[l, :n_s, t] for t in range(3))
        sh_p, sc_p, gt_p = (mod[l, n_s:n_s + 1, t] for t in range(3))
        w_o = w_out[l].astype(BF16)
        if kind == 0:
            kvw = A_KV_HEADS * HEAD_DIM
            w_in, widths, tn = w_in_a[j].astype(BF16), (d, kvw, kvw, d), 512
        elif kind == 1:
            w_in, widths, tn = w_in_b[j].astype(BF16), (d, d, d, d), 512
        else:
            w_in, widths, tn = w_in_c[j].astype(BF16), (d, d, d, d), 512
        rope = rope_tabs if kind != 2 else None
        qp, kp, vp, gp = _in_projection(xp, sh_p, sc_p, ln_g[l], w_in, qn_g[l], kn_g[l], widths, None, F32, tn)
        qs, ks, vs, gs = _in_projection(xs, sh_s, sc_s, ln_g[l], w_in, qn_g[l], kn_g[l], widths, rope, BF16, tn)
        if kind == 0:
            op, k_new, v_new = _attn_a_ctx(qp, kp, vp, gp, sink_a[j], n_p, seq_p)
            os_ = _attn_a_lat(qs, ks, vs, cache_a_k, cache_a_v, j, gs, sink_a[j], n_s, seq_s)
            new_kv[0][0].append(k_new)
            new_kv[0][1].append(v_new)
        elif kind == 1:
            lambda_init = 0.8 - 0.6 * math.exp(-0.3 * l)
            kc = cache_b_k[:, j].reshape(n_s, n_ctx, d)
            vc = cache_b_v[:, j].reshape(n_s, n_ctx, d)
            op, k_new = _attn_b(qp, kp, vp, gp, lam_b[j], subln_b[j], lambda_init, n_p, seq_p)
            os_ = _attn_b(qs, ks, vs, gs, lam_b[j], subln_b[j], lambda_init, n_s, seq_s, kc, vc)
            new_kv[1][0].append(k_new)
            new_kv[1][1].append(vp.reshape(n_p, seq_p, B_HEADS, 2 * HEAD_DIM))
        else:
            kc = cache_c_k[:, j].reshape(n_s, n_ctx, d)
            vc = cache_c_v[:, j].reshape(n_s, n_ctx, d)
            op = _attn_c_ctx(qp, kp, vp, gp, n_p, seq_p)
            os_ = _attn_c_lat(qs, ks, vs, kc, vc, gs, rpb_c[j], n_s, seq_s)
            new_kv[2][0].append(kp.reshape(n_p, seq_p, C_HEADS, HEAD_DIM))
            new_kv[2][1].append(vp.reshape(n_p, seq_p, C_HEADS, HEAD_DIM))
        xp = _out_projection(op, w_o, xp, gt_p)
        xs = _out_projection(os_, w_o, xs, gt_s)

    outs = [xp.reshape(x_prompt.shape), xs.reshape(x_sample.shape)]
    for kind in range(3):
        outs.append(jnp.stack(new_kv[kind][0], axis=1))
        outs.append(jnp.stack(new_kv[kind][1], axis=1))
    return tuple(outs)
```

```python
import functools
import math

import numpy as np
import jax
import jax.numpy as jnp
from jax import lax
from jax.experimental import pallas as pl
from jax.experimental.pallas import tpu as pltpu

D_MODEL = 2048
DEPTH = 4
GRID_W = 64
HEAD_DIM = 128
ROPE_BASE = 10000.0
NEG_INF = -1e30
N_MIXERS = 3
A_HEADS = D_MODEL // HEAD_DIM
A_KV_HEADS = A_HEADS // 4
A_GROUP = A_HEADS // A_KV_HEADS
WINDOW = 128
B_HEADS = D_MODEL // (2 * HEAD_DIM)
C_HEADS = D_MODEL // HEAD_DIM
NA_KH = 8
NA_KW = 16
SCALE = HEAD_DIM ** -0.5
LOG2E = math.log2(math.e)
Q_SCALE = SCALE * LOG2E
EPS = 1e-6

BF16 = jnp.bfloat16
F32 = jnp.float32

VMEM_LIMIT_BYTES = 56 * 1024 * 1024
ROW_TILE = 1024
OUT_ROW_TILE = 512
OUT_SUB = 512
PROJ_SUB = 128
A_QBLOCK = 128
B_QBLOCK = 1024
B_KCHUNK = 512
B_UNROLL = 4
C_ROWS = 4
C_HEAD_GROUP = 4


def _nt_dot(a, b):
    return lax.dot_general(a, b, (((1,), (1,)), ((), ())), preferred_element_type=F32)


def _params(*semantics):
    return pltpu.CompilerParams(dimension_semantics=semantics, vmem_limit_bytes=VMEM_LIMIT_BYTES)


def _ada_kernel(c_ref, w_ref, b_ref, o_ref):
    c = c_ref[...]
    a = (c * jax.nn.sigmoid(c)).astype(BF16)
    o_ref[...] = jnp.dot(a, w_ref[...].astype(BF16), preferred_element_type=F32) + b_ref[...]


def _ada_modulation(cvec, ada_w, ada_b):
    rows = cvec.shape[0]
    tn = 768
    return pl.pallas_call(
        _ada_kernel,
        out_shape=jax.ShapeDtypeStruct((DEPTH, rows, 3 * D_MODEL), F32),
        grid=(DEPTH, 3 * D_MODEL // tn),
        in_specs=[
            pl.BlockSpec((rows, D_MODEL), lambda l, j: (0, 0)),
            pl.BlockSpec((None, D_MODEL, tn), lambda l, j: (l, 0, j)),
            pl.BlockSpec((None, 1, tn), lambda l, j: (l, 0, j)),
        ],
        out_specs=pl.BlockSpec((None, rows, tn), lambda l, j: (l, 0, j)),
        compiler_params=_params("arbitrary", "arbitrary"),
        name="ada_modulation",
    )(cvec, ada_w, ada_b.reshape(DEPTH, 1, 3 * D_MODEL))


def _inproj_kernel(*refs, n_q, rope, heads_per_tile):
    x_ref, sh_ref, sc_ref, lng_ref, w1_ref, w2_ref, qg_ref, kg_ref = refs[:8]
    if rope:
        cos_ref, sa_ref, sb_ref = refs[8:11]
    q_out, k_out, v_out, g_out, h_scr = refs[-5:]
    j = pl.program_id(1)

    def modulated_norm(rows):
        x = x_ref[rows, :]
        ms = jnp.mean(x * x, axis=-1, keepdims=True)
        gain = lng_ref[...] * (1.0 + sc_ref[...])
        return (x * lax.rsqrt(ms + EPS) * gain + sh_ref[...]).astype(BF16)

    def normed(acc, rows, gain_ref, scale):
        outs = []
        for t in range(heads_per_tile):
            y = acc[:, t * HEAD_DIM:(t + 1) * HEAD_DIM]
            ms = jnp.mean(y * y, axis=-1, keepdims=True)
            y = y * lax.rsqrt(ms + EPS) * (gain_ref[...] * scale)
            if rope:
                y = (y * cos_ref[rows, :] + pltpu.roll(y, 96, 1) * sa_ref[rows, :]
                     + pltpu.roll(y, 32, 1) * sb_ref[rows, :])
            outs.append(y)
        return jnp.concatenate(outs, axis=-1)

    def project(w_ref, out_ref, epilogue, fill_h=False):
        for r in range(h_scr.shape[0] // PROJ_SUB):
            rows = slice(r * PROJ_SUB, (r + 1) * PROJ_SUB)
            if fill_h:
                h_scr[rows, :] = modulated_norm(rows)
            acc = jnp.dot(h_scr[rows, :], w_ref[...], preferred_element_type=F32)
            out_ref[rows, :] = epilogue(acc, rows).astype(out_ref.dtype)

    def q_and_gate(fill_h):
        project(w1_ref, q_out, lambda acc, rows: normed(acc, rows, qg_ref, Q_SCALE), fill_h)
        project(w2_ref, g_out, lambda acc, rows: acc * jax.nn.sigmoid(acc))

    @pl.when(j == 0)
    def _():
        q_and_gate(True)

    @pl.when((j > 0) & (j < n_q))
    def _():
        q_and_gate(False)

    @pl.when(j >= n_q)
    def _():
        project(w1_ref, k_out, lambda acc, rows: normed(acc, rows, kg_ref, 1.0))
        project(w2_ref, v_out, lambda acc, rows: acc)


def _in_projection(x, shift, scale, ln_g, w, qg, kg, widths, rope_tabs, kv_dtype, tn):
    t_rows = x.shape[0]
    qw, kw, vw, gw = widths
    assert qw == gw and kw == vw
    n_q, n_k = qw // tn, kw // tn
    tm = min(ROW_TILE, t_rows)
    rows_per_mod = t_rows // shift.shape[0]
    assert t_rows % tm == 0 and rows_per_mod % tm == 0
    rope = rope_tabs is not None
    g_col0, v_col0 = (qw + kw + vw) // tn, (qw + kw) // tn

    def mod_map(i, j):
        return ((i * tm) // rows_per_mod, 0, 0)

    in_specs = [
        pl.BlockSpec((tm, D_MODEL), lambda i, j: (i, 0)),
        pl.BlockSpec((None, 1, D_MODEL), mod_map),
        pl.BlockSpec((None, 1, D_MODEL), mod_map),
        pl.BlockSpec((1, D_MODEL), lambda i, j: (0, 0)),
        pl.BlockSpec((D_MODEL, tn), lambda i, j: (0, j)),
        pl.BlockSpec((D_MODEL, tn), lambda i, j: (0, jnp.where(j < n_q, g_col0 + j, v_col0 + j - n_q))),
        pl.BlockSpec((1, HEAD_DIM), lambda i, j: (0, 0)),
        pl.BlockSpec((1, HEAD_DIM), lambda i, j: (0, 0)),
    ]
    args = [x, shift, scale, ln_g.reshape(1, D_MODEL), w, w, qg.reshape(1, HEAD_DIM), kg.reshape(1, HEAD_DIM)]
    if rope:
        n_pos_tiles = rope_tabs[0].shape[0] // tm
        for tab in rope_tabs:
            in_specs.append(pl.BlockSpec((tm, HEAD_DIM), lambda i, j: (i % n_pos_tiles, 0)))
            args.append(tab)
    out_specs = [
        pl.BlockSpec((tm, tn), lambda i, j: (i, jnp.minimum(j, n_q - 1))),
        pl.BlockSpec((tm, tn), lambda i, j: (i, jnp.maximum(j - n_q, 0))),
        pl.BlockSpec((tm, tn), lambda i, j: (i, jnp.maximum(j - n_q, 0))),
        pl.BlockSpec((tm, tn), lambda i, j: (i, jnp.minimum(j, n_q - 1))),
    ]
    out_shape = [
        jax.ShapeDtypeStruct((t_rows, qw), BF16),
        jax.ShapeDtypeStruct((t_rows, kw), kv_dtype),
        jax.ShapeDtypeStruct((t_rows, vw), kv_dtype),
        jax.ShapeDtypeStruct((t_rows, gw), BF16),
    ]
    kern = functools.partial(_inproj_kernel, n_q=n_q, rope=rope, heads_per_tile=tn // HEAD_DIM)
    return pl.pallas_call(
        kern,
        out_shape=out_shape,
        grid=(t_rows // tm, n_q + n_k),
        in_specs=in_specs,
        out_specs=out_specs,
        scratch_shapes=[pltpu.VMEM((tm, D_MODEL), BF16)],
        compiler_params=_params("arbitrary", "arbitrary"),
        name="in_projection",
    )(*args)


def _outproj_kernel(o_ref, w_ref, x_ref, gt_ref, y_ref):
    for c in range(D_MODEL // OUT_SUB):
        cols = slice(c * OUT_SUB, (c + 1) * OUT_SUB)
        y = jnp.dot(o_ref[...], w_ref[:, cols], preferred_element_type=F32)
        y_ref[:, cols] = x_ref[:, cols] + gt_ref[:, cols] * y


def _out_projection(o, w, x, gate):
    t_rows = x.shape[0]
    tm = min(OUT_ROW_TILE, t_rows)
    rows_per_mod = t_rows // gate.shape[0]
    assert t_rows % tm == 0 and rows_per_mod % tm == 0
    row = lambda i: (i, 0)
    return pl.pallas_call(
        _outproj_kernel,
        out_shape=jax.ShapeDtypeStruct((t_rows, D_MODEL), F32),
        grid=(t_rows // tm,),
        in_specs=[
            pl.BlockSpec((tm, D_MODEL), row),
            pl.BlockSpec((D_MODEL, D_MODEL), lambda i: (0, 0)),
            pl.BlockSpec((tm, D_MODEL), row),
            pl.BlockSpec((None, 1, D_MODEL), lambda i: ((i * tm) // rows_per_mod, 0, 0)),
        ],
        out_specs=pl.BlockSpec((tm, D_MODEL), row),
        compiler_params=_params("arbitrary"),
        name="out_projection",
    )(o, w, x, gate)


def _lane_fold(x, op):
    r = x[:, 0:HEAD_DIM]
    for t in range(1, x.shape[1] // HEAD_DIM):
        r = op(r, x[:, t * HEAD_DIM:(t + 1) * HEAD_DIM])
    return r


def _with_ones(v):
    return jnp.concatenate([v, jnp.ones(v.shape, v.dtype)], axis=1)


def _softmax_pv(scores, values):
    m128 = _lane_fold(scores[0], jnp.maximum)
    for s in scores[1:]:
        m128 = jnp.maximum(m128, _lane_fold(s, jnp.maximum))
    m = jnp.broadcast_to(jnp.max(m128, axis=-1, keepdims=True), m128.shape)
    acc = None
    for s, v in zip(scores, values):
        p = jnp.exp2(s - jnp.concatenate([m] * (s.shape[1] // HEAD_DIM), axis=1)).astype(BF16)
        pv = jnp.dot(p, _with_ones(v), preferred_element_type=F32)
        acc = pv if acc is None else acc + pv
    return acc[:, :HEAD_DIM] * (1.0 / acc[:, HEAD_DIM:])


def _attn_a_kernel(*refs, latent, n_qblocks):
    if latent:
        (sink_ref, q_ref, kp_ref, ko_ref, kn_ref, vp_ref, vo_ref, vn_ref,
         kc_ref, vc_ref, sg_ref, o_ref, s_scr, m_scr, acc_scr, kc_scr, vc_scr) = refs
    else:
        sink_ref, q_ref, k_ref, v_ref, sg_ref = refs[:5]
        o_ref, kcache_ref, vcache_ref, s_scr, m_scr, acc_scr = refs[-6:]
        for kv in range(A_KV_HEADS):
            kcache_ref[:, kv, :] = k_ref[:, kv * HEAD_DIM:(kv + 1) * HEAD_DIM]
            vcache_ref[:, kv, :] = v_ref[:, kv * HEAD_DIM:(kv + 1) * HEAD_DIM]
    m_rows = q_ref.shape[0]
    n_keys = s_scr.shape[-1]
    n_loc = 3 * A_QBLOCK
    kv_cols = [slice(kv * HEAD_DIM, (kv + 1) * HEAD_DIM) for kv in range(A_KV_HEADS)]
    kv_heads = [[kv * A_GROUP + g for g in range(A_GROUP)] for kv in range(A_KV_HEADS)]

    if latent:
        blk = pl.program_id(1)
        qpos = lax.broadcasted_iota(jnp.int32, (A_QBLOCK, n_loc), 0)
        col = lax.broadcasted_iota(jnp.int32, (A_QBLOCK, n_loc), 1)
        rel = col - A_QBLOCK - qpos
        kpos = (blk - 1) * A_QBLOCK + col
        valid = (jnp.abs(rel) <= WINDOW) & (kpos >= 0) & (kpos < n_qblocks * A_QBLOCK)
        window_bias = jnp.concatenate([jnp.where(valid, 0.0, NEG_INF)] * A_GROUP, axis=0)

        @pl.when(blk == 0)
        def _():
            for kv in range(A_KV_HEADS):
                kc_scr[kv] = kc_ref[:, kv, :].astype(BF16)
                vc_scr[kv] = vc_ref[:, kv, :].astype(BF16)

    for kv, cols in enumerate(kv_cols):
        q = jnp.concatenate([q_ref[:, h * HEAD_DIM:(h + 1) * HEAD_DIM] for h in kv_heads[kv]], axis=0)
        if latent:
            k_loc = jnp.concatenate([kp_ref[:, cols], ko_ref[:, cols], kn_ref[:, cols]], axis=0)
            s_loc = _nt_dot(q, k_loc) + window_bias
            s_ctx = _nt_dot(q, kc_scr[kv])
            s_scr[kv, :, 0:n_loc] = s_loc
            s_scr[kv, :, n_loc:n_keys] = s_ctx
            m_scr[kv] = jnp.maximum(_lane_fold(s_loc, jnp.maximum), _lane_fold(s_ctx, jnp.maximum))
        else:
            s = _nt_dot(q, k_ref[:, cols].astype(BF16))
            s_scr[kv] = s
            m_scr[kv] = _lane_fold(s, jnp.maximum)

    def sink_lanes(kv):
        return jnp.concatenate([jnp.full((m_rows, HEAD_DIM), sink_ref[h] * LOG2E, F32) for h in kv_heads[kv]], axis=0)

    for kv in range(A_KV_HEADS):
        m = jnp.max(m_scr[kv], axis=-1, keepdims=True)
        m_scr[kv] = jnp.maximum(jnp.broadcast_to(m, m_scr.shape[1:]), sink_lanes(kv))

    for kv, cols in enumerate(kv_cols):
        p = jnp.exp2(s_scr[kv] - jnp.concatenate([m_scr[kv]] * (n_keys // HEAD_DIM), axis=1)).astype(BF16)
        if latent:
            v_loc = jnp.concatenate([vp_ref[:, cols], vo_ref[:, cols], vn_ref[:, cols]], axis=0)
            acc_scr[kv] = (jnp.dot(p[:, 0:n_loc], _with_ones(v_loc), preferred_element_type=F32)
                           + jnp.dot(p[:, n_loc:n_keys], _with_ones(vc_scr[kv]), preferred_element_type=F32))
        else:
            acc_scr[kv] = jnp.dot(p, _with_ones(v_ref[:, cols].astype(BF16)), preferred_element_type=F32)

    for kv in range(A_KV_HEADS):
        acc = acc_scr[kv]
        l = acc[:, HEAD_DIM:] + jnp.exp2(sink_lanes(kv) - m_scr[kv])
        o = acc[:, :HEAD_DIM] * (1.0 / l)
        for g, h in enumerate(kv_heads[kv]):
            hc = slice(h * HEAD_DIM, (h + 1) * HEAD_DIM)
            o_ref[:, hc] = (o[g * m_rows:(g + 1) * m_rows] * sg_ref[:, hc].astype(F32)).astype(o_ref.dtype)


def _attn_a_scratch(m_rows, n_keys):
    stacked = A_GROUP * m_rows
    return [pltpu.VMEM((A_KV_HEADS, stacked, n_keys), F32), pltpu.VMEM((A_KV_HEADS, stacked, HEAD_DIM), F32),
            pltpu.VMEM((A_KV_HEADS, stacked, 2 * HEAD_DIM), F32)]


def _attn_a_ctx(q, k, v, sg, sink, n_batch, seq, slot, n_slots, caches=None):
    kvw = A_KV_HEADS * HEAD_DIM
    row = lambda b: (b, 0)
    cache_shape = jax.ShapeDtypeStruct((n_batch, n_slots, seq, A_KV_HEADS, HEAD_DIM), F32)
    cache_spec = pl.BlockSpec((None, None, seq, A_KV_HEADS, HEAD_DIM), lambda b: (b, slot, 0, 0, 0))
    in_specs = [
        pl.BlockSpec(memory_space=pltpu.SMEM),
        pl.BlockSpec((seq, D_MODEL), row),
        pl.BlockSpec((seq, kvw), row),
        pl.BlockSpec((seq, kvw), row),
        pl.BlockSpec((seq, D_MODEL), row),
    ]
    args = [sink, q, k, v, sg]
    aliases = {}
    if caches is not None:
        aliases = {len(args): 1, len(args) + 1: 2}
        in_specs += [pl.BlockSpec(memory_space=pl.ANY)] * 2
        args += list(caches)
    return pl.pallas_call(
        functools.partial(_attn_a_kernel, latent=False, n_qblocks=1),
        out_shape=[jax.ShapeDtypeStruct(q.shape, BF16), cache_shape, cache_shape],
        grid=(n_batch,),
        in_specs=in_specs,
        out_specs=[pl.BlockSpec((seq, D_MODEL), row), cache_spec, cache_spec],
        scratch_shapes=_attn_a_scratch(seq, seq),
        input_output_aliases=aliases,
        compiler_params=_params("arbitrary"),
        name="attn_a_ctx",
    )(*args)


def _attn_a_lat(q, k, v, cache_k, cache_v, cache_slot, sg, sink, n_batch, seq):
    kvw = A_KV_HEADS * HEAD_DIM
    nb = seq // A_QBLOCK
    own = lambda b, i: (b * nb + i, 0)
    prev = lambda b, i: (b * nb + jnp.maximum(i - 1, 0), 0)
    nxt = lambda b, i: (b * nb + jnp.minimum(i + 1, nb - 1), 0)
    ctx = lambda b, i: (b, cache_slot, 0, 0, 0)
    n_ctx = cache_k.shape[2]
    ctx_spec = pl.BlockSpec((None, None, n_ctx, A_KV_HEADS, HEAD_DIM), ctx)
    ctx_scratch = [pltpu.VMEM((A_KV_HEADS, n_ctx, HEAD_DIM), BF16)] * 2
    return pl.pallas_call(
        functools.partial(_attn_a_kernel, latent=True, n_qblocks=nb),
        out_shape=jax.ShapeDtypeStruct(q.shape, BF16),
        grid=(n_batch, nb),
        in_specs=[
            pl.BlockSpec(memory_space=pltpu.SMEM),
            pl.BlockSpec((A_QBLOCK, D_MODEL), own),
            pl.BlockSpec((A_QBLOCK, kvw), prev),
            pl.BlockSpec((A_QBLOCK, kvw), own),
            pl.BlockSpec((A_QBLOCK, kvw), nxt),
            pl.BlockSpec((A_QBLOCK, kvw), prev),
            pl.BlockSpec((A_QBLOCK, kvw), own),
            pl.BlockSpec((A_QBLOCK, kvw), nxt),
            ctx_spec,
            ctx_spec,
            pl.BlockSpec((A_QBLOCK, D_MODEL), own),
        ],
        out_specs=pl.BlockSpec((A_QBLOCK, D_MODEL), own),
        scratch_shapes=_attn_a_scratch(A_QBLOCK, 3 * A_QBLOCK + n_ctx) + ctx_scratch,
        compiler_params=_params("arbitrary", "arbitrary"),
        name="attn_a_lat",
    )(sink, q, k, k, k, v, v, v, cache_k, cache_v, sg)


def _attn_b_kernel(*refs, has_ctx, lambda_init, k_chunk):
    if has_ctx:
        (lam_ref, sub_ref, q_ref, k_ref, v_ref, kc_ref, vc_ref, sg_ref, o_ref,
         s_scr, sc_scr, m_scr, l_scr, acc_scr) = refs
    else:
        lam_ref, sub_ref, q_ref, k_ref, v_ref, sg_ref, o_ref, kcache_ref, s_scr, m_scr, l_scr, acc_scr = refs
        kc_ref = vc_ref = sc_scr = None
    dv = 2 * HEAD_DIM
    for head in range(q_ref.shape[1] // dv):
        cols = slice(head * dv, (head + 1) * dv)
        if not has_ctx:
            for half in range(2):
                kcache_ref[:, head, half, :] = k_ref[:, head * dv + half * HEAD_DIM:head * dv + (half + 1) * HEAD_DIM]
        view = lambda ref: None if ref is None else ref.at[:, cols]
        _attn_b_head(lam_ref, sub_ref, view(q_ref), view(k_ref), view(v_ref), view(kc_ref), view(vc_ref),
                     view(sg_ref), view(o_ref), s_scr, sc_scr, m_scr, l_scr, acc_scr,
                     lambda_init=lambda_init, k_chunk=k_chunk)


def _attn_b_head(lam_ref, sub_ref, q_ref, k_ref, v_ref, kc_ref, vc_ref, sg_ref, o_ref,
                 s_scr, sc_scr, m_scr, l_scr, acc_scr, *, lambda_init, k_chunk):
    has_ctx = kc_ref is not None
    n_chunks = k_ref.shape[0] // k_chunk
    halves = [slice(h * HEAD_DIM, (h + 1) * HEAD_DIM) for h in range(2)]

    def fold_max(h, s, first):
        m = _lane_fold(s, jnp.maximum)
        m_scr[h] = m if first else jnp.maximum(m_scr[h], m)

    def scores(c, first=False):
        rows = pl.ds(pl.multiple_of(c * k_chunk, k_chunk), k_chunk)
        k = k_ref[rows, :].astype(BF16)
        for h, hc in enumerate(halves):
            s = _nt_dot(q_ref[:, hc], k[:, hc])
            s_scr[h, c] = s
            fold_max(h, s, first)

    if has_ctx:
        kc = kc_ref[...].astype(BF16)
        for h, hc in enumerate(halves):
            s = _nt_dot(q_ref[:, hc], kc[:, hc])
            sc_scr[h] = s
            fold_max(h, s, True)
        lax.fori_loop(0, n_chunks, lambda c, carry: (scores(c), carry)[1], 0, unroll=B_UNROLL)
    else:
        assert n_chunks == 1
        scores(0, first=True)

    for h in range(2):
        m_scr[h] = jnp.broadcast_to(jnp.max(m_scr[h], axis=-1, keepdims=True), m_scr.shape[1:])

    def accumulate(h, s, v, first):
        m = m_scr[h]
        p = jnp.exp2(s - jnp.concatenate([m] * (s.shape[1] // HEAD_DIM), axis=1))
        l = _lane_fold(p, jnp.add)
        pv = jnp.dot(p.astype(BF16), v, preferred_element_type=F32)
        l_scr[h] = l if first else l_scr[h] + l
        acc_scr[h] = pv if first else acc_scr[h] + pv

    def weighted(c, first=False):
        rows = pl.ds(pl.multiple_of(c * k_chunk, k_chunk), k_chunk)
        v = v_ref[rows, :].astype(BF16)
        for h in range(2):
            accumulate(h, s_scr[h, c], v, first)

    if has_ctx:
        vc = vc_ref[...].astype(BF16)
        for h in range(2):
            accumulate(h, sc_scr[h], vc, True)
        lax.fori_loop(0, n_chunks, lambda c, carry: (weighted(c), carry)[1], 0, unroll=B_UNROLL)
    else:
        weighted(0, first=True)

    lam = lam_ref[...]
    lam_full = (jnp.exp(jnp.sum(lam[0:1] * lam[1:2], axis=-1, keepdims=True))
                - jnp.exp(jnp.sum(lam[2:3] * lam[3:4], axis=-1, keepdims=True)) + lambda_init)
    r0 = 1.0 / jnp.sum(l_scr[0], axis=-1, keepdims=True)
    r1 = lam_full / jnp.sum(l_scr[1], axis=-1, keepdims=True)
    o = acc_scr[0] * r0 - acc_scr[1] * r1
    ms = jnp.mean(o * o, axis=-1, keepdims=True)
    o = o * lax.rsqrt(ms + EPS) * (sub_ref[...] * (1.0 - lambda_init))
    o_ref[...] = (o * sg_ref[...].astype(F32)).astype(o_ref.dtype)


def _attn_b(q, k, v, sg, lam, subln, lambda_init, n_batch, seq, k_ctx=None, v_ctx=None):
    dv = 2 * HEAD_DIM
    tq = min(B_QBLOCK, seq)
    nq = seq // tq
    k_chunk = min(B_KCHUNK, seq)
    has_ctx = k_ctx is not None
    heads_per_step = 1 if has_ctx else B_HEADS
    gw = heads_per_step * dv
    qmap = lambda b, h, i: (b * nq + i, h)
    kvmap = lambda b, h, i: (b, h)
    in_specs = [
        pl.BlockSpec((4, HEAD_DIM), lambda b, h, i: (0, 0)),
        pl.BlockSpec((1, dv), lambda b, h, i: (0, 0)),
        pl.BlockSpec((tq, gw), qmap),
        pl.BlockSpec((seq, gw), kvmap),
        pl.BlockSpec((seq, gw), kvmap),
    ]
    args = [lam, subln.reshape(1, dv), q, k, v]
    scratch = [pltpu.VMEM((2, seq // k_chunk, tq, k_chunk), F32)]
    if has_ctx:
        n_ctx = k_ctx.shape[1]
        in_specs += [pl.BlockSpec((None, n_ctx, gw), lambda b, h, i: (b, 0, h))] * 2
        args += [k_ctx, v_ctx]
        scratch.append(pltpu.VMEM((2, tq, n_ctx), F32))
    in_specs.append(pl.BlockSpec((tq, gw), qmap))
    args.append(sg)
    scratch += [pltpu.VMEM((2, tq, HEAD_DIM), F32), pltpu.VMEM((2, tq, HEAD_DIM), F32),
                pltpu.VMEM((2, tq, dv), F32)]
    out_shape = jax.ShapeDtypeStruct(q.shape, BF16)
    out_specs = pl.BlockSpec((tq, gw), qmap)
    if not has_ctx:
        assert nq == 1 and k.dtype == F32
        out_shape = [out_shape, jax.ShapeDtypeStruct((n_batch, seq, B_HEADS, 2, HEAD_DIM), F32)]
        out_specs = [out_specs, pl.BlockSpec((None, seq, B_HEADS, 2, HEAD_DIM), lambda b, h, i: (b, 0, 0, 0, 0))]
    return pl.pallas_call(
        functools.partial(_attn_b_kernel, has_ctx=has_ctx, lambda_init=lambda_init, k_chunk=k_chunk),
        out_shape=out_shape,
        grid=(n_batch, B_HEADS // heads_per_step, nq),
        in_specs=in_specs,
        out_specs=out_specs,
        scratch_shapes=scratch,
        compiler_params=_params("arbitrary", "arbitrary", "arbitrary"),
        name="attn_b_lat" if has_ctx else "attn_b_ctx",
    )(*args)


def _attn_c_ctx_kernel(q_ref, k_ref, v_ref, sg_ref, o_ref):
    for h in range(C_HEADS):
        hc = slice(h * HEAD_DIM, (h + 1) * HEAD_DIM)
        s = _nt_dot(q_ref[:, hc], k_ref[:, hc].astype(BF16))
        o = _softmax_pv([s], [v_ref[:, hc].astype(BF16)])
        o_ref[:, hc] = (o * sg_ref[:, hc].astype(F32)).astype(o_ref.dtype)


def _attn_c_ctx(q, k, v, sg, n_batch, seq):
    spec = pl.BlockSpec((seq, D_MODEL), lambda b: (b, 0))
    return pl.pallas_call(
        _attn_c_ctx_kernel,
        out_shape=jax.ShapeDtypeStruct(q.shape, BF16),
        grid=(n_batch,),
        in_specs=[spec, spec, spec, spec],
        out_specs=spec,
        compiler_params=_params("arbitrary"),
        name="attn_c_ctx",
    )(q, k, v, sg)


def _na_geometry(rows):
    kh = min(NA_KH, rows)
    key_rows = C_ROWS + kh - 1
    key_rows = min(key_rows + key_rows % 2, rows)
    n_blocks = rows // C_ROWS
    cols = np.arange(GRID_W)
    col_start = np.clip(cols - NA_KW // 2, 0, GRID_W - NA_KW)
    col_ok = (cols[None, :] >= col_start[:, None]) & (cols[None, :] < col_start[:, None] + NA_KW)
    col_delta = cols[None, :] - cols[:, None] + NA_KW - 1
    col_onehot = (col_delta[None] == np.arange(2 * NA_KW - 1)[:, None, None]) & col_ok[None]
    starts, class_ids, classes, keys = [], [], [], {}
    for blk in range(n_blocks):
        r = blk * C_ROWS + np.arange(C_ROWS)
        rs = np.clip(r - kh // 2, 0, rows - kh)
        start = int(np.clip(rs[0], 0, rows - key_rows))
        key = (tuple(rs - r), start - blk * C_ROWS)
        if key not in keys:
            keys[key] = len(classes)
            kr = start + np.arange(key_rows)
            row_ok = (kr[None, :] >= rs[:, None]) & (kr[None, :] < rs[:, None] + kh)
            row_idx = kr[None, :] - r[:, None] + NA_KH - 1
            classes.append((row_ok, row_idx))
        starts.append(start)
        class_ids.append(keys[key])
    geometry = (col_ok, col_onehot.astype(np.float32), classes)
    return key_rows, np.asarray(starts, np.int32), np.asarray(class_ids, np.int32), geometry


def _na_bias_tables(rpb, geometry):
    col_ok, col_onehot, classes = geometry
    n_heads = rpb.shape[0]
    toep = jnp.einsum("hdx,xck->hdck", rpb * LOG2E, jnp.asarray(col_onehot),
                      precision=lax.Precision.HIGHEST)
    toep = jnp.where(jnp.asarray(col_ok)[None, None], toep, NEG_INF)
    n_off = toep.shape[1]
    toep = jnp.concatenate([toep, jnp.full((n_heads, 1, GRID_W, GRID_W), NEG_INF, F32)], axis=1)
    select = np.stack([np.where(row_ok, row_idx, n_off) for row_ok, row_idx in classes])
    n_cls, n_r, n_kr = select.shape
    dnums = lax.GatherDimensionNumbers(offset_dims=(1, 3, 5), collapsed_slice_dims=(1,), start_index_map=(1,))
    tabs = lax.gather(toep, jnp.asarray(select[..., None], jnp.int32), dnums,
                      slice_sizes=(n_heads, 1, GRID_W, GRID_W))
    return tabs.reshape(n_cls, n_heads, n_r * GRID_W, n_kr * GRID_W)


def _attn_c_lat_kernel(start_ref, cls_ref, q_ref, k_ref, v_ref, kc_ref, vc_ref, bias_ref, sg_ref, o_ref,
                       s_scr, m_scr, acc_scr, *, key_rows):
    blk = pl.program_id(2)
    n_loc = key_rows * GRID_W
    n_keys = s_scr.shape[-1]
    rows = pl.ds(pl.multiple_of(start_ref[blk] * GRID_W, GRID_W), n_loc)
    head_cols = [slice(h * HEAD_DIM, (h + 1) * HEAD_DIM) for h in range(C_HEAD_GROUP)]

    for h, hc in enumerate(head_cols):
        q = q_ref[:, hc]
        s_loc = _nt_dot(q, k_ref[rows, hc]) + bias_ref[h]
        s_ctx = _nt_dot(q, kc_ref[:, hc].astype(BF16))
        s_scr[h, :, 0:n_loc] = s_loc
        s_scr[h, :, n_loc:n_keys] = s_ctx
        m_scr[h] = jnp.maximum(_lane_fold(s_loc, jnp.maximum), _lane_fold(s_ctx, jnp.maximum))

    for h in range(C_HEAD_GROUP):
        m_scr[h] = jnp.broadcast_to(jnp.max(m_scr[h], axis=-1, keepdims=True), m_scr.shape[1:])

    for h, hc in enumerate(head_cols):
        p = jnp.exp2(s_scr[h] - jnp.concatenate([m_scr[h]] * (n_keys // HEAD_DIM), axis=1)).astype(BF16)
        acc_scr[h] = (jnp.dot(p[:, 0:n_loc], _with_ones(v_ref[rows, hc]), preferred_element_type=F32)
                      + jnp.dot(p[:, n_loc:n_keys], _with_ones(vc_ref[:, hc].astype(BF16)),
                                preferred_element_type=F32))

    for h, hc in enumerate(head_cols):
        acc = acc_scr[h]
        o = acc[:, :HEAD_DIM] * (1.0 / acc[:, HEAD_DIM:])
        o_ref[:, hc] = (o * sg_ref[:, hc].astype(F32)).astype(o_ref.dtype)


def _attn_c_lat(q, k, v, k_ctx, v_ctx, sg, rpb, n_batch, seq):
    rows = seq // GRID_W
    key_rows, starts, class_ids, geometry = _na_geometry(rows)
    bias = _na_bias_tables(rpb, geometry)
    nblk = rows // C_ROWS
    tq = C_ROWS * GRID_W
    gw = C_HEAD_GROUP * HEAD_DIM
    n_ctx = k_ctx.shape[1]
    qmap = lambda b, g, i, st, cl: (b * nblk + i, g)
    kvmap = lambda b, g, i, st, cl: (b, g)
    ctxmap = lambda b, g, i, st, cl: (b, 0, g)
    grid_spec = pltpu.PrefetchScalarGridSpec(
        num_scalar_prefetch=2,
        grid=(n_batch, C_HEADS // C_HEAD_GROUP, nblk),
        in_specs=[
            pl.BlockSpec((tq, gw), qmap),
            pl.BlockSpec((seq, gw), kvmap),
            pl.BlockSpec((seq, gw), kvmap),
            pl.BlockSpec((None, n_ctx, gw), ctxmap),
            pl.BlockSpec((None, n_ctx, gw), ctxmap),
            pl.BlockSpec((None, C_HEAD_GROUP, tq, key_rows * GRID_W),
                         lambda b, g, i, st, cl: (cl[i], g, 0, 0)),
            pl.BlockSpec((tq, gw), qmap),
        ],
        out_specs=pl.BlockSpec((tq, gw), qmap),
        scratch_shapes=[pltpu.VMEM((C_HEAD_GROUP, tq, key_rows * GRID_W + n_ctx), F32),
                        pltpu.VMEM((C_HEAD_GROUP, tq, HEAD_DIM), F32),
                        pltpu.VMEM((C_HEAD_GROUP, tq, 2 * HEAD_DIM), F32)],
    )
    return pl.pallas_call(
        functools.partial(_attn_c_lat_kernel, key_rows=key_rows),
        out_shape=jax.ShapeDtypeStruct(q.shape, BF16),
        grid_spec=grid_spec,
        compiler_params=_params("arbitrary", "arbitrary", "arbitrary"),
        name="attn_c_lat",
    )(jnp.asarray(starts), jnp.asarray(class_ids), q, k, v, k_ctx, v_ctx, bias, sg)


def _rope_tables(n_tokens):
    d4 = HEAD_DIM // 4
    t = jnp.arange(n_tokens)
    pos = jnp.stack([t // GRID_W, t % GRID_W], axis=-1).astype(F32)
    inv_freq = ROPE_BASE ** (-jnp.arange(d4, dtype=F32) / d4)
    ang = pos[:, :, None] * inv_freq
    cos, sin = jnp.cos(ang), jnp.sin(ang)
    zero = jnp.zeros_like(sin[:, 0])
    cos_full = jnp.concatenate([cos[:, 0], cos[:, 0], cos[:, 1], cos[:, 1]], axis=-1)
    sin_up = jnp.concatenate([-sin[:, 0], zero, -sin[:, 1], zero], axis=-1)
    sin_dn = jnp.concatenate([zero, sin[:, 0], zero, sin[:, 1]], axis=-1)
    return cos_full, sin_up, sin_dn


def kernel(x_prompt, x_sample, cache_a_k, cache_a_v, cache_b_k, cache_b_v, cache_c_k, cache_c_v, c, c_ctx,
           ln_g, ada_w, ada_b, w_out, qn_g, kn_g, w_in_a, sink_a, w_in_b, lam_b, subln_b, w_in_c, rpb_c):
    n_p, seq_p, d = x_prompt.shape
    n_s, seq_s, _ = x_sample.shape
    n_ctx = cache_a_k.shape[2]
    xp = x_prompt.reshape(n_p * seq_p, d)
    xs = x_sample.reshape(n_s * seq_s, d)
    rope_tabs = _rope_tables(seq_s)

    pad = (-(n_s + 1)) % 8
    cvec = jnp.concatenate([c, c_ctx[None, :], jnp.zeros((pad, d), F32)], axis=0)
    mod = _ada_modulation(cvec, ada_w, ada_b)
    mod = mod.reshape(DEPTH, mod.shape[1], 3, 1, d)

    new_kv = {1: ([], []), 2: ([], [])}
    a_caches = None
    n_a_layers = len(range(0, DEPTH, N_MIXERS))
    for l in range(DEPTH):
        kind, j = l % N_MIXERS, l // N_MIXERS
        sh_s, sc_s, gt_s = (mod[l, :n_s, t] for t in range(3))
        sh_p, sc_p, gt_p = (mod[l, n_s:n_s + 1, t] for t in range(3))
        w_o = w_out[l].astype(BF16)
        if kind == 0:
            kvw = A_KV_HEADS * HEAD_DIM
            w_in, widths, tn = w_in_a[j].astype(BF16), (d, kvw, kvw, d), 512
        elif kind == 1:
            w_in, widths, tn = w_in_b[j].astype(BF16), (d, d, d, d), 512
        else:
            w_in, widths, tn = w_in_c[j].astype(BF16), (d, d, d, d), 512
        rope = rope_tabs if kind != 2 else None
        qp, kp, vp, gp = _in_projection(xp, sh_p, sc_p, ln_g[l], w_in, qn_g[l], kn_g[l], widths, None, F32, tn)
        qs, ks, vs, gs = _in_projection(xs, sh_s, sc_s, ln_g[l], w_in, qn_g[l], kn_g[l], widths, rope, BF16, tn)
        if kind == 0:
            op, *a_caches = _attn_a_ctx(qp, kp, vp, gp, sink_a[j], n_p, seq_p, j, n_a_layers, a_caches)
            os_ = _attn_a_lat(qs, ks, vs, cache_a_k, cache_a_v, j, gs, sink_a[j], n_s, seq_s)
        elif kind == 1:
            lambda_init = 0.8 - 0.6 * math.exp(-0.3 * l)
            kc = cache_b_k[:, j].reshape(n_s, n_ctx, d)
            vc = cache_b_v[:, j].reshape(n_s, n_ctx, d)
            op, k_new = _attn_b(qp, kp, vp, gp, lam_b[j], subln_b[j], lambda_init, n_p, seq_p)
            os_ = _attn_b(qs, ks, vs, gs, lam_b[j], subln_b[j], lambda_init, n_s, seq_s, kc, vc)
            new_kv[1][0].append(k_new)
            new_kv[1][1].append(vp.reshape(n_p, seq_p, B_HEADS, 2 * HEAD_DIM))
        else:
            kc = cache_c_k[:, j].reshape(n_s, n_ctx, d)
            vc = cache_c_v[:, j].reshape(n_s, n_ctx, d)
            op = _attn_c_ctx(qp, kp, vp, gp, n_p, seq_p)
            os_ = _attn_c_lat(qs, ks, vs, kc, vc, gs, rpb_c[j], n_s, seq_s)
            new_kv[2][0].append(kp.reshape(n_p, seq_p, C_HEADS, HEAD_DIM))
            new_kv[2][1].append(vp.reshape(n_p, seq_p, C_HEADS, HEAD_DIM))
        xp = _out_projection(op, w_o, xp, gt_p)
        xs = _out_projection(os_, w_o, xs, gt_s)

    outs = [xp.reshape(x_prompt.shape), xs.reshape(x_sample.shape)] + list(a_caches)
    for kind in (1, 2):
        outs.append(jnp.stack(new_kv[kind][0], axis=1))
        outs.append(jnp.stack(new_kv[kind][1], axis=1))
    return tuple(outs)
```

```python
import functools
import math

import numpy as np
import jax
import jax.numpy as jnp
from jax import lax
from jax.experimental import pallas as pl
from jax.experimental.pallas import tpu as pltpu

D_MODEL = 2048
DEPTH = 4
GRID_W = 64
HEAD_DIM = 128
ROPE_BASE = 10000.0
NEG_INF = -1e30
N_MIXERS = 3
A_HEADS = D_MODEL // HEAD_DIM
A_KV_HEADS = A_HEADS // 4
A_GROUP = A_HEADS // A_KV_HEADS
WINDOW = 128
B_HEADS = D_MODEL // (2 * HEAD_DIM)
C_HEADS = D_MODEL // HEAD_DIM
NA_KH = 8
NA_KW = 16
SCALE = HEAD_DIM ** -0.5
LOG2E = math.log2(math.e)
Q_SCALE = SCALE * LOG2E
EPS = 1e-6

BF16 = jnp.bfloat16
F32 = jnp.float32

VMEM_LIMIT_BYTES = 56 * 1024 * 1024
ROW_TILE = 1024
OUT_ROW_TILE = 512
OUT_SUB = 512
PROJ_SUB = 128
A_QBLOCK = 128
A_SUBBLOCKS = 4
B_QBLOCK = 1024
B_KCHUNK = 512
B_UNROLL = 4
C_ROWS = 4
C_HEAD_GROUP = 4


def _nt_dot(a, b):
    return lax.dot_general(a, b, (((1,), (1,)), ((), ())), preferred_element_type=F32)


def _params(*semantics):
    return pltpu.CompilerParams(dimension_semantics=semantics, vmem_limit_bytes=VMEM_LIMIT_BYTES)


def _ada_kernel(c_ref, w_ref, b_ref, o_ref):
    c = c_ref[...]
    a = (c * jax.nn.sigmoid(c)).astype(BF16)
    o_ref[...] = jnp.dot(a, w_ref[...].astype(BF16), preferred_element_type=F32) + b_ref[...]


def _ada_modulation(cvec, ada_w, ada_b):
    rows = cvec.shape[0]
    tn = 768
    return pl.pallas_call(
        _ada_kernel,
        out_shape=jax.ShapeDtypeStruct((DEPTH, rows, 3 * D_MODEL), F32),
        grid=(DEPTH, 3 * D_MODEL // tn),
        in_specs=[
            pl.BlockSpec((rows, D_MODEL), lambda l, j: (0, 0)),
            pl.BlockSpec((None, D_MODEL, tn), lambda l, j: (l, 0, j)),
            pl.BlockSpec((None, 1, tn), lambda l, j: (l, 0, j)),
        ],
        out_specs=pl.BlockSpec((None, rows, tn), lambda l, j: (l, 0, j)),
        compiler_params=_params("arbitrary", "arbitrary"),
        name="ada_modulation",
    )(cvec, ada_w, ada_b.reshape(DEPTH, 1, 3 * D_MODEL))


def _inproj_kernel(*refs, n_q, rope, heads_per_tile):
    x_ref, sh_ref, sc_ref, lng_ref, w1_ref, w2_ref, qg_ref, kg_ref = refs[:8]
    if rope:
        cos_ref, sa_ref, sb_ref = refs[8:11]
    q_out, k_out, v_out, g_out, h_scr = refs[-5:]
    j = pl.program_id(1)

    def modulated_norm(rows):
        x = x_ref[rows, :]
        ms = jnp.mean(x * x, axis=-1, keepdims=True)
        gain = lng_ref[...] * (1.0 + sc_ref[...])
        return (x * lax.rsqrt(ms + EPS) * gain + sh_ref[...]).astype(BF16)

    def normed(acc, rows, gain_ref, scale):
        outs = []
        for t in range(heads_per_tile):
            y = acc[:, t * HEAD_DIM:(t + 1) * HEAD_DIM]
            ms = jnp.mean(y * y, axis=-1, keepdims=True)
            y = y * lax.rsqrt(ms + EPS) * (gain_ref[...] * scale)
            if rope:
                y = (y * cos_ref[rows, :] + pltpu.roll(y, 96, 1) * sa_ref[rows, :]
                     + pltpu.roll(y, 32, 1) * sb_ref[rows, :])
            outs.append(y)
        return jnp.concatenate(outs, axis=-1)

    def project(w_ref, out_ref, epilogue, fill_h=False):
        for r in range(h_scr.shape[0] // PROJ_SUB):
            rows = slice(r * PROJ_SUB, (r + 1) * PROJ_SUB)
            if fill_h:
                h_scr[rows, :] = modulated_norm(rows)
            acc = jnp.dot(h_scr[rows, :], w_ref[...], preferred_element_type=F32)
            out_ref[rows, :] = epilogue(acc, rows).astype(out_ref.dtype)

    def q_and_gate(fill_h):
        project(w1_ref, q_out, lambda acc, rows: normed(acc, rows, qg_ref, Q_SCALE), fill_h)
        project(w2_ref, g_out, lambda acc, rows: acc * jax.nn.sigmoid(acc))

    @pl.when(j == 0)
    def _():
        q_and_gate(True)

    @pl.when((j > 0) & (j < n_q))
    def _():
        q_and_gate(False)

    @pl.when(j >= n_q)
    def _():
        project(w1_ref, k_out, lambda acc, rows: normed(acc, rows, kg_ref, 1.0))
        project(w2_ref, v_out, lambda acc, rows: acc)


def _in_projection(x, shift, scale, ln_g, w, qg, kg, widths, rope_tabs, kv_dtype, tn):
    t_rows = x.shape[0]
    qw, kw, vw, gw = widths
    assert qw == gw and kw == vw
    n_q, n_k = qw // tn, kw // tn
    tm = min(ROW_TILE, t_rows)
    rows_per_mod = t_rows // shift.shape[0]
    assert t_rows % tm == 0 and rows_per_mod % tm == 0
    rope = rope_tabs is not None
    g_col0, v_col0 = (qw + kw + vw) // tn, (qw + kw) // tn

    def mod_map(i, j):
        return ((i * tm) // rows_per_mod, 0, 0)

    in_specs = [
        pl.BlockSpec((tm, D_MODEL), lambda i, j: (i, 0)),
        pl.BlockSpec((None, 1, D_MODEL), mod_map),
        pl.BlockSpec((None, 1, D_MODEL), mod_map),
        pl.BlockSpec((1, D_MODEL), lambda i, j: (0, 0)),
        pl.BlockSpec((D_MODEL, tn), lambda i, j: (0, j)),
        pl.BlockSpec((D_MODEL, tn), lambda i, j: (0, jnp.where(j < n_q, g_col0 + j, v_col0 + j - n_q))),
        pl.BlockSpec((1, HEAD_DIM), lambda i, j: (0, 0)),
        pl.BlockSpec((1, HEAD_DIM), lambda i, j: (0, 0)),
    ]
    args = [x, shift, scale, ln_g.reshape(1, D_MODEL), w, w, qg.reshape(1, HEAD_DIM), kg.reshape(1, HEAD_DIM)]
    if rope:
        n_pos_tiles = rope_tabs[0].shape[0] // tm
        for tab in rope_tabs:
            in_specs.append(pl.BlockSpec((tm, HEAD_DIM), lambda i, j: (i % n_pos_tiles, 0)))
            args.append(tab)
    out_specs = [
        pl.BlockSpec((tm, tn), lambda i, j: (i, jnp.minimum(j, n_q - 1))),
        pl.BlockSpec((tm, tn), lambda i, j: (i, jnp.maximum(j - n_q, 0))),
        pl.BlockSpec((tm, tn), lambda i, j: (i, jnp.maximum(j - n_q, 0))),
        pl.BlockSpec((tm, tn), lambda i, j: (i, jnp.minimum(j, n_q - 1))),
    ]
    out_shape = [
        jax.ShapeDtypeStruct((t_rows, qw), BF16),
        jax.ShapeDtypeStruct((t_rows, kw), kv_dtype),
        jax.ShapeDtypeStruct((t_rows, vw), kv_dtype),
        jax.ShapeDtypeStruct((t_rows, gw), BF16),
    ]
    kern = functools.partial(_inproj_kernel, n_q=n_q, rope=rope, heads_per_tile=tn // HEAD_DIM)
    return pl.pallas_call(
        kern,
        out_shape=out_shape,
        grid=(t_rows // tm, n_q + n_k),
        in_specs=in_specs,
        out_specs=out_specs,
        scratch_shapes=[pltpu.VMEM((tm, D_MODEL), BF16)],
        compiler_params=_params("arbitrary", "arbitrary"),
        name="in_projection",
    )(*args)


def _outproj_kernel(o_ref, w_ref, x_ref, gt_ref, y_ref):
    for c in range(D_MODEL // OUT_SUB):
        cols = slice(c * OUT_SUB, (c + 1) * OUT_SUB)
        y = jnp.dot(o_ref[...], w_ref[:, cols], preferred_element_type=F32)
        y_ref[:, cols] = x_ref[:, cols] + gt_ref[:, cols] * y


def _out_projection(o, w, x, gate):
    t_rows = x.shape[0]
    tm = min(OUT_ROW_TILE, t_rows)
    rows_per_mod = t_rows // gate.shape[0]
    assert t_rows % tm == 0 and rows_per_mod % tm == 0
    row = lambda i: (i, 0)
    return pl.pallas_call(
        _outproj_kernel,
        out_shape=jax.ShapeDtypeStruct((t_rows, D_MODEL), F32),
        grid=(t_rows // tm,),
        in_specs=[
            pl.BlockSpec((tm, D_MODEL), row),
            pl.BlockSpec((D_MODEL, D_MODEL), lambda i: (0, 0)),
            pl.BlockSpec((tm, D_MODEL), row),
            pl.BlockSpec((None, 1, D_MODEL), lambda i: ((i * tm) // rows_per_mod, 0, 0)),
        ],
        out_specs=pl.BlockSpec((tm, D_MODEL), row),
        compiler_params=_params("arbitrary"),
        name="out_projection",
    )(o, w, x, gate)


def _lane_fold(x, op):
    r = x[:, 0:HEAD_DIM]
    for t in range(1, x.shape[1] // HEAD_DIM):
        r = op(r, x[:, t * HEAD_DIM:(t + 1) * HEAD_DIM])
    return r


def _with_ones(v):
    return jnp.concatenate([v, jnp.ones(v.shape, v.dtype)], axis=1)


def _softmax_pv(scores, values):
    m128 = _lane_fold(scores[0], jnp.maximum)
    for s in scores[1:]:
        m128 = jnp.maximum(m128, _lane_fold(s, jnp.maximum))
    m = jnp.broadcast_to(jnp.max(m128, axis=-1, keepdims=True), m128.shape)
    acc = None
    for s, v in zip(scores, values):
        p = jnp.exp2(s - jnp.concatenate([m] * (s.shape[1] // HEAD_DIM), axis=1)).astype(BF16)
        pv = jnp.dot(p, _with_ones(v), preferred_element_type=F32)
        acc = pv if acc is None else acc + pv
    return acc[:, :HEAD_DIM] * (1.0 / acc[:, HEAD_DIM:])


def _attn_a_kernel(*refs, latent, n_qblocks, n_sub):
    if latent:
        n_loc_blocks = n_sub + 2
        sink_ref, q_ref = refs[:2]
        k_loc_refs = refs[2:2 + n_loc_blocks]
        v_loc_refs = refs[2 + n_loc_blocks:2 + 2 * n_loc_blocks]
        kc_ref, vc_ref, sg_ref, o_ref, s_scr, m_scr, acc_scr, kc_scr, vc_scr = refs[2 + 2 * n_loc_blocks:]
    else:
        sink_ref, q_ref, k_ref, v_ref, sg_ref = refs[:5]
        o_ref, kcache_ref, vcache_ref, s_scr, m_scr, acc_scr = refs[-6:]
        for kv in range(A_KV_HEADS):
            kcache_ref[:, kv, :] = k_ref[:, kv * HEAD_DIM:(kv + 1) * HEAD_DIM]
            vcache_ref[:, kv, :] = v_ref[:, kv * HEAD_DIM:(kv + 1) * HEAD_DIM]
    m_rows = q_ref.shape[0] // n_sub
    n_keys = s_scr.shape[-1]
    n_loc = 3 * A_QBLOCK
    kv_cols = [slice(kv * HEAD_DIM, (kv + 1) * HEAD_DIM) for kv in range(A_KV_HEADS)]
    kv_heads = [[kv * A_GROUP + g for g in range(A_GROUP)] for kv in range(A_KV_HEADS)]
    units = [(sub, kv) for sub in range(n_sub) for kv in range(A_KV_HEADS)]
    sub_rows = [slice(sub * m_rows, (sub + 1) * m_rows) for sub in range(n_sub)]

    if latent:
        step = pl.program_id(1)
        qpos = lax.broadcasted_iota(jnp.int32, (A_QBLOCK, n_loc), 0)
        col = lax.broadcasted_iota(jnp.int32, (A_QBLOCK, n_loc), 1)
        in_window = jnp.abs(col - A_QBLOCK - qpos) <= WINDOW
        window_bias = []
        for sub in range(n_sub):
            kpos = (step * n_sub + sub - 1) * A_QBLOCK + col
            valid = in_window & (kpos >= 0) & (kpos < n_qblocks * A_QBLOCK)
            window_bias.append(jnp.concatenate([jnp.where(valid, 0.0, NEG_INF)] * A_GROUP, axis=0))

        @pl.when(step == 0)
        def _():
            for kv in range(A_KV_HEADS):
                kc_scr[kv] = kc_ref[:, kv, :].astype(BF16)
                vc_scr[kv] = vc_ref[:, kv, :].astype(BF16)

    for u, (sub, kv) in enumerate(units):
        cols = kv_cols[kv]
        q = jnp.concatenate([q_ref[sub_rows[sub], h * HEAD_DIM:(h + 1) * HEAD_DIM] for h in kv_heads[kv]], axis=0)
        if latent:
            k_loc = jnp.concatenate([ref[:, cols] for ref in k_loc_refs[sub:sub + 3]], axis=0)
            s_loc = _nt_dot(q, k_loc) + window_bias[sub]
            s_ctx = _nt_dot(q, kc_scr[kv])
            s_scr[u, :, 0:n_loc] = s_loc
            s_scr[u, :, n_loc:n_keys] = s_ctx
            m_scr[u] = jnp.maximum(_lane_fold(s_loc, jnp.maximum), _lane_fold(s_ctx, jnp.maximum))
        else:
            s = _nt_dot(q, k_ref[:, cols].astype(BF16))
            s_scr[u] = s
            m_scr[u] = _lane_fold(s, jnp.maximum)

    def sink_lanes(kv):
        return jnp.concatenate([jnp.full((m_rows, HEAD_DIM), sink_ref[h] * LOG2E, F32) for h in kv_heads[kv]], axis=0)

    for u, (sub, kv) in enumerate(units):
        m = jnp.max(m_scr[u], axis=-1, keepdims=True)
        m_scr[u] = jnp.maximum(jnp.broadcast_to(m, m_scr.shape[1:]), sink_lanes(kv))

    for u, (sub, kv) in enumerate(units):
        cols = kv_cols[kv]
        p = jnp.exp2(s_scr[u] - jnp.concatenate([m_scr[u]] * (n_keys // HEAD_DIM), axis=1)).astype(BF16)
        if latent:
            v_loc = jnp.concatenate([ref[:, cols] for ref in v_loc_refs[sub:sub + 3]], axis=0)
            acc_scr[u] = (jnp.dot(p[:, 0:n_loc], _with_ones(v_loc), preferred_element_type=F32)
                          + jnp.dot(p[:, n_loc:n_keys], _with_ones(vc_scr[kv]), preferred_element_type=F32))
        else:
            acc_scr[u] = jnp.dot(p, _with_ones(v_ref[:, cols].astype(BF16)), preferred_element_type=F32)

    for u, (sub, kv) in enumerate(units):
        acc = acc_scr[u]
        l = acc[:, HEAD_DIM:] + jnp.exp2(sink_lanes(kv) - m_scr[u])
        o = acc[:, :HEAD_DIM] * (1.0 / l)
        for g, h in enumerate(kv_heads[kv]):
            hc = slice(h * HEAD_DIM, (h + 1) * HEAD_DIM)
            gate = sg_ref[sub_rows[sub], hc].astype(F32)
            o_ref[sub_rows[sub], hc] = (o[g * m_rows:(g + 1) * m_rows] * gate).astype(o_ref.dtype)


def _attn_a_scratch(m_rows, n_keys, n_sub=1):
    stacked, n_units = A_GROUP * m_rows, n_sub * A_KV_HEADS
    return [pltpu.VMEM((n_units, stacked, n_keys), F32), pltpu.VMEM((n_units, stacked, HEAD_DIM), F32),
            pltpu.VMEM((n_units, stacked, 2 * HEAD_DIM), F32)]


def _attn_a_ctx(q, k, v, sg, sink, n_batch, seq, slot, n_slots, caches=None):
    kvw = A_KV_HEADS * HEAD_DIM
    row = lambda b: (b, 0)
    cache_shape = jax.ShapeDtypeStruct((n_batch, n_slots, seq, A_KV_HEADS, HEAD_DIM), F32)
    cache_spec = pl.BlockSpec((None, None, seq, A_KV_HEADS, HEAD_DIM), lambda b: (b, slot, 0, 0, 0))
    in_specs = [
        pl.BlockSpec(memory_space=pltpu.SMEM),
        pl.BlockSpec((seq, D_MODEL), row),
        pl.BlockSpec((seq, kvw), row),
        pl.BlockSpec((seq, kvw), row),
        pl.BlockSpec((seq, D_MODEL), row),
    ]
    args = [sink, q, k, v, sg]
    aliases = {}
    if caches is not None:
        aliases = {len(args): 1, len(args) + 1: 2}
        in_specs += [pl.BlockSpec(memory_space=pl.ANY)] * 2
        args += list(caches)
    return pl.pallas_call(
        functools.partial(_attn_a_kernel, latent=False, n_qblocks=1, n_sub=1),
        out_shape=[jax.ShapeDtypeStruct(q.shape, BF16), cache_shape, cache_shape],
        grid=(n_batch,),
        in_specs=in_specs,
        out_specs=[pl.BlockSpec((seq, D_MODEL), row), cache_spec, cache_spec],
        scratch_shapes=_attn_a_scratch(seq, seq),
        input_output_aliases=aliases,
        compiler_params=_params("arbitrary"),
        name="attn_a_ctx",
    )(*args)


def _attn_a_lat(q, k, v, cache_k, cache_v, cache_slot, sg, sink, n_batch, seq):
    kvw = A_KV_HEADS * HEAD_DIM
    nb = seq // A_QBLOCK
    n_sub = A_SUBBLOCKS
    assert nb % n_sub == 0
    steps = nb // n_sub
    own = lambda b, i: (b * steps + i, 0)
    loc = [lambda b, i, t=t: (b * nb + jnp.clip(i * n_sub + t - 1, 0, nb - 1), 0) for t in range(n_sub + 2)]
    loc_specs = [pl.BlockSpec((A_QBLOCK, kvw), m) for m in loc]
    ctx = lambda b, i: (b, cache_slot, 0, 0, 0)
    n_ctx = cache_k.shape[2]
    ctx_spec = pl.BlockSpec((None, None, n_ctx, A_KV_HEADS, HEAD_DIM), ctx)
    ctx_scratch = [pltpu.VMEM((A_KV_HEADS, n_ctx, HEAD_DIM), BF16)] * 2
    q_spec = pl.BlockSpec((n_sub * A_QBLOCK, D_MODEL), own)
    return pl.pallas_call(
        functools.partial(_attn_a_kernel, latent=True, n_qblocks=nb, n_sub=n_sub),
        out_shape=jax.ShapeDtypeStruct(q.shape, BF16),
        grid=(n_batch, steps),
        in_specs=[pl.BlockSpec(memory_space=pltpu.SMEM), q_spec] + loc_specs + loc_specs + [ctx_spec, ctx_spec, q_spec],
        out_specs=q_spec,
        scratch_shapes=_attn_a_scratch(A_QBLOCK, 3 * A_QBLOCK + n_ctx, n_sub) + ctx_scratch,
        compiler_params=_params("arbitrary", "arbitrary"),
        name="attn_a_lat",
    )(sink, q, *([k] * (n_sub + 2)), *([v] * (n_sub + 2)), cache_k, cache_v, sg)


def _attn_b_kernel(*refs, has_ctx, lambda_init, k_chunk):
    if has_ctx:
        (lam_ref, sub_ref, q_ref, k_ref, v_ref, kc_ref, vc_ref, sg_ref, o_ref,
         s_scr, sc_scr, m_scr, l_scr, acc_scr) = refs
    else:
        lam_ref, sub_ref, q_ref, k_ref, v_ref, sg_ref, o_ref, kcache_ref, s_scr, m_scr, l_scr, acc_scr = refs
        kc_ref = vc_ref = sc_scr = None
    dv = 2 * HEAD_DIM
    for head in range(q_ref.shape[1] // dv):
        cols = slice(head * dv, (head + 1) * dv)
        if not has_ctx:
            for half in range(2):
                kcache_ref[:, head, half, :] = k_ref[:, head * dv + half * HEAD_DIM:head * dv + (half + 1) * HEAD_DIM]
        view = lambda ref: None if ref is None else ref.at[:, cols]
        _attn_b_head(lam_ref, sub_ref, view(q_ref), view(k_ref), view(v_ref), view(kc_ref), view(vc_ref),
                     view(sg_ref), view(o_ref), s_scr, sc_scr, m_scr, l_scr, acc_scr,
                     lambda_init=lambda_init, k_chunk=k_chunk)


def _attn_b_head(lam_ref, sub_ref, q_ref, k_ref, v_ref, kc_ref, vc_ref, sg_ref, o_ref,
                 s_scr, sc_scr, m_scr, l_scr, acc_scr, *, lambda_init, k_chunk):
    has_ctx = kc_ref is not None
    n_chunks = k_ref.shape[0] // k_chunk
    halves = [slice(h * HEAD_DIM, (h + 1) * HEAD_DIM) for h in range(2)]

    def fold_max(h, s, first):
        m = _lane_fold(s, jnp.maximum)
        m_scr[h] = m if first else jnp.maximum(m_scr[h], m)

    def scores(c, first=False):
        rows = pl.ds(pl.multiple_of(c * k_chunk, k_chunk), k_chunk)
        k = k_ref[rows, :].astype(BF16)
        for h, hc in enumerate(halves):
            s = _nt_dot(q_ref[:, hc], k[:, hc])
            s_scr[h, c] = s
            fold_max(h, s, first)

    if has_ctx:
        kc = kc_ref[...].astype(BF16)
        for h, hc in enumerate(halves):
            s = _nt_dot(q_ref[:, hc], kc[:, hc])
            sc_scr[h] = s
            fold_max(h, s, True)
        lax.fori_loop(0, n_chunks, lambda c, carry: (scores(c), carry)[1], 0, unroll=B_UNROLL)
    else:
        assert n_chunks == 1
        scores(0, first=True)

    for h in range(2):
        m_scr[h] = jnp.broadcast_to(jnp.max(m_scr[h], axis=-1, keepdims=True), m_scr.shape[1:])

    def accumulate(h, s, v, first):
        m = m_scr[h]
        p = jnp.exp2(s - jnp.concatenate([m] * (s.shape[1] // HEAD_DIM), axis=1))
        l = _lane_fold(p, jnp.add)
        pv = jnp.dot(p.astype(BF16), v, preferred_element_type=F32)
        l_scr[h] = l if first else l_scr[h] + l
        acc_scr[h] = pv if first else acc_scr[h] + pv

    def weighted(c, first=False):
        rows = pl.ds(pl.multiple_of(c * k_chunk, k_chunk), k_chunk)
        v = v_ref[rows, :].astype(BF16)
        for h in range(2):
            accumulate(h, s_scr[h, c], v, first)

    if has_ctx:
        vc = vc_ref[...].astype(BF16)
        for h in range(2):
            accumulate(h, sc_scr[h], vc, True)
        lax.fori_loop(0, n_chunks, lambda c, carry: (weighted(c), carry)[1], 0, unroll=B_UNROLL)
    else:
        weighted(0, first=True)

    lam = lam_ref[...]
    lam_full = (jnp.exp(jnp.sum(lam[0:1] * lam[1:2], axis=-1, keepdims=True))
                - jnp.exp(jnp.sum(lam[2:3] * lam[3:4], axis=-1, keepdims=True)) + lambda_init)
    r0 = 1.0 / jnp.sum(l_scr[0], axis=-1, keepdims=True)
    r1 = lam_full / jnp.sum(l_scr[1], axis=-1, keepdims=True)
    o = acc_scr[0] * r0 - acc_scr[1] * r1
    ms = jnp.mean(o * o, axis=-1, keepdims=True)
    o = o * lax.rsqrt(ms + EPS) * (sub_ref[...] * (1.0 - lambda_init))
    o_ref[...] = (o * sg_ref[...].astype(F32)).astype(o_ref.dtype)


def _attn_b(q, k, v, sg, lam, subln, lambda_init, n_batch, seq, k_ctx=None, v_ctx=None):
    dv = 2 * HEAD_DIM
    tq = min(B_QBLOCK, seq)
    nq = seq // tq
    k_chunk = min(B_KCHUNK, seq)
    has_ctx = k_ctx is not None
    heads_per_step = 1 if has_ctx else B_HEADS
    gw = heads_per_step * dv
    qmap = lambda b, h, i: (b * nq + i, h)
    kvmap = lambda b, h, i: (b, h)
    in_specs = [
        pl.BlockSpec((4, HEAD_DIM), lambda b, h, i: (0, 0)),
        pl.BlockSpec((1, dv), lambda b, h, i: (0, 0)),
        pl.BlockSpec((tq, gw), qmap),
        pl.BlockSpec((seq, gw), kvmap),
        pl.BlockSpec((seq, gw), kvmap),
    ]
    args = [lam, subln.reshape(1, dv), q, k, v]
    scratch = [pltpu.VMEM((2, seq // k_chunk, tq, k_chunk), F32)]
    if has_ctx:
        n_ctx = k_ctx.shape[1]
        in_specs += [pl.BlockSpec((None, n_ctx, gw), lambda b, h, i: (b, 0, h))] * 2
        args += [k_ctx, v_ctx]
        scratch.append(pltpu.VMEM((2, tq, n_ctx), F32))
    in_specs.append(pl.BlockSpec((tq, gw), qmap))
    args.append(sg)
    scratch += [pltpu.VMEM((2, tq, HEAD_DIM), F32), pltpu.VMEM((2, tq, HEAD_DIM), F32),
                pltpu.VMEM((2, tq, dv), F32)]
    out_shape = jax.ShapeDtypeStruct(q.shape, BF16)
    out_specs = pl.BlockSpec((tq, gw), qmap)
    if not has_ctx:
        assert nq == 1 and k.dtype == F32
        out_shape = [out_shape, jax.ShapeDtypeStruct((n_batch, seq, B_HEADS, 2, HEAD_DIM), F32)]
        out_specs = [out_specs, pl.BlockSpec((None, seq, B_HEADS, 2, HEAD_DIM), lambda b, h, i: (b, 0, 0, 0, 0))]
    return pl.pallas_call(
        functools.partial(_attn_b_kernel, has_ctx=has_ctx, lambda_init=lambda_init, k_chunk=k_chunk),
        out_shape=out_shape,
        grid=(n_batch, B_HEADS // heads_per_step, nq),
        in_specs=in_specs,
        out_specs=out_specs,
        scratch_shapes=scratch,
        compiler_params=_params("arbitrary", "arbitrary", "arbitrary"),
        name="attn_b_lat" if has_ctx else "attn_b_ctx",
    )(*args)


def _attn_c_ctx_kernel(q_ref, k_ref, v_ref, sg_ref, o_ref):
    for h in range(C_HEADS):
        hc = slice(h * HEAD_DIM, (h + 1) * HEAD_DIM)
        s = _nt_dot(q_ref[:, hc], k_ref[:, hc].astype(BF16))
        o = _softmax_pv([s], [v_ref[:, hc].astype(BF16)])
        o_ref[:, hc] = (o * sg_ref[:, hc].astype(F32)).astype(o_ref.dtype)


def _attn_c_ctx(q, k, v, sg, n_batch, seq):
    spec = pl.BlockSpec((seq, D_MODEL), lambda b: (b, 0))
    return pl.pallas_call(
        _attn_c_ctx_kernel,
        out_shape=jax.ShapeDtypeStruct(q.shape, BF16),
        grid=(n_batch,),
        in_specs=[spec, spec, spec, spec],
        out_specs=spec,
        compiler_params=_params("arbitrary"),
        name="attn_c_ctx",
    )(q, k, v, sg)


def _na_geometry(rows):
    kh = min(NA_KH, rows)
    key_rows = C_ROWS + kh - 1
    key_rows = min(key_rows + key_rows % 2, rows)
    n_blocks = rows // C_ROWS
    cols = np.arange(GRID_W)
    col_start = np.clip(cols - NA_KW // 2, 0, GRID_W - NA_KW)
    col_ok = (cols[None, :] >= col_start[:, None]) & (cols[None, :] < col_start[:, None] + NA_KW)
    col_delta = cols[None, :] - cols[:, None] + NA_KW - 1
    col_onehot = (col_delta[None] == np.arange(2 * NA_KW - 1)[:, None, None]) & col_ok[None]
    starts, class_ids, classes, keys = [], [], [], {}
    for blk in range(n_blocks):
        r = blk * C_ROWS + np.arange(C_ROWS)
        rs = np.clip(r - kh // 2, 0, rows - kh)
        start = int(np.clip(rs[0], 0, rows - key_rows))
        key = (tuple(rs - r), start - blk * C_ROWS)
        if key not in keys:
            keys[key] = len(classes)
            kr = start + np.arange(key_rows)
            row_ok = (kr[None, :] >= rs[:, None]) & (kr[None, :] < rs[:, None] + kh)
            row_idx = kr[None, :] - r[:, None] + NA_KH - 1
            classes.append((row_ok, row_idx))
        starts.append(start)
        class_ids.append(keys[key])
    geometry = (col_ok, col_onehot.astype(np.float32), classes)
    return key_rows, np.asarray(starts, np.int32), np.asarray(class_ids, np.int32), geometry


def _na_bias_tables(rpb, geometry):
    col_ok, col_onehot, classes = geometry
    n_heads = rpb.shape[0]
    toep = jnp.einsum("hdx,xck->hdck", rpb * LOG2E, jnp.asarray(col_onehot),
                      precision=lax.Precision.HIGHEST)
    toep = jnp.where(jnp.asarray(col_ok)[None, None], toep, NEG_INF)
    n_off = toep.shape[1]
    toep = jnp.concatenate([toep, jnp.full((n_heads, 1, GRID_W, GRID_W), NEG_INF, F32)], axis=1)
    select = np.stack([np.where(row_ok, row_idx, n_off) for row_ok, row_idx in classes])
    n_cls, n_r, n_kr = select.shape
    dnums = lax.GatherDimensionNumbers(offset_dims=(1, 3, 5), collapsed_slice_dims=(1,), start_index_map=(1,))
    tabs = lax.gather(toep, jnp.asarray(select[..., None], jnp.int32), dnums,
                      slice_sizes=(n_heads, 1, GRID_W, GRID_W))
    return tabs.reshape(n_cls, n_heads, n_r * GRID_W, n_kr * GRID_W)


def _attn_c_lat_kernel(start_ref, cls_ref, q_ref, k_ref, v_ref, kc_ref, vc_ref, bias_ref, sg_ref, o_ref,
                       s_scr, m_scr, acc_scr, *, key_rows):
    blk = pl.program_id(2)
    n_loc = key_rows * GRID_W
    n_keys = s_scr.shape[-1]
    rows = pl.ds(pl.multiple_of(start_ref[blk] * GRID_W, GRID_W), n_loc)
    head_cols = [slice(h * HEAD_DIM, (h + 1) * HEAD_DIM) for h in range(C_HEAD_GROUP)]

    for h, hc in enumerate(head_cols):
        q = q_ref[:, hc]
        s_loc = _nt_dot(q, k_ref[rows, hc]) + bias_ref[h]
        s_ctx = _nt_dot(q, kc_ref[:, hc].astype(BF16))
        s_scr[h, :, 0:n_loc] = s_loc
        s_scr[h, :, n_loc:n_keys] = s_ctx
        m_scr[h] = jnp.maximum(_lane_fold(s_loc, jnp.maximum), _lane_fold(s_ctx, jnp.maximum))

    for h in range(C_HEAD_GROUP):
        m_scr[h] = jnp.broadcast_to(jnp.max(m_scr[h], axis=-1, keepdims=True), m_scr.shape[1:])

    for h, hc in enumerate(head_cols):
        p = jnp.exp2(s_scr[h] - jnp.concatenate([m_scr[h]] * (n_keys // HEAD_DIM), axis=1)).astype(BF16)
        acc_scr[h] = (jnp.dot(p[:, 0:n_loc], _with_ones(v_ref[rows, hc]), preferred_element_type=F32)
                      + jnp.dot(p[:, n_loc:n_keys], _with_ones(vc_ref[:, hc].astype(BF16)),
                                preferred_element_type=F32))

    for h, hc in enumerate(head_cols):
        acc = acc_scr[h]
        o = acc[:, :HEAD_DIM] * (1.0 / acc[:, HEAD_DIM:])
        o_ref[:, hc] = (o * sg_ref[:, hc].astype(F32)).astype(o_ref.dtype)


def _attn_c_lat(q, k, v, k_ctx, v_ctx, sg, rpb, n_batch, seq):
    rows = seq // GRID_W
    key_rows, starts, class_ids, geometry = _na_geometry(rows)
    bias = _na_bias_tables(rpb, geometry)
    nblk = rows // C_ROWS
    tq = C_ROWS * GRID_W
    gw = C_HEAD_GROUP * HEAD_DIM
    n_ctx = k_ctx.shape[1]
    qmap = lambda b, g, i, st, cl: (b * nblk + i, g)
    kvmap = lambda b, g, i, st, cl: (b, g)
    ctxmap = lambda b, g, i, st, cl: (b, 0, g)
    grid_spec = pltpu.PrefetchScalarGridSpec(
        num_scalar_prefetch=2,
        grid=(n_batch, C_HEADS // C_HEAD_GROUP, nblk),
        in_specs=[
            pl.BlockSpec((tq, gw), qmap),
            pl.BlockSpec((seq, gw), kvmap),
            pl.BlockSpec((seq, gw), kvmap),
            pl.BlockSpec((None, n_ctx, gw), ctxmap),
            pl.BlockSpec((None, n_ctx, gw), ctxmap),
            pl.BlockSpec((None, C_HEAD_GROUP, tq, key_rows * GRID_W),
                         lambda b, g, i, st, cl: (cl[i], g, 0, 0)),
            pl.BlockSpec((tq, gw), qmap),
        ],
        out_specs=pl.BlockSpec((tq, gw), qmap),
        scratch_shapes=[pltpu.VMEM((C_HEAD_GROUP, tq, key_rows * GRID_W + n_ctx), F32),
                        pltpu.VMEM((C_HEAD_GROUP, tq, HEAD_DIM), F32),
                        pltpu.VMEM((C_HEAD_GROUP, tq, 2 * HEAD_DIM), F32)],
    )
    return pl.pallas_call(
        functools.partial(_attn_c_lat_kernel, key_rows=key_rows),
        out_shape=jax.ShapeDtypeStruct(q.shape, BF16),
        grid_spec=grid_spec,
        compiler_params=_params("arbitrary", "arbitrary", "arbitrary"),
        name="attn_c_lat",
    )(jnp.asarray(starts), jnp.asarray(class_ids), q, k, v, k_ctx, v_ctx, bias, sg)


def _rope_tables(n_tokens):
    d4 = HEAD_DIM // 4
    t = jnp.arange(n_tokens)
    pos = jnp.stack([t // GRID_W, t % GRID_W], axis=-1).astype(F32)
    inv_freq = ROPE_BASE ** (-jnp.arange(d4, dtype=F32) / d4)
    ang = pos[:, :, None] * inv_freq
    cos, sin = jnp.cos(ang), jnp.sin(ang)
    zero = jnp.zeros_like(sin[:, 0])
    cos_full = jnp.concatenate([cos[:, 0], cos[:, 0], cos[:, 1], cos[:, 1]], axis=-1)
    sin_up = jnp.concatenate([-sin[:, 0], zero, -sin[:, 1], zero], axis=-1)
    sin_dn = jnp.concatenate([zero, sin[:, 0], zero, sin[:, 1]], axis=-1)
    return cos_full, sin_up, sin_dn


def kernel(x_prompt, x_sample, cache_a_k, cache_a_v, cache_b_k, cache_b_v, cache_c_k, cache_c_v, c, c_ctx,
           ln_g, ada_w, ada_b, w_out, qn_g, kn_g, w_in_a, sink_a, w_in_b, lam_b, subln_b, w_in_c, rpb_c):
    n_p, seq_p, d = x_prompt.shape
    n_s, seq_s, _ = x_sample.shape
    n_ctx = cache_a_k.shape[2]
    xp = x_prompt.reshape(n_p * seq_p, d)
    xs = x_sample.reshape(n_s * seq_s, d)
    rope_tabs = _rope_tables(seq_s)

    pad = (-(n_s + 1)) % 8
    cvec = jnp.concatenate([c, c_ctx[None, :], jnp.zeros((pad, d), F32)], axis=0)
    mod = _ada_modulation(cvec, ada_w, ada_b)
    mod = mod.reshape(DEPTH, mod.shape[1], 3, 1, d)

    new_kv = {1: ([], []), 2: ([], [])}
    a_caches = None
    n_a_layers = len(range(0, DEPTH, N_MIXERS))
    for l in range(DEPTH):
        kind, j = l % N_MIXERS, l // N_MIXERS
        sh_s, sc_s, gt_s = (mod[l, :n_s, t] for t in range(3))
        sh_p, sc_p, gt_p = (mod[l, n_s:n_s + 1, t] for t in range(3))
        w_o = w_out[l].astype(BF16)
        if kind == 0:
            kvw = A_KV_HEADS * HEAD_DIM
            w_in, widths, tn = w_in_a[j].astype(BF16), (d, kvw, kvw, d), 512
        elif kind == 1:
            w_in, widths, tn = w_in_b[j].astype(BF16), (d, d, d, d), 512
        else:
            w_in, widths, tn = w_in_c[j].astype(BF16), (d, d, d, d), 512
        rope = rope_tabs if kind != 2 else None
        qp, kp, vp, gp = _in_projection(xp, sh_p, sc_p, ln_g[l], w_in, qn_g[l], kn_g[l], widths, None, F32, tn)
        qs, ks, vs, gs = _in_projection(xs, sh_s, sc_s, ln_g[l], w_in, qn_g[l], kn_g[l], widths, rope, BF16, tn)
        if kind == 0:
            op, *a_caches = _attn_a_ctx(qp, kp, vp, gp, sink_a[j], n_p, seq_p, j, n_a_layers, a_caches)
            os_ = _attn_a_lat(qs, ks, vs, cache_a_k, cache_a_v, j, gs, sink_a[j], n_s, seq_s)
        elif kind == 1:
            lambda_init = 0.8 - 0.6 * math.exp(-0.3 * l)
            kc = cache_b_k[:, j].reshape(n_s, n_ctx, d)
            vc = cache_b_v[:, j].reshape(n_s, n_ctx, d)
            op, k_new = _attn_b(qp, kp, vp, gp, lam_b[j], subln_b[j], lambda_init, n_p, seq_p)
            os_ = _attn_b(qs, ks, vs, gs, lam_b[j], subln_b[j], lambda_init, n_s, seq_s, kc, vc)
            new_kv[1][0].append(k_new)
            new_kv[1][1].append(vp.reshape(n_p, seq_p, B_HEADS, 2 * HEAD_DIM))
        else:
            kc = cache_c_k[:, j].reshape(n_s, n_ctx, d)
            vc = cache_c_v[:, j].reshape(n_s, n_ctx, d)
            op = _attn_c_ctx(qp, kp, vp, gp, n_p, seq_p)
            os_ = _attn_c_lat(qs, ks, vs, kc, vc, gs, rpb_c[j], n_s, seq_s)
            new_kv[2][0].append(kp.reshape(n_p, seq_p, C_HEADS, HEAD_DIM))
            new_kv[2][1].append(vp.reshape(n_p, seq_p, C_HEADS, HEAD_DIM))
        xp = _out_projection(op, w_o, xp, gt_p)
        xs = _out_projection(os_, w_o, xs, gt_s)

    outs = [xp.reshape(x_prompt.shape), xs.reshape(x_sample.shape)] + list(a_caches)
    for kind in (1, 2):
        outs.append(jnp.stack(new_kv[kind][0], axis=1))
        outs.append(jnp.stack(new_kv[kind][1], axis=1))
    return tuple(outs)
```

```python
import functools
import math

import numpy as np
import jax
import jax.numpy as jnp
from jax import lax
from jax.experimental import pallas as pl
from jax.experimental.pallas import tpu as pltpu

D_MODEL = 2048
DEPTH = 4
GRID_W = 64
HEAD_DIM = 128
ROPE_BASE = 10000.0
NEG_INF = -1e30
N_MIXERS = 3
A_HEADS = D_MODEL // HEAD_DIM
A_KV_HEADS = A_HEADS // 4
A_GROUP = A_HEADS // A_KV_HEADS
WINDOW = 128
B_HEADS = D_MODEL // (2 * HEAD_DIM)
C_HEADS = D_MODEL // HEAD_DIM
NA_KH = 8
NA_KW = 16
SCALE = HEAD_DIM ** -0.5
LOG2E = math.log2(math.e)
Q_SCALE = SCALE * LOG2E
EPS = 1e-6

BF16 = jnp.bfloat16
F32 = jnp.float32

VMEM_LIMIT_BYTES = 56 * 1024 * 1024
ROW_TILE = 1024
OUT_ROW_TILE = 512
OUT_SUB = 512
PROJ_SUB = 128
A_QBLOCK = 128
A_SUBBLOCKS = 4
B_QBLOCK = 1024
B_KCHUNK = 512
B_UNROLL = 4
C_ROWS = 4
C_HEAD_GROUP = 4
C_SUBBLOCKS = 2


def _nt_dot(a, b):
    return lax.dot_general(a, b, (((1,), (1,)), ((), ())), preferred_element_type=F32)


def _params(*semantics):
    return pltpu.CompilerParams(dimension_semantics=semantics, vmem_limit_bytes=VMEM_LIMIT_BYTES)


def _ada_kernel(c_ref, w_ref, b_ref, o_ref):
    c = c_ref[...]
    a = (c * jax.nn.sigmoid(c)).astype(BF16)
    o_ref[...] = jnp.dot(a, w_ref[...].astype(BF16), preferred_element_type=F32) + b_ref[...]


def _ada_modulation(cvec, ada_w, ada_b):
    rows = cvec.shape[0]
    tn = 768
    return pl.pallas_call(
        _ada_kernel,
        out_shape=jax.ShapeDtypeStruct((DEPTH, rows, 3 * D_MODEL), F32),
        grid=(DEPTH, 3 * D_MODEL // tn),
        in_specs=[
            pl.BlockSpec((rows, D_MODEL), lambda l, j: (0, 0)),
            pl.BlockSpec((None, D_MODEL, tn), lambda l, j: (l, 0, j)),
            pl.BlockSpec((None, 1, tn), lambda l, j: (l, 0, j)),
        ],
        out_specs=pl.BlockSpec((None, rows, tn), lambda l, j: (l, 0, j)),
        compiler_params=_params("arbitrary", "arbitrary"),
        name="ada_modulation",
    )(cvec, ada_w, ada_b.reshape(DEPTH, 1, 3 * D_MODEL))


def _inproj_kernel(*refs, n_q, rope, heads_per_tile):
    x_ref, sh_ref, sc_ref, lng_ref, w1_ref, w2_ref, qg_ref, kg_ref = refs[:8]
    if rope:
        cos_ref, sa_ref, sb_ref = refs[8:11]
    q_out, k_out, v_out, g_out, h_scr = refs[-5:]
    j = pl.program_id(1)

    def modulated_norm(rows):
        x = x_ref[rows, :]
        ms = jnp.mean(x * x, axis=-1, keepdims=True)
        gain = lng_ref[...] * (1.0 + sc_ref[...])
        return (x * lax.rsqrt(ms + EPS) * gain + sh_ref[...]).astype(BF16)

    def normed(acc, rows, gain_ref, scale):
        outs = []
        for t in range(heads_per_tile):
            y = acc[:, t * HEAD_DIM:(t + 1) * HEAD_DIM]
            ms = jnp.mean(y * y, axis=-1, keepdims=True)
            y = y * lax.rsqrt(ms + EPS) * (gain_ref[...] * scale)
            if rope:
                y = (y * cos_ref[rows, :] + pltpu.roll(y, 96, 1) * sa_ref[rows, :]
                     + pltpu.roll(y, 32, 1) * sb_ref[rows, :])
            outs.append(y)
        return jnp.concatenate(outs, axis=-1)

    def project(w_ref, out_ref, epilogue, fill_h=False):
        for r in range(h_scr.shape[0] // PROJ_SUB):
            rows = slice(r * PROJ_SUB, (r + 1) * PROJ_SUB)
            if fill_h:
                h_scr[rows, :] = modulated_norm(rows)
            acc = jnp.dot(h_scr[rows, :], w_ref[...], preferred_element_type=F32)
            out_ref[rows, :] = epilogue(acc, rows).astype(out_ref.dtype)

    def q_and_gate(fill_h):
        project(w1_ref, q_out, lambda acc, rows: normed(acc, rows, qg_ref, Q_SCALE), fill_h)
        project(w2_ref, g_out, lambda acc, rows: acc * jax.nn.sigmoid(acc))

    @pl.when(j == 0)
    def _():
        q_and_gate(True)

    @pl.when((j > 0) & (j < n_q))
    def _():
        q_and_gate(False)

    @pl.when(j >= n_q)
    def _():
        project(w1_ref, k_out, lambda acc, rows: normed(acc, rows, kg_ref, 1.0))
        project(w2_ref, v_out, lambda acc, rows: acc)


def _in_projection(x, shift, scale, ln_g, w, qg, kg, widths, rope_tabs, kv_dtype, tn):
    t_rows = x.shape[0]
    qw, kw, vw, gw = widths
    assert qw == gw and kw == vw
    n_q, n_k = qw // tn, kw // tn
    tm = min(ROW_TILE, t_rows)
    rows_per_mod = t_rows // shift.shape[0]
    assert t_rows % tm == 0 and rows_per_mod % tm == 0
    rope = rope_tabs is not None
    g_col0, v_col0 = (qw + kw + vw) // tn, (qw + kw) // tn

    def mod_map(i, j):
        return ((i * tm) // rows_per_mod, 0, 0)

    in_specs = [
        pl.BlockSpec((tm, D_MODEL), lambda i, j: (i, 0)),
        pl.BlockSpec((None, 1, D_MODEL), mod_map),
        pl.BlockSpec((None, 1, D_MODEL), mod_map),
        pl.BlockSpec((1, D_MODEL), lambda i, j: (0, 0)),
        pl.BlockSpec((D_MODEL, tn), lambda i, j: (0, j)),
        pl.BlockSpec((D_MODEL, tn), lambda i, j: (0, jnp.where(j < n_q, g_col0 + j, v_col0 + j - n_q))),
        pl.BlockSpec((1, HEAD_DIM), lambda i, j: (0, 0)),
        pl.BlockSpec((1, HEAD_DIM), lambda i, j: (0, 0)),
    ]
    args = [x, shift, scale, ln_g.reshape(1, D_MODEL), w, w, qg.reshape(1, HEAD_DIM), kg.reshape(1, HEAD_DIM)]
    if rope:
        n_pos_tiles = rope_tabs[0].shape[0] // tm
        for tab in rope_tabs:
            in_specs.append(pl.BlockSpec((tm, HEAD_DIM), lambda i, j: (i % n_pos_tiles, 0)))
            args.append(tab)
    out_specs = [
        pl.BlockSpec((tm, tn), lambda i, j: (i, jnp.minimum(j, n_q - 1))),
        pl.BlockSpec((tm, tn), lambda i, j: (i, jnp.maximum(j - n_q, 0))),
        pl.BlockSpec((tm, tn), lambda i, j: (i, jnp.maximum(j - n_q, 0))),
        pl.BlockSpec((tm, tn), lambda i, j: (i, jnp.minimum(j, n_q - 1))),
    ]
    out_shape = [
        jax.ShapeDtypeStruct((t_rows, qw), BF16),
        jax.ShapeDtypeStruct((t_rows, kw), kv_dtype),
        jax.ShapeDtypeStruct((t_rows, vw), kv_dtype),
        jax.ShapeDtypeStruct((t_rows, gw), BF16),
    ]
    kern = functools.partial(_inproj_kernel, n_q=n_q, rope=rope, heads_per_tile=tn // HEAD_DIM)
    return pl.pallas_call(
        kern,
        out_shape=out_shape,
        grid=(t_rows // tm, n_q + n_k),
        in_specs=in_specs,
        out_specs=out_specs,
        scratch_shapes=[pltpu.VMEM((tm, D_MODEL), BF16)],
        compiler_params=_params("arbitrary", "arbitrary"),
        name="in_projection",
    )(*args)


def _outproj_kernel(o_ref, w_ref, x_ref, gt_ref, y_ref):
    for c in range(D_MODEL // OUT_SUB):
        cols = slice(c * OUT_SUB, (c + 1) * OUT_SUB)
        y = jnp.dot(o_ref[...], w_ref[:, cols], preferred_element_type=F32)
        y_ref[:, cols] = x_ref[:, cols] + gt_ref[:, cols] * y


def _out_projection(o, w, x, gate):
    t_rows = x.shape[0]
    tm = min(OUT_ROW_TILE, t_rows)
    rows_per_mod = t_rows // gate.shape[0]
    assert t_rows % tm == 0 and rows_per_mod % tm == 0
    row = lambda i: (i, 0)
    return pl.pallas_call(
        _outproj_kernel,
        out_shape=jax.ShapeDtypeStruct((t_rows, D_MODEL), F32),
        grid=(t_rows // tm,),
        in_specs=[
            pl.BlockSpec((tm, D_MODEL), row),
            pl.BlockSpec((D_MODEL, D_MODEL), lambda i: (0, 0)),
            pl.BlockSpec((tm, D_MODEL), row),
            pl.BlockSpec((None, 1, D_MODEL), lambda i: ((i * tm) // rows_per_mod, 0, 0)),
        ],
        out_specs=pl.BlockSpec((tm, D_MODEL), row),
        compiler_params=_params("arbitrary"),
        name="out_projection",
    )(o, w, x, gate)


def _lane_fold(x, op):
    r = x[:, 0:HEAD_DIM]
    for t in range(1, x.shape[1] // HEAD_DIM):
        r = op(r, x[:, t * HEAD_DIM:(t + 1) * HEAD_DIM])
    return r


def _with_ones(v):
    return jnp.concatenate([v, jnp.ones(v.shape, v.dtype)], axis=1)


def _softmax_pv(scores, values):
    m128 = _lane_fold(scores[0], jnp.maximum)
    for s in scores[1:]:
        m128 = jnp.maximum(m128, _lane_fold(s, jnp.maximum))
    m = jnp.broadcast_to(jnp.max(m128, axis=-1, keepdims=True), m128.shape)
    acc = None
    for s, v in zip(scores, values):
        p = jnp.exp2(s - jnp.concatenate([m] * (s.shape[1] // HEAD_DIM), axis=1)).astype(BF16)
        pv = jnp.dot(p, _with_ones(v), preferred_element_type=F32)
        acc = pv if acc is None else acc + pv
    return acc[:, :HEAD_DIM] * (1.0 / acc[:, HEAD_DIM:])


def _attn_a_kernel(*refs, latent, n_qblocks, n_sub):
    if latent:
        n_loc_blocks = n_sub + 2
        sink_ref, q_ref = refs[:2]
        k_loc_refs = refs[2:2 + n_loc_blocks]
        v_loc_refs = refs[2 + n_loc_blocks:2 + 2 * n_loc_blocks]
        kc_ref, vc_ref, sg_ref, o_ref, s_scr, m_scr, acc_scr, kc_scr, vc_scr = refs[2 + 2 * n_loc_blocks:]
    else:
        sink_ref, q_ref, k_ref, v_ref, sg_ref = refs[:5]
        o_ref, kcache_ref, vcache_ref, s_scr, m_scr, acc_scr = refs[-6:]
        for kv in range(A_KV_HEADS):
            kcache_ref[:, kv, :] = k_ref[:, kv * HEAD_DIM:(kv + 1) * HEAD_DIM]
            vcache_ref[:, kv, :] = v_ref[:, kv * HEAD_DIM:(kv + 1) * HEAD_DIM]
    m_rows = q_ref.shape[0] // n_sub
    n_keys = s_scr.shape[-1]
    n_loc = 3 * A_QBLOCK
    kv_cols = [slice(kv * HEAD_DIM, (kv + 1) * HEAD_DIM) for kv in range(A_KV_HEADS)]
    kv_heads = [[kv * A_GROUP + g for g in range(A_GROUP)] for kv in range(A_KV_HEADS)]
    units = [(sub, kv) for sub in range(n_sub) for kv in range(A_KV_HEADS)]
    sub_rows = [slice(sub * m_rows, (sub + 1) * m_rows) for sub in range(n_sub)]

    if latent:
        step = pl.program_id(1)
        qpos = lax.broadcasted_iota(jnp.int32, (A_QBLOCK, n_loc), 0)
        col = lax.broadcasted_iota(jnp.int32, (A_QBLOCK, n_loc), 1)
        in_window = jnp.abs(col - A_QBLOCK - qpos) <= WINDOW
        window_bias = []
        for sub in range(n_sub):
            kpos = (step * n_sub + sub - 1) * A_QBLOCK + col
            valid = in_window & (kpos >= 0) & (kpos < n_qblocks * A_QBLOCK)
            window_bias.append(jnp.concatenate([jnp.where(valid, 0.0, NEG_INF)] * A_GROUP, axis=0))

        @pl.when(step == 0)
        def _():
            for kv in range(A_KV_HEADS):
                kc_scr[kv] = kc_ref[:, kv, :].astype(BF16)
                vc_scr[kv] = vc_ref[:, kv, :].astype(BF16)

    for u, (sub, kv) in enumerate(units):
        cols = kv_cols[kv]
        q = jnp.concatenate([q_ref[sub_rows[sub], h * HEAD_DIM:(h + 1) * HEAD_DIM] for h in kv_heads[kv]], axis=0)
        if latent:
            k_loc = jnp.concatenate([ref[:, cols] for ref in k_loc_refs[sub:sub + 3]], axis=0)
            s_loc = _nt_dot(q, k_loc) + window_bias[sub]
            s_ctx = _nt_dot(q, kc_scr[kv])
            s_scr[u, :, 0:n_loc] = s_loc
            s_scr[u, :, n_loc:n_keys] = s_ctx
            m_scr[u] = jnp.maximum(_lane_fold(s_loc, jnp.maximum), _lane_fold(s_ctx, jnp.maximum))
        else:
            s = _nt_dot(q, k_ref[:, cols].astype(BF16))
            s_scr[u] = s
            m_scr[u] = _lane_fold(s, jnp.maximum)

    def sink_lanes(kv):
        return jnp.concatenate([jnp.full((m_rows, HEAD_DIM), sink_ref[h] * LOG2E, F32) for h in kv_heads[kv]], axis=0)

    for u, (sub, kv) in enumerate(units):
        m = jnp.max(m_scr[u], axis=-1, keepdims=True)
        m_scr[u] = jnp.maximum(jnp.broadcast_to(m, m_scr.shape[1:]), sink_lanes(kv))

    for u, (sub, kv) in enumerate(units):
        cols = kv_cols[kv]
        p = jnp.exp2(s_scr[u] - jnp.concatenate([m_scr[u]] * (n_keys // HEAD_DIM), axis=1)).astype(BF16)
        if latent:
            v_loc = jnp.concatenate([ref[:, cols] for ref in v_loc_refs[sub:sub + 3]], axis=0)
            acc_scr[u] = (jnp.dot(p[:, 0:n_loc], _with_ones(v_loc), preferred_element_type=F32)
                          + jnp.dot(p[:, n_loc:n_keys], _with_ones(vc_scr[kv]), preferred_element_type=F32))
        else:
            acc_scr[u] = jnp.dot(p, _with_ones(v_ref[:, cols].astype(BF16)), preferred_element_type=F32)

    for u, (sub, kv) in enumerate(units):
        acc = acc_scr[u]
        l = acc[:, HEAD_DIM:] + jnp.exp2(sink_lanes(kv) - m_scr[u])
        o = acc[:, :HEAD_DIM] * (1.0 / l)
        for g, h in enumerate(kv_heads[kv]):
            hc = slice(h * HEAD_DIM, (h + 1) * HEAD_DIM)
            gate = sg_ref[sub_rows[sub], hc].astype(F32)
            o_ref[sub_rows[sub], hc] = (o[g * m_rows:(g + 1) * m_rows] * gate).astype(o_ref.dtype)


def _attn_a_scratch(m_rows, n_keys, n_sub=1):
    stacked, n_units = A_GROUP * m_rows, n_sub * A_KV_HEADS
    return [pltpu.VMEM((n_units, stacked, n_keys), F32), pltpu.VMEM((n_units, stacked, HEAD_DIM), F32),
            pltpu.VMEM((n_units, stacked, 2 * HEAD_DIM), F32)]


def _attn_a_ctx(q, k, v, sg, sink, n_batch, seq, slot, n_slots, caches=None):
    kvw = A_KV_HEADS * HEAD_DIM
    row = lambda b: (b, 0)
    cache_shape = jax.ShapeDtypeStruct((n_batch, n_slots, seq, A_KV_HEADS, HEAD_DIM), F32)
    cache_spec = pl.BlockSpec((None, None, seq, A_KV_HEADS, HEAD_DIM), lambda b: (b, slot, 0, 0, 0))
    in_specs = [
        pl.BlockSpec(memory_space=pltpu.SMEM),
        pl.BlockSpec((seq, D_MODEL), row),
        pl.BlockSpec((seq, kvw), row),
        pl.BlockSpec((seq, kvw), row),
        pl.BlockSpec((seq, D_MODEL), row),
    ]
    args = [sink, q, k, v, sg]
    aliases = {}
    if caches is not None:
        aliases = {len(args): 1, len(args) + 1: 2}
        in_specs += [pl.BlockSpec(memory_space=pl.ANY)] * 2
        args += list(caches)
    return pl.pallas_call(
        functools.partial(_attn_a_kernel, latent=False, n_qblocks=1, n_sub=1),
        out_shape=[jax.ShapeDtypeStruct(q.shape, BF16), cache_shape, cache_shape],
        grid=(n_batch,),
        in_specs=in_specs,
        out_specs=[pl.BlockSpec((seq, D_MODEL), row), cache_spec, cache_spec],
        scratch_shapes=_attn_a_scratch(seq, seq),
        input_output_aliases=aliases,
        compiler_params=_params("arbitrary"),
        name="attn_a_ctx",
    )(*args)


def _attn_a_lat(q, k, v, cache_k, cache_v, cache_slot, sg, sink, n_batch, seq):
    kvw = A_KV_HEADS * HEAD_DIM
    nb = seq // A_QBLOCK
    n_sub = A_SUBBLOCKS
    assert nb % n_sub == 0
    steps = nb // n_sub
    own = lambda b, i: (b * steps + i, 0)
    loc = [lambda b, i, t=t: (b * nb + jnp.clip(i * n_sub + t - 1, 0, nb - 1), 0) for t in range(n_sub + 2)]
    loc_specs = [pl.BlockSpec((A_QBLOCK, kvw), m) for m in loc]
    ctx = lambda b, i: (b, cache_slot, 0, 0, 0)
    n_ctx = cache_k.shape[2]
    ctx_spec = pl.BlockSpec((None, None, n_ctx, A_KV_HEADS, HEAD_DIM), ctx)
    ctx_scratch = [pltpu.VMEM((A_KV_HEADS, n_ctx, HEAD_DIM), BF16)] * 2
    q_spec = pl.BlockSpec((n_sub * A_QBLOCK, D_MODEL), own)
    return pl.pallas_call(
        functools.partial(_attn_a_kernel, latent=True, n_qblocks=nb, n_sub=n_sub),
        out_shape=jax.ShapeDtypeStruct(q.shape, BF16),
        grid=(n_batch, steps),
        in_specs=[pl.BlockSpec(memory_space=pltpu.SMEM), q_spec] + loc_specs + loc_specs + [ctx_spec, ctx_spec, q_spec],
        out_specs=q_spec,
        scratch_shapes=_attn_a_scratch(A_QBLOCK, 3 * A_QBLOCK + n_ctx, n_sub) + ctx_scratch,
        compiler_params=_params("arbitrary", "arbitrary"),
        name="attn_a_lat",
    )(sink, q, *([k] * (n_sub + 2)), *([v] * (n_sub + 2)), cache_k, cache_v, sg)


def _attn_b_kernel(*refs, has_ctx, lambda_init, k_chunk):
    if has_ctx:
        (lam_ref, sub_ref, q_ref, k_ref, v_ref, kc_ref, vc_ref, sg_ref, o_ref,
         s_scr, sc_scr, m_scr, l_scr, acc_scr) = refs
    else:
        lam_ref, sub_ref, q_ref, k_ref, v_ref, sg_ref, o_ref, kcache_ref, s_scr, m_scr, l_scr, acc_scr = refs
        kc_ref = vc_ref = sc_scr = None
    dv = 2 * HEAD_DIM
    for head in range(q_ref.shape[1] // dv):
        cols = slice(head * dv, (head + 1) * dv)
        if not has_ctx:
            for half in range(2):
                kcache_ref[:, head, half, :] = k_ref[:, head * dv + half * HEAD_DIM:head * dv + (half + 1) * HEAD_DIM]
        view = lambda ref: None if ref is None else ref.at[:, cols]
        _attn_b_head(lam_ref, sub_ref, view(q_ref), view(k_ref), view(v_ref), view(kc_ref), view(vc_ref),
                     view(sg_ref), view(o_ref), s_scr, sc_scr, m_scr, l_scr, acc_scr,
                     lambda_init=lambda_init, k_chunk=k_chunk)


def _attn_b_head(lam_ref, sub_ref, q_ref, k_ref, v_ref, kc_ref, vc_ref, sg_ref, o_ref,
                 s_scr, sc_scr, m_scr, l_scr, acc_scr, *, lambda_init, k_chunk):
    has_ctx = kc_ref is not None
    n_chunks = k_ref.shape[0] // k_chunk
    halves = [slice(h * HEAD_DIM, (h + 1) * HEAD_DIM) for h in range(2)]

    def fold_max(h, s, first):
        m = _lane_fold(s, jnp.maximum)
        m_scr[h] = m if first else jnp.maximum(m_scr[h], m)

    def scores(c, first=False):
        rows = pl.ds(pl.multiple_of(c * k_chunk, k_chunk), k_chunk)
        k = k_ref[rows, :].astype(BF16)
        for h, hc in enumerate(halves):
            s = _nt_dot(q_ref[:, hc], k[:, hc])
            s_scr[h, c] = s
            fold_max(h, s, first)

    if has_ctx:
        kc = kc_ref[...].astype(BF16)
        for h, hc in enumerate(halves):
            s = _nt_dot(q_ref[:, hc], kc[:, hc])
            sc_scr[h] = s
            fold_max(h, s, True)
        lax.fori_loop(0, n_chunks, lambda c, carry: (scores(c), carry)[1], 0, unroll=B_UNROLL)
    else:
        assert n_chunks == 1
        scores(0, first=True)

    for h in range(2):
        m_scr[h] = jnp.broadcast_to(jnp.max(m_scr[h], axis=-1, keepdims=True), m_scr.shape[1:])

    def accumulate(h, s, v, first):
        m = m_scr[h]
        p = jnp.exp2(s - jnp.concatenate([m] * (s.shape[1] // HEAD_DIM), axis=1))
        l = _lane_fold(p, jnp.add)
        pv = jnp.dot(p.astype(BF16), v, preferred_element_type=F32)
        l_scr[h] = l if first else l_scr[h] + l
        acc_scr[h] = pv if first else acc_scr[h] + pv

    def weighted(c, first=False):
        rows = pl.ds(pl.multiple_of(c * k_chunk, k_chunk), k_chunk)
        v = v_ref[rows, :].astype(BF16)
        for h in range(2):
            accumulate(h, s_scr[h, c], v, first)

    if has_ctx:
        vc = vc_ref[...].astype(BF16)
        for h in range(2):
            accumulate(h, sc_scr[h], vc, True)
        lax.fori_loop(0, n_chunks, lambda c, carry: (weighted(c), carry)[1], 0, unroll=B_UNROLL)
    else:
        weighted(0, first=True)

    lam = lam_ref[...]
    lam_full = (jnp.exp(jnp.sum(lam[0:1] * lam[1:2], axis=-1, keepdims=True))
                - jnp.exp(jnp.sum(lam[2:3] * lam[3:4], axis=-1, keepdims=True)) + lambda_init)
    r0 = 1.0 / jnp.sum(l_scr[0], axis=-1, keepdims=True)
    r1 = lam_full / jnp.sum(l_scr[1], axis=-1, keepdims=True)
    o = acc_scr[0] * r0 - acc_scr[1] * r1
    ms = jnp.mean(o * o, axis=-1, keepdims=True)
    o = o * lax.rsqrt(ms + EPS) * (sub_ref[...] * (1.0 - lambda_init))
    o_ref[...] = (o * sg_ref[...].astype(F32)).astype(o_ref.dtype)


def _attn_b(q, k, v, sg, lam, subln, lambda_init, n_batch, seq, k_ctx=None, v_ctx=None):
    dv = 2 * HEAD_DIM
    tq = min(B_QBLOCK, seq)
    nq = seq // tq
    k_chunk = min(B_KCHUNK, seq)
    has_ctx = k_ctx is not None
    heads_per_step = 1 if has_ctx else B_HEADS
    gw = heads_per_step * dv
    qmap = lambda b, h, i: (b * nq + i, h)
    kvmap = lambda b, h, i: (b, h)
    in_specs = [
        pl.BlockSpec((4, HEAD_DIM), lambda b, h, i: (0, 0)),
        pl.BlockSpec((1, dv), lambda b, h, i: (0, 0)),
        pl.BlockSpec((tq, gw), qmap),
        pl.BlockSpec((seq, gw), kvmap),
        pl.BlockSpec((seq, gw), kvmap),
    ]
    args = [lam, subln.reshape(1, dv), q, k, v]
    scratch = [pltpu.VMEM((2, seq // k_chunk, tq, k_chunk), F32)]
    if has_ctx:
        n_ctx = k_ctx.shape[1]
        in_specs += [pl.BlockSpec((None, n_ctx, gw), lambda b, h, i: (b, 0, h))] * 2
        args += [k_ctx, v_ctx]
        scratch.append(pltpu.VMEM((2, tq, n_ctx), F32))
    in_specs.append(pl.BlockSpec((tq, gw), qmap))
    args.append(sg)
    scratch += [pltpu.VMEM((2, tq, HEAD_DIM), F32), pltpu.VMEM((2, tq, HEAD_DIM), F32),
                pltpu.VMEM((2, tq, dv), F32)]
    out_shape = jax.ShapeDtypeStruct(q.shape, BF16)
    out_specs = pl.BlockSpec((tq, gw), qmap)
    if not has_ctx:
        assert nq == 1 and k.dtype == F32
        out_shape = [out_shape, jax.ShapeDtypeStruct((n_batch, seq, B_HEADS, 2, HEAD_DIM), F32)]
        out_specs = [out_specs, pl.BlockSpec((None, seq, B_HEADS, 2, HEAD_DIM), lambda b, h, i: (b, 0, 0, 0, 0))]
    return pl.pallas_call(
        functools.partial(_attn_b_kernel, has_ctx=has_ctx, lambda_init=lambda_init, k_chunk=k_chunk),
        out_shape=out_shape,
        grid=(n_batch, B_HEADS // heads_per_step, nq),
        in_specs=in_specs,
        out_specs=out_specs,
        scratch_shapes=scratch,
        compiler_params=_params("arbitrary", "arbitrary", "arbitrary"),
        name="attn_b_lat" if has_ctx else "attn_b_ctx",
    )(*args)


def _attn_c_ctx_kernel(q_ref, k_ref, v_ref, sg_ref, o_ref):
    for h in range(C_HEADS):
        hc = slice(h * HEAD_DIM, (h + 1) * HEAD_DIM)
        s = _nt_dot(q_ref[:, hc], k_ref[:, hc].astype(BF16))
        o = _softmax_pv([s], [v_ref[:, hc].astype(BF16)])
        o_ref[:, hc] = (o * sg_ref[:, hc].astype(F32)).astype(o_ref.dtype)


def _attn_c_ctx(q, k, v, sg, n_batch, seq):
    spec = pl.BlockSpec((seq, D_MODEL), lambda b: (b, 0))
    return pl.pallas_call(
        _attn_c_ctx_kernel,
        out_shape=jax.ShapeDtypeStruct(q.shape, BF16),
        grid=(n_batch,),
        in_specs=[spec, spec, spec, spec],
        out_specs=spec,
        compiler_params=_params("arbitrary"),
        name="attn_c_ctx",
    )(q, k, v, sg)


def _na_geometry(rows):
    kh = min(NA_KH, rows)
    key_rows = C_ROWS + kh - 1
    key_rows = min(key_rows + key_rows % 2, rows)
    n_blocks = rows // C_ROWS
    cols = np.arange(GRID_W)
    col_start = np.clip(cols - NA_KW // 2, 0, GRID_W - NA_KW)
    col_ok = (cols[None, :] >= col_start[:, None]) & (cols[None, :] < col_start[:, None] + NA_KW)
    col_delta = cols[None, :] - cols[:, None] + NA_KW - 1
    col_onehot = (col_delta[None] == np.arange(2 * NA_KW - 1)[:, None, None]) & col_ok[None]
    starts, class_ids, classes, keys = [], [], [], {}
    for blk in range(n_blocks):
        r = blk * C_ROWS + np.arange(C_ROWS)
        rs = np.clip(r - kh // 2, 0, rows - kh)
        start = int(np.clip(rs[0], 0, rows - key_rows))
        key = (tuple(rs - r), start - blk * C_ROWS)
        if key not in keys:
            keys[key] = len(classes)
            kr = start + np.arange(key_rows)
            row_ok = (kr[None, :] >= rs[:, None]) & (kr[None, :] < rs[:, None] + kh)
            row_idx = kr[None, :] - r[:, None] + NA_KH - 1
            classes.append((row_ok, row_idx))
        starts.append(start)
        class_ids.append(keys[key])
    geometry = (col_ok, col_onehot.astype(np.float32), classes)
    return key_rows, np.asarray(starts, np.int32), np.asarray(class_ids, np.int32), geometry


def _na_bias_tables(rpb, geometry):
    col_ok, col_onehot, classes = geometry
    n_heads = rpb.shape[0]
    toep = jnp.einsum("hdx,xck->hdck", rpb * LOG2E, jnp.asarray(col_onehot),
                      precision=lax.Precision.HIGHEST)
    toep = jnp.where(jnp.asarray(col_ok)[None, None], toep, NEG_INF)
    n_off = toep.shape[1]
    toep = jnp.concatenate([toep, jnp.full((n_heads, 1, GRID_W, GRID_W), NEG_INF, F32)], axis=1)
    select = np.stack([np.where(row_ok, row_idx, n_off) for row_ok, row_idx in classes])
    n_cls, n_r, n_kr = select.shape
    dnums = lax.GatherDimensionNumbers(offset_dims=(1, 3, 5), collapsed_slice_dims=(1,), start_index_map=(1,))
    tabs = lax.gather(toep, jnp.asarray(select[..., None], jnp.int32), dnums,
                      slice_sizes=(n_heads, 1, GRID_W, GRID_W))
    return tabs.reshape(n_cls, n_heads, n_r * GRID_W, n_kr * GRID_W)


def _attn_c_lat_kernel(*refs, key_rows, n_sub):
    start_ref, cls_ref, q_ref, k_ref, v_ref, kc_ref, vc_ref = refs[:7]
    bias_refs = refs[7:7 + n_sub]
    sg_ref, o_ref, s_scr, m_scr, acc_scr = refs[7 + n_sub:]
    step = pl.program_id(2)
    n_loc = key_rows * GRID_W
    n_keys = s_scr.shape[-1]
    m_rows = q_ref.shape[0] // n_sub
    key_rows_of = [pl.ds(pl.multiple_of(start_ref[step * n_sub + sub] * GRID_W, GRID_W), n_loc)
                   for sub in range(n_sub)]
    sub_rows = [slice(sub * m_rows, (sub + 1) * m_rows) for sub in range(n_sub)]
    head_cols = [slice(h * HEAD_DIM, (h + 1) * HEAD_DIM) for h in range(C_HEAD_GROUP)]
    units = [(sub, h) for sub in range(n_sub) for h in range(C_HEAD_GROUP)]

    for u, (sub, h) in enumerate(units):
        hc = head_cols[h]
        q = q_ref[sub_rows[sub], hc]
        s_loc = _nt_dot(q, k_ref[key_rows_of[sub], hc]) + bias_refs[sub][h]
        s_ctx = _nt_dot(q, kc_ref[:, hc].astype(BF16))
        s_scr[u, :, 0:n_loc] = s_loc
        s_scr[u, :, n_loc:n_keys] = s_ctx
        m_scr[u] = jnp.maximum(_lane_fold(s_loc, jnp.maximum), _lane_fold(s_ctx, jnp.maximum))

    for u in range(len(units)):
        m_scr[u] = jnp.broadcast_to(jnp.max(m_scr[u], axis=-1, keepdims=True), m_scr.shape[1:])

    for u, (sub, h) in enumerate(units):
        hc = head_cols[h]
        p = jnp.exp2(s_scr[u] - jnp.concatenate([m_scr[u]] * (n_keys // HEAD_DIM), axis=1)).astype(BF16)
        acc_scr[u] = (jnp.dot(p[:, 0:n_loc], _with_ones(v_ref[key_rows_of[sub], hc]), preferred_element_type=F32)
                      + jnp.dot(p[:, n_loc:n_keys], _with_ones(vc_ref[:, hc].astype(BF16)),
                                preferred_element_type=F32))

    for u, (sub, h) in enumerate(units):
        hc = head_cols[h]
        acc = acc_scr[u]
        o = acc[:, :HEAD_DIM] * (1.0 / acc[:, HEAD_DIM:])
        o_ref[sub_rows[sub], hc] = (o * sg_ref[sub_rows[sub], hc].astype(F32)).astype(o_ref.dtype)


def _attn_c_lat(q, k, v, k_ctx, v_ctx, sg, rpb, n_batch, seq):
    rows = seq // GRID_W
    key_rows, starts, class_ids, geometry = _na_geometry(rows)
    bias = _na_bias_tables(rpb, geometry)
    n_sub = C_SUBBLOCKS
    nblk = rows // C_ROWS
    assert nblk % n_sub == 0
    steps = nblk // n_sub
    tq = C_ROWS * GRID_W
    gw = C_HEAD_GROUP * HEAD_DIM
    n_units = n_sub * C_HEAD_GROUP
    n_ctx = k_ctx.shape[1]
    qmap = lambda b, g, i, st, cl: (b * steps + i, g)
    kvmap = lambda b, g, i, st, cl: (b, g)
    ctxmap = lambda b, g, i, st, cl: (b, 0, g)
    bias_specs = [pl.BlockSpec((None, C_HEAD_GROUP, tq, key_rows * GRID_W),
                               lambda b, g, i, st, cl, sub=sub: (cl[i * n_sub + sub], g, 0, 0))
                  for sub in range(n_sub)]
    q_spec = pl.BlockSpec((n_sub * tq, gw), qmap)
    grid_spec = pltpu.PrefetchScalarGridSpec(
        num_scalar_prefetch=2,
        grid=(n_batch, C_HEADS // C_HEAD_GROUP, steps),
        in_specs=[
            q_spec,
            pl.BlockSpec((seq, gw), kvmap),
            pl.BlockSpec((seq, gw), kvmap),
            pl.BlockSpec((None, n_ctx, gw), ctxmap),
            pl.BlockSpec((None, n_ctx, gw), ctxmap),
        ] + bias_specs + [q_spec],
        out_specs=q_spec,
        scratch_shapes=[pltpu.VMEM((n_units, tq, key_rows * GRID_W + n_ctx), F32),
                        pltpu.VMEM((n_units, tq, HEAD_DIM), F32),
                        pltpu.VMEM((n_units, tq, 2 * HEAD_DIM), F32)],
    )
    return pl.pallas_call(
        functools.partial(_attn_c_lat_kernel, key_rows=key_rows, n_sub=n_sub),
        out_shape=jax.ShapeDtypeStruct(q.shape, BF16),
        grid_spec=grid_spec,
        compiler_params=_params("arbitrary", "arbitrary", "arbitrary"),
        name="attn_c_lat",
    )(jnp.asarray(starts), jnp.asarray(class_ids), q, k, v, k_ctx, v_ctx, *([bias] * n_sub), sg)


def _rope_tables(n_tokens):
    d4 = HEAD_DIM // 4
    t = jnp.arange(n_tokens)
    pos = jnp.stack([t // GRID_W, t % GRID_W], axis=-1).astype(F32)
    inv_freq = ROPE_BASE ** (-jnp.arange(d4, dtype=F32) / d4)
    ang = pos[:, :, None] * inv_freq
    cos, sin = jnp.cos(ang), jnp.sin(ang)
    zero = jnp.zeros_like(sin[:, 0])
    cos_full = jnp.concatenate([cos[:, 0], cos[:, 0], cos[:, 1], cos[:, 1]], axis=-1)
    sin_up = jnp.concatenate([-sin[:, 0], zero, -sin[:, 1], zero], axis=-1)
    sin_dn = jnp.concatenate([zero, sin[:, 0], zero, sin[:, 1]], axis=-1)
    return cos_full, sin_up, sin_dn


def kernel(x_prompt, x_sample, cache_a_k, cache_a_v, cache_b_k, cache_b_v, cache_c_k, cache_c_v, c, c_ctx,
           ln_g, ada_w, ada_b, w_out, qn_g, kn_g, w_in_a, sink_a, w_in_b, lam_b, subln_b, w_in_c, rpb_c):
    n_p, seq_p, d = x_prompt.shape
    n_s, seq_s, _ = x_sample.shape
    n_ctx = cache_a_k.shape[2]
    xp = x_prompt.reshape(n_p * seq_p, d)
    xs = x_sample.reshape(n_s * seq_s, d)
    rope_tabs = _rope_tables(seq_s)

    pad = (-(n_s + 1)) % 8
    cvec = jnp.concatenate([c, c_ctx[None, :], jnp.zeros((pad, d), F32)], axis=0)
    mod = _ada_modulation(cvec, ada_w, ada_b)
    mod = mod.reshape(DEPTH, mod.shape[1], 3, 1, d)

    new_kv = {1: ([], []), 2: ([], [])}
    a_caches = None
    n_a_layers = len(range(0, DEPTH, N_MIXERS))
    for l in range(DEPTH):
        kind, j = l % N_MIXERS, l // N_MIXERS
        sh_s, sc_s, gt_s = (mod[l, :n_s, t] for t in range(3))
        sh_p, sc_p, gt_p = (mod[l, n_s:n_s + 1, t] for t in range(3))
        w_o = w_out[l].astype(BF16)
        if kind == 0:
            kvw = A_KV_HEADS * HEAD_DIM
            w_in, widths, tn = w_in_a[j].astype(BF16), (d, kvw, kvw, d), 512
        elif kind == 1:
            w_in, widths, tn = w_in_b[j].astype(BF16), (d, d, d, d), 512
        else:
            w_in, widths, tn = w_in_c[j].astype(BF16), (d, d, d, d), 512
        rope = rope_tabs if kind != 2 else None
        qp, kp, vp, gp = _in_projection(xp, sh_p, sc_p, ln_g[l], w_in, qn_g[l], kn_g[l], widths, None, F32, tn)
        qs, ks, vs, gs = _in_projection(xs, sh_s, sc_s, ln_g[l], w_in, qn_g[l], kn_g[l], widths, rope, BF16, tn)
        if kind == 0:
            op, *a_caches = _attn_a_ctx(qp, kp, vp, gp, sink_a[j], n_p, seq_p, j, n_a_layers, a_caches)
            os_ = _attn_a_lat(qs, ks, vs, cache_a_k, cache_a_v, j, gs, sink_a[j], n_s, seq_s)
        elif kind == 1:
            lambda_init = 0.8 - 0.6 * math.exp(-0.3 * l)
            kc = cache_b_k[:, j].reshape(n_s, n_ctx, d)
            vc = cache_b_v[:, j].reshape(n_s, n_ctx, d)
            op, k_new = _attn_b(qp, kp, vp, gp, lam_b[j], subln_b[j], lambda_init, n_p, seq_p)
            os_ = _attn_b(qs, ks, vs, gs, lam_b[j], subln_b[j], lambda_init, n_s, seq_s, kc, vc)
            new_kv[1][0].append(k_new)
            new_kv[1][1].append(vp.reshape(n_p, seq_p, B_HEADS, 2 * HEAD_DIM))
        else:
            kc = cache_c_k[:, j].reshape(n_s, n_ctx, d)
            vc = cache_c_v[:, j].reshape(n_s, n_ctx, d)
            op = _attn_c_ctx(qp, kp, vp, gp, n_p, seq_p)
            os_ = _attn_c_lat(qs, ks, vs, kc, vc, gs, rpb_c[j], n_s, seq_s)
            new_kv[2][0].append(kp.reshape(n_p, seq_p, C_HEADS, HEAD_DIM))
            new_kv[2][1].append(vp.reshape(n_p, seq_p, C_HEADS, HEAD_DIM))
        xp = _out_projection(op, w_o, xp, gt_p)
        xs = _out_projection(os_, w_o, xs, gt_s)

    outs = [xp.reshape(x_prompt.shape), xs.reshape(x_sample.shape)] + list(a_caches)
    for kind in (1, 2):
        outs.append(jnp.stack(new_kv[kind][0], axis=1))
        outs.append(jnp.stack(new_kv[kind][1], axis=1))
    return tuple(outs)
```

```python
import functools
import math

import numpy as np
import jax
import jax.numpy as jnp
from jax import lax
from jax.experimental import pallas as pl
from jax.experimental.pallas import tpu as pltpu

D_MODEL = 2048
DEPTH = 4
GRID_W = 64
HEAD_DIM = 128
ROPE_BASE = 10000.0
NEG_INF = -1e30
N_MIXERS = 3
A_HEADS = D_MODEL // HEAD_DIM
A_KV_HEADS = A_HEADS // 4
A_GROUP = A_HEADS // A_KV_HEADS
WINDOW = 128
B_HEADS = D_MODEL // (2 * HEAD_DIM)
C_HEADS = D_MODEL // HEAD_DIM
NA_KH = 8
NA_KW = 16
SCALE = HEAD_DIM ** -0.5
LOG2E = math.log2(math.e)
Q_SCALE = SCALE * LOG2E
EPS = 1e-6

BF16 = jnp.bfloat16
F32 = jnp.float32

VMEM_LIMIT_BYTES = 56 * 1024 * 1024
ADA_COL_TILE = 768
ROW_TILE = 1024
PROJ_COL_TILE = 512
OUT_ROW_TILE = 512
OUT_SUB = 512
PROJ_SUB = 128
A_QBLOCK = 128
A_SUBBLOCKS = 4
B_QBLOCK = 1024
B_KCHUNK = 512
B_UNROLL = 4
C_ROWS = 4
C_HEAD_GROUP = 4
C_SUBBLOCKS = 2


def _nt_dot(a, b):
    return lax.dot_general(a, b, (((1,), (1,)), ((), ())), preferred_element_type=F32)


def _params(*semantics):
    return pltpu.CompilerParams(dimension_semantics=semantics, vmem_limit_bytes=VMEM_LIMIT_BYTES)


def _ada_kernel(c_ref, w_ref, b_ref, o_ref):
    c = c_ref[...]
    a = (c * jax.nn.sigmoid(c)).astype(BF16)
    o_ref[...] = jnp.dot(a, w_ref[...].astype(BF16), preferred_element_type=F32) + b_ref[...]


def _ada_modulation(cvec, ada_w, ada_b):
    rows = cvec.shape[0]
    tn = ADA_COL_TILE
    return pl.pallas_call(
        _ada_kernel,
        out_shape=jax.ShapeDtypeStruct((DEPTH, rows, 3 * D_MODEL), F32),
        grid=(DEPTH, 3 * D_MODEL // tn),
        in_specs=[
            pl.BlockSpec((rows, D_MODEL), lambda l, j: (0, 0)),
            pl.BlockSpec((None, D_MODEL, tn), lambda l, j: (l, 0, j)),
            pl.BlockSpec((None, 1, tn), lambda l, j: (l, 0, j)),
        ],
        out_specs=pl.BlockSpec((None, rows, tn), lambda l, j: (l, 0, j)),
        compiler_params=_params("arbitrary", "arbitrary"),
        name="ada_modulation",
    )(cvec, ada_w, ada_b.reshape(DEPTH, 1, 3 * D_MODEL))


def _inproj_kernel(*refs, n_q, rope, heads_per_tile):
    x_ref, sh_ref, sc_ref, lng_ref, w1_ref, w2_ref, qg_ref, kg_ref = refs[:8]
    if rope:
        cos_ref, sa_ref, sb_ref = refs[8:11]
    q_out, k_out, v_out, g_out, h_scr = refs[-5:]
    j = pl.program_id(1)

    def modulated_norm(rows):
        x = x_ref[rows, :]
        ms = jnp.mean(x * x, axis=-1, keepdims=True)
        gain = lng_ref[...] * (1.0 + sc_ref[...])
        return (x * lax.rsqrt(ms + EPS) * gain + sh_ref[...]).astype(BF16)

    def normed(acc, rows, gain_ref, scale):
        outs = []
        for t in range(heads_per_tile):
            y = acc[:, t * HEAD_DIM:(t + 1) * HEAD_DIM]
            ms = jnp.mean(y * y, axis=-1, keepdims=True)
            y = y * lax.rsqrt(ms + EPS) * (gain_ref[...] * scale)
            if rope:
                y = (y * cos_ref[rows, :] + pltpu.roll(y, 96, 1) * sa_ref[rows, :]
                     + pltpu.roll(y, 32, 1) * sb_ref[rows, :])
            outs.append(y)
        return jnp.concatenate(outs, axis=-1)

    def project(w_ref, out_ref, epilogue, fill_h=False):
        for r in range(h_scr.shape[0] // PROJ_SUB):
            rows = slice(r * PROJ_SUB, (r + 1) * PROJ_SUB)
            if fill_h:
                h_scr[rows, :] = modulated_norm(rows)
            acc = jnp.dot(h_scr[rows, :], w_ref[...], preferred_element_type=F32)
            out_ref[rows, :] = epilogue(acc, rows).astype(out_ref.dtype)

    def q_and_gate(fill_h):
        project(w1_ref, q_out, lambda acc, rows: normed(acc, rows, qg_ref, Q_SCALE), fill_h)
        project(w2_ref, g_out, lambda acc, rows: acc * jax.nn.sigmoid(acc))

    @pl.when(j == 0)
    def _():
        q_and_gate(True)

    @pl.when((j > 0) & (j < n_q))
    def _():
        q_and_gate(False)

    @pl.when(j >= n_q)
    def _():
        project(w1_ref, k_out, lambda acc, rows: normed(acc, rows, kg_ref, 1.0))
        project(w2_ref, v_out, lambda acc, rows: acc)


def _in_projection(x, shift, scale, ln_g, w, qg, kg, widths, rope_tabs, kv_dtype, tn):
    t_rows = x.shape[0]
    qw, kw, vw, gw = widths
    assert qw == gw and kw == vw
    n_q, n_k = qw // tn, kw // tn
    tm = min(ROW_TILE, t_rows)
    rows_per_mod = t_rows // shift.shape[0]
    assert t_rows % tm == 0 and rows_per_mod % tm == 0
    rope = rope_tabs is not None
    g_col0, v_col0 = (qw + kw + vw) // tn, (qw + kw) // tn

    def mod_map(i, j):
        return ((i * tm) // rows_per_mod, 0, 0)

    in_specs = [
        pl.BlockSpec((tm, D_MODEL), lambda i, j: (i, 0)),
        pl.BlockSpec((None, 1, D_MODEL), mod_map),
        pl.BlockSpec((None, 1, D_MODEL), mod_map),
        pl.BlockSpec((1, D_MODEL), lambda i, j: (0, 0)),
        pl.BlockSpec((D_MODEL, tn), lambda i, j: (0, j)),
        pl.BlockSpec((D_MODEL, tn), lambda i, j: (0, jnp.where(j < n_q, g_col0 + j, v_col0 + j - n_q))),
        pl.BlockSpec((1, HEAD_DIM), lambda i, j: (0, 0)),
        pl.BlockSpec((1, HEAD_DIM), lambda i, j: (0, 0)),
    ]
    args = [x, shift, scale, ln_g.reshape(1, D_MODEL), w, w, qg.reshape(1, HEAD_DIM), kg.reshape(1, HEAD_DIM)]
    if rope:
        n_pos_tiles = rope_tabs[0].shape[0] // tm
        for tab in rope_tabs:
            in_specs.append(pl.BlockSpec((tm, HEAD_DIM), lambda i, j: (i % n_pos_tiles, 0)))
            args.append(tab)
    out_specs = [
        pl.BlockSpec((tm, tn), lambda i, j: (i, jnp.minimum(j, n_q - 1))),
        pl.BlockSpec((tm, tn), lambda i, j: (i, jnp.maximum(j - n_q, 0))),
        pl.BlockSpec((tm, tn), lambda i, j: (i, jnp.maximum(j - n_q, 0))),
        pl.BlockSpec((tm, tn), lambda i, j: (i, jnp.minimum(j, n_q - 1))),
    ]
    out_shape = [
        jax.ShapeDtypeStruct((t_rows, qw), BF16),
        jax.ShapeDtypeStruct((t_rows, kw), kv_dtype),
        jax.ShapeDtypeStruct((t_rows, vw), kv_dtype),
        jax.ShapeDtypeStruct((t_rows, gw), BF16),
    ]
    kern = functools.partial(_inproj_kernel, n_q=n_q, rope=rope, heads_per_tile=tn // HEAD_DIM)
    return pl.pallas_call(
        kern,
        out_shape=out_shape,
        grid=(t_rows // tm, n_q + n_k),
        in_specs=in_specs,
        out_specs=out_specs,
        scratch_shapes=[pltpu.VMEM((tm, D_MODEL), BF16)],
        compiler_params=_params("arbitrary", "arbitrary"),
        name="in_projection",
    )(*args)


def _outproj_kernel(o_ref, w_ref, x_ref, gt_ref, y_ref):
    for c in range(D_MODEL // OUT_SUB):
        cols = slice(c * OUT_SUB, (c + 1) * OUT_SUB)
        y = jnp.dot(o_ref[...], w_ref[:, cols], preferred_element_type=F32)
        y_ref[:, cols] = x_ref[:, cols] + gt_ref[:, cols] * y


def _out_projection(o, w, x, gate):
    t_rows = x.shape[0]
    tm = min(OUT_ROW_TILE, t_rows)
    rows_per_mod = t_rows // gate.shape[0]
    assert t_rows % tm == 0 and rows_per_mod % tm == 0
    row = lambda i: (i, 0)
    return pl.pallas_call(
        _outproj_kernel,
        out_shape=jax.ShapeDtypeStruct((t_rows, D_MODEL), F32),
        grid=(t_rows // tm,),
        in_specs=[
            pl.BlockSpec((tm, D_MODEL), row),
            pl.BlockSpec((D_MODEL, D_MODEL), lambda i: (0, 0)),
            pl.BlockSpec((tm, D_MODEL), row),
            pl.BlockSpec((None, 1, D_MODEL), lambda i: ((i * tm) // rows_per_mod, 0, 0)),
        ],
        out_specs=pl.BlockSpec((tm, D_MODEL), row),
        compiler_params=_params("arbitrary"),
        name="out_projection",
    )(o, w, x, gate)


def _lane_fold(x, op):
    r = x[:, 0:HEAD_DIM]
    for t in range(1, x.shape[1] // HEAD_DIM):
        r = op(r, x[:, t * HEAD_DIM:(t + 1) * HEAD_DIM])
    return r


def _with_ones(v):
    return jnp.concatenate([v, jnp.ones(v.shape, v.dtype)], axis=1)


def _softmax_pv(scores, values):
    m128 = _lane_fold(scores[0], jnp.maximum)
    for s in scores[1:]:
        m128 = jnp.maximum(m128, _lane_fold(s, jnp.maximum))
    m = jnp.broadcast_to(jnp.max(m128, axis=-1, keepdims=True), m128.shape)
    acc = None
    for s, v in zip(scores, values):
        p = jnp.exp2(s - jnp.concatenate([m] * (s.shape[1] // HEAD_DIM), axis=1)).astype(BF16)
        pv = jnp.dot(p, _with_ones(v), preferred_element_type=F32)
        acc = pv if acc is None else acc + pv
    return acc[:, :HEAD_DIM] * (1.0 / acc[:, HEAD_DIM:])


def _attn_a_kernel(*refs, latent, n_qblocks, n_sub):
    if latent:
        n_loc_blocks = n_sub + 2
        sink_ref, q_ref = refs[:2]
        k_loc_refs = refs[2:2 + n_loc_blocks]
        v_loc_refs = refs[2 + n_loc_blocks:2 + 2 * n_loc_blocks]
        kc_ref, vc_ref, sg_ref, o_ref, s_scr, m_scr, acc_scr, kc_scr, vc_scr = refs[2 + 2 * n_loc_blocks:]
    else:
        sink_ref, q_ref, k_ref, v_ref, sg_ref = refs[:5]
        o_ref, kcache_ref, vcache_ref, s_scr, m_scr, acc_scr = refs[-6:]
        for kv in range(A_KV_HEADS):
            kcache_ref[:, kv, :] = k_ref[:, kv * HEAD_DIM:(kv + 1) * HEAD_DIM]
            vcache_ref[:, kv, :] = v_ref[:, kv * HEAD_DIM:(kv + 1) * HEAD_DIM]
    m_rows = q_ref.shape[0] // n_sub
    n_keys = s_scr.shape[-1]
    n_loc = 3 * A_QBLOCK
    kv_cols = [slice(kv * HEAD_DIM, (kv + 1) * HEAD_DIM) for kv in range(A_KV_HEADS)]
    kv_heads = [[kv * A_GROUP + g for g in range(A_GROUP)] for kv in range(A_KV_HEADS)]
    units = [(sub, kv) for sub in range(n_sub) for kv in range(A_KV_HEADS)]
    sub_rows = [slice(sub * m_rows, (sub + 1) * m_rows) for sub in range(n_sub)]

    if latent:
        step = pl.program_id(1)
        qpos = lax.broadcasted_iota(jnp.int32, (A_QBLOCK, n_loc), 0)
        col = lax.broadcasted_iota(jnp.int32, (A_QBLOCK, n_loc), 1)
        in_window = jnp.abs(col - A_QBLOCK - qpos) <= WINDOW
        window_bias = []
        for sub in range(n_sub):
            kpos = (step * n_sub + sub - 1) * A_QBLOCK + col
            valid = in_window & (kpos >= 0) & (kpos < n_qblocks * A_QBLOCK)
            window_bias.append(jnp.concatenate([jnp.where(valid, 0.0, NEG_INF)] * A_GROUP, axis=0))

        @pl.when(step == 0)
        def _():
            for kv in range(A_KV_HEADS):
                kc_scr[kv] = kc_ref[:, kv, :].astype(BF16)
                vc_scr[kv] = vc_ref[:, kv, :].astype(BF16)

    for u, (sub, kv) in enumerate(units):
        cols = kv_cols[kv]
        q = jnp.concatenate([q_ref[sub_rows[sub], h * HEAD_DIM:(h + 1) * HEAD_DIM] for h in kv_heads[kv]], axis=0)
        if latent:
            k_loc = jnp.concatenate([ref[:, cols] for ref in k_loc_refs[sub:sub + 3]], axis=0)
            s_loc = _nt_dot(q, k_loc) + window_bias[sub]
            s_ctx = _nt_dot(q, kc_scr[kv])
            s_scr[u, :, 0:n_loc] = s_loc
            s_scr[u, :, n_loc:n_keys] = s_ctx
            m_scr[u] = jnp.maximum(_lane_fold(s_loc, jnp.maximum), _lane_fold(s_ctx, jnp.maximum))
        else:
            s = _nt_dot(q, k_ref[:, cols].astype(BF16))
            s_scr[u] = s
            m_scr[u] = _lane_fold(s, jnp.maximum)

    def sink_lanes(kv):
        return jnp.concatenate([jnp.full((m_rows, HEAD_DIM), sink_ref[h] * LOG2E, F32) for h in kv_heads[kv]], axis=0)

    for u, (sub, kv) in enumerate(units):
        m = jnp.max(m_scr[u], axis=-1, keepdims=True)
        m_scr[u] = jnp.maximum(jnp.broadcast_to(m, m_scr.shape[1:]), sink_lanes(kv))

    for u, (sub, kv) in enumerate(units):
        cols = kv_cols[kv]
        p = jnp.exp2(s_scr[u] - jnp.concatenate([m_scr[u]] * (n_keys // HEAD_DIM), axis=1)).astype(BF16)
        if latent:
            v_loc = jnp.concatenate([ref[:, cols] for ref in v_loc_refs[sub:sub + 3]], axis=0)
            acc_scr[u] = (jnp.dot(p[:, 0:n_loc], _with_ones(v_loc), preferred_element_type=F32)
                          + jnp.dot(p[:, n_loc:n_keys], _with_ones(vc_scr[kv]), preferred_element_type=F32))
        else:
            acc_scr[u] = jnp.dot(p, _with_ones(v_ref[:, cols].astype(BF16)), preferred_element_type=F32)

    for u, (sub, kv) in enumerate(units):
        acc = acc_scr[u]
        l = acc[:, HEAD_DIM:] + jnp.exp2(sink_lanes(kv) - m_scr[u])
        o = acc[:, :HEAD_DIM] * (1.0 / l)
        for g, h in enumerate(kv_heads[kv]):
            hc = slice(h * HEAD_DIM, (h + 1) * HEAD_DIM)
            gate = sg_ref[sub_rows[sub], hc].astype(F32)
            o_ref[sub_rows[sub], hc] = (o[g * m_rows:(g + 1) * m_rows] * gate).astype(o_ref.dtype)


def _attn_a_scratch(m_rows, n_keys, n_sub=1):
    stacked, n_units = A_GROUP * m_rows, n_sub * A_KV_HEADS
    return [pltpu.VMEM((n_units, stacked, n_keys), F32), pltpu.VMEM((n_units, stacked, HEAD_DIM), F32),
            pltpu.VMEM((n_units, stacked, 2 * HEAD_DIM), F32)]


def _attn_a_ctx(q, k, v, sg, sink, n_batch, seq, slot, n_slots, caches=None):
    kvw = A_KV_HEADS * HEAD_DIM
    row = lambda b: (b, 0)
    cache_shape = jax.ShapeDtypeStruct((n_batch, n_slots, seq, A_KV_HEADS, HEAD_DIM), F32)
    cache_spec = pl.BlockSpec((None, None, seq, A_KV_HEADS, HEAD_DIM), lambda b: (b, slot, 0, 0, 0))
    in_specs = [
        pl.BlockSpec(memory_space=pltpu.SMEM),
        pl.BlockSpec((seq, D_MODEL), row),
        pl.BlockSpec((seq, kvw), row),
        pl.BlockSpec((seq, kvw), row),
        pl.BlockSpec((seq, D_MODEL), row),
    ]
    args = [sink, q, k, v, sg]
    aliases = {}
    if caches is not None:
        aliases = {len(args): 1, len(args) + 1: 2}
        in_specs += [pl.BlockSpec(memory_space=pl.ANY)] * 2
        args += list(caches)
    return pl.pallas_call(
        functools.partial(_attn_a_kernel, latent=False, n_qblocks=1, n_sub=1),
        out_shape=[jax.ShapeDtypeStruct(q.shape, BF16), cache_shape, cache_shape],
        grid=(n_batch,),
        in_specs=in_specs,
        out_specs=[pl.BlockSpec((seq, D_MODEL), row), cache_spec, cache_spec],
        scratch_shapes=_attn_a_scratch(seq, seq),
        input_output_aliases=aliases,
        compiler_params=_params("arbitrary"),
        name="attn_a_ctx",
    )(*args)


def _attn_a_lat(q, k, v, cache_k, cache_v, cache_slot, sg, sink, n_batch, seq):
    kvw = A_KV_HEADS * HEAD_DIM
    nb = seq // A_QBLOCK
    n_sub = A_SUBBLOCKS
    assert nb % n_sub == 0
    steps = nb // n_sub
    own = lambda b, i: (b * steps + i, 0)
    loc = [lambda b, i, t=t: (b * nb + jnp.clip(i * n_sub + t - 1, 0, nb - 1), 0) for t in range(n_sub + 2)]
    loc_specs = [pl.BlockSpec((A_QBLOCK, kvw), m) for m in loc]
    ctx = lambda b, i: (b, cache_slot, 0, 0, 0)
    n_ctx = cache_k.shape[2]
    ctx_spec = pl.BlockSpec((None, None, n_ctx, A_KV_HEADS, HEAD_DIM), ctx)
    ctx_scratch = [pltpu.VMEM((A_KV_HEADS, n_ctx, HEAD_DIM), BF16)] * 2
    q_spec = pl.BlockSpec((n_sub * A_QBLOCK, D_MODEL), own)
    return pl.pallas_call(
        functools.partial(_attn_a_kernel, latent=True, n_qblocks=nb, n_sub=n_sub),
        out_shape=jax.ShapeDtypeStruct(q.shape, BF16),
        grid=(n_batch, steps),
        in_specs=[pl.BlockSpec(memory_space=pltpu.SMEM), q_spec] + loc_specs + loc_specs + [ctx_spec, ctx_spec, q_spec],
        out_specs=q_spec,
        scratch_shapes=_attn_a_scratch(A_QBLOCK, 3 * A_QBLOCK + n_ctx, n_sub) + ctx_scratch,
        compiler_params=_params("arbitrary", "arbitrary"),
        name="attn_a_lat",
    )(sink, q, *([k] * (n_sub + 2)), *([v] * (n_sub + 2)), cache_k, cache_v, sg)


def _attn_b_kernel(*refs, has_ctx, lambda_init, k_chunk):
    if has_ctx:
        (lam_ref, sub_ref, q_ref, k_ref, v_ref, kc_ref, vc_ref, sg_ref, o_ref,
         s_scr, sc_scr, m_scr, l_scr, acc_scr) = refs
    else:
        lam_ref, sub_ref, q_ref, k_ref, v_ref, sg_ref, o_ref, kcache_ref, s_scr, m_scr, l_scr, acc_scr = refs
        kc_ref = vc_ref = sc_scr = None
    dv = 2 * HEAD_DIM
    for head in range(q_ref.shape[1] // dv):
        cols = slice(head * dv, (head + 1) * dv)
        if not has_ctx:
            for half in range(2):
                kcache_ref[:, head, half, :] = k_ref[:, head * dv + half * HEAD_DIM:head * dv + (half + 1) * HEAD_DIM]
        view = lambda ref: None if ref is None else ref.at[:, cols]
        _attn_b_head(lam_ref, sub_ref, view(q_ref), view(k_ref), view(v_ref), view(kc_ref), view(vc_ref),
                     view(sg_ref), view(o_ref), s_scr, sc_scr, m_scr, l_scr, acc_scr,
                     lambda_init=lambda_init, k_chunk=k_chunk)


def _attn_b_head(lam_ref, sub_ref, q_ref, k_ref, v_ref, kc_ref, vc_ref, sg_ref, o_ref,
                 s_scr, sc_scr, m_scr, l_scr, acc_scr, *, lambda_init, k_chunk):
    has_ctx = kc_ref is not None
    n_chunks = k_ref.shape[0] // k_chunk
    halves = [slice(h * HEAD_DIM, (h + 1) * HEAD_DIM) for h in range(2)]

    def fold_max(h, s, first):
        m = _lane_fold(s, jnp.maximum)
        m_scr[h] = m if first else jnp.maximum(m_scr[h], m)

    def scores(c, first=False):
        rows = pl.ds(pl.multiple_of(c * k_chunk, k_chunk), k_chunk)
        k = k_ref[rows, :].astype(BF16)
        for h, hc in enumerate(halves):
            s = _nt_dot(q_ref[:, hc], k[:, hc])
            s_scr[h, c] = s
            fold_max(h, s, first)

    if has_ctx:
        kc = kc_ref[...].astype(BF16)
        for h, hc in enumerate(halves):
            s = _nt_dot(q_ref[:, hc], kc[:, hc])
            sc_scr[h] = s
            fold_max(h, s, True)
        lax.fori_loop(0, n_chunks, lambda c, carry: (scores(c), carry)[1], 0, unroll=B_UNROLL)
    else:
        assert n_chunks == 1
        scores(0, first=True)

    for h in range(2):
        m_scr[h] = jnp.broadcast_to(jnp.max(m_scr[h], axis=-1, keepdims=True), m_scr.shape[1:])

    def accumulate(h, s, v, first):
        m = m_scr[h]
        p = jnp.exp2(s - jnp.concatenate([m] * (s.shape[1] // HEAD_DIM), axis=1))
        l = _lane_fold(p, jnp.add)
        pv = jnp.dot(p.astype(BF16), v, preferred_element_type=F32)
        l_scr[h] = l if first else l_scr[h] + l
        acc_scr[h] = pv if first else acc_scr[h] + pv

    def weighted(c, first=False):
        rows = pl.ds(pl.multiple_of(c * k_chunk, k_chunk), k_chunk)
        v = v_ref[rows, :].astype(BF16)
        for h in range(2):
            accumulate(h, s_scr[h, c], v, first)

    if has_ctx:
        vc = vc_ref[...].astype(BF16)
        for h in range(2):
            accumulate(h, sc_scr[h], vc, True)
        lax.fori_loop(0, n_chunks, lambda c, carry: (weighted(c), carry)[1], 0, unroll=B_UNROLL)
    else:
        weighted(0, first=True)

    lam = lam_ref[...]
    lam_full = (jnp.exp(jnp.sum(lam[0:1] * lam[1:2], axis=-1, keepdims=True))
                - jnp.exp(jnp.sum(lam[2:3] * lam[3:4], axis=-1, keepdims=True)) + lambda_init)
    r0 = 1.0 / jnp.sum(l_scr[0], axis=-1, keepdims=True)
    r1 = lam_full / jnp.sum(l_scr[1], axis=-1, keepdims=True)
    o = acc_scr[0] * r0 - acc_scr[1] * r1
    ms = jnp.mean(o * o, axis=-1, keepdims=True)
    o = o * lax.rsqrt(ms + EPS) * (sub_ref[...] * (1.0 - lambda_init))
    o_ref[...] = (o * sg_ref[...].astype(F32)).astype(o_ref.dtype)


def _attn_b(q, k, v, sg, lam, subln, lambda_init, n_batch, seq, k_ctx=None, v_ctx=None):
    dv = 2 * HEAD_DIM
    tq = min(B_QBLOCK, seq)
    nq = seq // tq
    k_chunk = min(B_KCHUNK, seq)
    has_ctx = k_ctx is not None
    heads_per_step = 1 if has_ctx else B_HEADS
    gw = heads_per_step * dv
    qmap = lambda b, h, i: (b * nq + i, h)
    kvmap = lambda b, h, i: (b, h)
    in_specs = [
        pl.BlockSpec((4, HEAD_DIM), lambda b, h, i: (0, 0)),
        pl.BlockSpec((1, dv), lambda b, h, i: (0, 0)),
        pl.BlockSpec((tq, gw), qmap),
        pl.BlockSpec((seq, gw), kvmap),
        pl.BlockSpec((seq, gw), kvmap),
    ]
    args = [lam, subln.reshape(1, dv), q, k, v]
    scratch = [pltpu.VMEM((2, seq // k_chunk, tq, k_chunk), F32)]
    if has_ctx:
        n_ctx = k_ctx.shape[1]
        in_specs += [pl.BlockSpec((None, n_ctx, gw), lambda b, h, i: (b, 0, h))] * 2
        args += [k_ctx, v_ctx]
        scratch.append(pltpu.VMEM((2, tq, n_ctx), F32))
    in_specs.append(pl.BlockSpec((tq, gw), qmap))
    args.append(sg)
    scratch += [pltpu.VMEM((2, tq, HEAD_DIM), F32), pltpu.VMEM((2, tq, HEAD_DIM), F32),
                pltpu.VMEM((2, tq, dv), F32)]
    out_shape = jax.ShapeDtypeStruct(q.shape, BF16)
    out_specs = pl.BlockSpec((tq, gw), qmap)
    if not has_ctx:
        assert nq == 1 and k.dtype == F32
        out_shape = [out_shape, jax.ShapeDtypeStruct((n_batch, seq, B_HEADS, 2, HEAD_DIM), F32)]
        out_specs = [out_specs, pl.BlockSpec((None, seq, B_HEADS, 2, HEAD_DIM), lambda b, h, i: (b, 0, 0, 0, 0))]
    return pl.pallas_call(
        functools.partial(_attn_b_kernel, has_ctx=has_ctx, lambda_init=lambda_init, k_chunk=k_chunk),
        out_shape=out_shape,
        grid=(n_batch, B_HEADS // heads_per_step, nq),
        in_specs=in_specs,
        out_specs=out_specs,
        scratch_shapes=scratch,
        compiler_params=_params("arbitrary", "arbitrary", "arbitrary"),
        name="attn_b_lat" if has_ctx else "attn_b_ctx",
    )(*args)


def _attn_c_ctx_kernel(q_ref, k_ref, v_ref, sg_ref, o_ref):
    for h in range(C_HEADS):
        hc = slice(h * HEAD_DIM, (h + 1) * HEAD_DIM)
        s = _nt_dot(q_ref[:, hc], k_ref[:, hc].astype(BF16))
        o = _softmax_pv([s], [v_ref[:, hc].astype(BF16)])
        o_ref[:, hc] = (o * sg_ref[:, hc].astype(F32)).astype(o_ref.dtype)


def _attn_c_ctx(q, k, v, sg, n_batch, seq):
    spec = pl.BlockSpec((seq, D_MODEL), lambda b: (b, 0))
    return pl.pallas_call(
        _attn_c_ctx_kernel,
        out_shape=jax.ShapeDtypeStruct(q.shape, BF16),
        grid=(n_batch,),
        in_specs=[spec, spec, spec, spec],
        out_specs=spec,
        compiler_params=_params("arbitrary"),
        name="attn_c_ctx",
    )(q, k, v, sg)


def _na_geometry(rows):
    kh = min(NA_KH, rows)
    key_rows = C_ROWS + kh - 1
    key_rows = min(key_rows + key_rows % 2, rows)
    n_blocks = rows // C_ROWS
    cols = np.arange(GRID_W)
    col_start = np.clip(cols - NA_KW // 2, 0, GRID_W - NA_KW)
    col_ok = (cols[None, :] >= col_start[:, None]) & (cols[None, :] < col_start[:, None] + NA_KW)
    col_delta = cols[None, :] - cols[:, None] + NA_KW - 1
    col_onehot = (col_delta[None] == np.arange(2 * NA_KW - 1)[:, None, None]) & col_ok[None]
    starts, class_ids, classes, keys = [], [], [], {}
    for blk in range(n_blocks):
        r = blk * C_ROWS + np.arange(C_ROWS)
        rs = np.clip(r - kh // 2, 0, rows - kh)
        start = int(np.clip(rs[0], 0, rows - key_rows))
        key = (tuple(rs - r), start - blk * C_ROWS)
        if key not in keys:
            keys[key] = len(classes)
            kr = start + np.arange(key_rows)
            row_ok = (kr[None, :] >= rs[:, None]) & (kr[None, :] < rs[:, None] + kh)
            row_idx = kr[None, :] - r[:, None] + NA_KH - 1
            classes.append((row_ok, row_idx))
        starts.append(start)
        class_ids.append(keys[key])
    geometry = (col_ok, col_onehot.astype(np.float32), classes)
    return key_rows, np.asarray(starts, np.int32), np.asarray(class_ids, np.int32), geometry


def _na_bias_tables(rpb, geometry):
    col_ok, col_onehot, classes = geometry
    n_heads = rpb.shape[0]
    toep = jnp.einsum("hdx,xck->hdck", rpb * LOG2E, jnp.asarray(col_onehot),
                      precision=lax.Precision.HIGHEST)
    toep = jnp.where(jnp.asarray(col_ok)[None, None], toep, NEG_INF)
    n_off = toep.shape[1]
    toep = jnp.concatenate([toep, jnp.full((n_heads, 1, GRID_W, GRID_W), NEG_INF, F32)], axis=1)
    select = np.stack([np.where(row_ok, row_idx, n_off) for row_ok, row_idx in classes])
    n_cls, n_r, n_kr = select.shape
    dnums = lax.GatherDimensionNumbers(offset_dims=(1, 3, 5), collapsed_slice_dims=(1,), start_index_map=(1,))
    tabs = lax.gather(toep, jnp.asarray(select[..., None], jnp.int32), dnums,
                      slice_sizes=(n_heads, 1, GRID_W, GRID_W))
    return tabs.reshape(n_cls, n_heads, n_r * GRID_W, n_kr * GRID_W)


def _attn_c_lat_kernel(*refs, key_rows, n_sub):
    start_ref, cls_ref, q_ref, k_ref, v_ref, kc_ref, vc_ref = refs[:7]
    bias_refs = refs[7:7 + n_sub]
    sg_ref, o_ref, s_scr, m_scr, acc_scr = refs[7 + n_sub:]
    step = pl.program_id(2)
    n_loc = key_rows * GRID_W
    n_keys = s_scr.shape[-1]
    m_rows = q_ref.shape[0] // n_sub
    key_rows_of = [pl.ds(pl.multiple_of(start_ref[step * n_sub + sub] * GRID_W, GRID_W), n_loc)
                   for sub in range(n_sub)]
    sub_rows = [slice(sub * m_rows, (sub + 1) * m_rows) for sub in range(n_sub)]
    head_cols = [slice(h * HEAD_DIM, (h + 1) * HEAD_DIM) for h in range(C_HEAD_GROUP)]
    units = [(sub, h) for sub in range(n_sub) for h in range(C_HEAD_GROUP)]

    for u, (sub, h) in enumerate(units):
        hc = head_cols[h]
        q = q_ref[sub_rows[sub], hc]
        s_loc = _nt_dot(q, k_ref[key_rows_of[sub], hc]) + bias_refs[sub][h]
        s_ctx = _nt_dot(q, kc_ref[:, hc].astype(BF16))
        s_scr[u, :, 0:n_loc] = s_loc
        s_scr[u, :, n_loc:n_keys] = s_ctx
        m_scr[u] = jnp.maximum(_lane_fold(s_loc, jnp.maximum), _lane_fold(s_ctx, jnp.maximum))

    for u in range(len(units)):
        m_scr[u] = jnp.broadcast_to(jnp.max(m_scr[u], axis=-1, keepdims=True), m_scr.shape[1:])

    for u, (sub, h) in enumerate(units):
        hc = head_cols[h]
        p = jnp.exp2(s_scr[u] - jnp.concatenate([m_scr[u]] * (n_keys // HEAD_DIM), axis=1)).astype(BF16)
        acc_scr[u] = (jnp.dot(p[:, 0:n_loc], _with_ones(v_ref[key_rows_of[sub], hc]), preferred_element_type=F32)
                      + jnp.dot(p[:, n_loc:n_keys], _with_ones(vc_ref[:, hc].astype(BF16)),
                                preferred_element_type=F32))

    for u, (sub, h) in enumerate(units):
        hc = head_cols[h]
        acc = acc_scr[u]
        o = acc[:, :HEAD_DIM] * (1.0 / acc[:, HEAD_DIM:])
        o_ref[sub_rows[sub], hc] = (o * sg_ref[sub_rows[sub], hc].astype(F32)).astype(o_ref.dtype)


def _attn_c_lat(q, k, v, k_ctx, v_ctx, sg, rpb, n_batch, seq):
    rows = seq // GRID_W
    key_rows, starts, class_ids, geometry = _na_geometry(rows)
    bias = _na_bias_tables(rpb, geometry)
    n_sub = C_SUBBLOCKS
    nblk = rows // C_ROWS
    assert nblk % n_sub == 0
    steps = nblk // n_sub
    tq = C_ROWS * GRID_W
    gw = C_HEAD_GROUP * HEAD_DIM
    n_units = n_sub * C_HEAD_GROUP
    n_ctx = k_ctx.shape[1]
    qmap = lambda b, g, i, st, cl: (b * steps + i, g)
    kvmap = lambda b, g, i, st, cl: (b, g)
    ctxmap = lambda b, g, i, st, cl: (b, 0, g)
    bias_specs = [pl.BlockSpec((None, C_HEAD_GROUP, tq, key_rows * GRID_W),
                               lambda b, g, i, st, cl, sub=sub: (cl[i * n_sub + sub], g, 0, 0))
                  for sub in range(n_sub)]
    q_spec = pl.BlockSpec((n_sub * tq, gw), qmap)
    grid_spec = pltpu.PrefetchScalarGridSpec(
        num_scalar_prefetch=2,
        grid=(n_batch, C_HEADS // C_HEAD_GROUP, steps),
        in_specs=[
            q_spec,
            pl.BlockSpec((seq, gw), kvmap),
            pl.BlockSpec((seq, gw), kvmap),
            pl.BlockSpec((None, n_ctx, gw), ctxmap),
            pl.BlockSpec((None, n_ctx, gw), ctxmap),
        ] + bias_specs + [q_spec],
        out_specs=q_spec,
        scratch_shapes=[pltpu.VMEM((n_units, tq, key_rows * GRID_W + n_ctx), F32),
                        pltpu.VMEM((n_units, tq, HEAD_DIM), F32),
                        pltpu.VMEM((n_units, tq, 2 * HEAD_DIM), F32)],
    )
    return pl.pallas_call(
        functools.partial(_attn_c_lat_kernel, key_rows=key_rows, n_sub=n_sub),
        out_shape=jax.ShapeDtypeStruct(q.shape, BF16),
        grid_spec=grid_spec,
        compiler_params=_params("arbitrary", "arbitrary", "arbitrary"),
        name="attn_c_lat",
    )(jnp.asarray(starts), jnp.asarray(class_ids), q, k, v, k_ctx, v_ctx, *([bias] * n_sub), sg)


def _rope_tables(n_tokens):
    d4 = HEAD_DIM // 4
    t = jnp.arange(n_tokens)
    pos = jnp.stack([t // GRID_W, t % GRID_W], axis=-1).astype(F32)
    inv_freq = ROPE_BASE ** (-jnp.arange(d4, dtype=F32) / d4)
    ang = pos[:, :, None] * inv_freq
    cos, sin = jnp.cos(ang), jnp.sin(ang)
    zero = jnp.zeros_like(sin[:, 0])
    cos_full = jnp.concatenate([cos[:, 0], cos[:, 0], cos[:, 1], cos[:, 1]], axis=-1)
    sin_up = jnp.concatenate([-sin[:, 0], zero, -sin[:, 1], zero], axis=-1)
    sin_dn = jnp.concatenate([zero, sin[:, 0], zero, sin[:, 1]], axis=-1)
    return cos_full, sin_up, sin_dn


def kernel(x_prompt, x_sample, cache_a_k, cache_a_v, cache_b_k, cache_b_v, cache_c_k, cache_c_v, c, c_ctx,
           ln_g, ada_w, ada_b, w_out, qn_g, kn_g, w_in_a, sink_a, w_in_b, lam_b, subln_b, w_in_c, rpb_c):
    n_p, seq_p, d = x_prompt.shape
    n_s, seq_s, _ = x_sample.shape
    n_ctx = cache_a_k.shape[2]
    xp = x_prompt.reshape(n_p * seq_p, d)
    xs = x_sample.reshape(n_s * seq_s, d)
    rope_tabs = _rope_tables(seq_s)

    pad = (-(n_s + 1)) % 8
    cvec = jnp.concatenate([c, c_ctx[None, :], jnp.zeros((pad, d), F32)], axis=0)
    mod = _ada_modulation(cvec, ada_w, ada_b)
    mod = mod.reshape(DEPTH, mod.shape[1], 3, 1, d)

    new_kv = {1: ([], []), 2: ([], [])}
    a_caches = None
    n_a_layers = len(range(0, DEPTH, N_MIXERS))
    for l in range(DEPTH):
        kind, j = l % N_MIXERS, l // N_MIXERS
        sh_s, sc_s, gt_s = (mod[l, :n_s, t] for t in range(3))
        sh_p, sc_p, gt_p = (mod[l, n_s:n_s + 1, t] for t in range(3))
        w_o = w_out[l].astype(BF16)
        if kind == 0:
            kvw = A_KV_HEADS * HEAD_DIM
            w_in, widths = w_in_a[j].astype(BF16), (d, kvw, kvw, d)
        elif kind == 1:
            w_in, widths = w_in_b[j].astype(BF16), (d, d, d, d)
        else:
            w_in, widths = w_in_c[j].astype(BF16), (d, d, d, d)
        tn = PROJ_COL_TILE
        rope = rope_tabs if kind != 2 else None
        qp, kp, vp, gp = _in_projection(xp, sh_p, sc_p, ln_g[l], w_in, qn_g[l], kn_g[l], widths, None, F32, tn)
        qs, ks, vs, gs = _in_projection(xs, sh_s, sc_s, ln_g[l], w_in, qn_g[l], kn_g[l], widths, rope, BF16, tn)
        if kind == 0:
            op, *a_caches = _attn_a_ctx(qp, kp, vp, gp, sink_a[j], n_p, seq_p, j, n_a_layers, a_caches)
            os_ = _attn_a_lat(qs, ks, vs, cache_a_k, cache_a_v, j, gs, sink_a[j], n_s, seq_s)
        elif kind == 1:
            lambda_init = 0.8 - 0.6 * math.exp(-0.3 * l)
            kc = cache_b_k[:, j].reshape(n_s, n_ctx, d)
            vc = cache_b_v[:, j].reshape(n_s, n_ctx, d)
            op, k_new = _attn_b(qp, kp, vp, gp, lam_b[j], subln_b[j], lambda_init, n_p, seq_p)
            os_ = _attn_b(qs, ks, vs, gs, lam_b[j], subln_b[j], lambda_init, n_s, seq_s, kc, vc)
            new_kv[1][0].append(k_new)
            new_kv[1][1].append(vp.reshape(n_p, seq_p, B_HEADS, 2 * HEAD_DIM))
        else:
            kc = cache_c_k[:, j].reshape(n_s, n_ctx, d)
            vc = cache_c_v[:, j].reshape(n_s, n_ctx, d)
            op = _attn_c_ctx(qp, kp, vp, gp, n_p, seq_p)
            os_ = _attn_c_lat(qs, ks, vs, kc, vc, gs, rpb_c[j], n_s, seq_s)
            new_kv[2][0].append(kp.reshape(n_p, seq_p, C_HEADS, HEAD_DIM))
            new_kv[2][1].append(vp.reshape(n_p, seq_p, C_HEADS, HEAD_DIM))
        xp = _out_projection(op, w_o, xp, gt_p)
        xs = _out_projection(os_, w_o, xs, gt_s)

    outs = [xp.reshape(x_prompt.shape), xs.reshape(x_sample.shape)] + list(a_caches)
    for kind in (1, 2):
        outs.append(jnp.stack(new_kv[kind][0], axis=1))
        outs.append(jnp.stack(new_kv[kind][1], axis=1))
    return tuple(outs)
```

```python
import functools
import math

import numpy as np
import jax
import jax.numpy as jnp
from jax import lax
from jax.experimental import pallas as pl
from jax.experimental.pallas import tpu as pltpu

D_MODEL = 2048
DEPTH = 4
GRID_W = 64
HEAD_DIM = 128
ROPE_BASE = 10000.0
NEG_INF = -1e30
N_MIXERS = 3
A_HEADS = D_MODEL // HEAD_DIM
A_KV_HEADS = A_HEADS // 4
A_GROUP = A_HEADS // A_KV_HEADS
WINDOW = 128
B_HEADS = D_MODEL // (2 * HEAD_DIM)
C_HEADS = D_MODEL // HEAD_DIM
NA_KH = 8
NA_KW = 16
SCALE = HEAD_DIM ** -0.5
LOG2E = math.log2(math.e)
Q_SCALE = SCALE * LOG2E
EPS = 1e-6

BF16 = jnp.bfloat16
F32 = jnp.float32

VMEM_LIMIT_BYTES = 56 * 1024 * 1024
ADA_COL_TILE = 768
ROW_TILE = 1024
PROJ_COL_TILE = 512
OUT_ROW_TILE = 1024
OUT_SUB = 512
PROJ_SUB = 128
A_QBLOCK = 128
A_SUBBLOCKS = 4
B_QBLOCK = 1024
B_KCHUNK = 512
B_UNROLL = 4
C_ROWS = 4
C_HEAD_GROUP = 4
C_SUBBLOCKS = 2


def _nt_dot(a, b):
    return lax.dot_general(a, b, (((1,), (1,)), ((), ())), preferred_element_type=F32)


def _params(*semantics):
    return pltpu.CompilerParams(dimension_semantics=semantics, vmem_limit_bytes=VMEM_LIMIT_BYTES)


def _ada_kernel(c_ref, w_ref, b_ref, o_ref):
    c = c_ref[...]
    a = (c * jax.nn.sigmoid(c)).astype(BF16)
    o_ref[...] = jnp.dot(a, w_ref[...].astype(BF16), preferred_element_type=F32) + b_ref[...]


def _ada_modulation(cvec, ada_w, ada_b):
    rows = cvec.shape[0]
    tn = ADA_COL_TILE
    return pl.pallas_call(
        _ada_kernel,
        out_shape=jax.ShapeDtypeStruct((DEPTH, rows, 3 * D_MODEL), F32),
        grid=(DEPTH, 3 * D_MODEL // tn),
        in_specs=[
            pl.BlockSpec((rows, D_MODEL), lambda l, j: (0, 0)),
            pl.BlockSpec((None, D_MODEL, tn), lambda l, j: (l, 0, j)),
            pl.BlockSpec((None, 1, tn), lambda l, j: (l, 0, j)),
        ],
        out_specs=pl.BlockSpec((None, rows, tn), lambda l, j: (l, 0, j)),
        compiler_params=_params("arbitrary", "arbitrary"),
        name="ada_modulation",
    )(cvec, ada_w, ada_b.reshape(DEPTH, 1, 3 * D_MODEL))


def _inproj_kernel(*refs, n_q, rope, heads_per_tile):
    x_ref, sh_ref, sc_ref, lng_ref, w1_ref, w2_ref, qg_ref, kg_ref = refs[:8]
    if rope:
        cos_ref, sa_ref, sb_ref = refs[8:11]
    q_out, k_out, v_out, g_out, h_scr = refs[-5:]
    j = pl.program_id(1)

    def modulated_norm(rows):
        x = x_ref[rows, :]
        ms = jnp.mean(x * x, axis=-1, keepdims=True)
        gain = lng_ref[...] * (1.0 + sc_ref[...])
        return (x * lax.rsqrt(ms + EPS) * gain + sh_ref[...]).astype(BF16)

    def normed(acc, rows, gain_ref, scale):
        outs = []
        for t in range(heads_per_tile):
            y = acc[:, t * HEAD_DIM:(t + 1) * HEAD_DIM]
            ms = jnp.mean(y * y, axis=-1, keepdims=True)
            y = y * lax.rsqrt(ms + EPS) * (gain_ref[...] * scale)
            if rope:
                y = (y * cos_ref[rows, :] + pltpu.roll(y, 96, 1) * sa_ref[rows, :]
                     + pltpu.roll(y, 32, 1) * sb_ref[rows, :])
            outs.append(y)
        return jnp.concatenate(outs, axis=-1)

    def project(w_ref, out_ref, epilogue, fill_h=False):
        for r in range(h_scr.shape[0] // PROJ_SUB):
            rows = slice(r * PROJ_SUB, (r + 1) * PROJ_SUB)
            if fill_h:
                h_scr[rows, :] = modulated_norm(rows)
            acc = jnp.dot(h_scr[rows, :], w_ref[...], preferred_element_type=F32)
            out_ref[rows, :] = epilogue(acc, rows).astype(out_ref.dtype)

    def q_and_gate(fill_h):
        project(w1_ref, q_out, lambda acc, rows: normed(acc, rows, qg_ref, Q_SCALE), fill_h)
        project(w2_ref, g_out, lambda acc, rows: acc * jax.nn.sigmoid(acc))

    @pl.when(j == 0)
    def _():
        q_and_gate(True)

    @pl.when((j > 0) & (j < n_q))
    def _():
        q_and_gate(False)

    @pl.when(j >= n_q)
    def _():
        project(w1_ref, k_out, lambda acc, rows: normed(acc, rows, kg_ref, 1.0))
        project(w2_ref, v_out, lambda acc, rows: acc)


def _in_projection(x, shift, scale, ln_g, w, qg, kg, widths, rope_tabs, kv_dtype, tn):
    t_rows = x.shape[0]
    qw, kw, vw, gw = widths
    assert qw == gw and kw == vw
    n_q, n_k = qw // tn, kw // tn
    tm = min(ROW_TILE, t_rows)
    rows_per_mod = t_rows // shift.shape[0]
    assert t_rows % tm == 0 and rows_per_mod % tm == 0
    rope = rope_tabs is not None
    g_col0, v_col0 = (qw + kw + vw) // tn, (qw + kw) // tn

    def mod_map(i, j):
        return ((i * tm) // rows_per_mod, 0, 0)

    in_specs = [
        pl.BlockSpec((tm, D_MODEL), lambda i, j: (i, 0)),
        pl.BlockSpec((None, 1, D_MODEL), mod_map),
        pl.BlockSpec((None, 1, D_MODEL), mod_map),
        pl.BlockSpec((1, D_MODEL), lambda i, j: (0, 0)),
        pl.BlockSpec((D_MODEL, tn), lambda i, j: (0, j)),
        pl.BlockSpec((D_MODEL, tn), lambda i, j: (0, jnp.where(j < n_q, g_col0 + j, v_col0 + j - n_q))),
        pl.BlockSpec((1, HEAD_DIM), lambda i, j: (0, 0)),
        pl.BlockSpec((1, HEAD_DIM), lambda i, j: (0, 0)),
    ]
    args = [x, shift, scale, ln_g.reshape(1, D_MODEL), w, w, qg.reshape(1, HEAD_DIM), kg.reshape(1, HEAD_DIM)]
    if rope:
        n_pos_tiles = rope_tabs[0].shape[0] // tm
        for tab in rope_tabs:
            in_specs.append(pl.BlockSpec((tm, HEAD_DIM), lambda i, j: (i % n_pos_tiles, 0)))
            args.append(tab)
    out_specs = [
        pl.BlockSpec((tm, tn), lambda i, j: (i, jnp.minimum(j, n_q - 1))),
        pl.BlockSpec((tm, tn), lambda i, j: (i, jnp.maximum(j - n_q, 0))),
        pl.BlockSpec((tm, tn), lambda i, j: (i, jnp.maximum(j - n_q, 0))),
        pl.BlockSpec((tm, tn), lambda i, j: (i, jnp.minimum(j, n_q - 1))),
    ]
    out_shape = [
        jax.ShapeDtypeStruct((t_rows, qw), BF16),
        jax.ShapeDtypeStruct((t_rows, kw), kv_dtype),
        jax.ShapeDtypeStruct((t_rows, vw), kv_dtype),
        jax.ShapeDtypeStruct((t_rows, gw), BF16),
    ]
    kern = functools.partial(_inproj_kernel, n_q=n_q, rope=rope, heads_per_tile=tn // HEAD_DIM)
    return pl.pallas_call(
        kern,
        out_shape=out_shape,
        grid=(t_rows // tm, n_q + n_k),
        in_specs=in_specs,
        out_specs=out_specs,
        scratch_shapes=[pltpu.VMEM((tm, D_MODEL), BF16)],
        compiler_params=_params("arbitrary", "arbitrary"),
        name="in_projection",
    )(*args)


def _outproj_kernel(o_ref, w_ref, x_ref, gt_ref, y_ref):
    for c in range(D_MODEL // OUT_SUB):
        cols = slice(c * OUT_SUB, (c + 1) * OUT_SUB)
        y = jnp.dot(o_ref[...], w_ref[:, cols], preferred_element_type=F32)
        y_ref[:, cols] = x_ref[:, cols] + gt_ref[:, cols] * y


def _out_projection(o, w, x, gate):
    t_rows = x.shape[0]
    tm = min(OUT_ROW_TILE, t_rows)
    rows_per_mod = t_rows // gate.shape[0]
    assert t_rows % tm == 0 and rows_per_mod % tm == 0
    row = lambda i: (i, 0)
    return pl.pallas_call(
        _outproj_kernel,
        out_shape=jax.ShapeDtypeStruct((t_rows, D_MODEL), F32),
        grid=(t_rows // tm,),
        in_specs=[
            pl.BlockSpec((tm, D_MODEL), row),
            pl.BlockSpec((D_MODEL, D_MODEL), lambda i: (0, 0)),
            pl.BlockSpec((tm, D_MODEL), row),
            pl.BlockSpec((None, 1, D_MODEL), lambda i: ((i * tm) // rows_per_mod, 0, 0)),
        ],
        out_specs=pl.BlockSpec((tm, D_MODEL), row),
        compiler_params=_params("arbitrary"),
        name="out_projection",
    )(o, w, x, gate)


def _lane_fold(x, op):
    r = x[:, 0:HEAD_DIM]
    for t in range(1, x.shape[1] // HEAD_DIM):
        r = op(r, x[:, t * HEAD_DIM:(t + 1) * HEAD_DIM])
    return r


def _with_ones(v):
    return jnp.concatenate([v, jnp.ones(v.shape, v.dtype)], axis=1)


def _softmax_pv(scores, values):
    m128 = _lane_fold(scores[0], jnp.maximum)
    for s in scores[1:]:
        m128 = jnp.maximum(m128, _lane_fold(s, jnp.maximum))
    m = jnp.broadcast_to(jnp.max(m128, axis=-1, keepdims=True), m128.shape)
    acc = None
    for s, v in zip(scores, values):
        p = jnp.exp2(s - jnp.concatenate([m] * (s.shape[1] // HEAD_DIM), axis=1)).astype(BF16)
        pv = jnp.dot(p, _with_ones(v), preferred_element_type=F32)
        acc = pv if acc is None else acc + pv
    return acc[:, :HEAD_DIM] * (1.0 / acc[:, HEAD_DIM:])


def _attn_a_kernel(*refs, latent, n_qblocks, n_sub):
    if latent:
        n_loc_blocks = n_sub + 2
        sink_ref, q_ref = refs[:2]
        k_loc_refs = refs[2:2 + n_loc_blocks]
        v_loc_refs = refs[2 + n_loc_blocks:2 + 2 * n_loc_blocks]
        kc_ref, vc_ref, sg_ref, o_ref, s_scr, m_scr, acc_scr, kc_scr, vc_scr = refs[2 + 2 * n_loc_blocks:]
    else:
        sink_ref, q_ref, k_ref, v_ref, sg_ref = refs[:5]
        o_ref, kcache_ref, vcache_ref, s_scr, m_scr, acc_scr = refs[-6:]
        for kv in range(A_KV_HEADS):
            kcache_ref[:, kv, :] = k_ref[:, kv * HEAD_DIM:(kv + 1) * HEAD_DIM]
            vcache_ref[:, kv, :] = v_ref[:, kv * HEAD_DIM:(kv + 1) * HEAD_DIM]
    m_rows = q_ref.shape[0] // n_sub
    n_keys = s_scr.shape[-1]
    n_loc = 3 * A_QBLOCK
    kv_cols = [slice(kv * HEAD_DIM, (kv + 1) * HEAD_DIM) for kv in range(A_KV_HEADS)]
    kv_heads = [[kv * A_GROUP + g for g in range(A_GROUP)] for kv in range(A_KV_HEADS)]
    units = [(sub, kv) for sub in range(n_sub) for kv in range(A_KV_HEADS)]
    sub_rows = [slice(sub * m_rows, (sub + 1) * m_rows) for sub in range(n_sub)]

    if latent:
        step = pl.program_id(1)
        qpos = lax.broadcasted_iota(jnp.int32, (A_QBLOCK, n_loc), 0)
        col = lax.broadcasted_iota(jnp.int32, (A_QBLOCK, n_loc), 1)
        in_window = jnp.abs(col - A_QBLOCK - qpos) <= WINDOW
        window_bias = []
        for sub in range(n_sub):
            kpos = (step * n_sub + sub - 1) * A_QBLOCK + col
            valid = in_window & (kpos >= 0) & (kpos < n_qblocks * A_QBLOCK)
            window_bias.append(jnp.concatenate([jnp.where(valid, 0.0, NEG_INF)] * A_GROUP, axis=0))

        @pl.when(step == 0)
        def _():
            for kv in range(A_KV_HEADS):
                kc_scr[kv] = kc_ref[:, kv, :].astype(BF16)
                vc_scr[kv] = vc_ref[:, kv, :].astype(BF16)

    for u, (sub, kv) in enumerate(units):
        cols = kv_cols[kv]
        q = jnp.concatenate([q_ref[sub_rows[sub], h * HEAD_DIM:(h + 1) * HEAD_DIM] for h in kv_heads[kv]], axis=0)
        if latent:
            k_loc = jnp.concatenate([ref[:, cols] for ref in k_loc_refs[sub:sub + 3]], axis=0)
            s_loc = _nt_dot(q, k_loc) + window_bias[sub]
            s_ctx = _nt_dot(q, kc_scr[kv])
            s_scr[u, :, 0:n_loc] = s_loc
            s_scr[u, :, n_loc:n_keys] = s_ctx
            m_scr[u] = jnp.maximum(_lane_fold(s_loc, jnp.maximum), _lane_fold(s_ctx, jnp.maximum))
        else:
            s = _nt_dot(q, k_ref[:, cols].astype(BF16))
            s_scr[u] = s
            m_scr[u] = _lane_fold(s, jnp.maximum)

    def sink_lanes(kv):
        return jnp.concatenate([jnp.full((m_rows, HEAD_DIM), sink_ref[h] * LOG2E, F32) for h in kv_heads[kv]], axis=0)

    for u, (sub, kv) in enumerate(units):
        m = jnp.max(m_scr[u], axis=-1, keepdims=True)
        m_scr[u] = jnp.maximum(jnp.broadcast_to(m, m_scr.shape[1:]), sink_lanes(kv))

    for u, (sub, kv) in enumerate(units):
        cols = kv_cols[kv]
        p = jnp.exp2(s_scr[u] - jnp.concatenate([m_scr[u]] * (n_keys // HEAD_DIM), axis=1)).astype(BF16)
        if latent:
            v_loc = jnp.concatenate([ref[:, cols] for ref in v_loc_refs[sub:sub + 3]], axis=0)
            acc_scr[u] = (jnp.dot(p[:, 0:n_loc], _with_ones(v_loc), preferred_element_type=F32)
                          + jnp.dot(p[:, n_loc:n_keys], _with_ones(vc_scr[kv]), preferred_element_type=F32))
        else:
            acc_scr[u] = jnp.dot(p, _with_ones(v_ref[:, cols].astype(BF16)), preferred_element_type=F32)

    for u, (sub, kv) in enumerate(units):
        acc = acc_scr[u]
        l = acc[:, HEAD_DIM:] + jnp.exp2(sink_lanes(kv) - m_scr[u])
        o = acc[:, :HEAD_DIM] * (1.0 / l)
        for g, h in enumerate(kv_heads[kv]):
            hc = slice(h * HEAD_DIM, (h + 1) * HEAD_DIM)
            gate = sg_ref[sub_rows[sub], hc].astype(F32)
            o_ref[sub_rows[sub], hc] = (o[g * m_rows:(g + 1) * m_rows] * gate).astype(o_ref.dtype)


def _attn_a_scratch(m_rows, n_keys, n_sub=1):
    stacked, n_units = A_GROUP * m_rows, n_sub * A_KV_HEADS
    return [pltpu.VMEM((n_units, stacked, n_keys), F32), pltpu.VMEM((n_units, stacked, HEAD_DIM), F32),
            pltpu.VMEM((n_units, stacked, 2 * HEAD_DIM), F32)]


def _attn_a_ctx(q, k, v, sg, sink, n_batch, seq, slot, n_slots, caches=None):
    kvw = A_KV_HEADS * HEAD_DIM
    row = lambda b: (b, 0)
    cache_shape = jax.ShapeDtypeStruct((n_batch, n_slots, seq, A_KV_HEADS, HEAD_DIM), F32)
    cache_spec = pl.BlockSpec((None, None, seq, A_KV_HEADS, HEAD_DIM), lambda b: (b, slot, 0, 0, 0))
    in_specs = [
        pl.BlockSpec(memory_space=pltpu.SMEM),
        pl.BlockSpec((seq, D_MODEL), row),
        pl.BlockSpec((seq, kvw), row),
        pl.BlockSpec((seq, kvw), row),
        pl.BlockSpec((seq, D_MODEL), row),
    ]
    args = [sink, q, k, v, sg]
    aliases = {}
    if caches is not None:
        aliases = {len(args): 1, len(args) + 1: 2}
        in_specs += [pl.BlockSpec(memory_space=pl.ANY)] * 2
        args += list(caches)
    return pl.pallas_call(
        functools.partial(_attn_a_kernel, latent=False, n_qblocks=1, n_sub=1),
        out_shape=[jax.ShapeDtypeStruct(q.shape, BF16), cache_shape, cache_shape],
        grid=(n_batch,),
        in_specs=in_specs,
        out_specs=[pl.BlockSpec((seq, D_MODEL), row), cache_spec, cache_spec],
        scratch_shapes=_attn_a_scratch(seq, seq),
        input_output_aliases=aliases,
        compiler_params=_params("arbitrary"),
        name="attn_a_ctx",
    )(*args)


def _attn_a_lat(q, k, v, cache_k, cache_v, cache_slot, sg, sink, n_batch, seq):
    kvw = A_KV_HEADS * HEAD_DIM
    nb = seq // A_QBLOCK
    n_sub = A_SUBBLOCKS
    assert nb % n_sub == 0
    steps = nb // n_sub
    own = lambda b, i: (b * steps + i, 0)
    loc = [lambda b, i, t=t: (b * nb + jnp.clip(i * n_sub + t - 1, 0, nb - 1), 0) for t in range(n_sub + 2)]
    loc_specs = [pl.BlockSpec((A_QBLOCK, kvw), m) for m in loc]
    ctx = lambda b, i: (b, cache_slot, 0, 0, 0)
    n_ctx = cache_k.shape[2]
    ctx_spec = pl.BlockSpec((None, None, n_ctx, A_KV_HEADS, HEAD_DIM), ctx)
    ctx_scratch = [pltpu.VMEM((A_KV_HEADS, n_ctx, HEAD_DIM), BF16)] * 2
    q_spec = pl.BlockSpec((n_sub * A_QBLOCK, D_MODEL), own)
    return pl.pallas_call(
        functools.partial(_attn_a_kernel, latent=True, n_qblocks=nb, n_sub=n_sub),
        out_shape=jax.ShapeDtypeStruct(q.shape, BF16),
        grid=(n_batch, steps),
        in_specs=[pl.BlockSpec(memory_space=pltpu.SMEM), q_spec] + loc_specs + loc_specs + [ctx_spec, ctx_spec, q_spec],
        out_specs=q_spec,
        scratch_shapes=_attn_a_scratch(A_QBLOCK, 3 * A_QBLOCK + n_ctx, n_sub) + ctx_scratch,
        compiler_params=_params("arbitrary", "arbitrary"),
        name="attn_a_lat",
    )(sink, q, *([k] * (n_sub + 2)), *([v] * (n_sub + 2)), cache_k, cache_v, sg)


def _attn_b_kernel(*refs, has_ctx, lambda_init, k_chunk):
    if has_ctx:
        (lam_ref, sub_ref, q_ref, k_ref, v_ref, kc_ref, vc_ref, sg_ref, o_ref,
         s_scr, sc_scr, m_scr, l_scr, acc_scr) = refs
    else:
        lam_ref, sub_ref, q_ref, k_ref, v_ref, sg_ref, o_ref, kcache_ref, s_scr, m_scr, l_scr, acc_scr = refs
        kc_ref = vc_ref = sc_scr = None
    dv = 2 * HEAD_DIM
    for head in range(q_ref.shape[1] // dv):
        cols = slice(head * dv, (head + 1) * dv)
        if not has_ctx:
            for half in range(2):
                kcache_ref[:, head, half, :] = k_ref[:, head * dv + half * HEAD_DIM:head * dv + (half + 1) * HEAD_DIM]
        view = lambda ref: None if ref is None else ref.at[:, cols]
        _attn_b_head(lam_ref, sub_ref, view(q_ref), view(k_ref), view(v_ref), view(kc_ref), view(vc_ref),
                     view(sg_ref), view(o_ref), s_scr, sc_scr, m_scr, l_scr, acc_scr,
                     lambda_init=lambda_init, k_chunk=k_chunk)


def _attn_b_head(lam_ref, sub_ref, q_ref, k_ref, v_ref, kc_ref, vc_ref, sg_ref, o_ref,
                 s_scr, sc_scr, m_scr, l_scr, acc_scr, *, lambda_init, k_chunk):
    has_ctx = kc_ref is not None
    n_chunks = k_ref.shape[0] // k_chunk
    halves = [slice(h * HEAD_DIM, (h + 1) * HEAD_DIM) for h in range(2)]

    def fold_max(h, s, first):
        m = _lane_fold(s, jnp.maximum)
        m_scr[h] = m if first else jnp.maximum(m_scr[h], m)

    def scores(c, first=False):
        rows = pl.ds(pl.multiple_of(c * k_chunk, k_chunk), k_chunk)
        k = k_ref[rows, :].astype(BF16)
        for h, hc in enumerate(halves):
            s = _nt_dot(q_ref[:, hc], k[:, hc])
            s_scr[h, c] = s
            fold_max(h, s, first)

    if has_ctx:
        kc = kc_ref[...].astype(BF16)
        for h, hc in enumerate(halves):
            s = _nt_dot(q_ref[:, hc], kc[:, hc])
            sc_scr[h] = s
            fold_max(h, s, True)
        lax.fori_loop(0, n_chunks, lambda c, carry: (scores(c), carry)[1], 0, unroll=B_UNROLL)
    else:
        assert n_chunks == 1
        scores(0, first=True)

    for h in range(2):
        m_scr[h] = jnp.broadcast_to(jnp.max(m_scr[h], axis=-1, keepdims=True), m_scr.shape[1:])

    def accumulate(h, s, v, first):
        m = m_scr[h]
        p = jnp.exp2(s - jnp.concatenate([m] * (s.shape[1] // HEAD_DIM), axis=1))
        l = _lane_fold(p, jnp.add)
        pv = jnp.dot(p.astype(BF16), v, preferred_element_type=F32)
        l_scr[h] = l if first else l_scr[h] + l
        acc_scr[h] = pv if first else acc_scr[h] + pv

    def weighted(c, first=False):
        rows = pl.ds(pl.multiple_of(c * k_chunk, k_chunk), k_chunk)
        v = v_ref[rows, :].astype(BF16)
        for h in range(2):
            accumulate(h, s_scr[h, c], v, first)

    if has_ctx:
        vc = vc_ref[...].astype(BF16)
        for h in range(2):
            accumulate(h, sc_scr[h], vc, True)
        lax.fori_loop(0, n_chunks, lambda c, carry: (weighted(c), carry)[1], 0, unroll=B_UNROLL)
    else:
        weighted(0, first=True)

    lam = lam_ref[...]
    lam_full = (jnp.exp(jnp.sum(lam[0:1] * lam[1:2], axis=-1, keepdims=True))
                - jnp.exp(jnp.sum(lam[2:3] * lam[3:4], axis=-1, keepdims=True)) + lambda_init)
    r0 = 1.0 / jnp.sum(l_scr[0], axis=-1, keepdims=True)
    r1 = lam_full / jnp.sum(l_scr[1], axis=-1, keepdims=True)
    o = acc_scr[0] * r0 - acc_scr[1] * r1
    ms = jnp.mean(o * o, axis=-1, keepdims=True)
    o = o * lax.rsqrt(ms + EPS) * (sub_ref[...] * (1.0 - lambda_init))
    o_ref[...] = (o * sg_ref[...].astype(F32)).astype(o_ref.dtype)


def _attn_b(q, k, v, sg, lam, subln, lambda_init, n_batch, seq, k_ctx=None, v_ctx=None):
    dv = 2 * HEAD_DIM
    tq = min(B_QBLOCK, seq)
    nq = seq // tq
    k_chunk = min(B_KCHUNK, seq)
    has_ctx = k_ctx is not None
    heads_per_step = 1 if has_ctx else B_HEADS
    gw = heads_per_step * dv
    qmap = lambda b, h, i: (b * nq + i, h)
    kvmap = lambda b, h, i: (b, h)
    in_specs = [
        pl.BlockSpec((4, HEAD_DIM), lambda b, h, i: (0, 0)),
        pl.BlockSpec((1, dv), lambda b, h, i: (0, 0)),
        pl.BlockSpec((tq, gw), qmap),
        pl.BlockSpec((seq, gw), kvmap),
        pl.BlockSpec((seq, gw), kvmap),
    ]
    args = [lam, subln.reshape(1, dv), q, k, v]
    scratch = [pltpu.VMEM((2, seq // k_chunk, tq, k_chunk), F32)]
    if has_ctx:
        n_ctx = k_ctx.shape[1]
        in_specs += [pl.BlockSpec((None, n_ctx, gw), lambda b, h, i: (b, 0, h))] * 2
        args += [k_ctx, v_ctx]
        scratch.append(pltpu.VMEM((2, tq, n_ctx), F32))
    in_specs.append(pl.BlockSpec((tq, gw), qmap))
    args.append(sg)
    scratch += [pltpu.VMEM((2, tq, HEAD_DIM), F32), pltpu.VMEM((2, tq, HEAD_DIM), F32),
                pltpu.VMEM((2, tq, dv), F32)]
    out_shape = jax.ShapeDtypeStruct(q.shape, BF16)
    out_specs = pl.BlockSpec((tq, gw), qmap)
    if not has_ctx:
        assert nq == 1 and k.dtype == F32
        out_shape = [out_shape, jax.ShapeDtypeStruct((n_batch, seq, B_HEADS, 2, HEAD_DIM), F32)]
        out_specs = [out_specs, pl.BlockSpec((None, seq, B_HEADS, 2, HEAD_DIM), lambda b, h, i: (b, 0, 0, 0, 0))]
    return pl.pallas_call(
        functools.partial(_attn_b_kernel, has_ctx=has_ctx, lambda_init=lambda_init, k_chunk=k_chunk),
        out_shape=out_shape,
        grid=(n_batch, B_HEADS // heads_per_step, nq),
        in_specs=in_specs,
        out_specs=out_specs,
        scratch_shapes=scratch,
        compiler_params=_params("arbitrary", "arbitrary", "arbitrary"),
        name="attn_b_lat" if has_ctx else "attn_b_ctx",
    )(*args)


def _attn_c_ctx_kernel(q_ref, k_ref, v_ref, sg_ref, o_ref):
    for h in range(C_HEADS):
        hc = slice(h * HEAD_DIM, (h + 1) * HEAD_DIM)
        s = _nt_dot(q_ref[:, hc], k_ref[:, hc].astype(BF16))
        o = _softmax_pv([s], [v_ref[:, hc].astype(BF16)])
        o_ref[:, hc] = (o * sg_ref[:, hc].astype(F32)).astype(o_ref.dtype)


def _attn_c_ctx(q, k, v, sg, n_batch, seq):
    spec = pl.BlockSpec((seq, D_MODEL), lambda b: (b, 0))
    return pl.pallas_call(
        _attn_c_ctx_kernel,
        out_shape=jax.ShapeDtypeStruct(q.shape, BF16),
        grid=(n_batch,),
        in_specs=[spec, spec, spec, spec],
        out_specs=spec,
        compiler_params=_params("arbitrary"),
        name="attn_c_ctx",
    )(q, k, v, sg)


def _na_geometry(rows):
    kh = min(NA_KH, rows)
    key_rows = C_ROWS + kh - 1
    key_rows = min(key_rows + key_rows % 2, rows)
    n_blocks = rows // C_ROWS
    cols = np.arange(GRID_W)
    col_start = np.clip(cols - NA_KW // 2, 0, GRID_W - NA_KW)
    col_ok = (cols[None, :] >= col_start[:, None]) & (cols[None, :] < col_start[:, None] + NA_KW)
    col_delta = cols[None, :] - cols[:, None] + NA_KW - 1
    col_onehot = (col_delta[None] == np.arange(2 * NA_KW - 1)[:, None, None]) & col_ok[None]
    starts, class_ids, classes, keys = [], [], [], {}
    for blk in range(n_blocks):
        r = blk * C_ROWS + np.arange(C_ROWS)
        rs = np.clip(r - kh // 2, 0, rows - kh)
        start = int(np.clip(rs[0], 0, rows - key_rows))
        key = (tuple(rs - r), start - blk * C_ROWS)
        if key not in keys:
            keys[key] = len(classes)
            kr = start + np.arange(key_rows)
            row_ok = (kr[None, :] >= rs[:, None]) & (kr[None, :] < rs[:, None] + kh)
            row_idx = kr[None, :] - r[:, None] + NA_KH - 1
            classes.append((row_ok, row_idx))
        starts.append(start)
        class_ids.append(keys[key])
    geometry = (col_ok, col_onehot.astype(np.float32), classes)
    return key_rows, np.asarray(starts, np.int32), np.asarray(class_ids, np.int32), geometry


def _na_bias_tables(rpb, geometry):
    col_ok, col_onehot, classes = geometry
    n_heads = rpb.shape[0]
    toep = jnp.einsum("hdx,xck->hdck", rpb * LOG2E, jnp.asarray(col_onehot),
                      precision=lax.Precision.HIGHEST)
    toep = jnp.where(jnp.asarray(col_ok)[None, None], toep, NEG_INF)
    n_off = toep.shape[1]
    toep = jnp.concatenate([toep, jnp.full((n_heads, 1, GRID_W, GRID_W), NEG_INF, F32)], axis=1)
    select = np.stack([np.where(row_ok, row_idx, n_off) for row_ok, row_idx in classes])
    n_cls, n_r, n_kr = select.shape
    dnums = lax.GatherDimensionNumbers(offset_dims=(1, 3, 5), collapsed_slice_dims=(1,), start_index_map=(1,))
    tabs = lax.gather(toep, jnp.asarray(select[..., None], jnp.int32), dnums,
                      slice_sizes=(n_heads, 1, GRID_W, GRID_W))
    return tabs.reshape(n_cls, n_heads, n_r * GRID_W, n_kr * GRID_W)


def _attn_c_lat_kernel(*refs, key_rows, n_sub):
    start_ref, cls_ref, q_ref, k_ref, v_ref, kc_ref, vc_ref = refs[:7]
    bias_refs = refs[7:7 + n_sub]
    sg_ref, o_ref, s_scr, m_scr, acc_scr = refs[7 + n_sub:]
    step = pl.program_id(2)
    n_loc = key_rows * GRID_W
    n_keys = s_scr.shape[-1]
    m_rows = q_ref.shape[0] // n_sub
    key_rows_of = [pl.ds(pl.multiple_of(start_ref[step * n_sub + sub] * GRID_W, GRID_W), n_loc)
                   for sub in range(n_sub)]
    sub_rows = [slice(sub * m_rows, (sub + 1) * m_rows) for sub in range(n_sub)]
    head_cols = [slice(h * HEAD_DIM, (h + 1) * HEAD_DIM) for h in range(C_HEAD_GROUP)]
    units = [(sub, h) for sub in range(n_sub) for h in range(C_HEAD_GROUP)]

    for u, (sub, h) in enumerate(units):
        hc = head_cols[h]
        q = q_ref[sub_rows[sub], hc]
        s_loc = _nt_dot(q, k_ref[key_rows_of[sub], hc]) + bias_refs[sub][h]
        s_ctx = _nt_dot(q, kc_ref[:, hc].astype(BF16))
        s_scr[u, :, 0:n_loc] = s_loc
        s_scr[u, :, n_loc:n_keys] = s_ctx
        m_scr[u] = jnp.maximum(_lane_fold(s_loc, jnp.maximum), _lane_fold(s_ctx, jnp.maximum))

    for u in range(len(units)):
        m_scr[u] = jnp.broadcast_to(jnp.max(m_scr[u], axis=-1, keepdims=True), m_scr.shape[1:])

    for u, (sub, h) in enumerate(units):
        hc = head_cols[h]
        p = jnp.exp2(s_scr[u] - jnp.concatenate([m_scr[u]] * (n_keys // HEAD_DIM), axis=1)).astype(BF16)
        acc_scr[u] = (jnp.dot(p[:, 0:n_loc], _with_ones(v_ref[key_rows_of[sub], hc]), preferred_element_type=F32)
                      + jnp.dot(p[:, n_loc:n_keys], _with_ones(vc_ref[:, hc].astype(BF16)),
                                preferred_element_type=F32))

    for u, (sub, h) in enumerate(units):
        hc = head_cols[h]
        acc = acc_scr[u]
        o = acc[:, :HEAD_DIM] * (1.0 / acc[:, HEAD_DIM:])
        o_ref[sub_rows[sub], hc] = (o * sg_ref[sub_rows[sub], hc].astype(F32)).astype(o_ref.dtype)


def _attn_c_lat(q, k, v, k_ctx, v_ctx, sg, rpb, n_batch, seq):
    rows = seq // GRID_W
    key_rows, starts, class_ids, geometry = _na_geometry(rows)
    bias = _na_bias_tables(rpb, geometry)
    n_sub = C_SUBBLOCKS
    nblk = rows // C_ROWS
    assert nblk % n_sub == 0
    steps = nblk // n_sub
    tq = C_ROWS * GRID_W
    gw = C_HEAD_GROUP * HEAD_DIM
    n_units = n_sub * C_HEAD_GROUP
    n_ctx = k_ctx.shape[1]
    qmap = lambda b, g, i, st, cl: (b * steps + i, g)
    kvmap = lambda b, g, i, st, cl: (b, g)
    ctxmap = lambda b, g, i, st, cl: (b, 0, g)
    bias_specs = [pl.BlockSpec((None, C_HEAD_GROUP, tq, key_rows * GRID_W),
                               lambda b, g, i, st, cl, sub=sub: (cl[i * n_sub + sub], g, 0, 0))
                  for sub in range(n_sub)]
    q_spec = pl.BlockSpec((n_sub * tq, gw), qmap)
    grid_spec = pltpu.PrefetchScalarGridSpec(
        num_scalar_prefetch=2,
        grid=(n_batch, C_HEADS // C_HEAD_GROUP, steps),
        in_specs=[
            q_spec,
            pl.BlockSpec((seq, gw), kvmap),
            pl.BlockSpec((seq, gw), kvmap),
            pl.BlockSpec((None, n_ctx, gw), ctxmap),
            pl.BlockSpec((None, n_ctx, gw), ctxmap),
        ] + bias_specs + [q_spec],
        out_specs=q_spec,
        scratch_shapes=[pltpu.VMEM((n_units, tq, key_rows * GRID_W + n_ctx), F32),
                        pltpu.VMEM((n_units, tq, HEAD_DIM), F32),
                        pltpu.VMEM((n_units, tq, 2 * HEAD_DIM), F32)],
    )
    return pl.pallas_call(
        functools.partial(_attn_c_lat_kernel, key_rows=key_rows, n_sub=n_sub),
        out_shape=jax.ShapeDtypeStruct(q.shape, BF16),
        grid_spec=grid_spec,
        compiler_params=_params("arbitrary", "arbitrary", "arbitrary"),
        name="attn_c_lat",
    )(jnp.asarray(starts), jnp.asarray(class_ids), q, k, v, k_ctx, v_ctx, *([bias] * n_sub), sg)


def _rope_tables(n_tokens):
    d4 = HEAD_DIM // 4
    t = jnp.arange(n_tokens)
    pos = jnp.stack([t // GRID_W, t % GRID_W], axis=-1).astype(F32)
    inv_freq = ROPE_BASE ** (-jnp.arange(d4, dtype=F32) / d4)
    ang = pos[:, :, None] * inv_freq
    cos, sin = jnp.cos(ang), jnp.sin(ang)
    zero = jnp.zeros_like(sin[:, 0])
    cos_full = jnp.concatenate([cos[:, 0], cos[:, 0], cos[:, 1], cos[:, 1]], axis=-1)
    sin_up = jnp.concatenate([-sin[:, 0], zero, -sin[:, 1], zero], axis=-1)
    sin_dn = jnp.concatenate([zero, sin[:, 0], zero, sin[:, 1]], axis=-1)
    return cos_full, sin_up, sin_dn


def kernel(x_prompt, x_sample, cache_a_k, cache_a_v, cache_b_k, cache_b_v, cache_c_k, cache_c_v, c, c_ctx,
           ln_g, ada_w, ada_b, w_out, qn_g, kn_g, w_in_a, sink_a, w_in_b, lam_b, subln_b, w_in_c, rpb_c):
    n_p, seq_p, d = x_prompt.shape
    n_s, seq_s, _ = x_sample.shape
    n_ctx = cache_a_k.shape[2]
    xp = x_prompt.reshape(n_p * seq_p, d)
    xs = x_sample.reshape(n_s * seq_s, d)
    rope_tabs = _rope_tables(seq_s)

    pad = (-(n_s + 1)) % 8
    cvec = jnp.concatenate([c, c_ctx[None, :], jnp.zeros((pad, d), F32)], axis=0)
    mod = _ada_modulation(cvec, ada_w, ada_b)
    mod = mod.reshape(DEPTH, mod.shape[1], 3, 1, d)

    new_kv = {1: ([], []), 2: ([], [])}
    a_caches = None
    n_a_layers = len(range(0, DEPTH, N_MIXERS))
    for l in range(DEPTH):
        kind, j = l % N_MIXERS, l // N_MIXERS
        sh_s, sc_s, gt_s = (mod[l, :n_s, t] for t in range(3))
        sh_p, sc_p, gt_p = (mod[l, n_s:n_s + 1, t] for t in range(3))
        w_o = w_out[l].astype(BF16)
        if kind == 0:
            kvw = A_KV_HEADS * HEAD_DIM
            w_in, widths = w_in_a[j].astype(BF16), (d, kvw, kvw, d)
        elif kind == 1:
            w_in, widths = w_in_b[j].astype(BF16), (d, d, d, d)
        else:
            w_in, widths = w_in_c[j].astype(BF16), (d, d, d, d)
        tn = PROJ_COL_TILE
        rope = rope_tabs if kind != 2 else None
        qp, kp, vp, gp = _in_projection(xp, sh_p, sc_p, ln_g[l], w_in, qn_g[l], kn_g[l], widths, None, F32, tn)
        qs, ks, vs, gs = _in_projection(xs, sh_s, sc_s, ln_g[l], w_in, qn_g[l], kn_g[l], widths, rope, BF16, tn)
        if kind == 0:
            op, *a_caches = _attn_a_ctx(qp, kp, vp, gp, sink_a[j], n_p, seq_p, j, n_a_layers, a_caches)
            os_ = _attn_a_lat(qs, ks, vs, cache_a_k, cache_a_v, j, gs, sink_a[j], n_s, seq_s)
        elif kind == 1:
            lambda_init = 0.8 - 0.6 * math.exp(-0.3 * l)
            kc = cache_b_k[:, j].reshape(n_s, n_ctx, d)
            vc = cache_b_v[:, j].reshape(n_s, n_ctx, d)
            op, k_new = _attn_b(qp, kp, vp, gp, lam_b[j], subln_b[j], lambda_init, n_p, seq_p)
            os_ = _attn_b(qs, ks, vs, gs, lam_b[j], subln_b[j], lambda_init, n_s, seq_s, kc, vc)
            new_kv[1][0].append(k_new)
            new_kv[1][1].append(vp.reshape(n_p, seq_p, B_HEADS, 2 * HEAD_DIM))
        else:
            kc = cache_c_k[:, j].reshape(n_s, n_ctx, d)
            vc = cache_c_v[:, j].reshape(n_s, n_ctx, d)
            op = _attn_c_ctx(qp, kp, vp, gp, n_p, seq_p)
            os_ = _attn_c_lat(qs, ks, vs, kc, vc, gs, rpb_c[j], n_s, seq_s)
            new_kv[2][0].append(kp.reshape(n_p, seq_p, C_HEADS, HEAD_DIM))
            new_kv[2][1].append(vp.reshape(n_p, seq_p, C_HEADS, HEAD_DIM))
        xp = _out_projection(op, w_o, xp, gt_p)
        xs = _out_projection(os_, w_o, xs, gt_s)

    outs = [xp.reshape(x_prompt.shape), xs.reshape(x_sample.shape)] + list(a_caches)
    for kind in (1, 2):
        outs.append(jnp.stack(new_kv[kind][0], axis=1))
        outs.append(jnp.stack(new_kv[kind][1], axis=1))
    return tuple(outs)
```
